```python
import jax, jax.numpy as jnp
from jax import lax
import numpy as np

D_MODEL = 2048
BATCH = 4
SEQ = 2048
DEPTH = 1
DEC_BATCH = 128
DEC_SEQ = 4
PAST_LEN = 16384
PAGE_SIZE = 128

N_META = 16
MIX_WIDTH = D_MODEL
RWKV_WIDTH = MIX_WIDTH // 2
RWKV_HEAD = 64
RWKV_HEADS = RWKV_WIDTH // RWKV_HEAD
DECAY_LORA = 64
AAA_LORA = 64
GATE_LORA = 160
RWKV_COLS = 3 * RWKV_WIDTH + DECAY_LORA + AAA_LORA + GATE_LORA
RET_WIDTH = MIX_WIDTH - RWKV_WIDTH
RET_HEADS = 4
RET_HEAD = RET_WIDTH // RET_HEADS
RET_COLS = 4 * RET_WIDTH
IN_COLS = RWKV_COLS + RET_COLS
RET_CHUNK = 128
D_FF = -(-8 * D_MODEL // (3 * 256)) * 256
RMS_EPS = 1e-6
RWKV_GN_EPS = 64e-5
RET_GN_EPS = 1e-6
ROPE_BASE = 10000.0

kernel_name = 'hybrid_rwkv7_retention_decode_step'

F32 = jnp.float32


def rms_norm(x, g):
    xf = x.astype(F32)
    y = xf * lax.rsqrt(jnp.mean(xf * xf, axis=-1, keepdims=True) + RMS_EPS)
    return (y * g.astype(F32)).astype(x.dtype)


def rwkv7_scan(r, w, k, v, a_vec, b_vec, S0):
    def step(S, inp):
        r_t, w_t, k_t, v_t, a_t, b_t = inp
        sa = jnp.einsum('bhij,bhj->bhi', S, a_t)
        S = S * w_t[:, :, None, :] + sa[..., None] * b_t[:, :, None, :] + v_t[..., None] * k_t[:, :, None, :]
        return S, jnp.einsum('bhij,bhj->bhi', S, r_t)
    xs = tuple(jnp.swapaxes(t, 0, 1) for t in (r, w, k, v, a_vec, b_vec))
    S, out = lax.scan(step, S0, xs)
    return jnp.swapaxes(out, 0, 1), S


def rwkv7_mixer(z, z_prev, S0, mu, w0, w2, a0, a2, g2, k_k, k_a, r_k, ln_w, ln_b):
    B, T, _ = z.shape
    W = RWKV_WIDTH
    zs = z + mu * (z_prev - z)
    r, k, v, wd, ad, gd = jnp.split(zs, [W, 2 * W, 3 * W, 3 * W + DECAY_LORA, 3 * W + DECAY_LORA + AAA_LORA], axis=-1)
    heads = lambda t: t.reshape(B, T, RWKV_HEADS, RWKV_HEAD)
    w = -jax.nn.softplus(-(w0 + jnp.tanh(wd) @ w2)) - 0.5
    decay = jnp.exp(-jnp.exp(w))
    a = jax.nn.sigmoid(a0 + ad @ a2)
    g = jax.nn.sigmoid(gd) @ g2
    kk = heads(k * k_k)
    kk = kk / jnp.maximum(jnp.sqrt(jnp.sum(kk * kk, axis=-1, keepdims=True)), 1e-12)
    k = k * (1.0 + (a - 1.0) * k_a)
    rh, kh, vh = heads(r), heads(k), heads(v)
    o, S = rwkv7_scan(rh, heads(decay), kh, vh, -kk, kk * heads(a), S0)
    mean = jnp.mean(o, axis=-1, keepdims=True)
    var = jnp.mean(jnp.square(o - mean), axis=-1, keepdims=True)
    o = ((o - mean) * lax.rsqrt(var + RWKV_GN_EPS)).reshape(B, T, W) * ln_w + ln_b
    bonus = jnp.sum(rh * kh * r_k, axis=-1, keepdims=True) * vh
    o = o + bonus.reshape(B, T, W)
    return o * g, S


def rotary(x, pos):
    inv_freq = 1.0 / (ROPE_BASE ** jnp.linspace(0.0, 1.0, RET_HEAD // 2, dtype=F32))
    ang = pos.astype(F32)[:, None] * inv_freq[None, :]
    cos = jnp.cos(ang)[None, :, None, :]
    sin = jnp.sin(ang)[None, :, None, :]
    x0 = x[..., 0::2]
    x1 = x[..., 1::2]
    return jnp.stack([x0 * cos - x1 * sin, x0 * sin + x1 * cos], axis=-1).reshape(x.shape)


def retention_chunk(q, k, v, S, log_gamma):
    C = q.shape[1]
    idx = jnp.arange(C, dtype=F32)
    diff = idx[:, None] - idx[None, :]
    dmask = jnp.where(diff[None] >= 0, jnp.exp(log_gamma[:, None, None] * jnp.maximum(diff, 0.0)[None]), 0.0)
    scores = jnp.einsum('bihd,bjhd->bhij', q, k) * dmask[None]
    o_intra = jnp.einsum('bhij,bjhe->bihe', scores, v)
    inter_scale = jnp.exp(log_gamma[None, :] * (idx + 1.0)[:, None])
    o_inter = jnp.einsum('bihd,bhde->bihe', q, S) * inter_scale[None, :, :, None]
    k_scale = jnp.exp(log_gamma[:, None] * (C - 1.0 - idx)[None, :])
    S_new = jnp.exp(log_gamma * C)[None, :, None, None] * S + jnp.einsum('bjhd,hj,bjhe->bhde', k, k_scale, v)
    return o_intra + o_inter, S_new


def retention_mixer(z, S0, pos, prompt):
    B, T, _ = z.shape
    q, k, v, g = jnp.split(z, 4, axis=-1)
    heads = lambda t: t.reshape(B, T, RET_HEADS, RET_HEAD)
    q = rotary(heads(q), pos)
    k = rotary(heads(k), pos) * (RET_HEAD ** -0.5)
    v = heads(v)
    log_gamma = jnp.log(1.0 - 2.0 ** (-5.0 - jnp.arange(RET_HEADS, dtype=F32)))
    if prompt:
        o_m, S = retention_chunk(q[:, :N_META], k[:, :N_META], v[:, :N_META], S0, log_gamma)
        n_chunks = (T - N_META) // RET_CHUNK
        to_chunks = lambda t: jnp.swapaxes(t[:, N_META:].reshape(B, n_chunks, RET_CHUNK, RET_HEADS, RET_HEAD), 0, 1)

        def body(S_c, qkv):
            o_c, S_c = retention_chunk(qkv[0], qkv[1], qkv[2], S_c, log_gamma)
            return S_c, o_c

        S, o_c = lax.scan(body, S, (to_chunks(q), to_chunks(k), to_chunks(v)))
        o = jnp.concatenate([o_m, jnp.swapaxes(o_c, 0, 1).reshape(B, T - N_META, RET_HEADS, RET_HEAD)], axis=1)
    else:
        o, S = retention_chunk(q, k, v, S0, log_gamma)
    o = o * lax.rsqrt(jnp.mean(o * o, axis=-1, keepdims=True) + RET_GN_EPS)
    return o.reshape(B, T, RET_WIDTH) * jax.nn.silu(g), S


def decoder_layer(x, shift0, S_a0, S_b0, pos, prompt, norm_mix, w_in, rwkv_mu, rwkv_w0, rwkv_w2,
                  rwkv_a0, rwkv_a2, rwkv_g2, rwkv_kk, rwkv_ka, rwkv_rk, rwkv_ln_w, rwkv_ln_b,
                  w_out, norm_ffn, w_gate, w_up, w_down):
    f = lambda t: t.astype(F32)
    xn = f(rms_norm(x, norm_mix))
    z = xn @ f(w_in)
    z_a, z_b = z[..., :RWKV_COLS], z[..., RWKV_COLS:]
    z_prev = jnp.concatenate([f(shift0)[:, None], z_a[:, :-1]], axis=1)
    o_a, S_a = rwkv7_mixer(z_a, z_prev, f(S_a0), f(rwkv_mu), f(rwkv_w0), f(rwkv_w2), f(rwkv_a0),
                           f(rwkv_a2), f(rwkv_g2), f(rwkv_kk), f(rwkv_ka), f(rwkv_rk),
                           f(rwkv_ln_w), f(rwkv_ln_b))
    o_b, S_b = retention_mixer(z_b, f(S_b0), pos, prompt)
    h = x + (jnp.concatenate([o_a, o_b], axis=-1) @ f(w_out)).astype(x.dtype)
    hn = rms_norm(h, norm_ffn)
    h = h + (jax.nn.silu(hn @ w_gate) * (hn @ w_up)) @ w_down
    return h, z_a[:, -1], S_a, S_b


def setup_inputs(seed: int = 0) -> dict:
    key = jax.random.key(seed)
    ks = jax.random.split(key, 32)
    n = lambda i, shape: jax.random.normal(ks[i], shape, F32)
    L = DEPTH
    return {
        'x_prompt': n(0, (BATCH, SEQ, D_MODEL)),
        'x_sample': n(1, (DEC_BATCH, DEC_SEQ, D_MODEL)),
        'state_shift': n(2, (L, DEC_BATCH, RWKV_COLS)),
        'state_rwkv': n(3, (L, DEC_BATCH, RWKV_HEADS, RWKV_HEAD, RWKV_HEAD)),
        'state_ret': 0.5 * n(4, (L, DEC_BATCH, RET_HEADS, RET_HEAD, RET_HEAD)),
        'meta_tokens': n(5, (N_META, D_MODEL)),
        'norm_mix': 1.0 + 0.05 * n(6, (L, D_MODEL)),
        'w_in': n(7, (L, D_MODEL, IN_COLS)) * D_MODEL ** -0.5,
        'rwkv_mu': jax.random.uniform(ks[8], (L, RWKV_COLS), F32),
        'rwkv_w0': jax.random.uniform(ks[9], (L, RWKV_WIDTH), F32, -6.0, -1.0),
        'rwkv_w2': n(10, (L, DECAY_LORA, RWKV_WIDTH)) * 0.1 * DECAY_LORA ** -0.5,
        'rwkv_a0': 0.1 * n(11, (L, RWKV_WIDTH)),
        'rwkv_a2': n(12, (L, AAA_LORA, RWKV_WIDTH)) * 0.5 * AAA_LORA ** -0.5,
        'rwkv_g2': n(13, (L, GATE_LORA, RWKV_WIDTH)) * GATE_LORA ** -0.5,
        'rwkv_kk': 0.85 + 0.05 * n(14, (L, RWKV_WIDTH)),
        'rwkv_ka': 1.0 + 0.05 * n(15, (L, RWKV_WIDTH)),
        'rwkv_rk': 0.1 * n(16, (L, RWKV_HEADS, RWKV_HEAD)),
        'rwkv_ln_w': 1.0 + 0.05 * n(17, (L, RWKV_WIDTH)),
        'rwkv_ln_b': 0.02 * n(18, (L, RWKV_WIDTH)),
        'w_out': n(19, (L, MIX_WIDTH, D_MODEL)) * 0.5 * MIX_WIDTH ** -0.5,
        'norm_ffn': 1.0 + 0.05 * n(20, (L, D_MODEL)),
        'w_gate': n(21, (L, D_MODEL, D_FF)) * D_MODEL ** -0.5,
        'w_up': n(22, (L, D_MODEL, D_FF)) * D_MODEL ** -0.5,
        'w_down': n(23, (L, D_FF, D_MODEL)) * D_FF ** -0.5,
        'norm_final': 1.0 + 0.05 * n(24, (D_MODEL,)),
    }


def reference(x_prompt, x_sample, state_shift, state_rwkv, state_ret, meta_tokens, norm_mix, w_in,
              rwkv_mu, rwkv_w0, rwkv_w2, rwkv_a0, rwkv_a2, rwkv_g2, rwkv_kk, rwkv_ka, rwkv_rk,
              rwkv_ln_w, rwkv_ln_b, w_out, norm_ffn, w_gate, w_up, w_down, norm_final):
    B_p = x_prompt.shape[0]
    meta = jnp.broadcast_to(meta_tokens[None].astype(x_prompt.dtype), (B_p, N_META, x_prompt.shape[-1]))
    h_p = jnp.concatenate([meta, x_prompt], axis=1)
    h_s = x_sample
    pos_p = jnp.arange(h_p.shape[1])
    pos_s = PAST_LEN + jnp.arange(x_sample.shape[1])
    shift0_p = jnp.zeros((B_p, RWKV_COLS), F32)
    rwkv0_p = jnp.zeros((B_p, RWKV_HEADS, RWKV_HEAD, RWKV_HEAD), F32)
    ret0_p = jnp.zeros((B_p, RET_HEADS, RET_HEAD, RET_HEAD), F32)
    sh_p, ra_p, rb_p, sh_s, ra_s, rb_s = [], [], [], [], [], []
    for l in range(DEPTH):
        lw = (norm_mix[l], w_in[l], rwkv_mu[l], rwkv_w0[l], rwkv_w2[l], rwkv_a0[l], rwkv_a2[l],
              rwkv_g2[l], rwkv_kk[l], rwkv_ka[l], rwkv_rk[l], rwkv_ln_w[l], rwkv_ln_b[l],
              w_out[l], norm_ffn[l], w_gate[l], w_up[l], w_down[l])
        h_p, s1, s2, s3 = decoder_layer(h_p, shift0_p, rwkv0_p, ret0_p, pos_p, True, *lw)
        sh_p.append(s1); ra_p.append(s2); rb_p.append(s3)
        h_s, s1, s2, s3 = decoder_layer(h_s, state_shift[l], state_rwkv[l], state_ret[l], pos_s, False, *lw)
        sh_s.append(s1); ra_s.append(s2); rb_s.append(s3)
    y_prompt = rms_norm(h_p, norm_final)[:, N_META:]
    y_sample = rms_norm(h_s, norm_final)
    return (y_prompt, y_sample, jnp.stack(sh_p), jnp.stack(ra_p), jnp.stack(rb_p),
            jnp.stack(sh_s), jnp.stack(ra_s), jnp.stack(rb_s))
```

```python
import functools
import math

import jax
import jax.numpy as jnp
from jax import lax
from jax.experimental import pallas as pl
from jax.experimental.pallas import tpu as pltpu

F32 = jnp.float32
BF16 = jnp.bfloat16

D_MODEL = 2048
N_META = 16
RW_WIDTH = 1024
RW_HEAD = 64
RW_HEADS = 16
DECAY_LORA = 64
AAA_LORA = 64
GATE_LORA = 160
RW_COLS = 3 * RW_WIDTH + DECAY_LORA + AAA_LORA + GATE_LORA
RET_WIDTH = 1024
RET_HEADS = 4
RET_HEAD = 256
RET_CHUNK = 128
D_FF = 5632
RMS_EPS = 1e-6
RW_GN_EPS = 64e-5
RET_GN_EPS = 1e-6
ROPE_BASE = 10000.0

LORA_PAD = 512
Z_COLS = 4 * RET_WIDTH + 3 * RW_WIDTH + LORA_PAD
RW_OFF = 4 * RET_WIDTH
SHIFT_PAD = 3 * RW_WIDTH + LORA_PAD

VMEM_LIMIT = 56 * 1024 * 1024


def _cparams(sem):
    return pltpu.CompilerParams(dimension_semantics=sem, vmem_limit_bytes=VMEM_LIMIT)


def _mm(a, b):
    return jnp.dot(a.astype(BF16), b.astype(BF16), preferred_element_type=F32)


def _mm_nt(a, b):
    return lax.dot_general(a.astype(BF16), b.astype(BF16), (((1,), (1,)), ((), ())),
                           preferred_element_type=F32)


def _mm_tn(a, b):
    return lax.dot_general(a.astype(BF16), b.astype(BF16), (((0,), (0,)), ((), ())),
                           preferred_element_type=F32)


def _mm_exact_lhs(m_bf16, x):
    hi = x.astype(BF16)
    r1 = x - hi.astype(F32)
    mid = r1.astype(BF16)
    lo = (r1 - mid.astype(F32)).astype(BF16)
    d = functools.partial(jnp.dot, preferred_element_type=F32)
    return d(m_bf16, hi) + d(m_bf16, mid) + d(m_bf16, lo)


def _inproj_kernel(x_ref, g_ref, w_ref, o_ref, xn_ref):
    @pl.when(pl.program_id(1) == 0)
    def _():
        x = x_ref[...]
        ms = jnp.mean(x * x, axis=-1, keepdims=True)
        xn_ref[...] = ((x * lax.rsqrt(ms + RMS_EPS)) * g_ref[...]).astype(BF16)

    o_ref[...] = jnp.dot(xn_ref[...], w_ref[...], preferred_element_type=F32)


def _in_projection(x_all, norm_g, w_in_p, tm, tn):
    m = x_all.shape[0]
    return pl.pallas_call(
        _inproj_kernel,
        grid=(m // tm, Z_COLS // tn),
        in_specs=[
            pl.BlockSpec((tm, D_MODEL), lambda i, j: (i, 0)),
            pl.BlockSpec((1, D_MODEL), lambda i, j: (0, 0)),
            pl.BlockSpec((D_MODEL, tn), lambda i, j: (0, j)),
        ],
        out_specs=pl.BlockSpec((tm, tn), lambda i, j: (i, j)),
        out_shape=jax.ShapeDtypeStruct((m, Z_COLS), F32),
        scratch_shapes=[pltpu.VMEM((tm, D_MODEL), BF16)],
        compiler_params=_cparams(("parallel", "arbitrary")),
        name="in_projection",
    )(x_all, norm_g, w_in_p)


def _rwkv_kernel(zr_ref, zk_ref, zv_ref, zl_ref, zprev0_ref, s0_ref,
                 mu_ref, w0_ref, w2_ref, a0_ref, a2_ref, g2_ref, kk_ref, ka_ref, rk_ref,
                 lnw_ref, lnb_ref,
                 o_ref, sout_ref,
                 carry_ref, r_s, km_s, v_s, kkn_s, a_s, lw_s, cum_s, o_s,
                 *, tb, c, n_factors, per_chunk_state, n_pad):
    i = pl.program_id(1)
    n_chunks = tb // c

    @pl.when(i == 0)
    def _():
        carry_ref[...] = zprev0_ref[0]
        if not per_chunk_state:
            sout_ref[...] = s0_ref[...]

    if per_chunk_state:
        sout_ref[...] = s0_ref[...]

    row = lax.broadcasted_iota(jnp.int32, (tb, 1), 0)
    if n_pad:
        valid = ((row % c) >= n_pad).astype(F32)
    else:
        valid = None

    def shifted(z_ref, lo, hi):
        z = z_ref[...]
        prev = jnp.where(row == 0, carry_ref[:, lo:hi], pltpu.roll(z, 1, axis=0))
        zs = z + mu_ref[:, lo:hi] * (prev - z)
        carry_ref[:, lo:hi] = z[tb - 1:tb, :]
        return zs

    w = RW_WIDTH
    r = shifted(zr_ref, 0, w)
    k = shifted(zk_ref, w, 2 * w)
    v = shifted(zv_ref, 2 * w, 3 * w)
    zl = shifted(zl_ref, 3 * w, 3 * w + LORA_PAD)

    lo2 = zl[:, 0:128]
    wlog = -jax.nn.softplus(-(w0_ref[...] + _mm(jnp.tanh(lo2), w2_ref[...]))) - 0.5
    lw = -jnp.exp(wlog)
    a = jax.nn.sigmoid(a0_ref[...] + _mm(lo2, a2_ref[...]))
    g = _mm(jax.nn.sigmoid(zl[:, 128:384]), g2_ref[...])
    kk = k * kk_ref[...]
    km = k * (1.0 + (a - 1.0) * ka_ref[...])
    if valid is not None:
        r = r * valid
        km = km * valid
        v = v * valid
        kk = kk * valid
        lw = lw * valid

    ri = lax.broadcasted_iota(jnp.int32, (tb, tb), 0)
    ci = lax.broadcasted_iota(jnp.int32, (tb, tb), 1)
    tri = jnp.where((ri // c == ci // c) & (ri >= ci), 1.0, 0.0).astype(BF16)
    cum = _mm_exact_lhs(tri, lw)

    for h in range(RW_HEADS):
        sl = slice(h * RW_HEAD, (h + 1) * RW_HEAD)
        r_s[h] = r[:, sl]
        km_s[h] = km[:, sl]
        v_s[h] = v[:, sl]
        a_s[h] = a[:, sl]
        lw_s[h] = lw[:, sl]
        cum_s[h] = cum[:, sl]
        kkh = kk[:, sl]
        ss = jnp.sum(kkh * kkh, axis=-1, keepdims=True)
        kkn_s[h] = kkh / jnp.maximum(jnp.sqrt(ss), 1e-12)

    rr = lax.broadcasted_iota(jnp.int32, (c, c), 0)
    cc = lax.broadcasted_iota(jnp.int32, (c, c), 1)
    strict = rr > cc
    incl = rr >= cc
    eye = jnp.where(rr == cc, 1.0, 0.0).astype(F32)

    def chunk_body(ch, carry):
        rows = pl.ds(pl.multiple_of(ch * c, c), c)
        sidx = ch if per_chunk_state else 0
        for h in range(RW_HEADS):
            hs = slice(h * RW_HEAD, (h + 1) * RW_HEAD)
            rh = r_s[h, rows, :]
            kmh = km_s[h, rows, :]
            vh = v_s[h, rows, :]
            ah = a_s[h, rows, :]
            lwh = lw_s[h, rows, :]
            cumh = cum_s[h, rows, :]
            kkn = kkn_s[h, rows, :]
            s_prev = sout_ref[sidx, h]

            bvec = kkn * ah
            cum_end = cumh[c - 1:c, :]
            p_inc = jnp.exp(cumh)
            p_exc = jnp.exp(cumh - lwh)
            p_inv = jnp.exp(-cumh)
            p_end = jnp.exp(cum_end)
            p_rel = jnp.exp(cum_end - cumh)
            a_t = -(kkn * p_exc)
            r_t = rh * p_inc
            b_t = bvec * p_inv
            k_t = kmh * p_inv
            b_h = bvec * p_rel
            k_h = kmh * p_rel

            ar = jnp.concatenate([a_t, r_t], axis=0)
            ab = _mm_nt(ar, b_t)
            ak = _mm_nt(ar, k_t)
            lmat = jnp.where(strict, ab[:c], 0.0)
            aak = jnp.where(strict, ak[:c], 0.0)
            arb = jnp.where(incl, ab[c:], 0.0)
            ark = jnp.where(incl, ak[c:], 0.0)
            tinv = eye + lmat
            lp = lmat
            for _ in range(n_factors - 1):
                lp = _mm(lp, lp)
                tinv = tinv + _mm(tinv, lp)
            as2 = _mm_nt(ar, s_prev)
            x = as2[:c] + _mm(aak, vh)
            u = _mm(tinv, x)
            o = as2[c:] + _mm(arb, u) + _mm(ark, vh)
            s_new = s_prev * p_end + _mm_tn(jnp.concatenate([u, vh], axis=0),
                                            jnp.concatenate([b_h, k_h], axis=0))
            sout_ref[sidx, h] = s_new

            mean = jnp.mean(o, axis=-1, keepdims=True)
            var = jnp.mean(jnp.square(o - mean), axis=-1, keepdims=True)
            on = ((o - mean) * lax.rsqrt(var + RW_GN_EPS)) * lnw_ref[:, hs] + lnb_ref[:, hs]
            bonus = jnp.sum(rh * kmh * rk_ref[:, hs], axis=-1, keepdims=True) * vh
            o_s[h, rows, :] = on + bonus
        return carry

    lax.fori_loop(0, n_chunks, chunk_body, 0)

    o_full = jnp.concatenate([o_s[h] for h in range(RW_HEADS)], axis=-1)
    o_ref[...] = (o_full * g).astype(BF16)


def _rwkv_mixer(z, row_block0, n_groups, n_tiles, zprev0, s0, p, *, tb, c, n_real, per_chunk_state):
    n_states = s0.shape[0]
    sb = n_states // n_groups
    n_factors = max(1, math.ceil(math.log2(n_real)))
    kern = functools.partial(_rwkv_kernel, tb=tb, c=c, n_factors=n_factors,
                             per_chunk_state=per_chunk_state, n_pad=c - n_real if per_chunk_state else 0)
    cb = RW_OFF // RW_WIDTH
    zspec = lambda col: pl.BlockSpec((tb, RW_WIDTH), lambda gi, i: (row_block0 + gi * n_tiles + i, col))
    full = lambda arr: pl.BlockSpec(arr.shape, lambda gi, i: (0,) * arr.ndim)
    lora_cb = (RW_OFF + 3 * RW_WIDTH) // LORA_PAD
    params = [p["mu"], p["w0"], p["w2"], p["a0"], p["a2"], p["g2"], p["kk"], p["ka"], p["rk"],
              p["lnw"], p["lnb"]]
    hm = lambda: pltpu.VMEM((RW_HEADS, tb, RW_HEAD), F32)
    n_rows_out = n_groups * n_tiles * tb
    return pl.pallas_call(
        kern,
        grid=(n_groups, n_tiles),
        in_specs=[zspec(cb), zspec(cb + 1), zspec(cb + 2),
                  pl.BlockSpec((tb, LORA_PAD), lambda gi, i: (row_block0 + gi * n_tiles + i, lora_cb)),
                  pl.BlockSpec((1, 1, SHIFT_PAD), lambda gi, i: (gi, 0, 0)),
                  pl.BlockSpec((sb, RW_HEADS, RW_HEAD, RW_HEAD), lambda gi, i: (gi, 0, 0, 0))]
                 + [full(a) for a in params],
        out_specs=[pl.BlockSpec((tb, RW_WIDTH), lambda gi, i: (gi * n_tiles + i, 0)),
                   pl.BlockSpec((sb, RW_HEADS, RW_HEAD, RW_HEAD), lambda gi, i: (gi, 0, 0, 0))],
        out_shape=[jax.ShapeDtypeStruct((n_rows_out, RW_WIDTH), BF16),
                   jax.ShapeDtypeStruct((n_states, RW_HEADS, RW_HEAD, RW_HEAD), F32)],
        scratch_shapes=[pltpu.VMEM((1, SHIFT_PAD), F32)] + [hm() for _ in range(8)],
        compiler_params=_cparams(("parallel", "arbitrary")),
        name="rwkv7_mixer",
    )(z, z, z, z, zprev0, s0, *params)


def _retention_kernel(zq_ref, zk_ref, zv_ref, zg_ref, cos_ref, sin_ref, dmask_ref, iscale_ref,
                      kscale_ref, sdec_ref, s0_ref, o_ref, sout_ref,
                      *, rows, c, per_chunk_state):
    i = pl.program_id(1)
    n_chunks = rows // c

    if per_chunk_state:
        sout_ref[...] = s0_ref[...]
    else:
        @pl.when(i == 0)
        def _():
            sout_ref[...] = s0_ref[...]

    lane = lax.broadcasted_iota(jnp.int32, (rows, RET_WIDTH), 1)
    even = (lane % 2) == 0
    cos = jnp.concatenate([cos_ref[...]] * RET_HEADS, axis=-1)
    sin = jnp.concatenate([sin_ref[...]] * RET_HEADS, axis=-1)

    def rot(x):
        partner = jnp.where(even, pltpu.roll(x, RET_WIDTH - 1, axis=1), pltpu.roll(x, 1, axis=1))
        return x * cos + partner * sin

    q = rot(zq_ref[...])
    k = rot(zk_ref[...]) * (RET_HEAD ** -0.5)
    v = zv_ref[...]
    g = zg_ref[...]

    out_rows = []
    for ch in range(n_chunks):
        rs = slice(ch * c, (ch + 1) * c)
        sidx = ch if per_chunk_state else 0
        out_heads = []
        for h in range(RET_HEADS):
            hs = slice(h * RET_HEAD, (h + 1) * RET_HEAD)
            qh, kh, vh = q[rs, hs], k[rs, hs], v[rs, hs]
            s_prev = sout_ref[sidx, h]
            scores = _mm_nt(qh, kh) * dmask_ref[h]
            o = _mm(scores, vh) + _mm(qh, s_prev) * iscale_ref[h]
            sout_ref[sidx, h] = s_prev * sdec_ref[h] + _mm_tn(kh * kscale_ref[h], vh)
            o = o * lax.rsqrt(jnp.mean(o * o, axis=-1, keepdims=True) + RET_GN_EPS)
            gh = g[rs, hs]
            out_heads.append(o * (gh * jax.nn.sigmoid(gh)))
        out_rows.append(jnp.concatenate(out_heads, axis=-1))
    o_ref[...] = jnp.concatenate(out_rows, axis=0).astype(BF16)


def _retention_mixer(z, row_block0, n_groups, n_tiles, cos, sin, tabs, s0, *, rows, c, per_chunk_state,
                     pos_per_tile):
    n_states = s0.shape[0]
    sb = n_states // n_groups
    kern = functools.partial(_retention_kernel, rows=rows, c=c, per_chunk_state=per_chunk_state)
    zspec = lambda col: pl.BlockSpec((rows, RET_WIDTH), lambda gi, i: (row_block0 + gi * n_tiles + i, col))
    full = lambda arr: pl.BlockSpec(arr.shape, lambda gi, i: (0,) * arr.ndim)
    if pos_per_tile:
        tspec = pl.BlockSpec((rows, RET_HEAD), lambda gi, i: (i, 0))
    else:
        tspec = pl.BlockSpec((rows, RET_HEAD), lambda gi, i: (0, 0))
    sspec = pl.BlockSpec((sb, RET_HEADS, RET_HEAD, RET_HEAD), lambda gi, i: (gi, 0, 0, 0))
    dmask, iscale, kscale, sdec = tabs
    return pl.pallas_call(
        kern,
        grid=(n_groups, n_tiles),
        in_specs=[zspec(0), zspec(1), zspec(2), zspec(3), tspec, tspec,
                  full(dmask), full(iscale), full(kscale), full(sdec), sspec],
        out_specs=[pl.BlockSpec((rows, RET_WIDTH), lambda gi, i: (gi * n_tiles + i, 0)), sspec],
        out_shape=[jax.ShapeDtypeStruct((n_groups * n_tiles * rows, RET_WIDTH), BF16),
                   jax.ShapeDtypeStruct(s0.shape, F32)],
        compiler_params=_cparams(("parallel", "arbitrary")),
        name="retention_mixer",
    )(z, z, z, z, cos, sin, dmask, iscale, kscale, sdec, s0)


def _retention_tables(c, n_real):
    log_gamma = jnp.log(1.0 - 2.0 ** (-5.0 - jnp.arange(RET_HEADS, dtype=F32)))
    r = jnp.arange(c, dtype=F32)
    idx = r - float(c - n_real)
    diff = r[:, None] - r[None, :]
    dmask = jnp.where(diff[None] >= 0,
                      jnp.exp(log_gamma[:, None, None] * jnp.maximum(diff, 0.0)[None]), 0.0)
    iscale = jnp.exp(log_gamma[:, None] * (idx + 1.0)[None, :])[:, :, None]
    kscale = jnp.exp(log_gamma[:, None] * (n_real - 1.0 - idx)[None, :])[:, :, None]
    sdec = jnp.broadcast_to(jnp.exp(log_gamma * n_real)[:, None, None], (RET_HEADS, 1, RET_HEAD))
    return dmask, iscale, kscale, sdec


def _rotary_tables(pos):
    inv_freq = 1.0 / (ROPE_BASE ** jnp.linspace(0.0, 1.0, RET_HEAD // 2, dtype=F32))
    ang = pos.astype(F32)[:, None] * inv_freq[None, :]
    cos = jnp.cos(ang)
    sin = jnp.sin(ang)
    cos2 = jnp.repeat(cos, 2, axis=-1)
    sin2 = jnp.stack([-sin, sin], axis=-1).reshape(pos.shape[0], RET_HEAD)
    return cos2, sin2


def _outproj_kernel(oa_ref, ob_ref, w_ref, x_ref, g_ref, h_ref, hn_ref):
    acc = jnp.dot(oa_ref[...], w_ref[0:RW_WIDTH, :], preferred_element_type=F32)
    acc = acc + jnp.dot(ob_ref[...], w_ref[RW_WIDTH:, :], preferred_element_type=F32)
    h = x_ref[...] + acc
    h_ref[...] = h
    ms = jnp.mean(h * h, axis=-1, keepdims=True)
    hn_ref[...] = ((h * lax.rsqrt(ms + RMS_EPS)) * g_ref[...]).astype(BF16)


def _out_projection(o_a, o_b, w_out, x_all, norm_g, tm):
    m = o_a.shape[0]
    return pl.pallas_call(
        _outproj_kernel,
        grid=(m // tm,),
        in_specs=[
            pl.BlockSpec((tm, RW_WIDTH), lambda i: (i, 0)),
            pl.BlockSpec((tm, RET_WIDTH), lambda i: (i, 0)),
            pl.BlockSpec((D_MODEL, D_MODEL), lambda i: (0, 0)),
            pl.BlockSpec((tm, D_MODEL), lambda i: (i, 0)),
            pl.BlockSpec((1, D_MODEL), lambda i: (0, 0)),
        ],
        out_specs=[pl.BlockSpec((tm, D_MODEL), lambda i: (i, 0)),
                   pl.BlockSpec((tm, D_MODEL), lambda i: (i, 0))],
        out_shape=[jax.ShapeDtypeStruct((m, D_MODEL), F32), jax.ShapeDtypeStruct((m, D_MODEL), BF16)],
        compiler_params=_cparams(("parallel",)),
        name="out_projection",
    )(o_a, o_b, w_out, x_all, norm_g)


def _ffn_kernel(hn_ref, wg_ref, wu_ref, wd_ref, h_ref, g_ref, y_ref, acc_ref):
    f = pl.program_id(1)

    @pl.when(f == 0)
    def _():
        acc_ref[...] = jnp.zeros_like(acc_ref)

    hn = hn_ref[...]
    gate = jnp.dot(hn, wg_ref[...], preferred_element_type=F32)
    up = jnp.dot(hn, wu_ref[...], preferred_element_type=F32)
    act = (gate * jax.nn.sigmoid(gate)) * up
    acc_ref[...] += jnp.dot(act.astype(BF16), wd_ref[...], preferred_element_type=F32)

    @pl.when(f == pl.num_programs(1) - 1)
    def _():
        h = h_ref[...] + acc_ref[...]
        ms = jnp.mean(h * h, axis=-1, keepdims=True)
        y_ref[...] = (h * lax.rsqrt(ms + RMS_EPS)) * g_ref[...]


def _ffn(hn, w_gate, w_up, w_down, h, norm_g, tm, tf):
    m = hn.shape[0]
    return pl.pallas_call(
        _ffn_kernel,
        grid=(m // tm, D_FF // tf),
        in_specs=[
            pl.BlockSpec((tm, D_MODEL), lambda i, f: (i, 0)),
            pl.BlockSpec((D_MODEL, tf), lambda i, f: (0, f)),
            pl.BlockSpec((D_MODEL, tf), lambda i, f: (0, f)),
            pl.BlockSpec((tf, D_MODEL), lambda i, f: (f, 0)),
            pl.BlockSpec((tm, D_MODEL), lambda i, f: (i, 0)),
            pl.BlockSpec((1, D_MODEL), lambda i, f: (0, 0)),
        ],
        out_specs=pl.BlockSpec((tm, D_MODEL), lambda i, f: (i, 0)),
        out_shape=jax.ShapeDtypeStruct((m, D_MODEL), F32),
        scratch_shapes=[pltpu.VMEM((tm, D_MODEL), F32)],
        compiler_params=_cparams(("parallel", "arbitrary")),
        name="swiglu_ffn",
    )(hn, w_gate, w_up, w_down, h, norm_g)


def _pad_cols(a, n):
    return jnp.pad(a, ((0, 0), (0, n - a.shape[1])))


def kernel(x_prompt, x_sample, state_shift, state_rwkv, state_ret, meta_tokens, norm_mix, w_in,
           rwkv_mu, rwkv_w0, rwkv_w2, rwkv_a0, rwkv_a2, rwkv_g2, rwkv_kk, rwkv_ka, rwkv_rk,
           rwkv_ln_w, rwkv_ln_b, w_out, norm_ffn, w_gate, w_up, w_down, norm_final):
    n_b, seq, d = x_prompt.shape
    n_s, dec_seq, _ = x_sample.shape
    n_p = n_b * seq
    n_d = n_s * dec_seq
    depth = w_in.shape[0]
    assert depth == 1 and d == D_MODEL

    wi = w_in[0]
    w_in_p = jnp.concatenate(
        [wi[:, RW_COLS:], wi[:, :RW_COLS], jnp.zeros((d, Z_COLS - wi.shape[1]), F32)], axis=1).astype(BF16)
    w_out_b = w_out[0].astype(BF16)
    w_gate_b = w_gate[0].astype(BF16)
    w_up_b = w_up[0].astype(BF16)
    w_down_b = w_down[0].astype(BF16)
    row = lambda a: a.reshape(1, -1).astype(F32)
    w2p = jnp.concatenate([rwkv_w2[0], jnp.zeros((128 - DECAY_LORA, RW_WIDTH), F32)], axis=0)
    a2p = jnp.concatenate([jnp.zeros((DECAY_LORA, RW_WIDTH), F32), rwkv_a2[0]], axis=0)
    g2p = jnp.concatenate([rwkv_g2[0], jnp.zeros((256 - GATE_LORA, RW_WIDTH), F32)], axis=0)
    rw_params = dict(mu=_pad_cols(row(rwkv_mu[0]), SHIFT_PAD), w0=row(rwkv_w0[0]), w2=w2p,
                     a0=row(rwkv_a0[0]), a2=a2p, g2=g2p, kk=row(rwkv_kk[0]), ka=row(rwkv_ka[0]),
                     rk=row(rwkv_rk[0]), lnw=row(rwkv_ln_w[0]), lnb=row(rwkv_ln_b[0]))

    x_all = jnp.concatenate([x_prompt.reshape(n_p, d), x_sample.reshape(n_d, d),
                             meta_tokens.astype(F32)], axis=0)
    n_all = x_all.shape[0]
    z = _in_projection(x_all, row(norm_mix[0]), w_in_p, tm=n_all // 10, tn=768)

    z_meta = jnp.pad(z[n_p + n_d:], ((RET_CHUNK - N_META, 0), (0, 0)))
    srow = 2 * dec_seq
    shift_row = jnp.pad(state_shift[0], ((0, 0), (RW_OFF, Z_COLS - RW_OFF - RW_COLS)))
    z_smp = jnp.concatenate(
        [jnp.zeros((n_s, srow - dec_seq - 1, Z_COLS), F32), shift_row[:, None, :],
         z[n_p:n_p + n_d].reshape(n_s, dec_seq, Z_COLS)], axis=1).reshape(n_s * srow, Z_COLS)

    zero_prev = jnp.zeros((1, 1, SHIFT_PAD), F32)
    zero_rw = jnp.zeros((1, RW_HEADS, RW_HEAD, RW_HEAD), F32)
    c_rw = 64
    _, s_rw_meta = _rwkv_mixer(z_meta, 1, 1, 1, zero_prev, zero_rw, rw_params,
                               tb=c_rw, c=c_rw, n_real=c_rw, per_chunk_state=False)
    zprev_p = jnp.broadcast_to(z_meta[-1:, RW_OFF:RW_OFF + SHIFT_PAD][None], (n_b, 1, SHIFT_PAD))
    tb_p = 256
    oa_p, rwkv_p = _rwkv_mixer(z, 0, n_b, seq // tb_p, zprev_p,
                               jnp.broadcast_to(s_rw_meta, (n_b,) + s_rw_meta.shape[1:]), rw_params,
                               tb=tb_p, c=c_rw, n_real=c_rw, per_chunk_state=False)
    sb_rw = 8
    oa_s, rwkv_s = _rwkv_mixer(z_smp, 0, n_s // sb_rw, 1, jnp.zeros((n_s // sb_rw, 1, SHIFT_PAD), F32),
                               state_rwkv[0], rw_params,
                               tb=sb_rw * srow, c=srow, n_real=dec_seq, per_chunk_state=True)

    past_len = 16384
    tabs_full = _retention_tables(RET_CHUNK, RET_CHUNK)
    tabs_meta = _retention_tables(RET_CHUNK, N_META)
    tabs_smp = _retention_tables(srow, dec_seq)
    cos_m, sin_m = _rotary_tables(jnp.arange(RET_CHUNK) - (RET_CHUNK - N_META))
    cos_p, sin_p = _rotary_tables(N_META + jnp.arange(seq))
    sb_rt = 4
    pos_s = past_len + jnp.tile(jnp.arange(srow) - (srow - dec_seq), sb_rt)
    cos_s, sin_s = _rotary_tables(pos_s)
    zero_rt = jnp.zeros((1, RET_HEADS, RET_HEAD, RET_HEAD), F32)
    _, s_rt_meta = _retention_mixer(z_meta, 0, 1, 1, cos_m, sin_m, tabs_meta, zero_rt,
                                    rows=RET_CHUNK, c=RET_CHUNK, per_chunk_state=False, pos_per_tile=False)
    ob_p, ret_p = _retention_mixer(z, 0, n_b, seq // RET_CHUNK, cos_p, sin_p, tabs_full,
                                   jnp.broadcast_to(s_rt_meta, (n_b,) + s_rt_meta.shape[1:]),
                                   rows=RET_CHUNK, c=RET_CHUNK, per_chunk_state=False, pos_per_tile=True)
    ob_s, ret_s = _retention_mixer(z_smp, 0, n_s // sb_rt, 1, cos_s, sin_s, tabs_smp, state_ret[0],
                                   rows=sb_rt * srow, c=srow, per_chunk_state=True, pos_per_tile=False)

    take = lambda o: o.reshape(n_s, srow, -1)[:, srow - dec_seq:].reshape(n_d, -1)
    o_a = jnp.concatenate([oa_p, take(oa_s)], axis=0)
    o_b = jnp.concatenate([ob_p, take(ob_s)], axis=0)
    h, hn = _out_projection(o_a, o_b, w_out_b, x_all, row(norm_ffn[0]), tm=512)
    y = _ffn(hn, w_gate_b, w_up_b, w_down_b, h, row(norm_final), tm=512, tf=512)

    y_prompt = y[:n_p].reshape(n_b, seq, d)
    y_sample = y[n_p:].reshape(n_s, dec_seq, d)
    za = z[:, RW_OFF:RW_OFF + RW_COLS]
    shift_p = za[:n_p].reshape(n_b, seq, RW_COLS)[:, -1][None]
    shift_s = za[n_p:n_p + n_d].reshape(n_s, dec_seq, RW_COLS)[:, -1][None]
    return (y_prompt, y_sample, shift_p, rwkv_p[None], ret_p[None], shift_s, rwkv_s[None], ret_s[None])
```

```python
import functools
import math

import jax
import jax.numpy as jnp
from jax import lax
from jax.experimental import pallas as pl
from jax.experimental.pallas import tpu as pltpu

F32 = jnp.float32
BF16 = jnp.bfloat16

D_MODEL = 2048
N_META = 16
RW_WIDTH = 1024
RW_HEAD = 64
RW_HEADS = 16
DECAY_LORA = 64
AAA_LORA = 64
GATE_LORA = 160
RW_COLS = 3 * RW_WIDTH + DECAY_LORA + AAA_LORA + GATE_LORA
RET_WIDTH = 1024
RET_HEADS = 4
RET_HEAD = 256
RET_CHUNK = 128
D_FF = 5632
RMS_EPS = 1e-6
RW_GN_EPS = 64e-5
RET_GN_EPS = 1e-6
ROPE_BASE = 10000.0

LORA_PAD = 512
Z_COLS = 4 * RET_WIDTH + 3 * RW_WIDTH + LORA_PAD
RW_OFF = 4 * RET_WIDTH
SHIFT_PAD = 3 * RW_WIDTH + LORA_PAD

VMEM_LIMIT = 56 * 1024 * 1024


def _cparams(sem):
    return pltpu.CompilerParams(dimension_semantics=sem, vmem_limit_bytes=VMEM_LIMIT)


def _mm(a, b):
    return jnp.dot(a.astype(BF16), b.astype(BF16), preferred_element_type=F32)


def _mm_nt(a, b):
    return lax.dot_general(a.astype(BF16), b.astype(BF16), (((1,), (1,)), ((), ())),
                           preferred_element_type=F32)


def _mm_tn(a, b):
    return lax.dot_general(a.astype(BF16), b.astype(BF16), (((0,), (0,)), ((), ())),
                           preferred_element_type=F32)


def _mm_exact_lhs(m_bf16, x):
    hi = x.astype(BF16)
    r1 = x - hi.astype(F32)
    mid = r1.astype(BF16)
    lo = (r1 - mid.astype(F32)).astype(BF16)
    d = functools.partial(jnp.dot, preferred_element_type=F32)
    return d(m_bf16, hi) + d(m_bf16, mid) + d(m_bf16, lo)


def _inproj_kernel(x_ref, g_ref, w_ref, o_ref, xn_ref):
    @pl.when(pl.program_id(1) == 0)
    def _():
        x = x_ref[...]
        ms = jnp.mean(x * x, axis=-1, keepdims=True)
        xn_ref[...] = ((x * lax.rsqrt(ms + RMS_EPS)) * g_ref[...]).astype(BF16)

    o_ref[...] = jnp.dot(xn_ref[...], w_ref[...], preferred_element_type=F32)


def _in_projection(x_all, norm_g, w_in_p, tm, tn):
    m = x_all.shape[0]
    return pl.pallas_call(
        _inproj_kernel,
        grid=(m // tm, Z_COLS // tn),
        in_specs=[
            pl.BlockSpec((tm, D_MODEL), lambda i, j: (i, 0)),
            pl.BlockSpec((1, D_MODEL), lambda i, j: (0, 0)),
            pl.BlockSpec((D_MODEL, tn), lambda i, j: (0, j)),
        ],
        out_specs=pl.BlockSpec((tm, tn), lambda i, j: (i, j)),
        out_shape=jax.ShapeDtypeStruct((m, Z_COLS), F32),
        scratch_shapes=[pltpu.VMEM((tm, D_MODEL), BF16)],
        compiler_params=_cparams(("parallel", "arbitrary")),
        name="in_projection",
    )(x_all, norm_g, w_in_p)


def _bmm(a, b):
    return jnp.einsum("bmk,bkn->bmn", a.astype(BF16), b.astype(BF16), preferred_element_type=F32)


def _bmm_nt(a, b):
    return jnp.einsum("bmk,bnk->bmn", a.astype(BF16), b.astype(BF16), preferred_element_type=F32)


def _bmm_tn(a, b):
    return jnp.einsum("bkm,bkn->bmn", a.astype(BF16), b.astype(BF16), preferred_element_type=F32)


def _rwkv_kernel(zr_ref, zk_ref, zv_ref, zl_ref, zprev0_ref, s0_ref,
                 mu_ref, w0_ref, w2_ref, a0_ref, a2_ref, g2_ref, kk_ref, ka_ref, rk_ref,
                 lnw_ref, lnb_ref,
                 o_ref, sout_ref,
                 carry_ref, r_s, km_s, v_s, kkn_s, a_s, lw_s, cum_s, o_s,
                 *, tb, c, n_factors, per_chunk_state, n_pad, seqs_per_iter):
    i = pl.program_id(1)
    n_chunks = tb // c
    n_seq = n_chunks if per_chunk_state else 1
    rps = tb // n_seq

    @pl.when(i == 0)
    def _():
        carry_ref[...] = zprev0_ref[0]
        if not per_chunk_state:
            sout_ref[...] = s0_ref[...]

    if per_chunk_state:
        sout_ref[...] = s0_ref[...]

    row = lax.broadcasted_iota(jnp.int32, (tb, 1), 0)
    if n_pad:
        valid = ((row % c) >= n_pad).astype(F32)
    else:
        valid = None

    def shifted(z_ref, lo, hi):
        z = z_ref[...]
        prev = jnp.where(row == 0, carry_ref[:, lo:hi], pltpu.roll(z, 1, axis=0))
        zs = z + mu_ref[:, lo:hi] * (prev - z)
        carry_ref[:, lo:hi] = z[tb - 1:tb, :]
        return zs

    w = RW_WIDTH
    r = shifted(zr_ref, 0, w)
    k = shifted(zk_ref, w, 2 * w)
    v = shifted(zv_ref, 2 * w, 3 * w)
    zl = shifted(zl_ref, 3 * w, 3 * w + LORA_PAD)

    lo2 = zl[:, 0:128]
    wlog = -jax.nn.softplus(-(w0_ref[...] + _mm(jnp.tanh(lo2), w2_ref[...]))) - 0.5
    lw = -jnp.exp(wlog)
    a = jax.nn.sigmoid(a0_ref[...] + _mm(lo2, a2_ref[...]))
    g = _mm(jax.nn.sigmoid(zl[:, 128:384]), g2_ref[...])
    kk = k * kk_ref[...]
    km = k * (1.0 + (a - 1.0) * ka_ref[...])
    if valid is not None:
        r = r * valid
        km = km * valid
        v = v * valid
        kk = kk * valid
        lw = lw * valid

    ri = lax.broadcasted_iota(jnp.int32, (tb, tb), 0)
    ci = lax.broadcasted_iota(jnp.int32, (tb, tb), 1)
    tri = jnp.where((ri // c == ci // c) & (ri >= ci), 1.0, 0.0).astype(BF16)
    cum = _mm_exact_lhs(tri, lw)

    def scatter(dst, x, h):
        dst[:, h] = x[:, h * RW_HEAD:(h + 1) * RW_HEAD].reshape(n_seq, rps, RW_HEAD)

    for h in range(RW_HEADS):
        scatter(r_s, r, h)
        scatter(km_s, km, h)
        scatter(v_s, v, h)
        scatter(a_s, a, h)
        scatter(lw_s, lw, h)
        scatter(cum_s, cum, h)
        kkh = kk[:, h * RW_HEAD:(h + 1) * RW_HEAD]
        ss = jnp.sum(kkh * kkh, axis=-1, keepdims=True)
        kkn_s[:, h] = (kkh / jnp.maximum(jnp.sqrt(ss), 1e-12)).reshape(n_seq, rps, RW_HEAD)

    rr = lax.broadcasted_iota(jnp.int32, (c, c), 0)
    cc = lax.broadcasted_iota(jnp.int32, (c, c), 1)
    strict = (rr > cc)[None]
    incl = (rr >= cc)[None]
    eye = jnp.where(rr == cc, 1.0, 0.0).astype(F32)[None]
    rk = rk_ref[...]
    lnw = lnw_ref[...]
    lnb = lnb_ref[...]

    def chunk_math(rh, kmh, vh, ah, lwh, cumh, kkn, s_prev):
        bvec = kkn * ah
        cum_end = cumh[:, c - 1:c, :]
        p_inc = jnp.exp(cumh)
        p_exc = jnp.exp(cumh - lwh)
        p_inv = jnp.exp(-cumh)
        p_end = jnp.exp(cum_end)
        p_rel = jnp.exp(cum_end - cumh)
        a_t = -(kkn * p_exc)
        r_t = rh * p_inc
        b_t = bvec * p_inv
        k_t = kmh * p_inv
        b_h = bvec * p_rel
        k_h = kmh * p_rel

        ar = jnp.concatenate([a_t, r_t], axis=1)
        ab = _bmm_nt(ar, b_t)
        ak = _bmm_nt(ar, k_t)
        lmat = jnp.where(strict, ab[:, :c], 0.0)
        aak = jnp.where(strict, ak[:, :c], 0.0)
        arb = jnp.where(incl, ab[:, c:], 0.0)
        ark = jnp.where(incl, ak[:, c:], 0.0)
        tinv = eye + lmat
        lp = lmat
        for _ in range(n_factors - 1):
            lp = _bmm(lp, lp)
            tinv = tinv + _bmm(tinv, lp)
        as2 = _bmm_nt(ar, s_prev)
        x = as2[:, :c] + _bmm(aak, vh)
        u = _bmm(tinv, x)
        o = as2[:, c:] + _bmm(arb, u) + _bmm(ark, vh)
        s_new = s_prev * p_end + _bmm_tn(jnp.concatenate([u, vh], axis=1),
                                         jnp.concatenate([b_h, k_h], axis=1))
        mean = jnp.mean(o, axis=-1, keepdims=True)
        var = jnp.mean(jnp.square(o - mean), axis=-1, keepdims=True)
        on = ((o - mean) * lax.rsqrt(var + RW_GN_EPS)) * lnw + lnb
        bonus = jnp.sum(rh * kmh * rk, axis=-1, keepdims=True) * vh
        return on + bonus, s_new

    srcs = (r_s, km_s, v_s, a_s, lw_s, cum_s, kkn_s)
    if per_chunk_state:
        spi = seqs_per_iter
        nb = spi * RW_HEADS

        def body(it, carry):
            sq = pl.ds(pl.multiple_of(it * spi, spi), spi)
            args = [s[sq].reshape(nb, c, RW_HEAD) for s in srcs]
            s_prev = sout_ref[sq].reshape(nb, RW_HEAD, RW_HEAD)
            o, s_new = chunk_math(*args, s_prev)
            sout_ref[sq] = s_new.reshape(spi, RW_HEADS, RW_HEAD, RW_HEAD)
            o_s[sq] = o.reshape(spi, RW_HEADS, c, RW_HEAD)
            return carry

        lax.fori_loop(0, n_seq // spi, body, 0)
    else:
        def body(ch, carry):
            rows = pl.ds(pl.multiple_of(ch * c, c), c)
            args = [s[0, :, rows, :] for s in srcs]
            o, s_new = chunk_math(*args, sout_ref[0])
            sout_ref[0] = s_new
            o_s[0, :, rows, :] = o
            return carry

        lax.fori_loop(0, n_chunks, body, 0)

    o_full = jnp.concatenate([o_s[:, h].reshape(tb, RW_HEAD) for h in range(RW_HEADS)], axis=-1)
    o_ref[...] = (o_full * g).astype(BF16)


def _rwkv_mixer(z, row_block0, n_groups, n_tiles, zprev0, s0, p, *, tb, c, n_real, per_chunk_state,
                seqs_per_iter=1):
    n_states = s0.shape[0]
    sb = n_states // n_groups
    n_factors = max(1, math.ceil(math.log2(n_real)))
    kern = functools.partial(_rwkv_kernel, tb=tb, c=c, n_factors=n_factors,
                             per_chunk_state=per_chunk_state, n_pad=c - n_real if per_chunk_state else 0,
                             seqs_per_iter=seqs_per_iter)
    cb = RW_OFF // RW_WIDTH
    zspec = lambda col: pl.BlockSpec((tb, RW_WIDTH), lambda gi, i: (row_block0 + gi * n_tiles + i, col))
    full = lambda arr: pl.BlockSpec(arr.shape, lambda gi, i: (0,) * arr.ndim)
    lora_cb = (RW_OFF + 3 * RW_WIDTH) // LORA_PAD
    per_head = lambda a: jnp.tile(a.reshape(RW_HEADS, 1, RW_HEAD), (seqs_per_iter, 1, 1))
    params = [p["mu"], p["w0"], p["w2"], p["a0"], p["a2"], p["g2"], p["kk"], p["ka"],
              per_head(p["rk"]), per_head(p["lnw"]), per_head(p["lnb"])]
    n_seq = tb // c if per_chunk_state else 1
    hm = lambda: pltpu.VMEM((n_seq, RW_HEADS, tb // n_seq, RW_HEAD), F32)
    n_rows_out = n_groups * n_tiles * tb
    return pl.pallas_call(
        kern,
        grid=(n_groups, n_tiles),
        in_specs=[zspec(cb), zspec(cb + 1), zspec(cb + 2),
                  pl.BlockSpec((tb, LORA_PAD), lambda gi, i: (row_block0 + gi * n_tiles + i, lora_cb)),
                  pl.BlockSpec((1, 1, SHIFT_PAD), lambda gi, i: (gi, 0, 0)),
                  pl.BlockSpec((sb, RW_HEADS, RW_HEAD, RW_HEAD), lambda gi, i: (gi, 0, 0, 0))]
                 + [full(a) for a in params],
        out_specs=[pl.BlockSpec((tb, RW_WIDTH), lambda gi, i: (gi * n_tiles + i, 0)),
                   pl.BlockSpec((sb, RW_HEADS, RW_HEAD, RW_HEAD), lambda gi, i: (gi, 0, 0, 0))],
        out_shape=[jax.ShapeDtypeStruct((n_rows_out, RW_WIDTH), BF16),
                   jax.ShapeDtypeStruct((n_states, RW_HEADS, RW_HEAD, RW_HEAD), F32)],
        scratch_shapes=[pltpu.VMEM((1, SHIFT_PAD), F32)] + [hm() for _ in range(8)],
        compiler_params=_cparams(("parallel", "arbitrary")),
        name="rwkv7_mixer",
    )(z, z, z, z, zprev0, s0, *params)


def _retention_kernel(zq_ref, zk_ref, zv_ref, zg_ref, cos_ref, sin_ref, dmask_ref, iscale_ref,
                      kscale_ref, sdec_ref, s0_ref, o_ref, sout_ref,
                      *, rows, c, per_chunk_state):
    i = pl.program_id(1)
    n_chunks = rows // c

    if per_chunk_state:
        sout_ref[...] = s0_ref[...]
    else:
        @pl.when(i == 0)
        def _():
            sout_ref[...] = s0_ref[...]

    lane = lax.broadcasted_iota(jnp.int32, (rows, RET_WIDTH), 1)
    even = (lane % 2) == 0
    cos = jnp.concatenate([cos_ref[...]] * RET_HEADS, axis=-1)
    sin = jnp.concatenate([sin_ref[...]] * RET_HEADS, axis=-1)

    def rot(x):
        partner = jnp.where(even, pltpu.roll(x, RET_WIDTH - 1, axis=1), pltpu.roll(x, 1, axis=1))
        return x * cos + partner * sin

    q = rot(zq_ref[...])
    k = rot(zk_ref[...]) * (RET_HEAD ** -0.5)
    v = zv_ref[...]
    g = zg_ref[...]

    out_rows = []
    for ch in range(n_chunks):
        rs = slice(ch * c, (ch + 1) * c)
        sidx = ch if per_chunk_state else 0
        out_heads = []
        for h in range(RET_HEADS):
            hs = slice(h * RET_HEAD, (h + 1) * RET_HEAD)
            qh, kh, vh = q[rs, hs], k[rs, hs], v[rs, hs]
            s_prev = sout_ref[sidx, h]
            scores = _mm_nt(qh, kh) * dmask_ref[h]
            o = _mm(scores, vh) + _mm(qh, s_prev) * iscale_ref[h]
            sout_ref[sidx, h] = s_prev * sdec_ref[h] + _mm_tn(kh * kscale_ref[h], vh)
            o = o * lax.rsqrt(jnp.mean(o * o, axis=-1, keepdims=True) + RET_GN_EPS)
            gh = g[rs, hs]
            out_heads.append(o * (gh * jax.nn.sigmoid(gh)))
        out_rows.append(jnp.concatenate(out_heads, axis=-1))
    o_ref[...] = jnp.concatenate(out_rows, axis=0).astype(BF16)


def _retention_mixer(z, row_block0, n_groups, n_tiles, cos, sin, tabs, s0, *, rows, c, per_chunk_state,
                     pos_per_tile):
    n_states = s0.shape[0]
    sb = n_states // n_groups
    kern = functools.partial(_retention_kernel, rows=rows, c=c, per_chunk_state=per_chunk_state)
    zspec = lambda col: pl.BlockSpec((rows, RET_WIDTH), lambda gi, i: (row_block0 + gi * n_tiles + i, col))
    full = lambda arr: pl.BlockSpec(arr.shape, lambda gi, i: (0,) * arr.ndim)
    if pos_per_tile:
        tspec = pl.BlockSpec((rows, RET_HEAD), lambda gi, i: (i, 0))
    else:
        tspec = pl.BlockSpec((rows, RET_HEAD), lambda gi, i: (0, 0))
    sspec = pl.BlockSpec((sb, RET_HEADS, RET_HEAD, RET_HEAD), lambda gi, i: (gi, 0, 0, 0))
    dmask, iscale, kscale, sdec = tabs
    return pl.pallas_call(
        kern,
        grid=(n_groups, n_tiles),
        in_specs=[zspec(0), zspec(1), zspec(2), zspec(3), tspec, tspec,
                  full(dmask), full(iscale), full(kscale), full(sdec), sspec],
        out_specs=[pl.BlockSpec((rows, RET_WIDTH), lambda gi, i: (gi * n_tiles + i, 0)), sspec],
        out_shape=[jax.ShapeDtypeStruct((n_groups * n_tiles * rows, RET_WIDTH), BF16),
                   jax.ShapeDtypeStruct(s0.shape, F32)],
        compiler_params=_cparams(("parallel", "arbitrary")),
        name="retention_mixer",
    )(z, z, z, z, cos, sin, dmask, iscale, kscale, sdec, s0)


def _retention_tables(c, n_real):
    log_gamma = jnp.log(1.0 - 2.0 ** (-5.0 - jnp.arange(RET_HEADS, dtype=F32)))
    r = jnp.arange(c, dtype=F32)
    idx = r - float(c - n_real)
    diff = r[:, None] - r[None, :]
    dmask = jnp.where(diff[None] >= 0,
                      jnp.exp(log_gamma[:, None, None] * jnp.maximum(diff, 0.0)[None]), 0.0)
    iscale = jnp.exp(log_gamma[:, None] * (idx + 1.0)[None, :])[:, :, None]
    kscale = jnp.exp(log_gamma[:, None] * (n_real - 1.0 - idx)[None, :])[:, :, None]
    sdec = jnp.broadcast_to(jnp.exp(log_gamma * n_real)[:, None, None], (RET_HEADS, 1, RET_HEAD))
    return dmask, iscale, kscale, sdec


def _rotary_tables(pos):
    inv_freq = 1.0 / (ROPE_BASE ** jnp.linspace(0.0, 1.0, RET_HEAD // 2, dtype=F32))
    ang = pos.astype(F32)[:, None] * inv_freq[None, :]
    cos = jnp.cos(ang)
    sin = jnp.sin(ang)
    cos2 = jnp.repeat(cos, 2, axis=-1)
    sin2 = jnp.stack([-sin, sin], axis=-1).reshape(pos.shape[0], RET_HEAD)
    return cos2, sin2


def _outproj_kernel(oa_ref, ob_ref, w_ref, x_ref, g_ref, h_ref, hn_ref):
    acc = jnp.dot(oa_ref[...], w_ref[0:RW_WIDTH, :], preferred_element_type=F32)
    acc = acc + jnp.dot(ob_ref[...], w_ref[RW_WIDTH:, :], preferred_element_type=F32)
    h = x_ref[...] + acc
    h_ref[...] = h
    ms = jnp.mean(h * h, axis=-1, keepdims=True)
    hn_ref[...] = ((h * lax.rsqrt(ms + RMS_EPS)) * g_ref[...]).astype(BF16)


def _out_projection(o_a, o_b, w_out, x_all, norm_g, tm):
    m = o_a.shape[0]
    return pl.pallas_call(
        _outproj_kernel,
        grid=(m // tm,),
        in_specs=[
            pl.BlockSpec((tm, RW_WIDTH), lambda i: (i, 0)),
            pl.BlockSpec((tm, RET_WIDTH), lambda i: (i, 0)),
            pl.BlockSpec((D_MODEL, D_MODEL), lambda i: (0, 0)),
            pl.BlockSpec((tm, D_MODEL), lambda i: (i, 0)),
            pl.BlockSpec((1, D_MODEL), lambda i: (0, 0)),
        ],
        out_specs=[pl.BlockSpec((tm, D_MODEL), lambda i: (i, 0)),
                   pl.BlockSpec((tm, D_MODEL), lambda i: (i, 0))],
        out_shape=[jax.ShapeDtypeStruct((m, D_MODEL), F32), jax.ShapeDtypeStruct((m, D_MODEL), BF16)],
        compiler_params=_cparams(("parallel",)),
        name="out_projection",
    )(o_a, o_b, w_out, x_all, norm_g)


def _ffn_kernel(hn_ref, wg_ref, wu_ref, wd_ref, h_ref, g_ref, y_ref, acc_ref):
    f = pl.program_id(1)

    @pl.when(f == 0)
    def _():
        acc_ref[...] = jnp.zeros_like(acc_ref)

    hn = hn_ref[...]
    gate = jnp.dot(hn, wg_ref[...], preferred_element_type=F32)
    up = jnp.dot(hn, wu_ref[...], preferred_element_type=F32)
    act = (gate * jax.nn.sigmoid(gate)) * up
    acc_ref[...] += jnp.dot(act.astype(BF16), wd_ref[...], preferred_element_type=F32)

    @pl.when(f == pl.num_programs(1) - 1)
    def _():
        h = h_ref[...] + acc_ref[...]
        ms = jnp.mean(h * h, axis=-1, keepdims=True)
        y_ref[...] = (h * lax.rsqrt(ms + RMS_EPS)) * g_ref[...]


def _ffn(hn, w_gate, w_up, w_down, h, norm_g, tm, tf):
    m = hn.shape[0]
    return pl.pallas_call(
        _ffn_kernel,
        grid=(m // tm, D_FF // tf),
        in_specs=[
            pl.BlockSpec((tm, D_MODEL), lambda i, f: (i, 0)),
            pl.BlockSpec((D_MODEL, tf), lambda i, f: (0, f)),
            pl.BlockSpec((D_MODEL, tf), lambda i, f: (0, f)),
            pl.BlockSpec((tf, D_MODEL), lambda i, f: (f, 0)),
            pl.BlockSpec((tm, D_MODEL), lambda i, f: (i, 0)),
            pl.BlockSpec((1, D_MODEL), lambda i, f: (0, 0)),
        ],
        out_specs=pl.BlockSpec((tm, D_MODEL), lambda i, f: (i, 0)),
        out_shape=jax.ShapeDtypeStruct((m, D_MODEL), F32),
        scratch_shapes=[pltpu.VMEM((tm, D_MODEL), F32)],
        compiler_params=_cparams(("parallel", "arbitrary")),
        name="swiglu_ffn",
    )(hn, w_gate, w_up, w_down, h, norm_g)


def _pad_cols(a, n):
    return jnp.pad(a, ((0, 0), (0, n - a.shape[1])))


def kernel(x_prompt, x_sample, state_shift, state_rwkv, state_ret, meta_tokens, norm_mix, w_in,
           rwkv_mu, rwkv_w0, rwkv_w2, rwkv_a0, rwkv_a2, rwkv_g2, rwkv_kk, rwkv_ka, rwkv_rk,
           rwkv_ln_w, rwkv_ln_b, w_out, norm_ffn, w_gate, w_up, w_down, norm_final):
    n_b, seq, d = x_prompt.shape
    n_s, dec_seq, _ = x_sample.shape
    n_p = n_b * seq
    n_d = n_s * dec_seq
    depth = w_in.shape[0]
    assert depth == 1 and d == D_MODEL

    wi = w_in[0]
    w_in_p = jnp.concatenate(
        [wi[:, RW_COLS:], wi[:, :RW_COLS], jnp.zeros((d, Z_COLS - wi.shape[1]), F32)], axis=1).astype(BF16)
    w_out_b = w_out[0].astype(BF16)
    w_gate_b = w_gate[0].astype(BF16)
    w_up_b = w_up[0].astype(BF16)
    w_down_b = w_down[0].astype(BF16)
    row = lambda a: a.reshape(1, -1).astype(F32)
    w2p = jnp.concatenate([rwkv_w2[0], jnp.zeros((128 - DECAY_LORA, RW_WIDTH), F32)], axis=0)
    a2p = jnp.concatenate([jnp.zeros((DECAY_LORA, RW_WIDTH), F32), rwkv_a2[0]], axis=0)
    g2p = jnp.concatenate([rwkv_g2[0], jnp.zeros((256 - GATE_LORA, RW_WIDTH), F32)], axis=0)
    rw_params = dict(mu=_pad_cols(row(rwkv_mu[0]), SHIFT_PAD), w0=row(rwkv_w0[0]), w2=w2p,
                     a0=row(rwkv_a0[0]), a2=a2p, g2=g2p, kk=row(rwkv_kk[0]), ka=row(rwkv_ka[0]),
                     rk=row(rwkv_rk[0]), lnw=row(rwkv_ln_w[0]), lnb=row(rwkv_ln_b[0]))

    x_all = jnp.concatenate([x_prompt.reshape(n_p, d), x_sample.reshape(n_d, d),
                             meta_tokens.astype(F32)], axis=0)
    n_all = x_all.shape[0]
    z = _in_projection(x_all, row(norm_mix[0]), w_in_p, tm=n_all // 10, tn=768)

    z_meta = jnp.pad(z[n_p + n_d:], ((RET_CHUNK - N_META, 0), (0, 0)))
    srow = 2 * dec_seq
    shift_row = jnp.pad(state_shift[0], ((0, 0), (RW_OFF, Z_COLS - RW_OFF - RW_COLS)))
    z_smp = jnp.concatenate(
        [jnp.zeros((n_s, srow - dec_seq - 1, Z_COLS), F32), shift_row[:, None, :],
         z[n_p:n_p + n_d].reshape(n_s, dec_seq, Z_COLS)], axis=1).reshape(n_s * srow, Z_COLS)

    zero_prev = jnp.zeros((1, 1, SHIFT_PAD), F32)
    zero_rw = jnp.zeros((1, RW_HEADS, RW_HEAD, RW_HEAD), F32)
    c_rw = 64
    _, s_rw_meta = _rwkv_mixer(z_meta, 1, 1, 1, zero_prev, zero_rw, rw_params,
                               tb=c_rw, c=c_rw, n_real=c_rw, per_chunk_state=False)
    zprev_p = jnp.broadcast_to(z_meta[-1:, RW_OFF:RW_OFF + SHIFT_PAD][None], (n_b, 1, SHIFT_PAD))
    tb_p = 256
    oa_p, rwkv_p = _rwkv_mixer(z, 0, n_b, seq // tb_p, zprev_p,
                               jnp.broadcast_to(s_rw_meta, (n_b,) + s_rw_meta.shape[1:]), rw_params,
                               tb=tb_p, c=c_rw, n_real=c_rw, per_chunk_state=False)
    sb_rw = 8
    oa_s, rwkv_s = _rwkv_mixer(z_smp, 0, n_s // sb_rw, 1, jnp.zeros((n_s // sb_rw, 1, SHIFT_PAD), F32),
                               state_rwkv[0], rw_params,
                               tb=sb_rw * srow, c=srow, n_real=dec_seq, per_chunk_state=True,
                               seqs_per_iter=2)

    past_len = 16384
    tabs_full = _retention_tables(RET_CHUNK, RET_CHUNK)
    tabs_meta = _retention_tables(RET_CHUNK, N_META)
    tabs_smp = _retention_tables(srow, dec_seq)
    cos_m, sin_m = _rotary_tables(jnp.arange(RET_CHUNK) - (RET_CHUNK - N_META))
    cos_p, sin_p = _rotary_tables(N_META + jnp.arange(seq))
    sb_rt = 4
    pos_s = past_len + jnp.tile(jnp.arange(srow) - (srow - dec_seq), sb_rt)
    cos_s, sin_s = _rotary_tables(pos_s)
    zero_rt = jnp.zeros((1, RET_HEADS, RET_HEAD, RET_HEAD), F32)
    _, s_rt_meta = _retention_mixer(z_meta, 0, 1, 1, cos_m, sin_m, tabs_meta, zero_rt,
                                    rows=RET_CHUNK, c=RET_CHUNK, per_chunk_state=False, pos_per_tile=False)
    ob_p, ret_p = _retention_mixer(z, 0, n_b, seq // RET_CHUNK, cos_p, sin_p, tabs_full,
                                   jnp.broadcast_to(s_rt_meta, (n_b,) + s_rt_meta.shape[1:]),
                                   rows=RET_CHUNK, c=RET_CHUNK, per_chunk_state=False, pos_per_tile=True)
    ob_s, ret_s = _retention_mixer(z_smp, 0, n_s // sb_rt, 1, cos_s, sin_s, tabs_smp, state_ret[0],
                                   rows=sb_rt * srow, c=srow, per_chunk_state=True, pos_per_tile=False)

    take = lambda o: o.reshape(n_s, srow, -1)[:, srow - dec_seq:].reshape(n_d, -1)
    o_a = jnp.concatenate([oa_p, take(oa_s)], axis=0)
    o_b = jnp.concatenate([ob_p, take(ob_s)], axis=0)
    h, hn = _out_projection(o_a, o_b, w_out_b, x_all, row(norm_ffn[0]), tm=512)
    y = _ffn(hn, w_gate_b, w_up_b, w_down_b, h, row(norm_final), tm=512, tf=512)

    y_prompt = y[:n_p].reshape(n_b, seq, d)
    y_sample = y[n_p:].reshape(n_s, dec_seq, d)
    za = z[:, RW_OFF:RW_OFF + RW_COLS]
    shift_p = za[:n_p].reshape(n_b, seq, RW_COLS)[:, -1][None]
    shift_s = za[n_p:n_p + n_d].reshape(n_s, dec_seq, RW_COLS)[:, -1][None]
    return (y_prompt, y_sample, shift_p, rwkv_p[None], ret_p[None], shift_s, rwkv_s[None], ret_s[None])
```

```python
import functools
import math

import jax
import jax.numpy as jnp
from jax import lax
from jax.experimental import pallas as pl
from jax.experimental.pallas import tpu as pltpu

F32 = jnp.float32
BF16 = jnp.bfloat16

D_MODEL = 2048
N_META = 16
RW_WIDTH = 1024
RW_HEAD = 64
RW_HEADS = 16
DECAY_LORA = 64
AAA_LORA = 64
GATE_LORA = 160
RW_COLS = 3 * RW_WIDTH + DECAY_LORA + AAA_LORA + GATE_LORA
RET_WIDTH = 1024
RET_HEADS = 4
RET_HEAD = 256
RET_CHUNK = 128
D_FF = 5632
RMS_EPS = 1e-6
RW_GN_EPS = 64e-5
RET_GN_EPS = 1e-6
ROPE_BASE = 10000.0

LORA_PAD = 512
Z_COLS = 4 * RET_WIDTH + 3 * RW_WIDTH + LORA_PAD
RW_OFF = 4 * RET_WIDTH
SHIFT_PAD = 3 * RW_WIDTH + LORA_PAD

VMEM_LIMIT = 56 * 1024 * 1024


def _cparams(sem):
    return pltpu.CompilerParams(dimension_semantics=sem, vmem_limit_bytes=VMEM_LIMIT)


def _mm(a, b):
    return jnp.dot(a.astype(BF16), b.astype(BF16), preferred_element_type=F32)


def _mm_nt(a, b):
    return lax.dot_general(a.astype(BF16), b.astype(BF16), (((1,), (1,)), ((), ())),
                           preferred_element_type=F32)


def _mm_tn(a, b):
    return lax.dot_general(a.astype(BF16), b.astype(BF16), (((0,), (0,)), ((), ())),
                           preferred_element_type=F32)


def _mm_exact_lhs(m_bf16, x):
    hi = x.astype(BF16)
    r1 = x - hi.astype(F32)
    mid = r1.astype(BF16)
    lo = (r1 - mid.astype(F32)).astype(BF16)
    d = functools.partial(jnp.dot, preferred_element_type=F32)
    return d(m_bf16, hi) + d(m_bf16, mid) + d(m_bf16, lo)


def _w_in_layout_kernel(w_ref, o_ref):
    w = w_ref[...]
    tk = w.shape[0]
    o_ref[...] = jnp.concatenate(
        [w[:, RW_COLS:], w[:, :RW_COLS], jnp.zeros((tk, Z_COLS - w.shape[1]), F32)], axis=1).astype(BF16)


def _w_in_layout(w_in2d, tk):
    d, n = w_in2d.shape
    return pl.pallas_call(
        _w_in_layout_kernel,
        grid=(d // tk,),
        in_specs=[pl.BlockSpec((tk, n), lambda i: (i, 0))],
        out_specs=pl.BlockSpec((tk, Z_COLS), lambda i: (i, 0)),
        out_shape=jax.ShapeDtypeStruct((d, Z_COLS), BF16),
        compiler_params=_cparams(("parallel",)),
        name="w_in_layout",
    )(w_in2d)


def _inproj_kernel(x_ref, g_ref, w_ref, o_ref, xn_ref):
    @pl.when(pl.program_id(1) == 0)
    def _():
        x = x_ref[...]
        ms = jnp.mean(x * x, axis=-1, keepdims=True)
        xn_ref[...] = ((x * lax.rsqrt(ms + RMS_EPS)) * g_ref[...]).astype(BF16)

    o_ref[...] = jnp.dot(xn_ref[...], w_ref[...], preferred_element_type=F32)


def _in_projection(x_all, norm_g, w_in_p, tm, tn):
    m = x_all.shape[0]
    return pl.pallas_call(
        _inproj_kernel,
        grid=(m // tm, Z_COLS // tn),
        in_specs=[
            pl.BlockSpec((tm, D_MODEL), lambda i, j: (i, 0)),
            pl.BlockSpec((1, D_MODEL), lambda i, j: (0, 0)),
            pl.BlockSpec((D_MODEL, tn), lambda i, j: (0, j)),
        ],
        out_specs=pl.BlockSpec((tm, tn), lambda i, j: (i, j)),
        out_shape=jax.ShapeDtypeStruct((m, Z_COLS), F32),
        scratch_shapes=[pltpu.VMEM((tm, D_MODEL), BF16)],
        compiler_params=_cparams(("parallel", "arbitrary")),
        name="in_projection",
    )(x_all, norm_g, w_in_p)


def _bmm(a, b):
    return jnp.einsum("bmk,bkn->bmn", a.astype(BF16), b.astype(BF16), preferred_element_type=F32)


def _bmm_nt(a, b):
    return jnp.einsum("bmk,bnk->bmn", a.astype(BF16), b.astype(BF16), preferred_element_type=F32)


def _bmm_tn(a, b):
    return jnp.einsum("bkm,bkn->bmn", a.astype(BF16), b.astype(BF16), preferred_element_type=F32)


def _rwkv_kernel(zr_ref, zk_ref, zv_ref, zl_ref, zprev0_ref, shift_ref, s0_ref,
                 mu_ref, w0_ref, w2_ref, a0_ref, a2_ref, g2_ref, kk_ref, ka_ref, rk_ref,
                 lnw_ref, lnb_ref,
                 o_ref, sout_ref,
                 carry_ref, r_s, km_s, v_s, kkn_s, a_s, lw_s, cum_s, o_s,
                 *, tb, c, n_factors, per_chunk_state, n_pad, seqs_per_iter):
    i = pl.program_id(1)
    n_chunks = tb // c
    n_seq = n_chunks if per_chunk_state else 1
    rps = tb // n_seq

    @pl.when(i == 0)
    def _():
        carry_ref[...] = zprev0_ref[0]
        if not per_chunk_state:
            sout_ref[...] = s0_ref[...]

    if per_chunk_state:
        sout_ref[...] = s0_ref[...]

    row = lax.broadcasted_iota(jnp.int32, (tb, 1), 0)
    if n_pad:
        valid = ((row % c) >= n_pad).astype(F32)
    else:
        valid = None

    if per_chunk_state:
        n_real = c - n_pad
        pr = lax.broadcasted_iota(jnp.int32, (tb, n_seq * n_real), 0)
        pc = lax.broadcasted_iota(jnp.int32, (tb, n_seq * n_real), 1)
        place = jnp.where((pr // c == pc // n_real) & (pr % c - n_pad == pc % n_real), 1.0, 0.0).astype(BF16)
        qr = lax.broadcasted_iota(jnp.int32, (tb, n_seq), 0)
        qc = lax.broadcasted_iota(jnp.int32, (tb, n_seq), 1)
        place_shift = jnp.where(qr == qc * c + (n_pad - 1), 1.0, 0.0).astype(BF16)

    def shifted(z_ref, lo, hi):
        z = z_ref[...]
        if per_chunk_state:
            z = _mm_exact_lhs(place, z) + _mm_exact_lhs(place_shift, shift_ref[:, lo:hi])
        prev = jnp.where(row == 0, carry_ref[:, lo:hi], pltpu.roll(z, 1, axis=0))
        zs = z + mu_ref[:, lo:hi] * (prev - z)
        carry_ref[:, lo:hi] = z[tb - 1:tb, :]
        return zs

    w = RW_WIDTH
    r = shifted(zr_ref, 0, w)
    k = shifted(zk_ref, w, 2 * w)
    v = shifted(zv_ref, 2 * w, 3 * w)
    zl = shifted(zl_ref, 3 * w, 3 * w + LORA_PAD)

    lo2 = zl[:, 0:128]
    wlog = -jax.nn.softplus(-(w0_ref[...] + _mm(jnp.tanh(lo2), w2_ref[...]))) - 0.5
    lw = -jnp.exp(wlog)
    a = jax.nn.sigmoid(a0_ref[...] + _mm(lo2, a2_ref[...]))
    g = _mm(jax.nn.sigmoid(zl[:, 128:384]), g2_ref[...])
    kk = k * kk_ref[...]
    km = k * (1.0 + (a - 1.0) * ka_ref[...])
    if valid is not None:
        r = r * valid
        km = km * valid
        v = v * valid
        kk = kk * valid
        lw = lw * valid

    ri = lax.broadcasted_iota(jnp.int32, (tb, tb), 0)
    ci = lax.broadcasted_iota(jnp.int32, (tb, tb), 1)
    tri = jnp.where((ri // c == ci // c) & (ri >= ci), 1.0, 0.0).astype(BF16)
    cum = _mm_exact_lhs(tri, lw)

    def scatter(dst, x, h):
        dst[:, h] = x[:, h * RW_HEAD:(h + 1) * RW_HEAD].reshape(n_seq, rps, RW_HEAD)

    for h in range(RW_HEADS):
        scatter(r_s, r, h)
        scatter(km_s, km, h)
        scatter(v_s, v, h)
        scatter(a_s, a, h)
        scatter(lw_s, lw, h)
        scatter(cum_s, cum, h)
        kkh = kk[:, h * RW_HEAD:(h + 1) * RW_HEAD]
        ss = jnp.sum(kkh * kkh, axis=-1, keepdims=True)
        kkn_s[:, h] = (kkh / jnp.maximum(jnp.sqrt(ss), 1e-12)).reshape(n_seq, rps, RW_HEAD)

    rr = lax.broadcasted_iota(jnp.int32, (c, c), 0)
    cc = lax.broadcasted_iota(jnp.int32, (c, c), 1)
    strict = (rr > cc)[None]
    incl = (rr >= cc)[None]
    eye = jnp.where(rr == cc, 1.0, 0.0).astype(F32)[None]
    rk = rk_ref[...]
    lnw = lnw_ref[...]
    lnb = lnb_ref[...]

    def chunk_math(rh, kmh, vh, ah, lwh, cumh, kkn, s_prev):
        bvec = kkn * ah
        cum_end = cumh[:, c - 1:c, :]
        p_inc = jnp.exp(cumh)
        p_exc = jnp.exp(cumh - lwh)
        p_inv = jnp.exp(-cumh)
        p_end = jnp.exp(cum_end)
        p_rel = jnp.exp(cum_end - cumh)
        a_t = -(kkn * p_exc)
        r_t = rh * p_inc
        b_t = bvec * p_inv
        k_t = kmh * p_inv
        b_h = bvec * p_rel
        k_h = kmh * p_rel

        ar = jnp.concatenate([a_t, r_t], axis=1)
        ab = _bmm_nt(ar, b_t)
        ak = _bmm_nt(ar, k_t)
        lmat = jnp.where(strict, ab[:, :c], 0.0)
        aak = jnp.where(strict, ak[:, :c], 0.0)
        arb = jnp.where(incl, ab[:, c:], 0.0)
        ark = jnp.where(incl, ak[:, c:], 0.0)
        tinv = eye + lmat
        lp = lmat
        for _ in range(n_factors - 1):
            lp = _bmm(lp, lp)
            tinv = tinv + _bmm(tinv, lp)
        as2 = _bmm_nt(ar, s_prev)
        x = as2[:, :c] + _bmm(aak, vh)
        u = _bmm(tinv, x)
        o = as2[:, c:] + _bmm(arb, u) + _bmm(ark, vh)
        s_new = s_prev * p_end + _bmm_tn(jnp.concatenate([u, vh], axis=1),
                                         jnp.concatenate([b_h, k_h], axis=1))
        mean = jnp.mean(o, axis=-1, keepdims=True)
        var = jnp.mean(jnp.square(o - mean), axis=-1, keepdims=True)
        on = ((o - mean) * lax.rsqrt(var + RW_GN_EPS)) * lnw + lnb
        bonus = jnp.sum(rh * kmh * rk, axis=-1, keepdims=True) * vh
        return on + bonus, s_new

    srcs = (r_s, km_s, v_s, a_s, lw_s, cum_s, kkn_s)
    if per_chunk_state:
        spi = seqs_per_iter
        nb = spi * RW_HEADS

        def body(it, carry):
            sq = pl.ds(pl.multiple_of(it * spi, spi), spi)
            args = [s[sq].reshape(nb, c, RW_HEAD) for s in srcs]
            s_prev = sout_ref[sq].reshape(nb, RW_HEAD, RW_HEAD)
            o, s_new = chunk_math(*args, s_prev)
            sout_ref[sq] = s_new.reshape(spi, RW_HEADS, RW_HEAD, RW_HEAD)
            o_s[sq] = o.reshape(spi, RW_HEADS, c, RW_HEAD)
            return carry

        lax.fori_loop(0, n_seq // spi, body, 0)
    else:
        def body(ch, carry):
            rows = pl.ds(pl.multiple_of(ch * c, c), c)
            args = [s[0, :, rows, :] for s in srcs]
            o, s_new = chunk_math(*args, sout_ref[0])
            sout_ref[0] = s_new
            o_s[0, :, rows, :] = o
            return carry

        lax.fori_loop(0, n_chunks, body, 0)

    o_full = jnp.concatenate([o_s[:, h].reshape(tb, RW_HEAD) for h in range(RW_HEADS)], axis=-1)
    o_out = (o_full * g).astype(BF16)
    if per_chunk_state:
        o_out = lax.dot_general(place, o_out, (((0,), (0,)), ((), ())),
                                preferred_element_type=F32).astype(BF16)
    o_ref[...] = o_out


def _rwkv_mixer(z, row_block0, n_groups, n_tiles, zprev0, s0, p, *, tb, c, n_real, per_chunk_state,
                seqs_per_iter=1, shift=None):
    n_states = s0.shape[0]
    sb = n_states // n_groups
    n_factors = max(1, math.ceil(math.log2(n_real)))
    kern = functools.partial(_rwkv_kernel, tb=tb, c=c, n_factors=n_factors,
                             per_chunk_state=per_chunk_state, n_pad=c - n_real if per_chunk_state else 0,
                             seqs_per_iter=seqs_per_iter)
    cb = RW_OFF // RW_WIDTH
    full = lambda arr: pl.BlockSpec(arr.shape, lambda gi, i: (0,) * arr.ndim)
    lora_cb = (RW_OFF + 3 * RW_WIDTH) // LORA_PAD
    per_head = lambda a: jnp.tile(a.reshape(RW_HEADS, 1, RW_HEAD), (seqs_per_iter, 1, 1))
    params = [p["mu"], p["w0"], p["w2"], p["a0"], p["a2"], p["g2"], p["kk"], p["ka"],
              per_head(p["rk"]), per_head(p["lnw"]), per_head(p["lnb"])]
    n_seq = tb // c if per_chunk_state else 1
    hm = lambda: pltpu.VMEM((n_seq, RW_HEADS, tb // n_seq, RW_HEAD), F32)
    rows_io = n_seq * n_real if per_chunk_state else tb
    zspec = lambda width, col: pl.BlockSpec((rows_io, width),
                                            lambda gi, i: (row_block0 + gi * n_tiles + i, col))
    if shift is None:
        shift = jnp.zeros((8, SHIFT_PAD), F32)
        shift_spec = pl.BlockSpec((8, SHIFT_PAD), lambda gi, i: (0, 0))
    else:
        shift_spec = pl.BlockSpec((n_seq, SHIFT_PAD), lambda gi, i: (gi, 0))
    return pl.pallas_call(
        kern,
        grid=(n_groups, n_tiles),
        in_specs=[zspec(RW_WIDTH, cb), zspec(RW_WIDTH, cb + 1), zspec(RW_WIDTH, cb + 2),
                  zspec(LORA_PAD, lora_cb),
                  pl.BlockSpec((1, 1, SHIFT_PAD),
                               (lambda gi, i: (gi, 0, 0)) if zprev0.shape[0] > 1 else (lambda gi, i: (0, 0, 0))),
                  shift_spec,
                  pl.BlockSpec((sb, RW_HEADS, RW_HEAD, RW_HEAD), lambda gi, i: (gi, 0, 0, 0))]
                 + [full(a) for a in params],
        out_specs=[pl.BlockSpec((rows_io, RW_WIDTH), lambda gi, i: (gi * n_tiles + i, 0)),
                   pl.BlockSpec((sb, RW_HEADS, RW_HEAD, RW_HEAD), lambda gi, i: (gi, 0, 0, 0))],
        out_shape=[jax.ShapeDtypeStruct((n_groups * n_tiles * rows_io, RW_WIDTH), BF16),
                   jax.ShapeDtypeStruct((n_states, RW_HEADS, RW_HEAD, RW_HEAD), F32)],
        scratch_shapes=[pltpu.VMEM((1, SHIFT_PAD), F32)] + [hm() for _ in range(8)],
        compiler_params=_cparams(("parallel", "arbitrary")),
        name="rwkv7_mixer",
    )(z, z, z, z, zprev0, shift, s0, *params)


def _retention_kernel(zq_ref, zk_ref, zv_ref, zg_ref, cos_ref, sin_ref, dmask_ref, iscale_ref,
                      kscale_ref, sdec_ref, s0_ref, o_ref, sout_ref,
                      *, rows, c, n_real, per_chunk_state):
    i = pl.program_id(1)
    n_chunks = rows // c

    if per_chunk_state:
        sout_ref[...] = s0_ref[...]
        pr = lax.broadcasted_iota(jnp.int32, (rows, n_chunks * n_real), 0)
        pc = lax.broadcasted_iota(jnp.int32, (rows, n_chunks * n_real), 1)
        place = jnp.where((pr // c == pc // n_real) & (pr % c - (c - n_real) == pc % n_real),
                          1.0, 0.0).astype(BF16)
        load = lambda ref: _mm_exact_lhs(place, ref[...])
    else:
        load = lambda ref: ref[...]

        @pl.when(i == 0)
        def _():
            sout_ref[...] = s0_ref[...]

    lane = lax.broadcasted_iota(jnp.int32, (rows, RET_WIDTH), 1)
    even = (lane % 2) == 0
    cos = jnp.concatenate([cos_ref[...]] * RET_HEADS, axis=-1)
    sin = jnp.concatenate([sin_ref[...]] * RET_HEADS, axis=-1)

    def rot(x):
        partner = jnp.where(even, pltpu.roll(x, RET_WIDTH - 1, axis=1), pltpu.roll(x, 1, axis=1))
        return x * cos + partner * sin

    q = rot(load(zq_ref))
    k = rot(load(zk_ref)) * (RET_HEAD ** -0.5)
    v = load(zv_ref)
    g = load(zg_ref)

    out_rows = []
    for ch in range(n_chunks):
        rs = slice(ch * c, (ch + 1) * c)
        sidx = ch if per_chunk_state else 0
        out_heads = []
        for h in range(RET_HEADS):
            hs = slice(h * RET_HEAD, (h + 1) * RET_HEAD)
            qh, kh, vh = q[rs, hs], k[rs, hs], v[rs, hs]
            s_prev = sout_ref[sidx, h]
            scores = _mm_nt(qh, kh) * dmask_ref[h]
            o = _mm(scores, vh) + _mm(qh, s_prev) * iscale_ref[h]
            sout_ref[sidx, h] = s_prev * sdec_ref[h] + _mm_tn(kh * kscale_ref[h], vh)
            o = o * lax.rsqrt(jnp.mean(o * o, axis=-1, keepdims=True) + RET_GN_EPS)
            gh = g[rs, hs]
            out_heads.append(o * (gh * jax.nn.sigmoid(gh)))
        out_rows.append(jnp.concatenate(out_heads, axis=-1))
    o_out = jnp.concatenate(out_rows, axis=0).astype(BF16)
    if per_chunk_state:
        o_out = lax.dot_general(place, o_out, (((0,), (0,)), ((), ())),
                                preferred_element_type=F32).astype(BF16)
    o_ref[...] = o_out


def _retention_mixer(z, row_block0, n_groups, n_tiles, cos, sin, tabs, s0, *, rows, c, n_real,
                     per_chunk_state, pos_per_tile):
    n_states = s0.shape[0]
    sb = n_states // n_groups
    kern = functools.partial(_retention_kernel, rows=rows, c=c, n_real=n_real,
                             per_chunk_state=per_chunk_state)
    rows_io = rows // c * n_real if per_chunk_state else rows
    zspec = lambda col: pl.BlockSpec((rows_io, RET_WIDTH), lambda gi, i: (row_block0 + gi * n_tiles + i, col))
    full = lambda arr: pl.BlockSpec(arr.shape, lambda gi, i: (0,) * arr.ndim)
    if pos_per_tile:
        tspec = pl.BlockSpec((rows, RET_HEAD), lambda gi, i: (i, 0))
    else:
        tspec = pl.BlockSpec((rows, RET_HEAD), lambda gi, i: (0, 0))
    sspec = pl.BlockSpec((sb, RET_HEADS, RET_HEAD, RET_HEAD), lambda gi, i: (gi, 0, 0, 0))
    dmask, iscale, kscale, sdec = tabs
    return pl.pallas_call(
        kern,
        grid=(n_groups, n_tiles),
        in_specs=[zspec(0), zspec(1), zspec(2), zspec(3), tspec, tspec,
                  full(dmask), full(iscale), full(kscale), full(sdec), sspec],
        out_specs=[pl.BlockSpec((rows_io, RET_WIDTH), lambda gi, i: (gi * n_tiles + i, 0)), sspec],
        out_shape=[jax.ShapeDtypeStruct((n_groups * n_tiles * rows_io, RET_WIDTH), BF16),
                   jax.ShapeDtypeStruct(s0.shape, F32)],
        compiler_params=_cparams(("parallel", "arbitrary")),
        name="retention_mixer",
    )(z, z, z, z, cos, sin, dmask, iscale, kscale, sdec, s0)


def _retention_tables(c, n_real):
    log_gamma = jnp.log(1.0 - 2.0 ** (-5.0 - jnp.arange(RET_HEADS, dtype=F32)))
    r = jnp.arange(c, dtype=F32)
    idx = r - float(c - n_real)
    diff = r[:, None] - r[None, :]
    dmask = jnp.where(diff[None] >= 0,
                      jnp.exp(log_gamma[:, None, None] * jnp.maximum(diff, 0.0)[None]), 0.0)
    iscale = jnp.exp(log_gamma[:, None] * (idx + 1.0)[None, :])[:, :, None]
    kscale = jnp.exp(log_gamma[:, None] * (n_real - 1.0 - idx)[None, :])[:, :, None]
    sdec = jnp.broadcast_to(jnp.exp(log_gamma * n_real)[:, None, None], (RET_HEADS, 1, RET_HEAD))
    return dmask, iscale, kscale, sdec


def _rotary_tables(pos):
    inv_freq = 1.0 / (ROPE_BASE ** jnp.linspace(0.0, 1.0, RET_HEAD // 2, dtype=F32))
    ang = pos.astype(F32)[:, None] * inv_freq[None, :]
    cos = jnp.cos(ang)
    sin = jnp.sin(ang)
    cos2 = jnp.repeat(cos, 2, axis=-1)
    sin2 = jnp.stack([-sin, sin], axis=-1).reshape(pos.shape[0], RET_HEAD)
    return cos2, sin2


def _outproj_kernel(oa_ref, ob_ref, w_ref, x_ref, g_ref, h_ref, hn_ref):
    acc = jnp.dot(oa_ref[...], w_ref[0:RW_WIDTH, :], preferred_element_type=F32)
    acc = acc + jnp.dot(ob_ref[...], w_ref[RW_WIDTH:, :], preferred_element_type=F32)
    h = x_ref[...] + acc
    h_ref[...] = h
    ms = jnp.mean(h * h, axis=-1, keepdims=True)
    hn_ref[...] = ((h * lax.rsqrt(ms + RMS_EPS)) * g_ref[...]).astype(BF16)


def _out_projection(o_a, o_b, w_out, x_all, norm_g, tm):
    m = o_a.shape[0]
    return pl.pallas_call(
        _outproj_kernel,
        grid=(m // tm,),
        in_specs=[
            pl.BlockSpec((tm, RW_WIDTH), lambda i: (i, 0)),
            pl.BlockSpec((tm, RET_WIDTH), lambda i: (i, 0)),
            pl.BlockSpec((D_MODEL, D_MODEL), lambda i: (0, 0)),
            pl.BlockSpec((tm, D_MODEL), lambda i: (i, 0)),
            pl.BlockSpec((1, D_MODEL), lambda i: (0, 0)),
        ],
        out_specs=[pl.BlockSpec((tm, D_MODEL), lambda i: (i, 0)),
                   pl.BlockSpec((tm, D_MODEL), lambda i: (i, 0))],
        out_shape=[jax.ShapeDtypeStruct((m, D_MODEL), F32), jax.ShapeDtypeStruct((m, D_MODEL), BF16)],
        compiler_params=_cparams(("parallel",)),
        name="out_projection",
    )(o_a, o_b, w_out, x_all, norm_g)


def _ffn_kernel(hn_ref, wg_ref, wu_ref, wd_ref, h_ref, g_ref, y_ref, acc_ref):
    f = pl.program_id(1)

    @pl.when(f == 0)
    def _():
        acc_ref[...] = jnp.zeros_like(acc_ref)

    hn = hn_ref[...]
    gate = jnp.dot(hn, wg_ref[...], preferred_element_type=F32)
    up = jnp.dot(hn, wu_ref[...], preferred_element_type=F32)
    act = (gate * jax.nn.sigmoid(gate)) * up
    acc_ref[...] += jnp.dot(act.astype(BF16), wd_ref[...], preferred_element_type=F32)

    @pl.when(f == pl.num_programs(1) - 1)
    def _():
        h = h_ref[...] + acc_ref[...]
        ms = jnp.mean(h * h, axis=-1, keepdims=True)
        y_ref[...] = (h * lax.rsqrt(ms + RMS_EPS)) * g_ref[...]


def _ffn(hn, w_gate, w_up, w_down, h, norm_g, tm, tf):
    m = hn.shape[0]
    return pl.pallas_call(
        _ffn_kernel,
        grid=(m // tm, D_FF // tf),
        in_specs=[
            pl.BlockSpec((tm, D_MODEL), lambda i, f: (i, 0)),
            pl.BlockSpec((D_MODEL, tf), lambda i, f: (0, f)),
            pl.BlockSpec((D_MODEL, tf), lambda i, f: (0, f)),
            pl.BlockSpec((tf, D_MODEL), lambda i, f: (f, 0)),
            pl.BlockSpec((tm, D_MODEL), lambda i, f: (i, 0)),
            pl.BlockSpec((1, D_MODEL), lambda i, f: (0, 0)),
        ],
        out_specs=pl.BlockSpec((tm, D_MODEL), lambda i, f: (i, 0)),
        out_shape=jax.ShapeDtypeStruct((m, D_MODEL), F32),
        scratch_shapes=[pltpu.VMEM((tm, D_MODEL), F32)],
        compiler_params=_cparams(("parallel", "arbitrary")),
        name="swiglu_ffn",
    )(hn, w_gate, w_up, w_down, h, norm_g)


def _pad_cols(a, n):
    return jnp.pad(a, ((0, 0), (0, n - a.shape[1])))


def kernel(x_prompt, x_sample, state_shift, state_rwkv, state_ret, meta_tokens, norm_mix, w_in,
           rwkv_mu, rwkv_w0, rwkv_w2, rwkv_a0, rwkv_a2, rwkv_g2, rwkv_kk, rwkv_ka, rwkv_rk,
           rwkv_ln_w, rwkv_ln_b, w_out, norm_ffn, w_gate, w_up, w_down, norm_final):
    n_b, seq, d = x_prompt.shape
    n_s, dec_seq, _ = x_sample.shape
    n_p = n_b * seq
    n_d = n_s * dec_seq
    depth = w_in.shape[0]
    assert depth == 1 and d == D_MODEL

    w_in_p = _w_in_layout(w_in.reshape(d, -1), tk=128)
    w_out_b = w_out[0].astype(BF16)
    w_gate_b = w_gate[0].astype(BF16)
    w_up_b = w_up[0].astype(BF16)
    w_down_b = w_down[0].astype(BF16)
    row = lambda a: a.reshape(1, -1).astype(F32)
    w2p = jnp.concatenate([rwkv_w2[0], jnp.zeros((128 - DECAY_LORA, RW_WIDTH), F32)], axis=0)
    a2p = jnp.concatenate([jnp.zeros((DECAY_LORA, RW_WIDTH), F32), rwkv_a2[0]], axis=0)
    g2p = jnp.concatenate([rwkv_g2[0], jnp.zeros((256 - GATE_LORA, RW_WIDTH), F32)], axis=0)
    rw_params = dict(mu=_pad_cols(row(rwkv_mu[0]), SHIFT_PAD), w0=row(rwkv_w0[0]), w2=w2p,
                     a0=row(rwkv_a0[0]), a2=a2p, g2=g2p, kk=row(rwkv_kk[0]), ka=row(rwkv_ka[0]),
                     rk=row(rwkv_rk[0]), lnw=row(rwkv_ln_w[0]), lnb=row(rwkv_ln_b[0]))

    x_p = x_prompt.reshape(n_p, d)
    x_s = x_sample.reshape(n_d, d)
    x_sm = jnp.concatenate([x_s, meta_tokens.astype(F32)], axis=0)
    g_mix = row(norm_mix[0])
    z_p = _in_projection(x_p, g_mix, w_in_p, tm=1024, tn=768)
    z_sm = _in_projection(x_sm, g_mix, w_in_p, tm=x_sm.shape[0], tn=768)

    z_meta = jnp.pad(z_sm[n_d:], ((RET_CHUNK - N_META, 0), (0, 0)))
    srow = 2 * dec_seq

    zero_prev = jnp.zeros((1, 1, SHIFT_PAD), F32)
    zero_rw = jnp.zeros((1, RW_HEADS, RW_HEAD, RW_HEAD), F32)
    c_rw = 64
    _, s_rw_meta = _rwkv_mixer(z_meta, 1, 1, 1, zero_prev, zero_rw, rw_params,
                               tb=c_rw, c=c_rw, n_real=c_rw, per_chunk_state=False)
    zprev_p = jnp.broadcast_to(z_meta[-1:, RW_OFF:RW_OFF + SHIFT_PAD][None], (n_b, 1, SHIFT_PAD))
    tb_p = 256
    oa_p, rwkv_p = _rwkv_mixer(z_p, 0, n_b, seq // tb_p, zprev_p,
                               jnp.broadcast_to(s_rw_meta, (n_b,) + s_rw_meta.shape[1:]), rw_params,
                               tb=tb_p, c=c_rw, n_real=c_rw, per_chunk_state=False)
    sb_rw = 8
    oa_s, rwkv_s = _rwkv_mixer(z_sm, 0, n_s // sb_rw, 1, zero_prev,
                               state_rwkv.reshape(n_s, RW_HEADS, RW_HEAD, RW_HEAD), rw_params,
                               tb=sb_rw * srow, c=srow, n_real=dec_seq, per_chunk_state=True,
                               seqs_per_iter=2, shift=_pad_cols(state_shift[0], SHIFT_PAD))

    past_len = 16384
    tabs_full = _retention_tables(RET_CHUNK, RET_CHUNK)
    tabs_meta = _retention_tables(RET_CHUNK, N_META)
    tabs_smp = _retention_tables(srow, dec_seq)
    cos_m, sin_m = _rotary_tables(jnp.arange(RET_CHUNK) - (RET_CHUNK - N_META))
    cos_p, sin_p = _rotary_tables(N_META + jnp.arange(seq))
    sb_rt = 4
    pos_s = past_len + jnp.tile(jnp.arange(srow) - (srow - dec_seq), sb_rt)
    cos_s, sin_s = _rotary_tables(pos_s)
    zero_rt = jnp.zeros((1, RET_HEADS, RET_HEAD, RET_HEAD), F32)
    _, s_rt_meta = _retention_mixer(z_meta, 0, 1, 1, cos_m, sin_m, tabs_meta, zero_rt,
                                    rows=RET_CHUNK, c=RET_CHUNK, n_real=N_META, per_chunk_state=False,
                                    pos_per_tile=False)
    ob_p, ret_p = _retention_mixer(z_p, 0, n_b, seq // RET_CHUNK, cos_p, sin_p, tabs_full,
                                   jnp.broadcast_to(s_rt_meta, (n_b,) + s_rt_meta.shape[1:]),
                                   rows=RET_CHUNK, c=RET_CHUNK, n_real=RET_CHUNK, per_chunk_state=False,
                                   pos_per_tile=True)
    ob_s, ret_s = _retention_mixer(z_sm, 0, n_s // sb_rt, 1, cos_s, sin_s, tabs_smp,
                                   state_ret.reshape(n_s, RET_HEADS, RET_HEAD, RET_HEAD),
                                   rows=sb_rt * srow, c=srow, n_real=dec_seq, per_chunk_state=True,
                                   pos_per_tile=False)

    g_ffn = row(norm_ffn[0])
    g_fin = row(norm_final)
    h_p, hn_p = _out_projection(oa_p, ob_p, w_out_b, x_p, g_ffn, tm=512)
    h_s, hn_s = _out_projection(oa_s, ob_s, w_out_b, x_s, g_ffn, tm=n_d)
    y_p = _ffn(hn_p, w_gate_b, w_up_b, w_down_b, h_p, g_fin, tm=512, tf=512)
    y_s = _ffn(hn_s, w_gate_b, w_up_b, w_down_b, h_s, g_fin, tm=n_d, tf=512)

    y_prompt = y_p.reshape(n_b, seq, d)
    y_sample = y_s.reshape(n_s, dec_seq, d)
    a_cols = slice(RW_OFF, RW_OFF + RW_COLS)
    shift_p = z_p[seq - 1:n_p:seq, a_cols].reshape(1, n_b, RW_COLS)
    shift_s = z_sm[dec_seq - 1:n_d:dec_seq, a_cols].reshape(1, n_s, RW_COLS)
    return (y_prompt, y_sample, shift_p, rwkv_p[None], ret_p[None], shift_s, rwkv_s[None], ret_s[None])
```

```python
import functools
import math

import jax
import jax.numpy as jnp
from jax import lax
from jax.experimental import pallas as pl
from jax.experimental.pallas import tpu as pltpu

F32 = jnp.float32
BF16 = jnp.bfloat16

D_MODEL = 2048
N_META = 16
RW_WIDTH = 1024
RW_HEAD = 64
RW_HEADS = 16
DECAY_LORA = 64
AAA_LORA = 64
GATE_LORA = 160
RW_COLS = 3 * RW_WIDTH + DECAY_LORA + AAA_LORA + GATE_LORA
RET_WIDTH = 1024
RET_HEADS = 4
RET_HEAD = 256
RET_CHUNK = 128
D_FF = 5632
RMS_EPS = 1e-6
RW_GN_EPS = 64e-5
RET_GN_EPS = 1e-6
ROPE_BASE = 10000.0
LANES = 128

LORA_PAD = 512
Z_COLS = 4 * RET_WIDTH + 3 * RW_WIDTH + LORA_PAD
RW_OFF = 4 * RET_WIDTH
SHIFT_PAD = 3 * RW_WIDTH + LORA_PAD

VMEM_LIMIT = 56 * 1024 * 1024


def _cparams(sem):
    return pltpu.CompilerParams(dimension_semantics=sem, vmem_limit_bytes=VMEM_LIMIT)


def _mm(a, b):
    return jnp.dot(a.astype(BF16), b.astype(BF16), preferred_element_type=F32)


def _mm_nt(a, b):
    return lax.dot_general(a.astype(BF16), b.astype(BF16), (((1,), (1,)), ((), ())),
                           preferred_element_type=F32)


def _mm_tn(a, b):
    return lax.dot_general(a.astype(BF16), b.astype(BF16), (((0,), (0,)), ((), ())),
                           preferred_element_type=F32)


def _bmm(a, b):
    return jnp.einsum("bmk,bkn->bmn", a.astype(BF16), b.astype(BF16), preferred_element_type=F32)


def _bmm_nt(a, b):
    return jnp.einsum("bmk,bnk->bmn", a.astype(BF16), b.astype(BF16), preferred_element_type=F32)


def _bmm_tn(a, b):
    return jnp.einsum("bkm,bkn->bmn", a.astype(BF16), b.astype(BF16), preferred_element_type=F32)


def _mm_exact_lhs(m_bf16, x):
    hi = x.astype(BF16)
    r1 = x - hi.astype(F32)
    mid = r1.astype(BF16)
    lo = (r1 - mid.astype(F32)).astype(BF16)
    d = functools.partial(jnp.dot, preferred_element_type=F32)
    return d(m_bf16, hi) + d(m_bf16, mid) + d(m_bf16, lo)


def _rms_norm_bf16(x, g):
    ms = jnp.mean(x * x, axis=-1, keepdims=True)
    return ((x * lax.rsqrt(ms + RMS_EPS)) * g).astype(BF16)


def _w_in_layout_kernel(w_ref, o_ref):
    o_ref[...] = w_ref[...].T.astype(BF16)


def _w_in_layout(w_in_t, tn):
    n, d = w_in_t.shape
    n_ret = 4 * RET_WIDTH // tn

    def src_row(c):
        align = math.gcd(RW_COLS, tn)
        return (pl.multiple_of(jnp.where(c < n_ret, RW_COLS + tn * c, tn * (c - n_ret)), align), 0)

    return pl.pallas_call(
        _w_in_layout_kernel,
        grid=(Z_COLS // tn,),
        in_specs=[pl.BlockSpec((pl.Element(tn), pl.Element(d)), src_row)],
        out_specs=pl.BlockSpec((d, tn), lambda c: (0, c)),
        out_shape=jax.ShapeDtypeStruct((d, Z_COLS), BF16),
        compiler_params=_cparams(("parallel",)),
        name="w_in_layout",
    )(w_in_t)


def _inproj_kernel(x_ref, g_ref, w_ref, o_ref, xn_ref):
    @pl.when(pl.program_id(1) == 0)
    def _():
        xn_ref[...] = _rms_norm_bf16(x_ref[...], g_ref[...])

    o_ref[...] = jnp.dot(xn_ref[...], w_ref[...], preferred_element_type=F32)


def _in_projection(x2d, norm_g, w_in_p, tm, tn):
    m = x2d.shape[0]
    return pl.pallas_call(
        _inproj_kernel,
        grid=(m // tm, Z_COLS // tn),
        in_specs=[
            pl.BlockSpec((tm, D_MODEL), lambda i, j: (i, 0)),
            pl.BlockSpec((1, D_MODEL), lambda i, j: (0, 0)),
            pl.BlockSpec((D_MODEL, tn), lambda i, j: (0, j)),
        ],
        out_specs=pl.BlockSpec((tm, tn), lambda i, j: (i, j)),
        out_shape=jax.ShapeDtypeStruct((m, Z_COLS), F32),
        scratch_shapes=[pltpu.VMEM((tm, D_MODEL), BF16)],
        compiler_params=_cparams(("parallel", "arbitrary")),
        name="in_projection",
    )(x2d, norm_g, w_in_p)


def _inproj_t_kernel(w_ref, x_ref, g_ref, o_ref, xn_ref):
    @pl.when(pl.program_id(0) == 0)
    def _():
        xn_ref[...] = _rms_norm_bf16(x_ref[...], g_ref[...])

    o_ref[...] = _mm_nt(w_ref[...], xn_ref[...])


def _in_projection_t(w_in_t, x2d, norm_g, n_rows, tm):
    m, d = x2d.shape
    return pl.pallas_call(
        _inproj_t_kernel,
        grid=(n_rows // tm,),
        in_specs=[
            pl.BlockSpec((tm, d), lambda i: (i, 0)),
            pl.BlockSpec((m, d), lambda i: (0, 0)),
            pl.BlockSpec((1, d), lambda i: (0, 0)),
        ],
        out_specs=pl.BlockSpec((tm, m), lambda i: (i, 0)),
        out_shape=jax.ShapeDtypeStruct((n_rows, m), F32),
        scratch_shapes=[pltpu.VMEM((m, d), BF16)],
        compiler_params=_cparams(("arbitrary",)),
        name="in_projection_t",
    )(w_in_t, x2d, norm_g)


def _rwkv_kernel(zr_ref, zk_ref, zv_ref, zl_ref, zprev0_ref, s0_ref,
                 mu_ref, w0_ref, w2_ref, a0_ref, a2_ref, g2_ref, kk_ref, ka_ref, rk_ref,
                 lnw_ref, lnb_ref,
                 o_ref, sout_ref,
                 carry_ref, r_s, km_s, v_s, kkn_s, a_s, lw_s, cum_s, o_s,
                 *, tb, c, n_factors):
    i = pl.program_id(1)
    n_chunks = tb // c

    @pl.when(i == 0)
    def _():
        carry_ref[...] = zprev0_ref[0]
        sout_ref[...] = s0_ref[...]

    row = lax.broadcasted_iota(jnp.int32, (tb, 1), 0)

    def shifted(z_ref, lo, hi):
        z = z_ref[...]
        prev = jnp.where(row == 0, carry_ref[:, lo:hi], pltpu.roll(z, 1, axis=0))
        zs = z + mu_ref[:, lo:hi] * (prev - z)
        carry_ref[:, lo:hi] = z[tb - 1:tb, :]
        return zs

    w = RW_WIDTH
    r = shifted(zr_ref, 0, w)
    k = shifted(zk_ref, w, 2 * w)
    v = shifted(zv_ref, 2 * w, 3 * w)
    zl = shifted(zl_ref, 3 * w, 3 * w + LORA_PAD)

    lo2 = zl[:, 0:128]
    wlog = -jax.nn.softplus(-(w0_ref[...] + _mm(jnp.tanh(lo2), w2_ref[...]))) - 0.5
    lw = -jnp.exp(wlog)
    a = jax.nn.sigmoid(a0_ref[...] + _mm(lo2, a2_ref[...]))
    g = _mm(jax.nn.sigmoid(zl[:, 128:384]), g2_ref[...])
    kk = k * kk_ref[...]
    km = k * (1.0 + (a - 1.0) * ka_ref[...])

    ri = lax.broadcasted_iota(jnp.int32, (tb, tb), 0)
    ci = lax.broadcasted_iota(jnp.int32, (tb, tb), 1)
    tri = jnp.where((ri // c == ci // c) & (ri >= ci), 1.0, 0.0).astype(BF16)
    cum = _mm_exact_lhs(tri, lw)

    for h in range(RW_HEADS):
        sl = slice(h * RW_HEAD, (h + 1) * RW_HEAD)
        r_s[h] = r[:, sl]
        km_s[h] = km[:, sl]
        v_s[h] = v[:, sl]
        a_s[h] = a[:, sl]
        lw_s[h] = lw[:, sl]
        cum_s[h] = cum[:, sl]
        kkh = kk[:, sl]
        ss = jnp.sum(kkh * kkh, axis=-1, keepdims=True)
        kkn_s[h] = kkh / jnp.maximum(jnp.sqrt(ss), 1e-12)

    rr = lax.broadcasted_iota(jnp.int32, (c, c), 0)
    cc = lax.broadcasted_iota(jnp.int32, (c, c), 1)
    strict = (rr > cc)[None]
    incl = (rr >= cc)[None]
    eye = jnp.where(rr == cc, 1.0, 0.0).astype(F32)[None]
    rk = rk_ref[...]
    lnw = lnw_ref[...]
    lnb = lnb_ref[...]

    def body(ch, carry):
        rows = pl.ds(pl.multiple_of(ch * c, c), c)
        rh, kmh, vh, ah, lwh, cumh, kkn = [s[:, rows, :] for s in (r_s, km_s, v_s, a_s, lw_s, cum_s, kkn_s)]
        s_prev = sout_ref[0]
        bvec = kkn * ah
        cum_end = cumh[:, c - 1:c, :]
        p_inc = jnp.exp(cumh)
        p_exc = jnp.exp(cumh - lwh)
        p_inv = jnp.exp(-cumh)
        p_end = jnp.exp(cum_end)
        p_rel = jnp.exp(cum_end - cumh)
        a_t = -(kkn * p_exc)
        r_t = rh * p_inc
        b_t = bvec * p_inv
        k_t = kmh * p_inv
        b_h = bvec * p_rel
        k_h = kmh * p_rel

        ar = jnp.concatenate([a_t, r_t], axis=1)
        ab = _bmm_nt(ar, b_t)
        ak = _bmm_nt(ar, k_t)
        lmat = jnp.where(strict, ab[:, :c], 0.0)
        aak = jnp.where(strict, ak[:, :c], 0.0)
        arb = jnp.where(incl, ab[:, c:], 0.0)
        ark = jnp.where(incl, ak[:, c:], 0.0)
        tinv = eye + lmat
        lp = lmat
        for _ in range(n_factors - 1):
            lp = _bmm(lp, lp)
            tinv = tinv + _bmm(tinv, lp)
        as2 = _bmm_nt(ar, s_prev)
        x = as2[:, :c] + _bmm(aak, vh)
        u = _bmm(tinv, x)
        o = as2[:, c:] + _bmm(arb, u) + _bmm(ark, vh)
        sout_ref[0] = s_prev * p_end + _bmm_tn(jnp.concatenate([u, vh], axis=1),
                                               jnp.concatenate([b_h, k_h], axis=1))
        mean = jnp.mean(o, axis=-1, keepdims=True)
        var = jnp.mean(jnp.square(o - mean), axis=-1, keepdims=True)
        on = ((o - mean) * lax.rsqrt(var + RW_GN_EPS)) * lnw + lnb
        bonus = jnp.sum(rh * kmh * rk, axis=-1, keepdims=True) * vh
        o_s[:, rows, :] = on + bonus
        return carry

    lax.fori_loop(0, n_chunks, body, 0)

    o_full = jnp.concatenate([o_s[h] for h in range(RW_HEADS)], axis=-1)
    o_ref[...] = (o_full * g).astype(BF16)


def _rwkv_mixer(z, row_block0, n_groups, n_tiles, zprev0, s0, p, *, tb, c):
    n_factors = max(1, math.ceil(math.log2(c)))
    kern = functools.partial(_rwkv_kernel, tb=tb, c=c, n_factors=n_factors)
    cb = RW_OFF // RW_WIDTH
    full = lambda arr: pl.BlockSpec(arr.shape, lambda gi, i: (0,) * arr.ndim)
    lora_cb = (RW_OFF + 3 * RW_WIDTH) // LORA_PAD
    per_head = lambda a: a.reshape(RW_HEADS, 1, RW_HEAD)
    params = [p["mu"], p["w0"], p["w2"], p["a0"], p["a2"], p["g2"], p["kk"], p["ka"],
              per_head(p["rk"]), per_head(p["lnw"]), per_head(p["lnb"])]
    hm = lambda: pltpu.VMEM((RW_HEADS, tb, RW_HEAD), F32)
    zspec = lambda width, col: pl.BlockSpec((tb, width), lambda gi, i: (row_block0 + gi * n_tiles + i, col))
    sspec = pl.BlockSpec((1, RW_HEADS, RW_HEAD, RW_HEAD), lambda gi, i: (gi, 0, 0, 0))
    return pl.pallas_call(
        kern,
        grid=(n_groups, n_tiles),
        in_specs=[zspec(RW_WIDTH, cb), zspec(RW_WIDTH, cb + 1), zspec(RW_WIDTH, cb + 2),
                  zspec(LORA_PAD, lora_cb),
                  pl.BlockSpec((1, 1, SHIFT_PAD), lambda gi, i: (gi, 0, 0)),
                  sspec]
                 + [full(a) for a in params],
        out_specs=[pl.BlockSpec((tb, RW_WIDTH), lambda gi, i: (gi * n_tiles + i, 0)), sspec],
        out_shape=[jax.ShapeDtypeStruct((n_groups * n_tiles * tb, RW_WIDTH), BF16),
                   jax.ShapeDtypeStruct((n_groups, RW_HEADS, RW_HEAD, RW_HEAD), F32)],
        scratch_shapes=[pltpu.VMEM((1, SHIFT_PAD), F32)] + [hm() for _ in range(8)],
        compiler_params=_cparams(("parallel", "arbitrary")),
        name="rwkv7_mixer",
    )(z, z, z, z, zprev0, s0, *params)


def _rwkv_sample_kernel(zr_ref, zk_ref, zv_ref, zl_ref, shr_ref, shk_ref, shv_ref, shl_ref,
                        mur_ref, muk_ref, muv_ref, mul_ref, s0_ref,
                        w0_ref, w2t_ref, a0_ref, a2t_ref, g2t_ref, kk_ref, ka_ref, rk_ref,
                        lnw_ref, lnb_ref,
                        o_ref, sout_ref,
                        dec_s, a_s, b_s, k_s, r_s, v_s, o_s,
                        *, n_tok, hpb):
    ns = LANES
    ch = hpb * RW_HEAD

    def lanes(x):
        return jnp.concatenate([x] * n_tok, axis=1)

    def shifted(z_ref, sh_ref, mu_ref):
        z = z_ref[...]
        prev = jnp.concatenate([sh_ref[...], z[:, :(n_tok - 1) * ns]], axis=1)
        return z + lanes(mu_ref[...]) * (prev - z)

    r = shifted(zr_ref, shr_ref, mur_ref)
    k = shifted(zk_ref, shk_ref, muk_ref)
    v = shifted(zv_ref, shv_ref, muv_ref)
    zl = shifted(zl_ref, shl_ref, mul_ref)
    wd = zl[0:DECAY_LORA]
    ad = zl[DECAY_LORA:DECAY_LORA + AAA_LORA]
    gd = zl[DECAY_LORA + AAA_LORA:DECAY_LORA + AAA_LORA + GATE_LORA]

    wlog = -jax.nn.softplus(-(lanes(w0_ref[...]) + _mm(w2t_ref[...], jnp.tanh(wd)))) - 0.5
    lw = -jnp.exp(wlog)
    a = jax.nn.sigmoid(lanes(a0_ref[...]) + _mm(a2t_ref[...], ad))
    g = _mm(g2t_ref[...], jax.nn.sigmoid(gd))
    kk = k * lanes(kk_ref[...])
    km = k * (1.0 + (a - 1.0) * lanes(ka_ref[...]))

    def head_sum(x):
        x3 = x.reshape(hpb, RW_HEAD, n_tok * ns)
        s = jnp.sum(x3, axis=1, keepdims=True)
        return jnp.broadcast_to(s, x3.shape).reshape(ch, n_tok * ns)

    kkn = kk / jnp.maximum(jnp.sqrt(head_sum(kk * kk)), 1e-12)
    dec_s[...] = jnp.exp(lw)
    a_s[...] = -kkn
    b_s[...] = kkn * a
    k_s[...] = km
    r_s[...] = r
    v_s[...] = v

    for hh in range(hpb):
        hrows = slice(hh * RW_HEAD, (hh + 1) * RW_HEAD)

        def body(i8, carry, hh=hh, hrows=hrows):
            base = pl.multiple_of(i8 * 8, 8)
            rows8 = pl.ds(pl.multiple_of(hh * RW_HEAD + base, 8), 8)
            o_rows = [[] for _ in range(n_tok)]
            for j in range(8):
                s = s0_ref[hh, base + j]
                for t in range(n_tok):
                    tl = slice(t * ns, (t + 1) * ns)
                    sa = jnp.sum(s * a_s[hrows, tl], axis=0, keepdims=True)
                    v_row = v_s[rows8, tl][j:j + 1, :]
                    s = s * dec_s[hrows, tl] + sa * b_s[hrows, tl] + v_row * k_s[hrows, tl]
                    o_rows[t].append(jnp.sum(s * r_s[hrows, tl], axis=0, keepdims=True))
                sout_ref[hh, base + j] = s
            for t in range(n_tok):
                o_s[rows8, t * ns:(t + 1) * ns] = jnp.concatenate(o_rows[t], axis=0)
            return carry

        lax.fori_loop(0, RW_HEAD // 8, body, 0)

    o = o_s[...]
    inv_n = 1.0 / RW_HEAD
    mean = head_sum(o) * inv_n
    var = head_sum(jnp.square(o - mean)) * inv_n
    on = ((o - mean) * lax.rsqrt(var + RW_GN_EPS)) * lanes(lnw_ref[...]) + lanes(lnb_ref[...])
    bonus = head_sum(r * km * lanes(rk_ref[...])) * v
    out = (on + bonus) * g
    for t in range(n_tok):
        o_ref[t] = out[:, t * ns:(t + 1) * ns].T.astype(BF16)


def _rwkv_sample_mixer(z_t, shift_t, mu_t, state_t, p_t, *, n_tok, hpb):
    ch = hpb * RW_HEAD
    n_steps = RW_HEADS // hpb
    nl = n_tok * LANES
    seg = RW_WIDTH // ch
    zspec = lambda s: pl.BlockSpec((ch, nl), lambda h: (s * seg + h, 0))
    cspec = lambda s: pl.BlockSpec((ch, LANES), lambda h: (s * seg + h, 0))
    lora_blk = 3 * RW_WIDTH // LORA_PAD
    zl_spec = pl.BlockSpec((LORA_PAD, nl), lambda h: (lora_blk, 0))
    cl_spec = pl.BlockSpec((LORA_PAD, LANES), lambda h: (lora_blk, 0))
    hspec = pl.BlockSpec((ch, LANES), lambda h: (h, 0))
    wspec = lambda k: pl.BlockSpec((ch, k), lambda h: (h, 0))
    sspec = pl.BlockSpec((hpb, RW_HEAD, RW_HEAD, LANES), lambda h: (h, 0, 0, 0))
    buf = lambda: pltpu.VMEM((ch, nl), F32)
    return pl.pallas_call(
        functools.partial(_rwkv_sample_kernel, n_tok=n_tok, hpb=hpb),
        grid=(n_steps,),
        in_specs=[zspec(0), zspec(1), zspec(2), zl_spec,
                  cspec(0), cspec(1), cspec(2), cl_spec,
                  cspec(0), cspec(1), cspec(2), cl_spec,
                  sspec,
                  hspec, wspec(DECAY_LORA), hspec, wspec(AAA_LORA), wspec(GATE_LORA),
                  hspec, hspec, hspec, hspec, hspec],
        out_specs=[pl.BlockSpec((n_tok, LANES, ch), lambda h: (0, 0, h)), sspec],
        out_shape=[jax.ShapeDtypeStruct((n_tok, LANES, RW_WIDTH), BF16),
                   jax.ShapeDtypeStruct(state_t.shape, F32)],
        scratch_shapes=[buf() for _ in range(7)],
        compiler_params=_cparams(("parallel",)),
        name="rwkv7_sample",
    )(z_t, z_t, z_t, z_t, shift_t, shift_t, shift_t, shift_t, mu_t, mu_t, mu_t, mu_t, state_t,
      p_t["w0"], p_t["w2t"], p_t["a0"], p_t["a2t"], p_t["g2t"], p_t["kk"], p_t["ka"], p_t["rk"],
      p_t["lnw"], p_t["lnb"])


def _retention_kernel(zq_ref, zk_ref, zv_ref, zg_ref, cos_ref, sin_ref, dmask_ref, iscale_ref,
                      kscale_ref, sdec_ref, s0_ref, o_ref, sout_ref,
                      *, rows, c, n_real, per_chunk_state):
    i = pl.program_id(1)
    n_chunks = rows // c

    if per_chunk_state:
        sout_ref[...] = s0_ref[...]
        pr = lax.broadcasted_iota(jnp.int32, (rows, n_chunks * n_real), 0)
        pc = lax.broadcasted_iota(jnp.int32, (rows, n_chunks * n_real), 1)
        place = jnp.where((pr // c == pc // n_real) & (pr % c - (c - n_real) == pc % n_real),
                          1.0, 0.0).astype(BF16)
        load = lambda ref: _mm_exact_lhs(place, ref[...])
    else:
        load = lambda ref: ref[...]

        @pl.when(i == 0)
        def _():
            sout_ref[...] = s0_ref[...]

    lane = lax.broadcasted_iota(jnp.int32, (rows, RET_WIDTH), 1)
    even = (lane % 2) == 0
    cos = jnp.concatenate([cos_ref[...]] * RET_HEADS, axis=-1)
    sin = jnp.concatenate([sin_ref[...]] * RET_HEADS, axis=-1)

    def rot(x):
        partner = jnp.where(even, pltpu.roll(x, RET_WIDTH - 1, axis=1), pltpu.roll(x, 1, axis=1))
        return x * cos + partner * sin

    q = rot(load(zq_ref))
    k = rot(load(zk_ref)) * (RET_HEAD ** -0.5)
    v = load(zv_ref)
    g = load(zg_ref)

    out_rows = []
    for ch in range(n_chunks):
        rs = slice(ch * c, (ch + 1) * c)
        sidx = ch if per_chunk_state else 0
        out_heads = []
        for h in range(RET_HEADS):
            hs = slice(h * RET_HEAD, (h + 1) * RET_HEAD)
            qh, kh, vh = q[rs, hs], k[rs, hs], v[rs, hs]
            s_prev = sout_ref[sidx, h]
            scores = _mm_nt(qh, kh) * dmask_ref[h]
            o = _mm(scores, vh) + _mm(qh, s_prev) * iscale_ref[h]
            sout_ref[sidx, h] = s_prev * sdec_ref[h] + _mm_tn(kh * kscale_ref[h], vh)
            o = o * lax.rsqrt(jnp.mean(o * o, axis=-1, keepdims=True) + RET_GN_EPS)
            gh = g[rs, hs]
            out_heads.append(o * (gh * jax.nn.sigmoid(gh)))
        out_rows.append(jnp.concatenate(out_heads, axis=-1))
    o_out = jnp.concatenate(out_rows, axis=0).astype(BF16)
    if per_chunk_state:
        o_out = lax.dot_general(place, o_out, (((0,), (0,)), ((), ())),
                                preferred_element_type=F32).astype(BF16)
    o_ref[...] = o_out


def _retention_mixer(z, row_block0, n_groups, n_tiles, cos, sin, tabs, s0, *, rows, c, n_real,
                     per_chunk_state, pos_per_tile):
    n_states = s0.shape[0]
    sb = n_states // n_groups
    kern = functools.partial(_retention_kernel, rows=rows, c=c, n_real=n_real,
                             per_chunk_state=per_chunk_state)
    rows_io = rows // c * n_real if per_chunk_state else rows
    zspec = lambda col: pl.BlockSpec((rows_io, RET_WIDTH), lambda gi, i: (row_block0 + gi * n_tiles + i, col))
    full = lambda arr: pl.BlockSpec(arr.shape, lambda gi, i: (0,) * arr.ndim)
    if pos_per_tile:
        tspec = pl.BlockSpec((rows, RET_HEAD), lambda gi, i: (i, 0))
    else:
        tspec = pl.BlockSpec((rows, RET_HEAD), lambda gi, i: (0, 0))
    sspec = pl.BlockSpec((sb, RET_HEADS, RET_HEAD, RET_HEAD), lambda gi, i: (gi, 0, 0, 0))
    dmask, iscale, kscale, sdec = tabs
    return pl.pallas_call(
        kern,
        grid=(n_groups, n_tiles),
        in_specs=[zspec(0), zspec(1), zspec(2), zspec(3), tspec, tspec,
                  full(dmask), full(iscale), full(kscale), full(sdec), sspec],
        out_specs=[pl.BlockSpec((rows_io, RET_WIDTH), lambda gi, i: (gi * n_tiles + i, 0)), sspec],
        out_shape=[jax.ShapeDtypeStruct((n_groups * n_tiles * rows_io, RET_WIDTH), BF16),
                   jax.ShapeDtypeStruct(s0.shape, F32)],
        compiler_params=_cparams(("parallel", "arbitrary")),
        name="retention_mixer",
    )(z, z, z, z, cos, sin, dmask, iscale, kscale, sdec, s0)


def _retention_tables(c, n_real):
    log_gamma = jnp.log(1.0 - 2.0 ** (-5.0 - jnp.arange(RET_HEADS, dtype=F32)))
    r = jnp.arange(c, dtype=F32)
    idx = r - float(c - n_real)
    diff = r[:, None] - r[None, :]
    dmask = jnp.where(diff[None] >= 0,
                      jnp.exp(log_gamma[:, None, None] * jnp.maximum(diff, 0.0)[None]), 0.0)
    iscale = jnp.exp(log_gamma[:, None] * (idx + 1.0)[None, :])[:, :, None]
    kscale = jnp.exp(log_gamma[:, None] * (n_real - 1.0 - idx)[None, :])[:, :, None]
    sdec = jnp.broadcast_to(jnp.exp(log_gamma * n_real)[:, None, None], (RET_HEADS, 1, RET_HEAD))
    return dmask, iscale, kscale, sdec


def _rotary_tables(pos):
    inv_freq = 1.0 / (ROPE_BASE ** jnp.linspace(0.0, 1.0, RET_HEAD // 2, dtype=F32))
    ang = pos.astype(F32)[:, None] * inv_freq[None, :]
    cos = jnp.cos(ang)
    sin = jnp.sin(ang)
    cos2 = jnp.repeat(cos, 2, axis=-1)
    sin2 = jnp.stack([-sin, sin], axis=-1).reshape(pos.shape[0], RET_HEAD)
    return cos2, sin2


def _outproj_kernel(oa_ref, ob_ref, w_ref, x_ref, g_ref, h_ref, hn_ref):
    acc = jnp.dot(oa_ref[...], w_ref[0:RW_WIDTH, :], preferred_element_type=F32)
    acc = acc + jnp.dot(ob_ref[...], w_ref[RW_WIDTH:, :], preferred_element_type=F32)
    h = x_ref[...] + acc
    h_ref[...] = h
    hn_ref[...] = _rms_norm_bf16(h, g_ref[...])


def _out_projection(o_a, o_b, w_out, x2d, norm_g, tm):
    m = o_a.shape[0]
    return pl.pallas_call(
        _outproj_kernel,
        grid=(m // tm,),
        in_specs=[
            pl.BlockSpec((tm, RW_WIDTH), lambda i: (i, 0)),
            pl.BlockSpec((tm, RET_WIDTH), lambda i: (i, 0)),
            pl.BlockSpec((D_MODEL, D_MODEL), lambda i: (0, 0)),
            pl.BlockSpec((tm, D_MODEL), lambda i: (i, 0)),
            pl.BlockSpec((1, D_MODEL), lambda i: (0, 0)),
        ],
        out_specs=[pl.BlockSpec((tm, D_MODEL), lambda i: (i, 0)),
                   pl.BlockSpec((tm, D_MODEL), lambda i: (i, 0))],
        out_shape=[jax.ShapeDtypeStruct((m, D_MODEL), F32), jax.ShapeDtypeStruct((m, D_MODEL), BF16)],
        compiler_params=_cparams(("parallel",)),
        name="out_projection",
    )(o_a, o_b, w_out, x2d, norm_g)


def _ffn_kernel(hn_ref, wg_ref, wu_ref, wd_ref, h_ref, g_ref, y_ref, acc_ref):
    f = pl.program_id(1)

    @pl.when(f == 0)
    def _():
        acc_ref[...] = jnp.zeros_like(acc_ref)

    hn = hn_ref[...]
    gate = jnp.dot(hn, wg_ref[...], preferred_element_type=F32)
    up = jnp.dot(hn, wu_ref[...], preferred_element_type=F32)
    act = (gate * jax.nn.sigmoid(gate)) * up
    acc_ref[...] += jnp.dot(act.astype(BF16), wd_ref[...], preferred_element_type=F32)

    @pl.when(f == pl.num_programs(1) - 1)
    def _():
        h = h_ref[...] + acc_ref[...]
        ms = jnp.mean(h * h, axis=-1, keepdims=True)
        y_ref[...] = (h * lax.rsqrt(ms + RMS_EPS)) * g_ref[...]


def _ffn(hn, w_gate, w_up, w_down, h, norm_g, tm, tf):
    m = hn.shape[0]
    return pl.pallas_call(
        _ffn_kernel,
        grid=(m // tm, D_FF // tf),
        in_specs=[
            pl.BlockSpec((tm, D_MODEL), lambda i, f: (i, 0)),
            pl.BlockSpec((D_MODEL, tf), lambda i, f: (0, f)),
            pl.BlockSpec((D_MODEL, tf), lambda i, f: (0, f)),
            pl.BlockSpec((tf, D_MODEL), lambda i, f: (f, 0)),
            pl.BlockSpec((tm, D_MODEL), lambda i, f: (i, 0)),
            pl.BlockSpec((1, D_MODEL), lambda i, f: (0, 0)),
        ],
        out_specs=pl.BlockSpec((tm, D_MODEL), lambda i, f: (i, 0)),
        out_shape=jax.ShapeDtypeStruct((m, D_MODEL), F32),
        scratch_shapes=[pltpu.VMEM((tm, D_MODEL), F32)],
        compiler_params=_cparams(("parallel", "arbitrary")),
        name="swiglu_ffn",
    )(hn, w_gate, w_up, w_down, h, norm_g)


def _pad_cols(a, n):
    return jnp.pad(a, ((0, 0), (0, n - a.shape[1])))


def _pad_rows(a, n):
    return jnp.pad(a, ((0, n - a.shape[0]), (0, 0)))


def kernel(x_prompt, x_sample, state_shift, state_rwkv, state_ret, meta_tokens, norm_mix, w_in,
           rwkv_mu, rwkv_w0, rwkv_w2, rwkv_a0, rwkv_a2, rwkv_g2, rwkv_kk, rwkv_ka, rwkv_rk,
           rwkv_ln_w, rwkv_ln_b, w_out, norm_ffn, w_gate, w_up, w_down, norm_final):
    n_b, seq, d = x_prompt.shape
    n_s, dec_seq, _ = x_sample.shape
    n_p = n_b * seq
    n_d = n_s * dec_seq
    depth = w_in.shape[0]
    assert depth == 1 and d == D_MODEL and n_s == LANES

    w_in_t = jnp.transpose(w_in[0])
    w_in_p = _w_in_layout(w_in_t, tn=512)
    w_out_b = w_out[0].astype(BF16)
    w_gate_b = w_gate[0].astype(BF16)
    w_up_b = w_up[0].astype(BF16)
    w_down_b = w_down[0].astype(BF16)
    row = lambda a: a.reshape(1, -1).astype(F32)
    w2p = jnp.concatenate([rwkv_w2[0], jnp.zeros((128 - DECAY_LORA, RW_WIDTH), F32)], axis=0)
    a2p = jnp.concatenate([jnp.zeros((DECAY_LORA, RW_WIDTH), F32), rwkv_a2[0]], axis=0)
    g2p = jnp.concatenate([rwkv_g2[0], jnp.zeros((256 - GATE_LORA, RW_WIDTH), F32)], axis=0)
    rw_params = dict(mu=_pad_cols(row(rwkv_mu[0]), SHIFT_PAD), w0=row(rwkv_w0[0]), w2=w2p,
                     a0=row(rwkv_a0[0]), a2=a2p, g2=g2p, kk=row(rwkv_kk[0]), ka=row(rwkv_ka[0]),
                     rk=row(rwkv_rk[0]), lnw=row(rwkv_ln_w[0]), lnb=row(rwkv_ln_b[0]))
    col = lambda a: jnp.broadcast_to(a.reshape(-1, 1).astype(F32), (a.size, LANES))
    rw_params_t = dict(w0=col(rwkv_w0[0]), w2t=rwkv_w2[0].T, a0=col(rwkv_a0[0]), a2t=rwkv_a2[0].T,
                       g2t=rwkv_g2[0].T, kk=col(rwkv_kk[0]), ka=col(rwkv_ka[0]), rk=col(rwkv_rk[0]),
                       lnw=col(rwkv_ln_w[0]), lnb=col(rwkv_ln_b[0]))

    x_p = x_prompt.reshape(n_p, d)
    x_s = x_sample.reshape(n_d, d)
    x_sm = jnp.concatenate([x_s, meta_tokens.astype(F32)], axis=0)
    x_ts = jnp.transpose(x_sample, (1, 0, 2)).reshape(n_d, d)
    g_mix = row(norm_mix[0])
    z_p = _in_projection(x_p, g_mix, w_in_p, tm=1024, tn=768)
    z_sm = _in_projection(x_sm, g_mix, w_in_p, tm=x_sm.shape[0], tn=768)
    z_st = _in_projection_t(w_in_t, x_ts, g_mix, SHIFT_PAD, tm=512)

    z_meta = jnp.pad(z_sm[n_d:], ((RET_CHUNK - N_META, 0), (0, 0)))
    srow = 2 * dec_seq

    zero_prev = jnp.zeros((1, 1, SHIFT_PAD), F32)
    zero_rw = jnp.zeros((1, RW_HEADS, RW_HEAD, RW_HEAD), F32)
    c_rw = 64
    _, s_rw_meta = _rwkv_mixer(z_meta, 1, 1, 1, zero_prev, zero_rw, rw_params, tb=c_rw, c=c_rw)
    zprev_p = jnp.broadcast_to(z_meta[-1:, RW_OFF:RW_OFF + SHIFT_PAD][None], (n_b, 1, SHIFT_PAD))
    oa_p, rwkv_p = _rwkv_mixer(z_p, 0, n_b, seq // 256, zprev_p,
                               jnp.broadcast_to(s_rw_meta, (n_b,) + s_rw_meta.shape[1:]), rw_params,
                               tb=256, c=c_rw)
    shift_t = _pad_rows(jnp.transpose(state_shift[0]), SHIFT_PAD)
    mu_t = _pad_rows(col(rwkv_mu[0]), SHIFT_PAD)
    state_t = jnp.transpose(state_rwkv[0], (1, 2, 3, 0))
    oa_st, rwkv_st = _rwkv_sample_mixer(z_st, shift_t, mu_t, state_t, rw_params_t, n_tok=dec_seq, hpb=2)
    oa_s = jnp.transpose(oa_st, (1, 0, 2)).reshape(n_d, RW_WIDTH)
    rwkv_s = jnp.transpose(rwkv_st, (3, 0, 1, 2))

    past_len = 16384
    tabs_full = _retention_tables(RET_CHUNK, RET_CHUNK)
    tabs_meta = _retention_tables(RET_CHUNK, N_META)
    tabs_smp = _retention_tables(srow, dec_seq)
    cos_m, sin_m = _rotary_tables(jnp.arange(RET_CHUNK) - (RET_CHUNK - N_META))
    cos_p, sin_p = _rotary_tables(N_META + jnp.arange(seq))
    sb_rt = 4
    pos_s = past_len + jnp.tile(jnp.arange(srow) - (srow - dec_seq), sb_rt)
    cos_s, sin_s = _rotary_tables(pos_s)
    zero_rt = jnp.zeros((1, RET_HEADS, RET_HEAD, RET_HEAD), F32)
    _, s_rt_meta = _retention_mixer(z_meta, 0, 1, 1, cos_m, sin_m, tabs_meta, zero_rt,
                                    rows=RET_CHUNK, c=RET_CHUNK, n_real=N_META, per_chunk_state=False,
                                    pos_per_tile=False)
    ob_p, ret_p = _retention_mixer(z_p, 0, n_b, seq // RET_CHUNK, cos_p, sin_p, tabs_full,
                                   jnp.broadcast_to(s_rt_meta, (n_b,) + s_rt_meta.shape[1:]),
                                   rows=RET_CHUNK, c=RET_CHUNK, n_real=RET_CHUNK, per_chunk_state=False,
                                   pos_per_tile=True)
    ob_s, ret_s = _retention_mixer(z_sm, 0, n_s // sb_rt, 1, cos_s, sin_s, tabs_smp,
                                   state_ret.reshape(n_s, RET_HEADS, RET_HEAD, RET_HEAD),
                                   rows=sb_rt * srow, c=srow, n_real=dec_seq, per_chunk_state=True,
                                   pos_per_tile=False)

    g_ffn = row(norm_ffn[0])
    g_fin = row(norm_final)
    h_p, hn_p = _out_projection(oa_p, ob_p, w_out_b, x_p, g_ffn, tm=512)
    h_s, hn_s = _out_projection(oa_s, ob_s, w_out_b, x_s, g_ffn, tm=n_d)
    y_p = _ffn(hn_p, w_gate_b, w_up_b, w_down_b, h_p, g_fin, tm=512, tf=512)
    y_s = _ffn(hn_s, w_gate_b, w_up_b, w_down_b, h_s, g_fin, tm=n_d, tf=512)

    y_prompt = y_p.reshape(n_b, seq, d)
    y_sample = y_s.reshape(n_s, dec_seq, d)
    shift_p = z_p[seq - 1:n_p:seq, RW_OFF:RW_OFF + RW_COLS].reshape(1, n_b, RW_COLS)
    shift_s = jnp.transpose(z_st[:RW_COLS, (dec_seq - 1) * n_s:])[None]
    return (y_prompt, y_sample, shift_p, rwkv_p[None], ret_p[None], shift_s, rwkv_s[None], ret_s[None])
```

```python
import functools
import math

import jax
import jax.numpy as jnp
from jax import lax
from jax.experimental import pallas as pl
from jax.experimental.pallas import tpu as pltpu

F32 = jnp.float32
BF16 = jnp.bfloat16

D_MODEL = 2048
N_META = 16
RW_WIDTH = 1024
RW_HEAD = 64
RW_HEADS = 16
DECAY_LORA = 64
AAA_LORA = 64
GATE_LORA = 160
RW_COLS = 3 * RW_WIDTH + DECAY_LORA + AAA_LORA + GATE_LORA
RET_WIDTH = 1024
RET_HEADS = 4
RET_HEAD = 256
RET_CHUNK = 128
D_FF = 5632
RMS_EPS = 1e-6
RW_GN_EPS = 64e-5
RET_GN_EPS = 1e-6
ROPE_BASE = 10000.0
LANES = 128

LORA_PAD = 512
Z_COLS = 4 * RET_WIDTH + 3 * RW_WIDTH + LORA_PAD
RW_OFF = 4 * RET_WIDTH
SHIFT_PAD = 3 * RW_WIDTH + LORA_PAD

VMEM_LIMIT = 60 * 1024 * 1024


def _cparams(sem):
    return pltpu.CompilerParams(dimension_semantics=sem, vmem_limit_bytes=VMEM_LIMIT)


def _mm(a, b):
    return jnp.dot(a.astype(BF16), b.astype(BF16), preferred_element_type=F32)


def _mm_nt(a, b):
    return lax.dot_general(a.astype(BF16), b.astype(BF16), (((1,), (1,)), ((), ())),
                           preferred_element_type=F32)


def _mm_tn(a, b):
    return lax.dot_general(a.astype(BF16), b.astype(BF16), (((0,), (0,)), ((), ())),
                           preferred_element_type=F32)


def _bmm(a, b):
    return jnp.einsum("bmk,bkn->bmn", a.astype(BF16), b.astype(BF16), preferred_element_type=F32)


def _bmm_nt(a, b):
    return jnp.einsum("bmk,bnk->bmn", a.astype(BF16), b.astype(BF16), preferred_element_type=F32)


def _bmm_tn(a, b):
    return jnp.einsum("bkm,bkn->bmn", a.astype(BF16), b.astype(BF16), preferred_element_type=F32)


def _mm_exact_lhs(m_bf16, x):
    hi = x.astype(BF16)
    r1 = x - hi.astype(F32)
    mid = r1.astype(BF16)
    lo = (r1 - mid.astype(F32)).astype(BF16)
    d = functools.partial(jnp.dot, preferred_element_type=F32)
    return d(m_bf16, hi) + d(m_bf16, mid) + d(m_bf16, lo)


def _rms_norm_bf16(x, g):
    ms = jnp.mean(x * x, axis=-1, keepdims=True)
    return ((x * lax.rsqrt(ms + RMS_EPS)) * g).astype(BF16)


def _w_in_layout_kernel(w_ref, o_ref):
    o_ref[...] = w_ref[...].T.astype(BF16)


def _w_in_layout(w_in_t, tn):
    n, d = w_in_t.shape
    n_ret = 4 * RET_WIDTH // tn

    def src_row(c):
        align = math.gcd(RW_COLS, tn)
        return (pl.multiple_of(jnp.where(c < n_ret, RW_COLS + tn * c, tn * (c - n_ret)), align), 0)

    return pl.pallas_call(
        _w_in_layout_kernel,
        grid=(Z_COLS // tn,),
        in_specs=[pl.BlockSpec((pl.Element(tn), pl.Element(d)), src_row)],
        out_specs=pl.BlockSpec((d, tn), lambda c: (0, c)),
        out_shape=jax.ShapeDtypeStruct((d, Z_COLS), BF16),
        compiler_params=_cparams(("parallel",)),
        name="w_in_layout",
    )(w_in_t)


def _inproj_kernel(x_ref, g_ref, w_ref, o_ref, xn_ref):
    @pl.when(pl.program_id(1) == 0)
    def _():
        xn_ref[...] = _rms_norm_bf16(x_ref[...], g_ref[...])

    o_ref[...] = jnp.dot(xn_ref[...], w_ref[...], preferred_element_type=F32)


def _in_projection(x2d, norm_g, w_in_p, tm, tn):
    m = x2d.shape[0]
    return pl.pallas_call(
        _inproj_kernel,
        grid=(m // tm, Z_COLS // tn),
        in_specs=[
            pl.BlockSpec((tm, D_MODEL), lambda i, j: (i, 0)),
            pl.BlockSpec((1, D_MODEL), lambda i, j: (0, 0)),
            pl.BlockSpec((D_MODEL, tn), lambda i, j: (0, j)),
        ],
        out_specs=pl.BlockSpec((tm, tn), lambda i, j: (i, j)),
        out_shape=jax.ShapeDtypeStruct((m, Z_COLS), F32),
        scratch_shapes=[pltpu.VMEM((tm, D_MODEL), BF16)],
        compiler_params=_cparams(("parallel", "arbitrary")),
        name="in_projection",
    )(x2d, norm_g, w_in_p)


def _inproj_t_kernel(w_ref, x_ref, g_ref, o_ref, xn_ref):
    @pl.when(pl.program_id(0) == 0)
    def _():
        xn_ref[...] = _rms_norm_bf16(x_ref[...], g_ref[...])

    o_ref[...] = _mm_nt(w_ref[...], xn_ref[...])


def _in_projection_t(w_in_t, x2d, norm_g, n_rows, tm):
    m, d = x2d.shape
    return pl.pallas_call(
        _inproj_t_kernel,
        grid=(n_rows // tm,),
        in_specs=[
            pl.BlockSpec((tm, d), lambda i: (i, 0)),
            pl.BlockSpec((m, d), lambda i: (0, 0)),
            pl.BlockSpec((1, d), lambda i: (0, 0)),
        ],
        out_specs=pl.BlockSpec((tm, m), lambda i: (i, 0)),
        out_shape=jax.ShapeDtypeStruct((n_rows, m), F32),
        scratch_shapes=[pltpu.VMEM((m, d), BF16)],
        compiler_params=_cparams(("arbitrary",)),
        name="in_projection_t",
    )(w_in_t, x2d, norm_g)


def _rwkv_kernel(zr_ref, zk_ref, zv_ref, zl_ref, zprev0_ref, s0_ref,
                 mu_ref, w0_ref, w2_ref, a0_ref, a2_ref, g2_ref, kk_ref, ka_ref, rk_ref,
                 lnw_ref, lnb_ref,
                 o_ref, sout_ref, zlast_ref,
                 carry_ref, r_s, km_s, v_s, kkn_s, a_s, lw_s, cum_s, o_s,
                 *, tb, c, n_factors):
    i = pl.program_id(1)
    n_chunks = tb // c

    @pl.when(i == 0)
    def _():
        carry_ref[...] = zprev0_ref[0]
        sout_ref[...] = s0_ref[...]

    row = lax.broadcasted_iota(jnp.int32, (tb, 1), 0)

    def shifted(z_ref, lo, hi):
        z = z_ref[...]
        prev = jnp.where(row == 0, carry_ref[:, lo:hi], pltpu.roll(z, 1, axis=0))
        zs = z + mu_ref[:, lo:hi] * (prev - z)
        carry_ref[:, lo:hi] = z[tb - 1:tb, :]
        return zs

    w = RW_WIDTH
    r = shifted(zr_ref, 0, w)
    k = shifted(zk_ref, w, 2 * w)
    v = shifted(zv_ref, 2 * w, 3 * w)
    zl = shifted(zl_ref, 3 * w, 3 * w + LORA_PAD)

    lo2 = zl[:, 0:128]
    wlog = -jax.nn.softplus(-(w0_ref[...] + _mm(jnp.tanh(lo2), w2_ref[...]))) - 0.5
    lw = -jnp.exp(wlog)
    a = jax.nn.sigmoid(a0_ref[...] + _mm(lo2, a2_ref[...]))
    g = _mm(jax.nn.sigmoid(zl[:, 128:384]), g2_ref[...])
    kk = k * kk_ref[...]
    km = k * (1.0 + (a - 1.0) * ka_ref[...])

    ri = lax.broadcasted_iota(jnp.int32, (tb, tb), 0)
    ci = lax.broadcasted_iota(jnp.int32, (tb, tb), 1)
    tri = jnp.where((ri // c == ci // c) & (ri >= ci), 1.0, 0.0).astype(BF16)
    cum = _mm_exact_lhs(tri, lw)

    for h in range(RW_HEADS):
        sl = slice(h * RW_HEAD, (h + 1) * RW_HEAD)
        r_s[h] = r[:, sl]
        km_s[h] = km[:, sl]
        v_s[h] = v[:, sl]
        a_s[h] = a[:, sl]
        lw_s[h] = lw[:, sl]
        cum_s[h] = cum[:, sl]
        kkh = kk[:, sl]
        ss = jnp.sum(kkh * kkh, axis=-1, keepdims=True)
        kkn_s[h] = kkh / jnp.maximum(jnp.sqrt(ss), 1e-12)

    rr = lax.broadcasted_iota(jnp.int32, (c, c), 0)
    cc = lax.broadcasted_iota(jnp.int32, (c, c), 1)
    strict = (rr > cc)[None]
    incl = (rr >= cc)[None]
    eye = jnp.where(rr == cc, 1.0, 0.0).astype(F32)[None]
    rk = rk_ref[...]
    lnw = lnw_ref[...]
    lnb = lnb_ref[...]

    def body(ch, carry):
        rows = pl.ds(pl.multiple_of(ch * c, c), c)
        rh, kmh, vh, ah, lwh, cumh, kkn = [s[:, rows, :] for s in (r_s, km_s, v_s, a_s, lw_s, cum_s, kkn_s)]
        s_prev = sout_ref[0]
        bvec = kkn * ah
        cum_end = cumh[:, c - 1:c, :]
        p_inc = jnp.exp(cumh)
        p_exc = jnp.exp(cumh - lwh)
        p_inv = jnp.exp(-cumh)
        p_end = jnp.exp(cum_end)
        p_rel = jnp.exp(cum_end - cumh)
        a_t = -(kkn * p_exc)
        r_t = rh * p_inc
        b_t = bvec * p_inv
        k_t = kmh * p_inv
        b_h = bvec * p_rel
        k_h = kmh * p_rel

        ar = jnp.concatenate([a_t, r_t], axis=1)
        ab = _bmm_nt(ar, b_t)
        ak = _bmm_nt(ar, k_t)
        lmat = jnp.where(strict, ab[:, :c], 0.0)
        aak = jnp.where(strict, ak[:, :c], 0.0)
        arb = jnp.where(incl, ab[:, c:], 0.0)
        ark = jnp.where(incl, ak[:, c:], 0.0)
        tinv = eye + lmat
        lp = lmat
        for _ in range(n_factors - 1):
            lp = _bmm(lp, lp)
            tinv = tinv + _bmm(tinv, lp)
        as2 = _bmm_nt(ar, s_prev)
        x = as2[:, :c] + _bmm(aak, vh)
        u = _bmm(tinv, x)
        o = as2[:, c:] + _bmm(arb, u) + _bmm(ark, vh)
        sout_ref[0] = s_prev * p_end + _bmm_tn(jnp.concatenate([u, vh], axis=1),
                                               jnp.concatenate([b_h, k_h], axis=1))
        mean = jnp.mean(o, axis=-1, keepdims=True)
        var = jnp.mean(jnp.square(o - mean), axis=-1, keepdims=True)
        on = ((o - mean) * lax.rsqrt(var + RW_GN_EPS)) * lnw + lnb
        bonus = jnp.sum(rh * kmh * rk, axis=-1, keepdims=True) * vh
        o_s[:, rows, :] = on + bonus
        return carry

    lax.fori_loop(0, n_chunks, body, 0)

    o_full = jnp.concatenate([o_s[h] for h in range(RW_HEADS)], axis=-1)
    o_ref[...] = (o_full * g).astype(BF16)
    zlast_ref[0] = carry_ref[...]


def _rwkv_mixer(z, row_block0, n_groups, n_tiles, zprev0, s0, p, *, tb, c):
    n_factors = max(1, math.ceil(math.log2(c)))
    kern = functools.partial(_rwkv_kernel, tb=tb, c=c, n_factors=n_factors)
    cb = RW_OFF // RW_WIDTH
    full = lambda arr: pl.BlockSpec(arr.shape, lambda gi, i: (0,) * arr.ndim)
    lora_cb = (RW_OFF + 3 * RW_WIDTH) // LORA_PAD
    per_head = lambda a: a.reshape(RW_HEADS, 1, RW_HEAD)
    params = [p["mu"], p["w0"], p["w2"], p["a0"], p["a2"], p["g2"], p["kk"], p["ka"],
              per_head(p["rk"]), per_head(p["lnw"]), per_head(p["lnb"])]
    hm = lambda: pltpu.VMEM((RW_HEADS, tb, RW_HEAD), F32)
    zspec = lambda width, col: pl.BlockSpec((tb, width), lambda gi, i: (row_block0 + gi * n_tiles + i, col))
    sspec = pl.BlockSpec((1, RW_HEADS, RW_HEAD, RW_HEAD), lambda gi, i: (gi, 0, 0, 0))
    return pl.pallas_call(
        kern,
        grid=(n_groups, n_tiles),
        in_specs=[zspec(RW_WIDTH, cb), zspec(RW_WIDTH, cb + 1), zspec(RW_WIDTH, cb + 2),
                  zspec(LORA_PAD, lora_cb),
                  pl.BlockSpec((1, 1, SHIFT_PAD), lambda gi, i: (gi, 0, 0)),
                  sspec]
                 + [full(a) for a in params],
        out_specs=[pl.BlockSpec((tb, RW_WIDTH), lambda gi, i: (gi * n_tiles + i, 0)), sspec,
                   pl.BlockSpec((1, 1, SHIFT_PAD), lambda gi, i: (gi, 0, 0))],
        out_shape=[jax.ShapeDtypeStruct((n_groups * n_tiles * tb, RW_WIDTH), BF16),
                   jax.ShapeDtypeStruct((n_groups, RW_HEADS, RW_HEAD, RW_HEAD), F32),
                   jax.ShapeDtypeStruct((n_groups, 1, SHIFT_PAD), F32)],
        scratch_shapes=[pltpu.VMEM((1, SHIFT_PAD), F32)] + [hm() for _ in range(8)],
        compiler_params=_cparams(("parallel", "arbitrary")),
        name="rwkv7_mixer",
    )(z, z, z, z, zprev0, s0, *params)


def _rwkv_sample_kernel(zr_ref, zk_ref, zv_ref, zl_ref, shr_ref, shk_ref, shv_ref, shl_ref,
                        mur_ref, muk_ref, muv_ref, mul_ref, s0_ref,
                        w0_ref, w2t_ref, a0_ref, a2t_ref, g2t_ref, kk_ref, ka_ref, rk_ref,
                        lnw_ref, lnb_ref,
                        o_ref, sout_ref,
                        dec_s, a_s, b_s, k_s, r_s, v_s, o_s,
                        *, n_tok, hpb):
    ns = LANES
    ch = hpb * RW_HEAD

    def lanes(x):
        return jnp.concatenate([x] * n_tok, axis=1)

    def shifted(z_ref, sh_ref, mu_ref):
        z = z_ref[...]
        prev = jnp.concatenate([sh_ref[...], z[:, :(n_tok - 1) * ns]], axis=1)
        return z + lanes(mu_ref[...]) * (prev - z)

    r = shifted(zr_ref, shr_ref, mur_ref)
    k = shifted(zk_ref, shk_ref, muk_ref)
    v = shifted(zv_ref, shv_ref, muv_ref)
    zl = shifted(zl_ref, shl_ref, mul_ref)
    wd = zl[0:DECAY_LORA]
    ad = zl[DECAY_LORA:DECAY_LORA + AAA_LORA]
    gd = zl[DECAY_LORA + AAA_LORA:DECAY_LORA + AAA_LORA + GATE_LORA]

    wlog = -jax.nn.softplus(-(lanes(w0_ref[...]) + _mm(w2t_ref[...], jnp.tanh(wd)))) - 0.5
    lw = -jnp.exp(wlog)
    a = jax.nn.sigmoid(lanes(a0_ref[...]) + _mm(a2t_ref[...], ad))
    g = _mm(g2t_ref[...], jax.nn.sigmoid(gd))
    kk = k * lanes(kk_ref[...])
    km = k * (1.0 + (a - 1.0) * lanes(ka_ref[...]))

    def head_sum(x):
        x3 = x.reshape(hpb, RW_HEAD, n_tok * ns)
        s = jnp.sum(x3, axis=1, keepdims=True)
        return jnp.broadcast_to(s, x3.shape).reshape(ch, n_tok * ns)

    kkn = kk / jnp.maximum(jnp.sqrt(head_sum(kk * kk)), 1e-12)
    dec_s[...] = jnp.exp(lw)
    a_s[...] = -kkn
    b_s[...] = kkn * a
    k_s[...] = km
    r_s[...] = r
    v_s[...] = v

    for hh in range(hpb):
        hrows = slice(hh * RW_HEAD, (hh + 1) * RW_HEAD)

        def body(i8, carry, hh=hh, hrows=hrows):
            base = pl.multiple_of(i8 * 8, 8)
            rows8 = pl.ds(pl.multiple_of(hh * RW_HEAD + base, 8), 8)
            o_rows = [[] for _ in range(n_tok)]
            for j in range(8):
                s = s0_ref[hh, base + j]
                for t in range(n_tok):
                    tl = slice(t * ns, (t + 1) * ns)
                    sa = jnp.sum(s * a_s[hrows, tl], axis=0, keepdims=True)
                    v_row = v_s[rows8, tl][j:j + 1, :]
                    s = s * dec_s[hrows, tl] + sa * b_s[hrows, tl] + v_row * k_s[hrows, tl]
                    o_rows[t].append(jnp.sum(s * r_s[hrows, tl], axis=0, keepdims=True))
                sout_ref[hh, base + j] = s
            for t in range(n_tok):
                o_s[rows8, t * ns:(t + 1) * ns] = jnp.concatenate(o_rows[t], axis=0)
            return carry

        lax.fori_loop(0, RW_HEAD // 8, body, 0)

    o = o_s[...]
    inv_n = 1.0 / RW_HEAD
    mean = head_sum(o) * inv_n
    var = head_sum(jnp.square(o - mean)) * inv_n
    on = ((o - mean) * lax.rsqrt(var + RW_GN_EPS)) * lanes(lnw_ref[...]) + lanes(lnb_ref[...])
    bonus = head_sum(r * km * lanes(rk_ref[...])) * v
    out = (on + bonus) * g
    for t in range(n_tok):
        o_ref[t] = out[:, t * ns:(t + 1) * ns].T.astype(BF16)


def _rwkv_sample_mixer(z_t, shift_t, mu_t, state_t, p_t, *, n_tok, hpb):
    ch = hpb * RW_HEAD
    n_steps = RW_HEADS // hpb
    nl = n_tok * LANES
    seg = RW_WIDTH // ch
    zspec = lambda s: pl.BlockSpec((ch, nl), lambda h: (s * seg + h, 0))
    cspec = lambda s: pl.BlockSpec((ch, LANES), lambda h: (s * seg + h, 0))
    lora_blk = 3 * RW_WIDTH // LORA_PAD
    zl_spec = pl.BlockSpec((LORA_PAD, nl), lambda h: (lora_blk, 0))
    cl_spec = pl.BlockSpec((LORA_PAD, LANES), lambda h: (lora_blk, 0))
    hspec = pl.BlockSpec((ch, LANES), lambda h: (h, 0))
    wspec = lambda k: pl.BlockSpec((ch, k), lambda h: (h, 0))
    sspec = pl.BlockSpec((hpb, RW_HEAD, RW_HEAD, LANES), lambda h: (h, 0, 0, 0))
    buf = lambda: pltpu.VMEM((ch, nl), F32)
    return pl.pallas_call(
        functools.partial(_rwkv_sample_kernel, n_tok=n_tok, hpb=hpb),
        grid=(n_steps,),
        in_specs=[zspec(0), zspec(1), zspec(2), zl_spec,
                  cspec(0), cspec(1), cspec(2), cl_spec,
                  cspec(0), cspec(1), cspec(2), cl_spec,
                  sspec,
                  hspec, wspec(DECAY_LORA), hspec, wspec(AAA_LORA), wspec(GATE_LORA),
                  hspec, hspec, hspec, hspec, hspec],
        out_specs=[pl.BlockSpec((n_tok, LANES, ch), lambda h: (0, 0, h)), sspec],
        out_shape=[jax.ShapeDtypeStruct((n_tok, LANES, RW_WIDTH), BF16),
                   jax.ShapeDtypeStruct(state_t.shape, F32)],
        scratch_shapes=[buf() for _ in range(7)],
        compiler_params=_cparams(("parallel",)),
        name="rwkv7_sample",
    )(z_t, z_t, z_t, z_t, shift_t, shift_t, shift_t, shift_t, mu_t, mu_t, mu_t, mu_t, state_t,
      p_t["w0"], p_t["w2t"], p_t["a0"], p_t["a2t"], p_t["g2t"], p_t["kk"], p_t["ka"], p_t["rk"],
      p_t["lnw"], p_t["lnb"])


def _retention_kernel(zq_ref, zk_ref, zv_ref, zg_ref, cos_ref, sin_ref, dmask_ref, iscale_ref,
                      kscale_ref, sdec_ref, s0_ref, o_ref, sout_ref,
                      *, rows, c, n_real, per_chunk_state):
    i = pl.program_id(1)
    n_chunks = rows // c

    if per_chunk_state:
        sout_ref[...] = s0_ref[...]
        pr = lax.broadcasted_iota(jnp.int32, (rows, n_chunks * n_real), 0)
        pc = lax.broadcasted_iota(jnp.int32, (rows, n_chunks * n_real), 1)
        place = jnp.where((pr // c == pc // n_real) & (pr % c - (c - n_real) == pc % n_real),
                          1.0, 0.0).astype(BF16)
        load = lambda ref: _mm_exact_lhs(place, ref[...])
    else:
        load = lambda ref: ref[...]

        @pl.when(i == 0)
        def _():
            sout_ref[...] = s0_ref[...]

    lane = lax.broadcasted_iota(jnp.int32, (rows, RET_WIDTH), 1)
    even = (lane % 2) == 0
    cos = jnp.concatenate([cos_ref[...]] * RET_HEADS, axis=-1)
    sin = jnp.concatenate([sin_ref[...]] * RET_HEADS, axis=-1)

    def rot(x):
        partner = jnp.where(even, pltpu.roll(x, RET_WIDTH - 1, axis=1), pltpu.roll(x, 1, axis=1))
        return x * cos + partner * sin

    q = rot(load(zq_ref))
    k = rot(load(zk_ref)) * (RET_HEAD ** -0.5)
    v = load(zv_ref)
    g = load(zg_ref)

    out_rows = []
    for ch in range(n_chunks):
        rs = slice(ch * c, (ch + 1) * c)
        sidx = ch if per_chunk_state else 0
        out_heads = []
        for h in range(RET_HEADS):
            hs = slice(h * RET_HEAD, (h + 1) * RET_HEAD)
            qh, kh, vh = q[rs, hs], k[rs, hs], v[rs, hs]
            s_prev = sout_ref[sidx, h]
            scores = _mm_nt(qh, kh) * dmask_ref[h]
            o = _mm(scores, vh) + _mm(qh, s_prev) * iscale_ref[h]
            sout_ref[sidx, h] = s_prev * sdec_ref[h] + _mm_tn(kh * kscale_ref[h], vh)
            o = o * lax.rsqrt(jnp.mean(o * o, axis=-1, keepdims=True) + RET_GN_EPS)
            gh = g[rs, hs]
            out_heads.append(o * (gh * jax.nn.sigmoid(gh)))
        out_rows.append(jnp.concatenate(out_heads, axis=-1))
    o_out = jnp.concatenate(out_rows, axis=0).astype(BF16)
    if per_chunk_state:
        o_out = lax.dot_general(place, o_out, (((0,), (0,)), ((), ())),
                                preferred_element_type=F32).astype(BF16)
    o_ref[...] = o_out


def _retention_mixer(z, row_block0, n_groups, n_tiles, cos, sin, tabs, s0, *, rows, c, n_real,
                     per_chunk_state, pos_per_tile):
    n_states = s0.shape[0]
    sb = n_states // n_groups
    kern = functools.partial(_retention_kernel, rows=rows, c=c, n_real=n_real,
                             per_chunk_state=per_chunk_state)
    rows_io = rows // c * n_real if per_chunk_state else rows
    zspec = lambda col: pl.BlockSpec((rows_io, RET_WIDTH), lambda gi, i: (row_block0 + gi * n_tiles + i, col))
    full = lambda arr: pl.BlockSpec(arr.shape, lambda gi, i: (0,) * arr.ndim)
    if pos_per_tile:
        tspec = pl.BlockSpec((rows, RET_HEAD), lambda gi, i: (i, 0))
    else:
        tspec = pl.BlockSpec((rows, RET_HEAD), lambda gi, i: (0, 0))
    sspec = pl.BlockSpec((sb, RET_HEADS, RET_HEAD, RET_HEAD), lambda gi, i: (gi, 0, 0, 0))
    dmask, iscale, kscale, sdec = tabs
    return pl.pallas_call(
        kern,
        grid=(n_groups, n_tiles),
        in_specs=[zspec(0), zspec(1), zspec(2), zspec(3), tspec, tspec,
                  full(dmask), full(iscale), full(kscale), full(sdec), sspec],
        out_specs=[pl.BlockSpec((rows_io, RET_WIDTH), lambda gi, i: (gi * n_tiles + i, 0)), sspec],
        out_shape=[jax.ShapeDtypeStruct((n_groups * n_tiles * rows_io, RET_WIDTH), BF16),
                   jax.ShapeDtypeStruct(s0.shape, F32)],
        compiler_params=_cparams(("parallel", "arbitrary")),
        name="retention_mixer",
    )(z, z, z, z, cos, sin, dmask, iscale, kscale, sdec, s0)


def _retention_tables(c, n_real):
    log_gamma = jnp.log(1.0 - 2.0 ** (-5.0 - jnp.arange(RET_HEADS, dtype=F32)))
    r = jnp.arange(c, dtype=F32)
    idx = r - float(c - n_real)
    diff = r[:, None] - r[None, :]
    dmask = jnp.where(diff[None] >= 0,
                      jnp.exp(log_gamma[:, None, None] * jnp.maximum(diff, 0.0)[None]), 0.0)
    iscale = jnp.exp(log_gamma[:, None] * (idx + 1.0)[None, :])[:, :, None]
    kscale = jnp.exp(log_gamma[:, None] * (n_real - 1.0 - idx)[None, :])[:, :, None]
    sdec = jnp.broadcast_to(jnp.exp(log_gamma * n_real)[:, None, None], (RET_HEADS, 1, RET_HEAD))
    return dmask, iscale, kscale, sdec


def _rotary_tables(pos):
    inv_freq = 1.0 / (ROPE_BASE ** jnp.linspace(0.0, 1.0, RET_HEAD // 2, dtype=F32))
    ang = pos.astype(F32)[:, None] * inv_freq[None, :]
    cos = jnp.cos(ang)
    sin = jnp.sin(ang)
    cos2 = jnp.repeat(cos, 2, axis=-1)
    sin2 = jnp.stack([-sin, sin], axis=-1).reshape(pos.shape[0], RET_HEAD)
    return cos2, sin2


def _outproj_kernel(oa_ref, ob_ref, w_ref, x_ref, g_ref, h_ref, hn_ref):
    acc = jnp.dot(oa_ref[...], w_ref[0:RW_WIDTH, :], preferred_element_type=F32)
    acc = acc + jnp.dot(ob_ref[...], w_ref[RW_WIDTH:, :], preferred_element_type=F32)
    h = x_ref[...] + acc
    h_ref[...] = h
    hn_ref[...] = _rms_norm_bf16(h, g_ref[...])


def _out_projection(o_a, o_b, w_out, x2d, norm_g, tm):
    m = o_a.shape[0]
    return pl.pallas_call(
        _outproj_kernel,
        grid=(m // tm,),
        in_specs=[
            pl.BlockSpec((tm, RW_WIDTH), lambda i: (i, 0)),
            pl.BlockSpec((tm, RET_WIDTH), lambda i: (i, 0)),
            pl.BlockSpec((D_MODEL, D_MODEL), lambda i: (0, 0)),
            pl.BlockSpec((tm, D_MODEL), lambda i: (i, 0)),
            pl.BlockSpec((1, D_MODEL), lambda i: (0, 0)),
        ],
        out_specs=[pl.BlockSpec((tm, D_MODEL), lambda i: (i, 0)),
                   pl.BlockSpec((tm, D_MODEL), lambda i: (i, 0))],
        out_shape=[jax.ShapeDtypeStruct((m, D_MODEL), F32), jax.ShapeDtypeStruct((m, D_MODEL), BF16)],
        compiler_params=_cparams(("parallel",)),
        name="out_projection",
    )(o_a, o_b, w_out, x2d, norm_g)


def _ffn_kernel(hn_ref, wg_ref, wu_ref, wd_ref, h_ref, g_ref, y_ref, acc_ref):
    f = pl.program_id(1)

    @pl.when(f == 0)
    def _():
        acc_ref[...] = jnp.zeros_like(acc_ref)

    hn = hn_ref[...]
    gate = jnp.dot(hn, wg_ref[...], preferred_element_type=F32)
    up = jnp.dot(hn, wu_ref[...], preferred_element_type=F32)
    act = (gate * jax.nn.sigmoid(gate)) * up
    acc_ref[...] += jnp.dot(act.astype(BF16), wd_ref[...], preferred_element_type=F32)

    @pl.when(f == pl.num_programs(1) - 1)
    def _():
        h = h_ref[...] + acc_ref[...]
        ms = jnp.mean(h * h, axis=-1, keepdims=True)
        y_ref[...] = (h * lax.rsqrt(ms + RMS_EPS)) * g_ref[...]


def _ffn(hn, w_gate, w_up, w_down, h, norm_g, tm, tf):
    m = hn.shape[0]
    once = pl.Buffered(1)
    return pl.pallas_call(
        _ffn_kernel,
        grid=(m // tm, D_FF // tf),
        in_specs=[
            pl.BlockSpec((tm, D_MODEL), lambda i, f: (i, 0)),
            pl.BlockSpec((D_MODEL, tf), lambda i, f: (0, f)),
            pl.BlockSpec((D_MODEL, tf), lambda i, f: (0, f)),
            pl.BlockSpec((tf, D_MODEL), lambda i, f: (f, 0)),
            pl.BlockSpec((tm, D_MODEL), lambda i, f: (i, 0), pipeline_mode=once),
            pl.BlockSpec((1, D_MODEL), lambda i, f: (0, 0)),
        ],
        out_specs=pl.BlockSpec((tm, D_MODEL), lambda i, f: (i, 0), pipeline_mode=once),
        out_shape=jax.ShapeDtypeStruct((m, D_MODEL), F32),
        scratch_shapes=[pltpu.VMEM((tm, D_MODEL), F32)],
        compiler_params=_cparams(("parallel", "arbitrary")),
        name="swiglu_ffn",
    )(hn, w_gate, w_up, w_down, h, norm_g)


def _pad_cols(a, n):
    return jnp.pad(a, ((0, 0), (0, n - a.shape[1])))


def _pad_rows(a, n):
    return jnp.pad(a, ((0, n - a.shape[0]), (0, 0)))


def kernel(x_prompt, x_sample, state_shift, state_rwkv, state_ret, meta_tokens, norm_mix, w_in,
           rwkv_mu, rwkv_w0, rwkv_w2, rwkv_a0, rwkv_a2, rwkv_g2, rwkv_kk, rwkv_ka, rwkv_rk,
           rwkv_ln_w, rwkv_ln_b, w_out, norm_ffn, w_gate, w_up, w_down, norm_final):
    n_b, seq, d = x_prompt.shape
    n_s, dec_seq, _ = x_sample.shape
    n_p = n_b * seq
    n_d = n_s * dec_seq
    depth = w_in.shape[0]
    assert depth == 1 and d == D_MODEL and n_s == LANES

    w_in_t = jnp.transpose(w_in[0])
    w_in_p = _w_in_layout(w_in_t, tn=512)
    w_out_b = w_out[0].astype(BF16)
    w_gate_b = w_gate[0].astype(BF16)
    w_up_b = w_up[0].astype(BF16)
    w_down_b = w_down[0].astype(BF16)
    row = lambda a: a.reshape(1, -1).astype(F32)
    w2p = jnp.concatenate([rwkv_w2[0], jnp.zeros((128 - DECAY_LORA, RW_WIDTH), F32)], axis=0)
    a2p = jnp.concatenate([jnp.zeros((DECAY_LORA, RW_WIDTH), F32), rwkv_a2[0]], axis=0)
    g2p = jnp.concatenate([rwkv_g2[0], jnp.zeros((256 - GATE_LORA, RW_WIDTH), F32)], axis=0)
    rw_params = dict(mu=_pad_cols(row(rwkv_mu[0]), SHIFT_PAD), w0=row(rwkv_w0[0]), w2=w2p,
                     a0=row(rwkv_a0[0]), a2=a2p, g2=g2p, kk=row(rwkv_kk[0]), ka=row(rwkv_ka[0]),
                     rk=row(rwkv_rk[0]), lnw=row(rwkv_ln_w[0]), lnb=row(rwkv_ln_b[0]))
    col = lambda a: jnp.broadcast_to(a.reshape(-1, 1).astype(F32), (a.size, LANES))
    rw_params_t = dict(w0=col(rwkv_w0[0]), w2t=rwkv_w2[0].T, a0=col(rwkv_a0[0]), a2t=rwkv_a2[0].T,
                       g2t=rwkv_g2[0].T, kk=col(rwkv_kk[0]), ka=col(rwkv_ka[0]), rk=col(rwkv_rk[0]),
                       lnw=col(rwkv_ln_w[0]), lnb=col(rwkv_ln_b[0]))

    x_p = x_prompt.reshape(n_p, d)
    x_s = x_sample.reshape(n_d, d)
    x_sm = jnp.concatenate([x_s, meta_tokens.astype(F32)], axis=0)
    x_ts = jnp.transpose(x_sample, (1, 0, 2)).reshape(n_d, d)
    g_mix = row(norm_mix[0])
    z_p = _in_projection(x_p, g_mix, w_in_p, tm=1024, tn=1536)
    z_sm = _in_projection(x_sm, g_mix, w_in_p, tm=x_sm.shape[0], tn=768)
    z_st = _in_projection_t(w_in_t, x_ts, g_mix, SHIFT_PAD, tm=512)

    z_meta = jnp.pad(z_sm[n_d:], ((RET_CHUNK - N_META, 0), (0, 0)))
    srow = 2 * dec_seq

    zero_prev = jnp.zeros((1, 1, SHIFT_PAD), F32)
    zero_rw = jnp.zeros((1, RW_HEADS, RW_HEAD, RW_HEAD), F32)
    c_rw = 64
    _, s_rw_meta, zlast_meta = _rwkv_mixer(z_meta, 1, 1, 1, zero_prev, zero_rw, rw_params, tb=c_rw, c=c_rw)
    oa_p, rwkv_p, zlast_p = _rwkv_mixer(z_p, 0, n_b, seq // 256,
                                        jnp.broadcast_to(zlast_meta, (n_b, 1, SHIFT_PAD)),
                                        jnp.broadcast_to(s_rw_meta, (n_b,) + s_rw_meta.shape[1:]),
                                        rw_params, tb=256, c=c_rw)
    shift_t = _pad_rows(jnp.transpose(state_shift[0]), SHIFT_PAD)
    mu_t = _pad_rows(col(rwkv_mu[0]), SHIFT_PAD)
    state_t = jnp.transpose(state_rwkv[0], (1, 2, 3, 0))
    oa_st, rwkv_st = _rwkv_sample_mixer(z_st, shift_t, mu_t, state_t, rw_params_t, n_tok=dec_seq, hpb=2)
    oa_s = jnp.transpose(oa_st, (1, 0, 2)).reshape(n_d, RW_WIDTH)
    rwkv_s = jnp.transpose(rwkv_st, (3, 0, 1, 2))

    past_len = 16384
    tabs_full = _retention_tables(RET_CHUNK, RET_CHUNK)
    tabs_meta = _retention_tables(RET_CHUNK, N_META)
    tabs_smp = _retention_tables(srow, dec_seq)
    cos_m, sin_m = _rotary_tables(jnp.arange(RET_CHUNK) - (RET_CHUNK - N_META))
    cos_p, sin_p = _rotary_tables(N_META + jnp.arange(seq))
    sb_rt = 4
    pos_s = past_len + jnp.tile(jnp.arange(srow) - (srow - dec_seq), sb_rt)
    cos_s, sin_s = _rotary_tables(pos_s)
    zero_rt = jnp.zeros((1, RET_HEADS, RET_HEAD, RET_HEAD), F32)
    _, s_rt_meta = _retention_mixer(z_meta, 0, 1, 1, cos_m, sin_m, tabs_meta, zero_rt,
                                    rows=RET_CHUNK, c=RET_CHUNK, n_real=N_META, per_chunk_state=False,
                                    pos_per_tile=False)
    ob_p, ret_p = _retention_mixer(z_p, 0, n_b, seq // RET_CHUNK, cos_p, sin_p, tabs_full,
                                   jnp.broadcast_to(s_rt_meta, (n_b,) + s_rt_meta.shape[1:]),
                                   rows=RET_CHUNK, c=RET_CHUNK, n_real=RET_CHUNK, per_chunk_state=False,
                                   pos_per_tile=True)
    ob_s, ret_s = _retention_mixer(z_sm, 0, n_s // sb_rt, 1, cos_s, sin_s, tabs_smp,
                                   state_ret.reshape(n_s, RET_HEADS, RET_HEAD, RET_HEAD),
                                   rows=sb_rt * srow, c=srow, n_real=dec_seq, per_chunk_state=True,
                                   pos_per_tile=False)

    g_ffn = row(norm_ffn[0])
    g_fin = row(norm_final)
    h_p, hn_p = _out_projection(oa_p, ob_p, w_out_b, x_p, g_ffn, tm=512)
    h_s, hn_s = _out_projection(oa_s, ob_s, w_out_b, x_s, g_ffn, tm=n_d)
    y_p = _ffn(hn_p, w_gate_b, w_up_b, w_down_b, h_p, g_fin, tm=1024, tf=512)
    y_s = _ffn(hn_s, w_gate_b, w_up_b, w_down_b, h_s, g_fin, tm=n_d, tf=512)

    y_prompt = y_p.reshape(n_b, seq, d)
    y_sample = y_s.reshape(n_s, dec_seq, d)
    shift_p = zlast_p[:, 0, :RW_COLS][None]
    shift_s = jnp.transpose(z_st[:RW_COLS, (dec_seq - 1) * n_s:])[None]
    return (y_prompt, y_sample, shift_p, rwkv_p[None], ret_p[None], shift_s, rwkv_s[None], ret_s[None])
```

```python
import functools
import math

import jax
import jax.numpy as jnp
from jax import lax
from jax.experimental import pallas as pl
from jax.experimental.pallas import tpu as pltpu

F32 = jnp.float32
BF16 = jnp.bfloat16

D_MODEL = 2048
N_META = 16
RW_WIDTH = 1024
RW_HEAD = 64
RW_HEADS = 16
DECAY_LORA = 64
AAA_LORA = 64
GATE_LORA = 160
RW_COLS = 3 * RW_WIDTH + DECAY_LORA + AAA_LORA + GATE_LORA
RET_WIDTH = 1024
RET_HEADS = 4
RET_HEAD = 256
RET_CHUNK = 128
D_FF = 5632
RMS_EPS = 1e-6
RW_GN_EPS = 64e-5
RET_GN_EPS = 1e-6
ROPE_BASE = 10000.0
LANES = 128

LORA_PAD = 512
Z_COLS = 4 * RET_WIDTH + 3 * RW_WIDTH + LORA_PAD
RW_OFF = 4 * RET_WIDTH
SHIFT_PAD = 3 * RW_WIDTH + LORA_PAD

VMEM_LIMIT = 60 * 1024 * 1024


def _cparams(sem):
    return pltpu.CompilerParams(dimension_semantics=sem, vmem_limit_bytes=VMEM_LIMIT)


def _mm(a, b):
    return jnp.dot(a.astype(BF16), b.astype(BF16), preferred_element_type=F32)


def _mm_nt(a, b):
    return lax.dot_general(a.astype(BF16), b.astype(BF16), (((1,), (1,)), ((), ())),
                           preferred_element_type=F32)


def _mm_tn(a, b):
    return lax.dot_general(a.astype(BF16), b.astype(BF16), (((0,), (0,)), ((), ())),
                           preferred_element_type=F32)


def _bmm(a, b):
    return jnp.einsum("bmk,bkn->bmn", a.astype(BF16), b.astype(BF16), preferred_element_type=F32)


def _bmm_nt(a, b):
    return jnp.einsum("bmk,bnk->bmn", a.astype(BF16), b.astype(BF16), preferred_element_type=F32)


def _bmm_tn(a, b):
    return jnp.einsum("bkm,bkn->bmn", a.astype(BF16), b.astype(BF16), preferred_element_type=F32)


def _mm_exact_lhs(m_bf16, x):
    hi = x.astype(BF16)
    r1 = x - hi.astype(F32)
    mid = r1.astype(BF16)
    lo = (r1 - mid.astype(F32)).astype(BF16)
    d = functools.partial(jnp.dot, preferred_element_type=F32)
    return d(m_bf16, hi) + d(m_bf16, mid) + d(m_bf16, lo)


def _rms_norm_bf16(x, g):
    ms = jnp.mean(x * x, axis=-1, keepdims=True)
    return ((x * lax.rsqrt(ms + RMS_EPS)) * g).astype(BF16)


def _w_in_layout_kernel(w_ref, o_ref):
    o_ref[...] = w_ref[...].T.astype(BF16)


def _w_in_layout(w_in_t, tn):
    n, d = w_in_t.shape
    n_ret = 4 * RET_WIDTH // tn

    def src_row(c):
        align = math.gcd(RW_COLS, tn)
        return (pl.multiple_of(jnp.where(c < n_ret, RW_COLS + tn * c, tn * (c - n_ret)), align), 0)

    return pl.pallas_call(
        _w_in_layout_kernel,
        grid=(Z_COLS // tn,),
        in_specs=[pl.BlockSpec((pl.Element(tn), pl.Element(d)), src_row)],
        out_specs=pl.BlockSpec((d, tn), lambda c: (0, c)),
        out_shape=jax.ShapeDtypeStruct((d, Z_COLS), BF16),
        compiler_params=_cparams(("parallel",)),
        name="w_in_layout",
    )(w_in_t)


def _inproj_kernel(x_ref, g_ref, w_ref, o_ref, xn_ref):
    @pl.when(pl.program_id(1) == 0)
    def _():
        xn_ref[...] = _rms_norm_bf16(x_ref[...], g_ref[...])

    o_ref[...] = jnp.dot(xn_ref[...], w_ref[...], preferred_element_type=F32)


def _in_projection(x2d, norm_g, w_in_p, tm, tn):
    m = x2d.shape[0]
    return pl.pallas_call(
        _inproj_kernel,
        grid=(m // tm, Z_COLS // tn),
        in_specs=[
            pl.BlockSpec((tm, D_MODEL), lambda i, j: (i, 0)),
            pl.BlockSpec((1, D_MODEL), lambda i, j: (0, 0)),
            pl.BlockSpec((D_MODEL, tn), lambda i, j: (0, j)),
        ],
        out_specs=pl.BlockSpec((tm, tn), lambda i, j: (i, j)),
        out_shape=jax.ShapeDtypeStruct((m, Z_COLS), F32),
        scratch_shapes=[pltpu.VMEM((tm, D_MODEL), BF16)],
        compiler_params=_cparams(("parallel", "arbitrary")),
        name="in_projection",
    )(x2d, norm_g, w_in_p)


def _inproj_t_kernel(w_ref, x_ref, g_ref, o_ref, xn_ref):
    @pl.when(pl.program_id(0) == 0)
    def _():
        xn_ref[...] = _rms_norm_bf16(x_ref[...], g_ref[...])

    o_ref[...] = _mm_nt(w_ref[...], xn_ref[...])


def _in_projection_t(w_in_t, x2d, norm_g, n_rows, tm):
    m, d = x2d.shape
    return pl.pallas_call(
        _inproj_t_kernel,
        grid=(n_rows // tm,),
        in_specs=[
            pl.BlockSpec((tm, d), lambda i: (i, 0)),
            pl.BlockSpec((m, d), lambda i: (0, 0)),
            pl.BlockSpec((1, d), lambda i: (0, 0)),
        ],
        out_specs=pl.BlockSpec((tm, m), lambda i: (i, 0)),
        out_shape=jax.ShapeDtypeStruct((n_rows, m), F32),
        scratch_shapes=[pltpu.VMEM((m, d), BF16)],
        compiler_params=_cparams(("arbitrary",)),
        name="in_projection_t",
    )(w_in_t, x2d, norm_g)


def _rwkv_kernel(zr_ref, zk_ref, zv_ref, zl_ref, zprev0_ref, s0_ref,
                 mu_ref, w0_ref, w2_ref, a0_ref, a2_ref, g2_ref, kk_ref, ka_ref, rk_ref,
                 lnw_ref, lnb_ref,
                 o_ref, sout_ref, zlast_ref,
                 carry_ref, r_s, km_s, v_s, kkn_s, a_s, lw_s, cum_s, o_s,
                 *, tb, c, n_factors):
    i = pl.program_id(1)
    n_chunks = tb // c

    @pl.when(i == 0)
    def _():
        carry_ref[...] = zprev0_ref[0]
        sout_ref[...] = s0_ref[...]

    row = lax.broadcasted_iota(jnp.int32, (tb, 1), 0)

    def shifted(z_ref, lo, hi):
        z = z_ref[...]
        prev = jnp.where(row == 0, carry_ref[:, lo:hi], pltpu.roll(z, 1, axis=0))
        zs = z + mu_ref[:, lo:hi] * (prev - z)
        carry_ref[:, lo:hi] = z[tb - 1:tb, :]
        return zs

    w = RW_WIDTH
    r = shifted(zr_ref, 0, w)
    k = shifted(zk_ref, w, 2 * w)
    v = shifted(zv_ref, 2 * w, 3 * w)
    zl = shifted(zl_ref, 3 * w, 3 * w + LORA_PAD)

    lo2 = zl[:, 0:128]
    wlog = -jax.nn.softplus(-(w0_ref[...] + _mm(jnp.tanh(lo2), w2_ref[...]))) - 0.5
    lw = -jnp.exp(wlog)
    a = jax.nn.sigmoid(a0_ref[...] + _mm(lo2, a2_ref[...]))
    g = _mm(jax.nn.sigmoid(zl[:, 128:384]), g2_ref[...])
    kk = k * kk_ref[...]
    km = k * (1.0 + (a - 1.0) * ka_ref[...])

    ri = lax.broadcasted_iota(jnp.int32, (tb, tb), 0)
    ci = lax.broadcasted_iota(jnp.int32, (tb, tb), 1)
    tri = jnp.where((ri // c == ci // c) & (ri >= ci), 1.0, 0.0).astype(BF16)
    cum = _mm_exact_lhs(tri, lw)

    for h in range(RW_HEADS):
        sl = slice(h * RW_HEAD, (h + 1) * RW_HEAD)
        r_s[h] = r[:, sl]
        km_s[h] = km[:, sl]
        v_s[h] = v[:, sl]
        a_s[h] = a[:, sl]
        lw_s[h] = lw[:, sl]
        cum_s[h] = cum[:, sl]
        kkh = kk[:, sl]
        ss = jnp.sum(kkh * kkh, axis=-1, keepdims=True)
        kkn_s[h] = kkh * lax.rsqrt(jnp.maximum(ss, 1e-24))

    rr = lax.broadcasted_iota(jnp.int32, (c, c), 0)
    cc = lax.broadcasted_iota(jnp.int32, (c, c), 1)
    strict = (rr > cc)[None]
    eye = jnp.where(rr == cc, 1.0, 0.0).astype(F32)[None]
    rr2 = lax.broadcasted_iota(jnp.int32, (c, 2 * c), 0)
    cc2 = lax.broadcasted_iota(jnp.int32, (c, 2 * c), 1)
    strict_k = ((cc2 >= c) & (rr2 > cc2 - c))[None]
    incl_bk = (rr2 >= cc2 % c)[None]
    rk = rk_ref[...]
    lnw = lnw_ref[...]
    lnb = lnb_ref[...]

    def body(ch, carry):
        rows = pl.ds(pl.multiple_of(ch * c, c), c)
        rh, kmh, vh, ah, lwh, cumh, kkn = [s[:, rows, :] for s in (r_s, km_s, v_s, a_s, lw_s, cum_s, kkn_s)]
        s_prev = sout_ref[0]
        bvec = kkn * ah
        cum_end = cumh[:, c - 1:c, :]
        p_inc = jnp.exp(cumh)
        p_exc = jnp.exp(cumh - lwh)
        p_inv = jnp.exp(-cumh)
        p_end = jnp.exp(cum_end)
        p_rel = jnp.exp(cum_end - cumh)
        a_t = -(kkn * p_exc)
        r_t = rh * p_inc
        b_t = bvec * p_inv
        k_t = kmh * p_inv
        b_h = bvec * p_rel
        k_h = kmh * p_rel

        ar = jnp.concatenate([a_t, r_t], axis=1)
        prod = _bmm_nt(ar, jnp.concatenate([b_t, k_t, s_prev], axis=1))
        a_rows, r_rows = prod[:, :c], prod[:, c:]
        lmat = jnp.where(strict, a_rows[:, :, :c], 0.0)
        aak_wide = jnp.where(strict_k, a_rows[:, :, :2 * c], 0.0)
        arbk = jnp.where(incl_bk, r_rows[:, :, :2 * c], 0.0)
        tinv = eye + lmat
        if n_factors > 1:
            lp = _bmm(lmat, lmat)
            for _ in range(n_factors - 2):
                st = _bmm(jnp.concatenate([lp, tinv], axis=1), lp)
                lp = st[:, :c]
                tinv = tinv + st[:, c:]
            tinv = tinv + _bmm(tinv, lp)
        x = a_rows[:, :, 2 * c:] + _bmm(aak_wide, jnp.concatenate([vh, vh], axis=1))
        u = _bmm(tinv, x)
        uv = jnp.concatenate([u, vh], axis=1)
        o = r_rows[:, :, 2 * c:] + _bmm(arbk, uv)
        sout_ref[0] = s_prev * p_end + _bmm_tn(uv, jnp.concatenate([b_h, k_h], axis=1))
        mean = jnp.mean(o, axis=-1, keepdims=True)
        var = jnp.mean(jnp.square(o - mean), axis=-1, keepdims=True)
        on = ((o - mean) * lax.rsqrt(var + RW_GN_EPS)) * lnw + lnb
        bonus = jnp.sum(rh * kmh * rk, axis=-1, keepdims=True) * vh
        o_s[:, rows, :] = on + bonus
        return carry

    lax.fori_loop(0, n_chunks, body, 0)

    o_full = jnp.concatenate([o_s[h] for h in range(RW_HEADS)], axis=-1)
    o_ref[...] = (o_full * g).astype(BF16)
    zlast_ref[0] = carry_ref[...]


N_RWKV_INPUTS = 17
N_RWKV_OUTPUTS = 3


def _rwkv_and_cast_kernel(*refs, n_cast, **kw):
    ins = refs[:N_RWKV_INPUTS]
    cast_in = refs[N_RWKV_INPUTS:N_RWKV_INPUTS + n_cast]
    o0 = N_RWKV_INPUTS + n_cast
    outs = refs[o0:o0 + N_RWKV_OUTPUTS]
    cast_out = refs[o0 + N_RWKV_OUTPUTS:o0 + N_RWKV_OUTPUTS + n_cast]
    scratch = refs[o0 + N_RWKV_OUTPUTS + n_cast:]
    for src, dst in zip(cast_in, cast_out):
        dst[...] = src[...].astype(BF16)
    _rwkv_kernel(*ins, *outs, *scratch, **kw)


def _rwkv_mixer(z, row_block0, n_groups, n_tiles, zprev0, s0, p, *, tb, c, cast=()):
    n_factors = max(1, math.ceil(math.log2(c)))
    kern = functools.partial(_rwkv_and_cast_kernel, n_cast=len(cast), tb=tb, c=c, n_factors=n_factors)
    n_steps = n_groups * n_tiles
    cast_specs = [pl.BlockSpec((a.shape[0] // n_steps, a.shape[1]), lambda gi, i: (gi * n_tiles + i, 0))
                  for a in cast]
    cast_shapes = [jax.ShapeDtypeStruct(a.shape, BF16) for a in cast]
    cb = RW_OFF // RW_WIDTH
    full = lambda arr: pl.BlockSpec(arr.shape, lambda gi, i: (0,) * arr.ndim)
    lora_cb = (RW_OFF + 3 * RW_WIDTH) // LORA_PAD
    per_head = lambda a: a.reshape(RW_HEADS, 1, RW_HEAD)
    params = [p["mu"], p["w0"], p["w2"], p["a0"], p["a2"], p["g2"], p["kk"], p["ka"],
              per_head(p["rk"]), per_head(p["lnw"]), per_head(p["lnb"])]
    hm = lambda: pltpu.VMEM((RW_HEADS, tb, RW_HEAD), F32)
    zspec = lambda width, col: pl.BlockSpec((tb, width), lambda gi, i: (row_block0 + gi * n_tiles + i, col))
    sspec = pl.BlockSpec((1, RW_HEADS, RW_HEAD, RW_HEAD), lambda gi, i: (gi, 0, 0, 0))
    return pl.pallas_call(
        kern,
        grid=(n_groups, n_tiles),
        in_specs=[zspec(RW_WIDTH, cb), zspec(RW_WIDTH, cb + 1), zspec(RW_WIDTH, cb + 2),
                  zspec(LORA_PAD, lora_cb),
                  pl.BlockSpec((1, 1, SHIFT_PAD), lambda gi, i: (gi, 0, 0)),
                  sspec]
                 + [full(a) for a in params] + cast_specs,
        out_specs=[pl.BlockSpec((tb, RW_WIDTH), lambda gi, i: (gi * n_tiles + i, 0)), sspec,
                   pl.BlockSpec((1, 1, SHIFT_PAD), lambda gi, i: (gi, 0, 0))] + cast_specs,
        out_shape=[jax.ShapeDtypeStruct((n_groups * n_tiles * tb, RW_WIDTH), BF16),
                   jax.ShapeDtypeStruct((n_groups, RW_HEADS, RW_HEAD, RW_HEAD), F32),
                   jax.ShapeDtypeStruct((n_groups, 1, SHIFT_PAD), F32)] + cast_shapes,
        scratch_shapes=[pltpu.VMEM((1, SHIFT_PAD), F32)] + [hm() for _ in range(8)],
        compiler_params=_cparams(("parallel", "arbitrary")),
        name="rwkv7_mixer",
    )(z, z, z, z, zprev0, s0, *params, *cast)


def _rwkv_sample_kernel(zr_ref, zk_ref, zv_ref, zl_ref, shr_ref, shk_ref, shv_ref, shl_ref,
                        mur_ref, muk_ref, muv_ref, mul_ref, s0_ref,
                        w0_ref, w2t_ref, a0_ref, a2t_ref, g2t_ref, kk_ref, ka_ref, rk_ref,
                        lnw_ref, lnb_ref,
                        o_ref, sout_ref,
                        dec_s, a_s, b_s, k_s, r_s, v_s, o_s,
                        *, n_tok, hpb):
    ns = LANES
    ch = hpb * RW_HEAD

    def lanes(x):
        return jnp.concatenate([x] * n_tok, axis=1)

    def shifted(z_ref, sh_ref, mu_ref):
        z = z_ref[...]
        prev = jnp.concatenate([sh_ref[...], z[:, :(n_tok - 1) * ns]], axis=1)
        return z + lanes(mu_ref[...]) * (prev - z)

    r = shifted(zr_ref, shr_ref, mur_ref)
    k = shifted(zk_ref, shk_ref, muk_ref)
    v = shifted(zv_ref, shv_ref, muv_ref)
    zl = shifted(zl_ref, shl_ref, mul_ref)
    wd = zl[0:DECAY_LORA]
    ad = zl[DECAY_LORA:DECAY_LORA + AAA_LORA]
    gd = zl[DECAY_LORA + AAA_LORA:DECAY_LORA + AAA_LORA + GATE_LORA]

    wlog = -jax.nn.softplus(-(lanes(w0_ref[...]) + _mm(w2t_ref[...], jnp.tanh(wd)))) - 0.5
    lw = -jnp.exp(wlog)
    a = jax.nn.sigmoid(lanes(a0_ref[...]) + _mm(a2t_ref[...], ad))
    g = _mm(g2t_ref[...], jax.nn.sigmoid(gd))
    kk = k * lanes(kk_ref[...])
    km = k * (1.0 + (a - 1.0) * lanes(ka_ref[...]))

    def head_sum(x):
        x3 = x.reshape(hpb, RW_HEAD, n_tok * ns)
        s = jnp.sum(x3, axis=1, keepdims=True)
        return jnp.broadcast_to(s, x3.shape).reshape(ch, n_tok * ns)

    kkn = kk * lax.rsqrt(jnp.maximum(head_sum(kk * kk), 1e-24))
    dec_s[...] = jnp.exp(lw)
    a_s[...] = -kkn
    b_s[...] = kkn * a
    k_s[...] = km
    r_s[...] = r
    v_s[...] = v

    for hh in range(hpb):
        hrows = slice(hh * RW_HEAD, (hh + 1) * RW_HEAD)

        def body(i8, carry, hh=hh, hrows=hrows):
            base = pl.multiple_of(i8 * 8, 8)
            rows8 = pl.ds(pl.multiple_of(hh * RW_HEAD + base, 8), 8)
            o_rows = [[] for _ in range(n_tok)]
            for j in range(8):
                s = s0_ref[hh, base + j]
                for t in range(n_tok):
                    tl = slice(t * ns, (t + 1) * ns)
                    sa = jnp.sum(s * a_s[hrows, tl], axis=0, keepdims=True)
                    v_row = v_s[rows8, tl][j:j + 1, :]
                    s = s * dec_s[hrows, tl] + sa * b_s[hrows, tl] + v_row * k_s[hrows, tl]
                    o_rows[t].append(jnp.sum(s * r_s[hrows, tl], axis=0, keepdims=True))
                sout_ref[hh, base + j] = s
            for t in range(n_tok):
                o_s[rows8, t * ns:(t + 1) * ns] = jnp.concatenate(o_rows[t], axis=0)
            return carry

        lax.fori_loop(0, RW_HEAD // 8, body, 0)

    o = o_s[...]
    inv_n = 1.0 / RW_HEAD
    mean = head_sum(o) * inv_n
    var = head_sum(jnp.square(o - mean)) * inv_n
    on = ((o - mean) * lax.rsqrt(var + RW_GN_EPS)) * lanes(lnw_ref[...]) + lanes(lnb_ref[...])
    bonus = head_sum(r * km * lanes(rk_ref[...])) * v
    out = (on + bonus) * g
    for t in range(n_tok):
        o_ref[t] = out[:, t * ns:(t + 1) * ns].T.astype(BF16)


def _rwkv_sample_mixer(z_t, shift_t, mu_t, state_t, p_t, *, n_tok, hpb):
    ch = hpb * RW_HEAD
    n_steps = RW_HEADS // hpb
    nl = n_tok * LANES
    seg = RW_WIDTH // ch
    zspec = lambda s: pl.BlockSpec((ch, nl), lambda h: (s * seg + h, 0))
    cspec = lambda s: pl.BlockSpec((ch, LANES), lambda h: (s * seg + h, 0))
    lora_blk = 3 * RW_WIDTH // LORA_PAD
    zl_spec = pl.BlockSpec((LORA_PAD, nl), lambda h: (lora_blk, 0))
    cl_spec = pl.BlockSpec((LORA_PAD, LANES), lambda h: (lora_blk, 0))
    hspec = pl.BlockSpec((ch, LANES), lambda h: (h, 0))
    wspec = lambda k: pl.BlockSpec((ch, k), lambda h: (h, 0))
    sspec = pl.BlockSpec((hpb, RW_HEAD, RW_HEAD, LANES), lambda h: (h, 0, 0, 0))
    buf = lambda: pltpu.VMEM((ch, nl), F32)
    return pl.pallas_call(
        functools.partial(_rwkv_sample_kernel, n_tok=n_tok, hpb=hpb),
        grid=(n_steps,),
        in_specs=[zspec(0), zspec(1), zspec(2), zl_spec,
                  cspec(0), cspec(1), cspec(2), cl_spec,
                  cspec(0), cspec(1), cspec(2), cl_spec,
                  sspec,
                  hspec, wspec(DECAY_LORA), hspec, wspec(AAA_LORA), wspec(GATE_LORA),
                  hspec, hspec, hspec, hspec, hspec],
        out_specs=[pl.BlockSpec((n_tok, LANES, ch), lambda h: (0, 0, h)), sspec],
        out_shape=[jax.ShapeDtypeStruct((n_tok, LANES, RW_WIDTH), BF16),
                   jax.ShapeDtypeStruct(state_t.shape, F32)],
        scratch_shapes=[buf() for _ in range(7)],
        compiler_params=_cparams(("parallel",)),
        name="rwkv7_sample",
    )(z_t, z_t, z_t, z_t, shift_t, shift_t, shift_t, shift_t, mu_t, mu_t, mu_t, mu_t, state_t,
      p_t["w0"], p_t["w2t"], p_t["a0"], p_t["a2t"], p_t["g2t"], p_t["kk"], p_t["ka"], p_t["rk"],
      p_t["lnw"], p_t["lnb"])


def _retention_kernel(zq_ref, zk_ref, zv_ref, zg_ref, cos_ref, sin_ref, dmask_ref, iscale_ref,
                      kscale_ref, sdec_ref, s0_ref, o_ref, sout_ref,
                      *, rows, c, n_real, per_chunk_state):
    i = pl.program_id(1)
    n_chunks = rows // c

    if per_chunk_state:
        sout_ref[...] = s0_ref[...]
        pr = lax.broadcasted_iota(jnp.int32, (rows, n_chunks * n_real), 0)
        pc = lax.broadcasted_iota(jnp.int32, (rows, n_chunks * n_real), 1)
        place = jnp.where((pr // c == pc // n_real) & (pr % c - (c - n_real) == pc % n_real),
                          1.0, 0.0).astype(BF16)
        load = lambda ref: _mm_exact_lhs(place, ref[...])
    else:
        load = lambda ref: ref[...]

        @pl.when(i == 0)
        def _():
            sout_ref[...] = s0_ref[...]

    lane = lax.broadcasted_iota(jnp.int32, (rows, RET_WIDTH), 1)
    even = (lane % 2) == 0
    cos = jnp.concatenate([cos_ref[...]] * RET_HEADS, axis=-1)
    sin = jnp.concatenate([sin_ref[...]] * RET_HEADS, axis=-1)

    def rot(x):
        partner = jnp.where(even, pltpu.roll(x, RET_WIDTH - 1, axis=1), pltpu.roll(x, 1, axis=1))
        return x * cos + partner * sin

    q = rot(load(zq_ref))
    k = rot(load(zk_ref)) * (RET_HEAD ** -0.5)
    v = load(zv_ref)
    g = load(zg_ref)

    out_rows = []
    for ch in range(n_chunks):
        rs = slice(ch * c, (ch + 1) * c)
        sidx = ch if per_chunk_state else 0
        out_heads = []
        for h in range(RET_HEADS):
            hs = slice(h * RET_HEAD, (h + 1) * RET_HEAD)
            qh, kh, vh = q[rs, hs], k[rs, hs], v[rs, hs]
            s_prev = sout_ref[sidx, h]
            scores = _mm_nt(qh, kh) * dmask_ref[h]
            o = _mm(scores, vh) + _mm(qh, s_prev) * iscale_ref[h]
            sout_ref[sidx, h] = s_prev * sdec_ref[h] + _mm_tn(kh * kscale_ref[h], vh)
            o = o * lax.rsqrt(jnp.mean(o * o, axis=-1, keepdims=True) + RET_GN_EPS)
            gh = g[rs, hs]
            out_heads.append(o * (gh * jax.nn.sigmoid(gh)))
        out_rows.append(jnp.concatenate(out_heads, axis=-1))
    o_out = jnp.concatenate(out_rows, axis=0).astype(BF16)
    if per_chunk_state:
        o_out = lax.dot_general(place, o_out, (((0,), (0,)), ((), ())),
                                preferred_element_type=F32).astype(BF16)
    o_ref[...] = o_out


def _retention_mixer(z, row_block0, n_groups, n_tiles, cos, sin, tabs, s0, *, rows, c, n_real,
                     per_chunk_state, pos_per_tile):
    n_states = s0.shape[0]
    sb = n_states // n_groups
    kern = functools.partial(_retention_kernel, rows=rows, c=c, n_real=n_real,
                             per_chunk_state=per_chunk_state)
    rows_io = rows // c * n_real if per_chunk_state else rows
    zspec = lambda col: pl.BlockSpec((rows_io, RET_WIDTH), lambda gi, i: (row_block0 + gi * n_tiles + i, col))
    full = lambda arr: pl.BlockSpec(arr.shape, lambda gi, i: (0,) * arr.ndim)
    if pos_per_tile:
        tspec = pl.BlockSpec((rows, RET_HEAD), lambda gi, i: (i, 0))
    else:
        tspec = pl.BlockSpec((rows, RET_HEAD), lambda gi, i: (0, 0))
    sspec = pl.BlockSpec((sb, RET_HEADS, RET_HEAD, RET_HEAD), lambda gi, i: (gi, 0, 0, 0))
    dmask, iscale, kscale, sdec = tabs
    return pl.pallas_call(
        kern,
        grid=(n_groups, n_tiles),
        in_specs=[zspec(0), zspec(1), zspec(2), zspec(3), tspec, tspec,
                  full(dmask), full(iscale), full(kscale), full(sdec), sspec],
        out_specs=[pl.BlockSpec((rows_io, RET_WIDTH), lambda gi, i: (gi * n_tiles + i, 0)), sspec],
        out_shape=[jax.ShapeDtypeStruct((n_groups * n_tiles * rows_io, RET_WIDTH), BF16),
                   jax.ShapeDtypeStruct(s0.shape, F32)],
        compiler_params=_cparams(("parallel", "arbitrary")),
        name="retention_mixer",
    )(z, z, z, z, cos, sin, dmask, iscale, kscale, sdec, s0)


def _retention_tables(c, n_real):
    log_gamma = jnp.log(1.0 - 2.0 ** (-5.0 - jnp.arange(RET_HEADS, dtype=F32)))
    r = jnp.arange(c, dtype=F32)
    idx = r - float(c - n_real)
    diff = r[:, None] - r[None, :]
    dmask = jnp.where(diff[None] >= 0,
                      jnp.exp(log_gamma[:, None, None] * jnp.maximum(diff, 0.0)[None]), 0.0)
    iscale = jnp.exp(log_gamma[:, None] * (idx + 1.0)[None, :])[:, :, None]
    kscale = jnp.exp(log_gamma[:, None] * (n_real - 1.0 - idx)[None, :])[:, :, None]
    sdec = jnp.broadcast_to(jnp.exp(log_gamma * n_real)[:, None, None], (RET_HEADS, 1, RET_HEAD))
    return dmask, iscale, kscale, sdec


def _rotary_tables(pos):
    inv_freq = 1.0 / (ROPE_BASE ** jnp.linspace(0.0, 1.0, RET_HEAD // 2, dtype=F32))
    ang = pos.astype(F32)[:, None] * inv_freq[None, :]
    cos = jnp.cos(ang)
    sin = jnp.sin(ang)
    cos2 = jnp.repeat(cos, 2, axis=-1)
    sin2 = jnp.stack([-sin, sin], axis=-1).reshape(pos.shape[0], RET_HEAD)
    return cos2, sin2


def _outproj_kernel(oa_ref, ob_ref, w_ref, x_ref, g_ref, h_ref, hn_ref):
    acc = jnp.dot(oa_ref[...], w_ref[0:RW_WIDTH, :], preferred_element_type=F32)
    acc = acc + jnp.dot(ob_ref[...], w_ref[RW_WIDTH:, :], preferred_element_type=F32)
    h = x_ref[...] + acc
    h_ref[...] = h
    hn_ref[...] = _rms_norm_bf16(h, g_ref[...])


def _out_projection(o_a, o_b, w_out, x2d, norm_g, tm):
    m = o_a.shape[0]
    return pl.pallas_call(
        _outproj_kernel,
        grid=(m // tm,),
        in_specs=[
            pl.BlockSpec((tm, RW_WIDTH), lambda i: (i, 0)),
            pl.BlockSpec((tm, RET_WIDTH), lambda i: (i, 0)),
            pl.BlockSpec((D_MODEL, D_MODEL), lambda i: (0, 0)),
            pl.BlockSpec((tm, D_MODEL), lambda i: (i, 0)),
            pl.BlockSpec((1, D_MODEL), lambda i: (0, 0)),
        ],
        out_specs=[pl.BlockSpec((tm, D_MODEL), lambda i: (i, 0)),
                   pl.BlockSpec((tm, D_MODEL), lambda i: (i, 0))],
        out_shape=[jax.ShapeDtypeStruct((m, D_MODEL), F32), jax.ShapeDtypeStruct((m, D_MODEL), BF16)],
        compiler_params=_cparams(("parallel",)),
        name="out_projection",
    )(o_a, o_b, w_out, x2d, norm_g)


def _ffn_kernel(hn_ref, wg_ref, wu_ref, wd_ref, h_ref, g_ref, y_ref, acc_ref):
    f = pl.program_id(1)

    @pl.when(f == 0)
    def _():
        acc_ref[...] = jnp.zeros_like(acc_ref)

    hn = hn_ref[...]
    gate = jnp.dot(hn, wg_ref[...], preferred_element_type=F32)
    up = jnp.dot(hn, wu_ref[...], preferred_element_type=F32)
    act = (gate * jax.nn.sigmoid(gate)) * up
    acc_ref[...] += jnp.dot(act.astype(BF16), wd_ref[...], preferred_element_type=F32)

    @pl.when(f == pl.num_programs(1) - 1)
    def _():
        h = h_ref[...] + acc_ref[...]
        ms = jnp.mean(h * h, axis=-1, keepdims=True)
        y_ref[...] = (h * lax.rsqrt(ms + RMS_EPS)) * g_ref[...]


def _ffn(hn, w_gate, w_up, w_down, h, norm_g, tm, tf):
    m = hn.shape[0]
    return pl.pallas_call(
        _ffn_kernel,
        grid=(m // tm, D_FF // tf),
        in_specs=[
            pl.BlockSpec((tm, D_MODEL), lambda i, f: (i, 0)),
            pl.BlockSpec((D_MODEL, tf), lambda i, f: (0, f)),
            pl.BlockSpec((D_MODEL, tf), lambda i, f: (0, f)),
            pl.BlockSpec((tf, D_MODEL), lambda i, f: (f, 0)),
            pl.BlockSpec((tm, D_MODEL), lambda i, f: (i, 0)),
            pl.BlockSpec((1, D_MODEL), lambda i, f: (0, 0)),
        ],
        out_specs=pl.BlockSpec((tm, D_MODEL), lambda i, f: (i, 0)),
        out_shape=jax.ShapeDtypeStruct((m, D_MODEL), F32),
        scratch_shapes=[pltpu.VMEM((tm, D_MODEL), F32)],
        compiler_params=_cparams(("parallel", "arbitrary")),
        name="swiglu_ffn",
    )(hn, w_gate, w_up, w_down, h, norm_g)


def _pad_cols(a, n):
    return jnp.pad(a, ((0, 0), (0, n - a.shape[1])))


def _pad_rows(a, n):
    return jnp.pad(a, ((0, n - a.shape[0]), (0, 0)))


def kernel(x_prompt, x_sample, state_shift, state_rwkv, state_ret, meta_tokens, norm_mix, w_in,
           rwkv_mu, rwkv_w0, rwkv_w2, rwkv_a0, rwkv_a2, rwkv_g2, rwkv_kk, rwkv_ka, rwkv_rk,
           rwkv_ln_w, rwkv_ln_b, w_out, norm_ffn, w_gate, w_up, w_down, norm_final):
    n_b, seq, d = x_prompt.shape
    n_s, dec_seq, _ = x_sample.shape
    n_p = n_b * seq
    n_d = n_s * dec_seq
    depth = w_in.shape[0]
    assert depth == 1 and d == D_MODEL and n_s == LANES

    w_in_t = jnp.transpose(w_in[0])
    w_in_p = _w_in_layout(w_in_t, tn=512)
    row = lambda a: a.reshape(1, -1).astype(F32)
    w2p = jnp.concatenate([rwkv_w2[0], jnp.zeros((128 - DECAY_LORA, RW_WIDTH), F32)], axis=0)
    a2p = jnp.concatenate([jnp.zeros((DECAY_LORA, RW_WIDTH), F32), rwkv_a2[0]], axis=0)
    g2p = jnp.concatenate([rwkv_g2[0], jnp.zeros((256 - GATE_LORA, RW_WIDTH), F32)], axis=0)
    rw_params = dict(mu=_pad_cols(row(rwkv_mu[0]), SHIFT_PAD), w0=row(rwkv_w0[0]), w2=w2p,
                     a0=row(rwkv_a0[0]), a2=a2p, g2=g2p, kk=row(rwkv_kk[0]), ka=row(rwkv_ka[0]),
                     rk=row(rwkv_rk[0]), lnw=row(rwkv_ln_w[0]), lnb=row(rwkv_ln_b[0]))
    col = lambda a: jnp.broadcast_to(a.reshape(-1, 1).astype(F32), (a.size, LANES))
    rw_params_t = dict(w0=col(rwkv_w0[0]), w2t=rwkv_w2[0].T, a0=col(rwkv_a0[0]), a2t=rwkv_a2[0].T,
                       g2t=rwkv_g2[0].T, kk=col(rwkv_kk[0]), ka=col(rwkv_ka[0]), rk=col(rwkv_rk[0]),
                       lnw=col(rwkv_ln_w[0]), lnb=col(rwkv_ln_b[0]))

    x_p = x_prompt.reshape(n_p, d)
    x_s = x_sample.reshape(n_d, d)
    x_sm = jnp.concatenate([x_s, meta_tokens.astype(F32)], axis=0)
    x_ts = jnp.transpose(x_sample, (1, 0, 2)).reshape(n_d, d)
    g_mix = row(norm_mix[0])
    z_p = _in_projection(x_p, g_mix, w_in_p, tm=1024, tn=1536)
    z_sm = _in_projection(x_sm, g_mix, w_in_p, tm=x_sm.shape[0], tn=768)
    z_st = _in_projection_t(w_in_t, x_ts, g_mix, SHIFT_PAD, tm=512)

    z_meta = jnp.pad(z_sm[n_d:], ((RET_CHUNK - N_META, 0), (0, 0)))
    srow = 2 * dec_seq

    zero_prev = jnp.zeros((1, 1, SHIFT_PAD), F32)
    zero_rw = jnp.zeros((1, RW_HEADS, RW_HEAD, RW_HEAD), F32)
    c_rw = 64
    _, s_rw_meta, zlast_meta = _rwkv_mixer(z_meta, 1, 1, 1, zero_prev, zero_rw, rw_params, tb=c_rw, c=c_rw)
    oa_p, rwkv_p, zlast_p, w_out_b, w_gate_b, w_up_b, w_down_b = _rwkv_mixer(
        z_p, 0, n_b, seq // 256, jnp.broadcast_to(zlast_meta, (n_b, 1, SHIFT_PAD)),
        jnp.broadcast_to(s_rw_meta, (n_b,) + s_rw_meta.shape[1:]), rw_params, tb=256, c=c_rw,
        cast=(w_out[0], w_gate[0], w_up[0], w_down[0]))
    shift_t = _pad_rows(jnp.transpose(state_shift[0]), SHIFT_PAD)
    mu_t = _pad_rows(col(rwkv_mu[0]), SHIFT_PAD)
    state_t = jnp.transpose(state_rwkv[0], (1, 2, 3, 0))
    oa_st, rwkv_st = _rwkv_sample_mixer(z_st, shift_t, mu_t, state_t, rw_params_t, n_tok=dec_seq, hpb=2)
    oa_s = jnp.transpose(oa_st, (1, 0, 2)).reshape(n_d, RW_WIDTH)
    rwkv_s = jnp.transpose(rwkv_st, (3, 0, 1, 2))

    past_len = 16384
    tabs_full = _retention_tables(RET_CHUNK, RET_CHUNK)
    tabs_meta = _retention_tables(RET_CHUNK, N_META)
    tabs_smp = _retention_tables(srow, dec_seq)
    cos_m, sin_m = _rotary_tables(jnp.arange(RET_CHUNK) - (RET_CHUNK - N_META))
    cos_p, sin_p = _rotary_tables(N_META + jnp.arange(seq))
    sb_rt = 4
    pos_s = past_len + jnp.tile(jnp.arange(srow) - (srow - dec_seq), sb_rt)
    cos_s, sin_s = _rotary_tables(pos_s)
    zero_rt = jnp.zeros((1, RET_HEADS, RET_HEAD, RET_HEAD), F32)
    _, s_rt_meta = _retention_mixer(z_meta, 0, 1, 1, cos_m, sin_m, tabs_meta, zero_rt,
                                    rows=RET_CHUNK, c=RET_CHUNK, n_real=N_META, per_chunk_state=False,
                                    pos_per_tile=False)
    ob_p, ret_p = _retention_mixer(z_p, 0, n_b, seq // RET_CHUNK, cos_p, sin_p, tabs_full,
                                   jnp.broadcast_to(s_rt_meta, (n_b,) + s_rt_meta.shape[1:]),
                                   rows=RET_CHUNK, c=RET_CHUNK, n_real=RET_CHUNK, per_chunk_state=False,
                                   pos_per_tile=True)
    ob_s, ret_s = _retention_mixer(z_sm, 0, n_s // sb_rt, 1, cos_s, sin_s, tabs_smp,
                                   state_ret.reshape(n_s, RET_HEADS, RET_HEAD, RET_HEAD),
                                   rows=sb_rt * srow, c=srow, n_real=dec_seq, per_chunk_state=True,
                                   pos_per_tile=False)

    g_ffn = row(norm_ffn[0])
    g_fin = row(norm_final)
    h_p, hn_p = _out_projection(oa_p, ob_p, w_out_b, x_p, g_ffn, tm=512)
    h_s, hn_s = _out_projection(oa_s, ob_s, w_out_b, x_s, g_ffn, tm=n_d)
    y_p = _ffn(hn_p, w_gate_b, w_up_b, w_down_b, h_p, g_fin, tm=512, tf=512)
    y_s = _ffn(hn_s, w_gate_b, w_up_b, w_down_b, h_s, g_fin, tm=n_d, tf=512)

    y_prompt = y_p.reshape(n_b, seq, d)
    y_sample = y_s.reshape(n_s, dec_seq, d)
    shift_p = zlast_p[:, 0, :RW_COLS][None]
    shift_s = jnp.transpose(z_st[:RW_COLS, (dec_seq - 1) * n_s:])[None]
    return (y_prompt, y_sample, shift_p, rwkv_p[None], ret_p[None], shift_s, rwkv_s[None], ret_s[None])
```

```python
import functools
import math

import jax
import jax.numpy as jnp
from jax import lax
from jax.experimental import pallas as pl
from jax.experimental.pallas import tpu as pltpu

F32 = jnp.float32
BF16 = jnp.bfloat16

D_MODEL = 2048
N_META = 16
RW_WIDTH = 1024
RW_HEAD = 64
RW_HEADS = 16
DECAY_LORA = 64
AAA_LORA = 64
GATE_LORA = 160
RW_COLS = 3 * RW_WIDTH + DECAY_LORA + AAA_LORA + GATE_LORA
RET_WIDTH = 1024
RET_HEADS = 4
RET_HEAD = 256
RET_CHUNK = 128
D_FF = 5632
RMS_EPS = 1e-6
RW_GN_EPS = 64e-5
RET_GN_EPS = 1e-6
ROPE_BASE = 10000.0
LANES = 128

LORA_PAD = 512
Z_COLS = 4 * RET_WIDTH + 3 * RW_WIDTH + LORA_PAD
RW_OFF = 4 * RET_WIDTH
SHIFT_PAD = 3 * RW_WIDTH + LORA_PAD

VMEM_LIMIT = 60 * 1024 * 1024


def _cparams(sem):
    return pltpu.CompilerParams(dimension_semantics=sem, vmem_limit_bytes=VMEM_LIMIT)


def _mm(a, b):
    return jnp.dot(a.astype(BF16), b.astype(BF16), preferred_element_type=F32)


def _mm_nt(a, b):
    return lax.dot_general(a.astype(BF16), b.astype(BF16), (((1,), (1,)), ((), ())),
                           preferred_element_type=F32)


def _mm_tn(a, b):
    return lax.dot_general(a.astype(BF16), b.astype(BF16), (((0,), (0,)), ((), ())),
                           preferred_element_type=F32)


def _bmm(a, b):
    return jnp.einsum("bmk,bkn->bmn", a.astype(BF16), b.astype(BF16), preferred_element_type=F32)


def _bmm_nt(a, b):
    return jnp.einsum("bmk,bnk->bmn", a.astype(BF16), b.astype(BF16), preferred_element_type=F32)


def _bmm_tn(a, b):
    return jnp.einsum("bkm,bkn->bmn", a.astype(BF16), b.astype(BF16), preferred_element_type=F32)


def _mm_exact_lhs(m_bf16, x):
    hi = x.astype(BF16)
    r1 = x - hi.astype(F32)
    mid = r1.astype(BF16)
    lo = (r1 - mid.astype(F32)).astype(BF16)
    d = functools.partial(jnp.dot, preferred_element_type=F32)
    return d(m_bf16, hi) + d(m_bf16, mid) + d(m_bf16, lo)


def _rms_norm_bf16(x, g):
    ms = jnp.mean(x * x, axis=-1, keepdims=True)
    return ((x * lax.rsqrt(ms + RMS_EPS)) * g).astype(BF16)


def _w_in_layout_kernel(w_ref, o_ref):
    o_ref[...] = w_ref[...].T.astype(BF16)


def _w_in_layout(w_in_t, tn):
    n, d = w_in_t.shape
    n_ret = 4 * RET_WIDTH // tn

    def src_row(c):
        align = math.gcd(RW_COLS, tn)
        return (pl.multiple_of(jnp.where(c < n_ret, RW_COLS + tn * c, tn * (c - n_ret)), align), 0)

    return pl.pallas_call(
        _w_in_layout_kernel,
        grid=(Z_COLS // tn,),
        in_specs=[pl.BlockSpec((pl.Element(tn), pl.Element(d)), src_row)],
        out_specs=pl.BlockSpec((d, tn), lambda c: (0, c)),
        out_shape=jax.ShapeDtypeStruct((d, Z_COLS), BF16),
        compiler_params=_cparams(("parallel",)),
        name="w_in_layout",
    )(w_in_t)


def _inproj_kernel(x_ref, g_ref, w_ref, o_ref, xn_ref):
    @pl.when(pl.program_id(1) == 0)
    def _():
        xn_ref[...] = _rms_norm_bf16(x_ref[...], g_ref[...])

    o_ref[...] = jnp.dot(xn_ref[...], w_ref[...], preferred_element_type=F32)


def _in_projection(x2d, norm_g, w_in_p, tm, tn):
    m = x2d.shape[0]
    return pl.pallas_call(
        _inproj_kernel,
        grid=(m // tm, Z_COLS // tn),
        in_specs=[
            pl.BlockSpec((tm, D_MODEL), lambda i, j: (i, 0)),
            pl.BlockSpec((1, D_MODEL), lambda i, j: (0, 0)),
            pl.BlockSpec((D_MODEL, tn), lambda i, j: (0, j)),
        ],
        out_specs=pl.BlockSpec((tm, tn), lambda i, j: (i, j)),
        out_shape=jax.ShapeDtypeStruct((m, Z_COLS), F32),
        scratch_shapes=[pltpu.VMEM((tm, D_MODEL), BF16)],
        compiler_params=_cparams(("parallel", "arbitrary")),
        name="in_projection",
    )(x2d, norm_g, w_in_p)


def _inproj_t_kernel(w_ref, x_ref, g_ref, o_ref, xn_ref):
    @pl.when(pl.program_id(0) == 0)
    def _():
        xn_ref[...] = _rms_norm_bf16(x_ref[...], g_ref[...])

    o_ref[...] = _mm_nt(w_ref[...], xn_ref[...])


def _in_projection_t(w_in_t, x2d, norm_g, n_rows, tm):
    m, d = x2d.shape
    return pl.pallas_call(
        _inproj_t_kernel,
        grid=(n_rows // tm,),
        in_specs=[
            pl.BlockSpec((tm, d), lambda i: (i, 0)),
            pl.BlockSpec((m, d), lambda i: (0, 0)),
            pl.BlockSpec((1, d), lambda i: (0, 0)),
        ],
        out_specs=pl.BlockSpec((tm, m), lambda i: (i, 0)),
        out_shape=jax.ShapeDtypeStruct((n_rows, m), F32),
        scratch_shapes=[pltpu.VMEM((m, d), BF16)],
        compiler_params=_cparams(("arbitrary",)),
        name="in_projection_t",
    )(w_in_t, x2d, norm_g)


def _rwkv_kernel(zr_ref, zk_ref, zv_ref, zl_ref, zprev0_ref, s0_ref,
                 mu_ref, w0_ref, w2_ref, a0_ref, a2_ref, g2_ref, kk_ref, ka_ref, rk_ref,
                 lnw_ref, lnb_ref,
                 o_ref, sout_ref, zlast_ref,
                 carry_ref, r_s, km_s, v_s, kkn_s, a_s, lw_s, cum_s, o_s,
                 *, tb, c, n_factors):
    i = pl.program_id(1)
    n_chunks = tb // c

    @pl.when(i == 0)
    def _():
        carry_ref[...] = zprev0_ref[0]
        sout_ref[...] = s0_ref[...]

    row = lax.broadcasted_iota(jnp.int32, (tb, 1), 0)

    def shifted(z_ref, lo, hi):
        z = z_ref[...]
        prev = jnp.where(row == 0, carry_ref[:, lo:hi], pltpu.roll(z, 1, axis=0))
        zs = z + mu_ref[:, lo:hi] * (prev - z)
        carry_ref[:, lo:hi] = z[tb - 1:tb, :]
        return zs

    w = RW_WIDTH
    r = shifted(zr_ref, 0, w)
    k = shifted(zk_ref, w, 2 * w)
    v = shifted(zv_ref, 2 * w, 3 * w)
    zl = shifted(zl_ref, 3 * w, 3 * w + LORA_PAD)

    lo2 = zl[:, 0:128]
    wlog = -jax.nn.softplus(-(w0_ref[...] + _mm(jnp.tanh(lo2), w2_ref[...]))) - 0.5
    lw = -jnp.exp(wlog)
    a = jax.nn.sigmoid(a0_ref[...] + _mm(lo2, a2_ref[...]))
    g = _mm(jax.nn.sigmoid(zl[:, 128:384]), g2_ref[...])
    kk = k * kk_ref[...]
    km = k * (1.0 + (a - 1.0) * ka_ref[...])

    ri = lax.broadcasted_iota(jnp.int32, (tb, tb), 0)
    ci = lax.broadcasted_iota(jnp.int32, (tb, tb), 1)
    tri = jnp.where((ri // c == ci // c) & (ri >= ci), 1.0, 0.0).astype(BF16)
    cum = _mm_exact_lhs(tri, lw)

    for h in range(RW_HEADS):
        sl = slice(h * RW_HEAD, (h + 1) * RW_HEAD)
        r_s[h] = r[:, sl]
        km_s[h] = km[:, sl]
        v_s[h] = v[:, sl]
        a_s[h] = a[:, sl]
        lw_s[h] = lw[:, sl]
        cum_s[h] = cum[:, sl]
        kkn_s[h] = kk[:, sl]

    rr = lax.broadcasted_iota(jnp.int32, (c, c), 0)
    cc = lax.broadcasted_iota(jnp.int32, (c, c), 1)
    strict = (rr > cc)[None]
    eye = jnp.where(rr == cc, 1.0, 0.0).astype(F32)[None]
    rr2 = lax.broadcasted_iota(jnp.int32, (c, 2 * c), 0)
    cc2 = lax.broadcasted_iota(jnp.int32, (c, 2 * c), 1)
    strict_k = ((cc2 >= c) & (rr2 > cc2 - c))[None]
    incl_bk = (rr2 >= cc2 % c)[None]
    rk = rk_ref[...]
    lnw = lnw_ref[...]
    lnb = lnb_ref[...]

    def body(ch, carry):
        rows = pl.ds(pl.multiple_of(ch * c, c), c)
        rh, kmh, vh, ah, lwh, cumh, kkh = [s[:, rows, :] for s in (r_s, km_s, v_s, a_s, lw_s, cum_s, kkn_s)]
        s_prev = sout_ref[0]
        kkn = kkh * lax.rsqrt(jnp.maximum(jnp.sum(kkh * kkh, axis=-1, keepdims=True), 1e-24))
        bvec = kkn * ah
        cum_end = cumh[:, c - 1:c, :]
        p_inc = jnp.exp(cumh)
        p_exc = jnp.exp(cumh - lwh)
        p_inv = jnp.exp(-cumh)
        p_end = jnp.exp(cum_end)
        p_rel = jnp.exp(cum_end - cumh)
        a_t = -(kkn * p_exc)
        r_t = rh * p_inc
        b_t = bvec * p_inv
        k_t = kmh * p_inv
        b_h = bvec * p_rel
        k_h = kmh * p_rel

        ar = jnp.concatenate([a_t, r_t], axis=1)
        prod = _bmm_nt(ar, jnp.concatenate([b_t, k_t, s_prev], axis=1))
        a_rows, r_rows = prod[:, :c], prod[:, c:]
        lmat = jnp.where(strict, a_rows[:, :, :c], 0.0)
        aak_wide = jnp.where(strict_k, a_rows[:, :, :2 * c], 0.0)
        arbk = jnp.where(incl_bk, r_rows[:, :, :2 * c], 0.0)
        tinv = eye + lmat
        if n_factors > 1:
            lp = _bmm(lmat, lmat)
            for _ in range(n_factors - 2):
                st = _bmm(jnp.concatenate([lp, tinv], axis=1), lp)
                lp = st[:, :c]
                tinv = tinv + st[:, c:]
            tinv = tinv + _bmm(tinv, lp)
        x = a_rows[:, :, 2 * c:] + _bmm(aak_wide, jnp.concatenate([vh, vh], axis=1))
        u = _bmm(tinv, x)
        uv = jnp.concatenate([u, vh], axis=1)
        o = r_rows[:, :, 2 * c:] + _bmm(arbk, uv)
        sout_ref[0] = s_prev * p_end + _bmm_tn(uv, jnp.concatenate([b_h, k_h], axis=1))
        mean = jnp.mean(o, axis=-1, keepdims=True)
        var = jnp.mean(jnp.square(o - mean), axis=-1, keepdims=True)
        on = ((o - mean) * lax.rsqrt(var + RW_GN_EPS)) * lnw + lnb
        bonus = jnp.sum(rh * kmh * rk, axis=-1, keepdims=True) * vh
        o_s[:, rows, :] = on + bonus
        return carry

    lax.fori_loop(0, n_chunks, body, 0)

    o_full = jnp.concatenate([o_s[h] for h in range(RW_HEADS)], axis=-1)
    o_ref[...] = (o_full * g).astype(BF16)
    zlast_ref[0] = carry_ref[...]


N_RWKV_INPUTS = 17
N_RWKV_OUTPUTS = 3


def _rwkv_and_cast_kernel(*refs, n_cast, **kw):
    ins = refs[:N_RWKV_INPUTS]
    cast_in = refs[N_RWKV_INPUTS:N_RWKV_INPUTS + n_cast]
    o0 = N_RWKV_INPUTS + n_cast
    outs = refs[o0:o0 + N_RWKV_OUTPUTS]
    cast_out = refs[o0 + N_RWKV_OUTPUTS:o0 + N_RWKV_OUTPUTS + n_cast]
    scratch = refs[o0 + N_RWKV_OUTPUTS + n_cast:]
    for src, dst in zip(cast_in, cast_out):
        dst[...] = src[...].astype(BF16)
    _rwkv_kernel(*ins, *outs, *scratch, **kw)


def _rwkv_mixer(z, row_block0, n_groups, n_tiles, zprev0, s0, p, *, tb, c, cast=()):
    n_factors = max(1, math.ceil(math.log2(c)))
    kern = functools.partial(_rwkv_and_cast_kernel, n_cast=len(cast), tb=tb, c=c, n_factors=n_factors)
    n_steps = n_groups * n_tiles
    cast_specs = [pl.BlockSpec((a.shape[0] // n_steps, a.shape[1]), lambda gi, i: (gi * n_tiles + i, 0))
                  for a in cast]
    cast_shapes = [jax.ShapeDtypeStruct(a.shape, BF16) for a in cast]
    cb = RW_OFF // RW_WIDTH
    full = lambda arr: pl.BlockSpec(arr.shape, lambda gi, i: (0,) * arr.ndim)
    lora_cb = (RW_OFF + 3 * RW_WIDTH) // LORA_PAD
    per_head = lambda a: a.reshape(RW_HEADS, 1, RW_HEAD)
    params = [p["mu"], p["w0"], p["w2"], p["a0"], p["a2"], p["g2"], p["kk"], p["ka"],
              per_head(p["rk"]), per_head(p["lnw"]), per_head(p["lnb"])]
    hm = lambda: pltpu.VMEM((RW_HEADS, tb, RW_HEAD), F32)
    zspec = lambda width, col: pl.BlockSpec((tb, width), lambda gi, i: (row_block0 + gi * n_tiles + i, col))
    sspec = pl.BlockSpec((1, RW_HEADS, RW_HEAD, RW_HEAD), lambda gi, i: (gi, 0, 0, 0))
    return pl.pallas_call(
        kern,
        grid=(n_groups, n_tiles),
        in_specs=[zspec(RW_WIDTH, cb), zspec(RW_WIDTH, cb + 1), zspec(RW_WIDTH, cb + 2),
                  zspec(LORA_PAD, lora_cb),
                  pl.BlockSpec((1, 1, SHIFT_PAD), lambda gi, i: (gi, 0, 0)),
                  sspec]
                 + [full(a) for a in params] + cast_specs,
        out_specs=[pl.BlockSpec((tb, RW_WIDTH), lambda gi, i: (gi * n_tiles + i, 0)), sspec,
                   pl.BlockSpec((1, 1, SHIFT_PAD), lambda gi, i: (gi, 0, 0))] + cast_specs,
        out_shape=[jax.ShapeDtypeStruct((n_groups * n_tiles * tb, RW_WIDTH), BF16),
                   jax.ShapeDtypeStruct((n_groups, RW_HEADS, RW_HEAD, RW_HEAD), F32),
                   jax.ShapeDtypeStruct((n_groups, 1, SHIFT_PAD), F32)] + cast_shapes,
        scratch_shapes=[pltpu.VMEM((1, SHIFT_PAD), F32)] + [hm() for _ in range(8)],
        compiler_params=_cparams(("parallel", "arbitrary")),
        name="rwkv7_mixer",
    )(z, z, z, z, zprev0, s0, *params, *cast)


def _rwkv_sample_kernel(zr_ref, zk_ref, zv_ref, zl_ref, shr_ref, shk_ref, shv_ref, shl_ref,
                        mur_ref, muk_ref, muv_ref, mul_ref, s0_ref,
                        w0_ref, w2t_ref, a0_ref, a2t_ref, g2t_ref, kk_ref, ka_ref, rk_ref,
                        lnw_ref, lnb_ref,
                        o_ref, sout_ref,
                        dec_s, a_s, b_s, k_s, r_s, v_s, o_s,
                        *, n_tok, hpb):
    ns = LANES
    ch = hpb * RW_HEAD

    def lanes(x):
        return jnp.concatenate([x] * n_tok, axis=1)

    def shifted(z_ref, sh_ref, mu_ref):
        z = z_ref[...]
        prev = jnp.concatenate([sh_ref[...], z[:, :(n_tok - 1) * ns]], axis=1)
        return z + lanes(mu_ref[...]) * (prev - z)

    r = shifted(zr_ref, shr_ref, mur_ref)
    k = shifted(zk_ref, shk_ref, muk_ref)
    v = shifted(zv_ref, shv_ref, muv_ref)
    zl = shifted(zl_ref, shl_ref, mul_ref)
    wd = zl[0:DECAY_LORA]
    ad = zl[DECAY_LORA:DECAY_LORA + AAA_LORA]
    gd = zl[DECAY_LORA + AAA_LORA:DECAY_LORA + AAA_LORA + GATE_LORA]

    wlog = -jax.nn.softplus(-(lanes(w0_ref[...]) + _mm(w2t_ref[...], jnp.tanh(wd)))) - 0.5
    lw = -jnp.exp(wlog)
    a = jax.nn.sigmoid(lanes(a0_ref[...]) + _mm(a2t_ref[...], ad))
    g = _mm(g2t_ref[...], jax.nn.sigmoid(gd))
    kk = k * lanes(kk_ref[...])
    km = k * (1.0 + (a - 1.0) * lanes(ka_ref[...]))

    def head_sum(x):
        x3 = x.reshape(hpb, RW_HEAD, n_tok * ns)
        s = jnp.sum(x3, axis=1, keepdims=True)
        return jnp.broadcast_to(s, x3.shape).reshape(ch, n_tok * ns)

    kkn = kk * lax.rsqrt(jnp.maximum(head_sum(kk * kk), 1e-24))
    dec_s[...] = jnp.exp(lw)
    a_s[...] = -kkn
    b_s[...] = kkn * a
    k_s[...] = km
    r_s[...] = r
    v_s[...] = v

    for hh in range(hpb):
        hrows = slice(hh * RW_HEAD, (hh + 1) * RW_HEAD)

        def body(i8, carry, hh=hh, hrows=hrows):
            base = pl.multiple_of(i8 * 8, 8)
            rows8 = pl.ds(pl.multiple_of(hh * RW_HEAD + base, 8), 8)
            o_rows = [[] for _ in range(n_tok)]
            for j in range(8):
                s = s0_ref[hh, base + j]
                for t in range(n_tok):
                    tl = slice(t * ns, (t + 1) * ns)
                    sa = jnp.sum(s * a_s[hrows, tl], axis=0, keepdims=True)
                    v_row = v_s[rows8, tl][j:j + 1, :]
                    s = s * dec_s[hrows, tl] + sa * b_s[hrows, tl] + v_row * k_s[hrows, tl]
                    o_rows[t].append(jnp.sum(s * r_s[hrows, tl], axis=0, keepdims=True))
                sout_ref[hh, base + j] = s
            for t in range(n_tok):
                o_s[rows8, t * ns:(t + 1) * ns] = jnp.concatenate(o_rows[t], axis=0)
            return carry

        lax.fori_loop(0, RW_HEAD // 8, body, 0)

    o = o_s[...]
    inv_n = 1.0 / RW_HEAD
    mean = head_sum(o) * inv_n
    var = head_sum(jnp.square(o - mean)) * inv_n
    on = ((o - mean) * lax.rsqrt(var + RW_GN_EPS)) * lanes(lnw_ref[...]) + lanes(lnb_ref[...])
    bonus = head_sum(r * km * lanes(rk_ref[...])) * v
    out = (on + bonus) * g
    for t in range(n_tok):
        o_ref[t] = out[:, t * ns:(t + 1) * ns].T.astype(BF16)


def _rwkv_sample_mixer(z_t, shift_t, mu_t, state_t, p_t, *, n_tok, hpb):
    ch = hpb * RW_HEAD
    n_steps = RW_HEADS // hpb
    nl = n_tok * LANES
    seg = RW_WIDTH // ch
    zspec = lambda s: pl.BlockSpec((ch, nl), lambda h: (s * seg + h, 0))
    cspec = lambda s: pl.BlockSpec((ch, LANES), lambda h: (s * seg + h, 0))
    lora_blk = 3 * RW_WIDTH // LORA_PAD
    zl_spec = pl.BlockSpec((LORA_PAD, nl), lambda h: (lora_blk, 0))
    cl_spec = pl.BlockSpec((LORA_PAD, LANES), lambda h: (lora_blk, 0))
    hspec = pl.BlockSpec((ch, LANES), lambda h: (h, 0))
    wspec = lambda k: pl.BlockSpec((ch, k), lambda h: (h, 0))
    sspec = pl.BlockSpec((hpb, RW_HEAD, RW_HEAD, LANES), lambda h: (h, 0, 0, 0))
    buf = lambda: pltpu.VMEM((ch, nl), F32)
    return pl.pallas_call(
        functools.partial(_rwkv_sample_kernel, n_tok=n_tok, hpb=hpb),
        grid=(n_steps,),
        in_specs=[zspec(0), zspec(1), zspec(2), zl_spec,
                  cspec(0), cspec(1), cspec(2), cl_spec,
                  cspec(0), cspec(1), cspec(2), cl_spec,
                  sspec,
                  hspec, wspec(DECAY_LORA), hspec, wspec(AAA_LORA), wspec(GATE_LORA),
                  hspec, hspec, hspec, hspec, hspec],
        out_specs=[pl.BlockSpec((n_tok, LANES, ch), lambda h: (0, 0, h)), sspec],
        out_shape=[jax.ShapeDtypeStruct((n_tok, LANES, RW_WIDTH), BF16),
                   jax.ShapeDtypeStruct(state_t.shape, F32)],
        scratch_shapes=[buf() for _ in range(7)],
        compiler_params=_cparams(("parallel",)),
        name="rwkv7_sample",
    )(z_t, z_t, z_t, z_t, shift_t, shift_t, shift_t, shift_t, mu_t, mu_t, mu_t, mu_t, state_t,
      p_t["w0"], p_t["w2t"], p_t["a0"], p_t["a2t"], p_t["g2t"], p_t["kk"], p_t["ka"], p_t["rk"],
      p_t["lnw"], p_t["lnb"])


def _retention_kernel(zq_ref, zk_ref, zv_ref, zg_ref, cos_ref, sin_ref, dmask_ref, iscale_ref,
                      kscale_ref, sdec_ref, s0_ref, o_ref, sout_ref,
                      *, rows, c, n_real, per_chunk_state):
    i = pl.program_id(1)
    n_chunks = rows // c

    if per_chunk_state:
        sout_ref[...] = s0_ref[...]
        pr = lax.broadcasted_iota(jnp.int32, (rows, n_chunks * n_real), 0)
        pc = lax.broadcasted_iota(jnp.int32, (rows, n_chunks * n_real), 1)
        place = jnp.where((pr // c == pc // n_real) & (pr % c - (c - n_real) == pc % n_real),
                          1.0, 0.0).astype(BF16)
        load = lambda ref: _mm_exact_lhs(place, ref[...])
    else:
        load = lambda ref: ref[...]

        @pl.when(i == 0)
        def _():
            sout_ref[...] = s0_ref[...]

    lane = lax.broadcasted_iota(jnp.int32, (rows, RET_WIDTH), 1)
    even = (lane % 2) == 0
    cos = jnp.concatenate([cos_ref[...]] * RET_HEADS, axis=-1)
    sin = jnp.concatenate([sin_ref[...]] * RET_HEADS, axis=-1)

    def rot(x):
        partner = jnp.where(even, pltpu.roll(x, RET_WIDTH - 1, axis=1), pltpu.roll(x, 1, axis=1))
        return x * cos + partner * sin

    q = rot(load(zq_ref))
    k = rot(load(zk_ref)) * (RET_HEAD ** -0.5)
    v = load(zv_ref)
    g = load(zg_ref)

    out_rows = []
    for ch in range(n_chunks):
        rs = slice(ch * c, (ch + 1) * c)
        sidx = ch if per_chunk_state else 0
        out_heads = []
        for h in range(RET_HEADS):
            hs = slice(h * RET_HEAD, (h + 1) * RET_HEAD)
            qh, kh, vh = q[rs, hs], k[rs, hs], v[rs, hs]
            s_prev = sout_ref[sidx, h]
            scores = _mm_nt(qh, kh) * dmask_ref[h]
            o = _mm(scores, vh) + _mm(qh, s_prev) * iscale_ref[h]
            sout_ref[sidx, h] = s_prev * sdec_ref[h] + _mm_tn(kh * kscale_ref[h], vh)
            o = o * lax.rsqrt(jnp.mean(o * o, axis=-1, keepdims=True) + RET_GN_EPS)
            gh = g[rs, hs]
            out_heads.append(o * (gh * jax.nn.sigmoid(gh)))
        out_rows.append(jnp.concatenate(out_heads, axis=-1))
    o_out = jnp.concatenate(out_rows, axis=0).astype(BF16)
    if per_chunk_state:
        o_out = lax.dot_general(place, o_out, (((0,), (0,)), ((), ())),
                                preferred_element_type=F32).astype(BF16)
    o_ref[...] = o_out


def _retention_mixer(z, row_block0, n_groups, n_tiles, cos, sin, tabs, s0, *, rows, c, n_real,
                     per_chunk_state, pos_per_tile):
    n_states = s0.shape[0]
    sb = n_states // n_groups
    kern = functools.partial(_retention_kernel, rows=rows, c=c, n_real=n_real,
                             per_chunk_state=per_chunk_state)
    rows_io = rows // c * n_real if per_chunk_state else rows
    zspec = lambda col: pl.BlockSpec((rows_io, RET_WIDTH), lambda gi, i: (row_block0 + gi * n_tiles + i, col))
    full = lambda arr: pl.BlockSpec(arr.shape, lambda gi, i: (0,) * arr.ndim)
    if pos_per_tile:
        tspec = pl.BlockSpec((rows, RET_HEAD), lambda gi, i: (i, 0))
    else:
        tspec = pl.BlockSpec((rows, RET_HEAD), lambda gi, i: (0, 0))
    sspec = pl.BlockSpec((sb, RET_HEADS, RET_HEAD, RET_HEAD), lambda gi, i: (gi, 0, 0, 0))
    dmask, iscale, kscale, sdec = tabs
    return pl.pallas_call(
        kern,
        grid=(n_groups, n_tiles),
        in_specs=[zspec(0), zspec(1), zspec(2), zspec(3), tspec, tspec,
                  full(dmask), full(iscale), full(kscale), full(sdec), sspec],
        out_specs=[pl.BlockSpec((rows_io, RET_WIDTH), lambda gi, i: (gi * n_tiles + i, 0)), sspec],
        out_shape=[jax.ShapeDtypeStruct((n_groups * n_tiles * rows_io, RET_WIDTH), BF16),
                   jax.ShapeDtypeStruct(s0.shape, F32)],
        compiler_params=_cparams(("parallel", "arbitrary")),
        name="retention_mixer",
    )(z, z, z, z, cos, sin, dmask, iscale, kscale, sdec, s0)


def _retention_tables(c, n_real):
    log_gamma = jnp.log(1.0 - 2.0 ** (-5.0 - jnp.arange(RET_HEADS, dtype=F32)))
    r = jnp.arange(c, dtype=F32)
    idx = r - float(c - n_real)
    diff = r[:, None] - r[None, :]
    dmask = jnp.where(diff[None] >= 0,
                      jnp.exp(log_gamma[:, None, None] * jnp.maximum(diff, 0.0)[None]), 0.0)
    iscale = jnp.exp(log_gamma[:, None] * (idx + 1.0)[None, :])[:, :, None]
    kscale = jnp.exp(log_gamma[:, None] * (n_real - 1.0 - idx)[None, :])[:, :, None]
    sdec = jnp.broadcast_to(jnp.exp(log_gamma * n_real)[:, None, None], (RET_HEADS, 1, RET_HEAD))
    return dmask, iscale, kscale, sdec


def _rotary_tables(pos):
    inv_freq = 1.0 / (ROPE_BASE ** jnp.linspace(0.0, 1.0, RET_HEAD // 2, dtype=F32))
    ang = pos.astype(F32)[:, None] * inv_freq[None, :]
    cos = jnp.cos(ang)
    sin = jnp.sin(ang)
    cos2 = jnp.repeat(cos, 2, axis=-1)
    sin2 = jnp.stack([-sin, sin], axis=-1).reshape(pos.shape[0], RET_HEAD)
    return cos2, sin2


def _outproj_kernel(oa_ref, ob_ref, w_ref, x_ref, g_ref, h_ref, hn_ref):
    acc = jnp.dot(oa_ref[...], w_ref[0:RW_WIDTH, :], preferred_element_type=F32)
    acc = acc + jnp.dot(ob_ref[...], w_ref[RW_WIDTH:, :], preferred_element_type=F32)
    h = x_ref[...] + acc
    h_ref[...] = h
    hn_ref[...] = _rms_norm_bf16(h, g_ref[...])


def _out_projection(o_a, o_b, w_out, x2d, norm_g, tm):
    m = o_a.shape[0]
    return pl.pallas_call(
        _outproj_kernel,
        grid=(m // tm,),
        in_specs=[
            pl.BlockSpec((tm, RW_WIDTH), lambda i: (i, 0)),
            pl.BlockSpec((tm, RET_WIDTH), lambda i: (i, 0)),
            pl.BlockSpec((D_MODEL, D_MODEL), lambda i: (0, 0)),
            pl.BlockSpec((tm, D_MODEL), lambda i: (i, 0)),
            pl.BlockSpec((1, D_MODEL), lambda i: (0, 0)),
        ],
        out_specs=[pl.BlockSpec((tm, D_MODEL), lambda i: (i, 0)),
                   pl.BlockSpec((tm, D_MODEL), lambda i: (i, 0))],
        out_shape=[jax.ShapeDtypeStruct((m, D_MODEL), F32), jax.ShapeDtypeStruct((m, D_MODEL), BF16)],
        compiler_params=_cparams(("parallel",)),
        name="out_projection",
    )(o_a, o_b, w_out, x2d, norm_g)


def _ffn_kernel(hn_ref, wg_ref, wu_ref, wd_ref, h_ref, g_ref, y_ref, acc_ref, *, h_slices):
    f = pl.program_id(1)

    @pl.when(f == 0)
    def _():
        acc_ref[...] = jnp.zeros_like(acc_ref)

    @pl.when(f < h_slices)
    def _():
        rows = h_ref.shape[0]
        sl = pl.ds(pl.multiple_of(f * rows, rows), rows)
        acc_ref[sl, :] += h_ref[...]

    hn = hn_ref[...]
    gate = jnp.dot(hn, wg_ref[...], preferred_element_type=F32)
    up = jnp.dot(hn, wu_ref[...], preferred_element_type=F32)
    act = (gate * jax.nn.sigmoid(gate)) * up
    acc_ref[...] += jnp.dot(act.astype(BF16), wd_ref[...], preferred_element_type=F32)

    @pl.when(f == pl.num_programs(1) - 1)
    def _():
        h = acc_ref[...]
        ms = jnp.mean(h * h, axis=-1, keepdims=True)
        y_ref[...] = (h * lax.rsqrt(ms + RMS_EPS)) * g_ref[...]


def _ffn(hn, w_gate, w_up, w_down, h, norm_g, tm, tf, h_slices=8):
    m = hn.shape[0]
    assert D_FF // tf >= h_slices
    return pl.pallas_call(
        functools.partial(_ffn_kernel, h_slices=h_slices),
        grid=(m // tm, D_FF // tf),
        in_specs=[
            pl.BlockSpec((tm, D_MODEL), lambda i, f: (i, 0)),
            pl.BlockSpec((D_MODEL, tf), lambda i, f: (0, f)),
            pl.BlockSpec((D_MODEL, tf), lambda i, f: (0, f)),
            pl.BlockSpec((tf, D_MODEL), lambda i, f: (f, 0)),
            pl.BlockSpec((tm // h_slices, D_MODEL),
                         lambda i, f: (i * h_slices + jnp.minimum(f, h_slices - 1), 0)),
            pl.BlockSpec((1, D_MODEL), lambda i, f: (0, 0)),
        ],
        out_specs=pl.BlockSpec((tm, D_MODEL), lambda i, f: (i, 0)),
        out_shape=jax.ShapeDtypeStruct((m, D_MODEL), F32),
        scratch_shapes=[pltpu.VMEM((tm, D_MODEL), F32)],
        compiler_params=_cparams(("parallel", "arbitrary")),
        name="swiglu_ffn",
    )(hn, w_gate, w_up, w_down, h, norm_g)


def _pad_cols(a, n):
    return jnp.pad(a, ((0, 0), (0, n - a.shape[1])))


def _pad_rows(a, n):
    return jnp.pad(a, ((0, n - a.shape[0]), (0, 0)))


def kernel(x_prompt, x_sample, state_shift, state_rwkv, state_ret, meta_tokens, norm_mix, w_in,
           rwkv_mu, rwkv_w0, rwkv_w2, rwkv_a0, rwkv_a2, rwkv_g2, rwkv_kk, rwkv_ka, rwkv_rk,
           rwkv_ln_w, rwkv_ln_b, w_out, norm_ffn, w_gate, w_up, w_down, norm_final):
    n_b, seq, d = x_prompt.shape
    n_s, dec_seq, _ = x_sample.shape
    n_p = n_b * seq
    n_d = n_s * dec_seq
    depth = w_in.shape[0]
    assert depth == 1 and d == D_MODEL and n_s == LANES

    w_in_t = jnp.transpose(w_in[0])
    w_in_p = _w_in_layout(w_in_t, tn=512)
    row = lambda a: a.reshape(1, -1).astype(F32)
    w2p = jnp.concatenate([rwkv_w2[0], jnp.zeros((128 - DECAY_LORA, RW_WIDTH), F32)], axis=0)
    a2p = jnp.concatenate([jnp.zeros((DECAY_LORA, RW_WIDTH), F32), rwkv_a2[0]], axis=0)
    g2p = jnp.concatenate([rwkv_g2[0], jnp.zeros((256 - GATE_LORA, RW_WIDTH), F32)], axis=0)
    rw_params = dict(mu=_pad_cols(row(rwkv_mu[0]), SHIFT_PAD), w0=row(rwkv_w0[0]), w2=w2p,
                     a0=row(rwkv_a0[0]), a2=a2p, g2=g2p, kk=row(rwkv_kk[0]), ka=row(rwkv_ka[0]),
                     rk=row(rwkv_rk[0]), lnw=row(rwkv_ln_w[0]), lnb=row(rwkv_ln_b[0]))
    col = lambda a: jnp.broadcast_to(a.reshape(-1, 1).astype(F32), (a.size, LANES))
    rw_params_t = dict(w0=col(rwkv_w0[0]), w2t=rwkv_w2[0].T, a0=col(rwkv_a0[0]), a2t=rwkv_a2[0].T,
                       g2t=rwkv_g2[0].T, kk=col(rwkv_kk[0]), ka=col(rwkv_ka[0]), rk=col(rwkv_rk[0]),
                       lnw=col(rwkv_ln_w[0]), lnb=col(rwkv_ln_b[0]))

    x_p = x_prompt.reshape(n_p, d)
    x_s = x_sample.reshape(n_d, d)
    x_sm = jnp.concatenate([x_s, meta_tokens.astype(F32)], axis=0)
    x_ts = jnp.transpose(x_sample, (1, 0, 2)).reshape(n_d, d)
    g_mix = row(norm_mix[0])
    z_p = _in_projection(x_p, g_mix, w_in_p, tm=1024, tn=1536)
    z_sm = _in_projection(x_sm, g_mix, w_in_p, tm=x_sm.shape[0], tn=768)
    z_st = _in_projection_t(w_in_t, x_ts, g_mix, SHIFT_PAD, tm=512)

    z_meta = jnp.pad(z_sm[n_d:], ((RET_CHUNK - N_META, 0), (0, 0)))
    srow = 2 * dec_seq

    zero_prev = jnp.zeros((1, 1, SHIFT_PAD), F32)
    zero_rw = jnp.zeros((1, RW_HEADS, RW_HEAD, RW_HEAD), F32)
    c_rw = 64
    _, s_rw_meta, zlast_meta = _rwkv_mixer(z_meta, 1, 1, 1, zero_prev, zero_rw, rw_params, tb=c_rw, c=c_rw)
    oa_p, rwkv_p, zlast_p, w_out_b, w_gate_b, w_up_b, w_down_b = _rwkv_mixer(
        z_p, 0, n_b, seq // 256, jnp.broadcast_to(zlast_meta, (n_b, 1, SHIFT_PAD)),
        jnp.broadcast_to(s_rw_meta, (n_b,) + s_rw_meta.shape[1:]), rw_params, tb=256, c=c_rw,
        cast=(w_out[0], w_gate[0], w_up[0], w_down[0]))
    shift_t = _pad_rows(jnp.transpose(state_shift[0]), SHIFT_PAD)
    mu_t = _pad_rows(col(rwkv_mu[0]), SHIFT_PAD)
    state_t = jnp.transpose(state_rwkv[0], (1, 2, 3, 0))
    oa_st, rwkv_st = _rwkv_sample_mixer(z_st, shift_t, mu_t, state_t, rw_params_t, n_tok=dec_seq, hpb=2)
    oa_s = jnp.transpose(oa_st, (1, 0, 2)).reshape(n_d, RW_WIDTH)
    rwkv_s = jnp.transpose(rwkv_st, (3, 0, 1, 2))

    past_len = 16384
    tabs_full = _retention_tables(RET_CHUNK, RET_CHUNK)
    tabs_meta = _retention_tables(RET_CHUNK, N_META)
    tabs_smp = _retention_tables(srow, dec_seq)
    cos_m, sin_m = _rotary_tables(jnp.arange(RET_CHUNK) - (RET_CHUNK - N_META))
    cos_p, sin_p = _rotary_tables(N_META + jnp.arange(seq))
    sb_rt = 4
    pos_s = past_len + jnp.tile(jnp.arange(srow) - (srow - dec_seq), sb_rt)
    cos_s, sin_s = _rotary_tables(pos_s)
    zero_rt = jnp.zeros((1, RET_HEADS, RET_HEAD, RET_HEAD), F32)
    _, s_rt_meta = _retention_mixer(z_meta, 0, 1, 1, cos_m, sin_m, tabs_meta, zero_rt,
                                    rows=RET_CHUNK, c=RET_CHUNK, n_real=N_META, per_chunk_state=False,
                                    pos_per_tile=False)
    ob_p, ret_p = _retention_mixer(z_p, 0, n_b, seq // RET_CHUNK, cos_p, sin_p, tabs_full,
                                   jnp.broadcast_to(s_rt_meta, (n_b,) + s_rt_meta.shape[1:]),
                                   rows=RET_CHUNK, c=RET_CHUNK, n_real=RET_CHUNK, per_chunk_state=False,
                                   pos_per_tile=True)
    ob_s, ret_s = _retention_mixer(z_sm, 0, n_s // sb_rt, 1, cos_s, sin_s, tabs_smp,
                                   state_ret.reshape(n_s, RET_HEADS, RET_HEAD, RET_HEAD),
                                   rows=sb_rt * srow, c=srow, n_real=dec_seq, per_chunk_state=True,
                                   pos_per_tile=False)

    g_ffn = row(norm_ffn[0])
    g_fin = row(norm_final)
    h_p, hn_p = _out_projection(oa_p, ob_p, w_out_b, x_p, g_ffn, tm=512)
    h_s, hn_s = _out_projection(oa_s, ob_s, w_out_b, x_s, g_ffn, tm=n_d)
    y_p = _ffn(hn_p, w_gate_b, w_up_b, w_down_b, h_p, g_fin, tm=1024, tf=512)
    y_s = _ffn(hn_s, w_gate_b, w_up_b, w_down_b, h_s, g_fin, tm=n_d, tf=512)

    y_prompt = y_p.reshape(n_b, seq, d)
    y_sample = y_s.reshape(n_s, dec_seq, d)
    shift_p = zlast_p[:, 0, :RW_COLS][None]
    shift_s = jnp.transpose(z_st[:RW_COLS, (dec_seq - 1) * n_s:])[None]
    return (y_prompt, y_sample, shift_p, rwkv_p[None], ret_p[None], shift_s, rwkv_s[None], ret_s[None])
```

```python
import functools
import math

import jax
import jax.numpy as jnp
from jax import lax
from jax.experimental import pallas as pl
from jax.experimental.pallas import tpu as pltpu

F32 = jnp.float32
BF16 = jnp.bfloat16

D_MODEL = 2048
N_META = 16
RW_WIDTH = 1024
RW_HEAD = 64
RW_HEADS = 16
DECAY_LORA = 64
AAA_LORA = 64
GATE_LORA = 160
RW_COLS = 3 * RW_WIDTH + DECAY_LORA + AAA_LORA + GATE_LORA
RET_WIDTH = 1024
RET_HEADS = 4
RET_HEAD = 256
RET_CHUNK = 128
D_FF = 5632
RMS_EPS = 1e-6
RW_GN_EPS = 64e-5
RET_GN_EPS = 1e-6
ROPE_BASE = 10000.0
LANES = 128

LORA_PAD = 512
Z_COLS = 4 * RET_WIDTH + 3 * RW_WIDTH + LORA_PAD
RW_OFF = 4 * RET_WIDTH
SHIFT_PAD = 3 * RW_WIDTH + LORA_PAD

VMEM_LIMIT = 60 * 1024 * 1024


def _cparams(sem):
    return pltpu.CompilerParams(dimension_semantics=sem, vmem_limit_bytes=VMEM_LIMIT)


def _mm(a, b):
    return jnp.dot(a.astype(BF16), b.astype(BF16), preferred_element_type=F32)


def _mm_nt(a, b):
    return lax.dot_general(a.astype(BF16), b.astype(BF16), (((1,), (1,)), ((), ())),
                           preferred_element_type=F32)


def _mm_tn(a, b):
    return lax.dot_general(a.astype(BF16), b.astype(BF16), (((0,), (0,)), ((), ())),
                           preferred_element_type=F32)


def _bmm(a, b):
    return jnp.einsum("bmk,bkn->bmn", a.astype(BF16), b.astype(BF16), preferred_element_type=F32)


def _bmm_nt(a, b):
    return jnp.einsum("bmk,bnk->bmn", a.astype(BF16), b.astype(BF16), preferred_element_type=F32)


def _bmm_tn(a, b):
    return jnp.einsum("bkm,bkn->bmn", a.astype(BF16), b.astype(BF16), preferred_element_type=F32)


def _mm_exact_lhs(m_bf16, x):
    hi = x.astype(BF16)
    r1 = x - hi.astype(F32)
    mid = r1.astype(BF16)
    lo = (r1 - mid.astype(F32)).astype(BF16)
    d = functools.partial(jnp.dot, preferred_element_type=F32)
    return d(m_bf16, hi) + d(m_bf16, mid) + d(m_bf16, lo)


def _rms_norm_bf16(x, g):
    ms = jnp.mean(x * x, axis=-1, keepdims=True)
    return ((x * lax.rsqrt(ms + RMS_EPS)) * g).astype(BF16)


def _w_in_layout_kernel(w_ref, o_ref):
    o_ref[...] = w_ref[...].T.astype(BF16)


def _w_in_layout(w_in_t, tn):
    n, d = w_in_t.shape
    n_ret = 4 * RET_WIDTH // tn

    def src_row(c):
        align = math.gcd(RW_COLS, tn)
        return (pl.multiple_of(jnp.where(c < n_ret, RW_COLS + tn * c, tn * (c - n_ret)), align), 0)

    return pl.pallas_call(
        _w_in_layout_kernel,
        grid=(Z_COLS // tn,),
        in_specs=[pl.BlockSpec((pl.Element(tn), pl.Element(d)), src_row)],
        out_specs=pl.BlockSpec((d, tn), lambda c: (0, c)),
        out_shape=jax.ShapeDtypeStruct((d, Z_COLS), BF16),
        compiler_params=_cparams(("parallel",)),
        name="w_in_layout",
    )(w_in_t)


def _inproj_kernel(x_ref, g_ref, w_ref, o_ref, xn_ref):
    @pl.when(pl.program_id(1) == 0)
    def _():
        xn_ref[...] = _rms_norm_bf16(x_ref[...], g_ref[...])

    o_ref[...] = jnp.dot(xn_ref[...], w_ref[...], preferred_element_type=F32)


def _in_projection(x2d, norm_g, w_in_p, tm, tn):
    m = x2d.shape[0]
    return pl.pallas_call(
        _inproj_kernel,
        grid=(m // tm, Z_COLS // tn),
        in_specs=[
            pl.BlockSpec((tm, D_MODEL), lambda i, j: (i, 0)),
            pl.BlockSpec((1, D_MODEL), lambda i, j: (0, 0)),
            pl.BlockSpec((D_MODEL, tn), lambda i, j: (0, j)),
        ],
        out_specs=pl.BlockSpec((tm, tn), lambda i, j: (i, j)),
        out_shape=jax.ShapeDtypeStruct((m, Z_COLS), F32),
        scratch_shapes=[pltpu.VMEM((tm, D_MODEL), BF16)],
        compiler_params=_cparams(("parallel", "arbitrary")),
        name="in_projection",
    )(x2d, norm_g, w_in_p)


def _inproj_t_kernel(w_ref, x_ref, g_ref, o_ref, xn_ref):
    @pl.when(pl.program_id(0) == 0)
    def _():
        xn_ref[...] = _rms_norm_bf16(x_ref[...], g_ref[...])

    o_ref[...] = _mm_nt(w_ref[...], xn_ref[...])


def _in_projection_t(w_in_t, x2d, norm_g, n_rows, tm):
    m, d = x2d.shape
    return pl.pallas_call(
        _inproj_t_kernel,
        grid=(n_rows // tm,),
        in_specs=[
            pl.BlockSpec((tm, d), lambda i: (i, 0)),
            pl.BlockSpec((m, d), lambda i: (0, 0)),
            pl.BlockSpec((1, d), lambda i: (0, 0)),
        ],
        out_specs=pl.BlockSpec((tm, m), lambda i: (i, 0)),
        out_shape=jax.ShapeDtypeStruct((n_rows, m), F32),
        scratch_shapes=[pltpu.VMEM((m, d), BF16)],
        compiler_params=_cparams(("arbitrary",)),
        name="in_projection_t",
    )(w_in_t, x2d, norm_g)


def _rwkv_kernel(zr_ref, zk_ref, zv_ref, zl_ref, zprev0_ref, s0_ref,
                 mu_ref, w0_ref, w2_ref, a0_ref, a2_ref, g2_ref, kk_ref, ka_ref, rk_ref,
                 lnw_ref, lnb_ref,
                 o_ref, sout_ref, zlast_ref,
                 carry_ref, sbd_s, r_s, km_s, v_s, kk_s, a_s, lw_s, cum_s, o_s,
                 *, tb, c, n_factors):
    i = pl.program_id(1)
    n_chunks = tb // c
    n_pairs = RW_HEADS // 2
    pw = 2 * RW_HEAD

    @pl.when(i == 0)
    def _():
        carry_ref[...] = zprev0_ref[0]
        zero = jnp.zeros((RW_HEAD, RW_HEAD), F32)
        for p in range(n_pairs):
            top = jnp.concatenate([s0_ref[0, 2 * p], zero], axis=1)
            bot = jnp.concatenate([zero, s0_ref[0, 2 * p + 1]], axis=1)
            sbd_s[p] = jnp.concatenate([top, bot], axis=0)

    row = lax.broadcasted_iota(jnp.int32, (tb, 1), 0)

    def shifted(z_ref, lo, hi):
        z = z_ref[...]
        prev = pltpu.roll(z, 1, axis=0)
        prev = jnp.concatenate([jnp.where(row[:8] == 0, carry_ref[:, lo:hi], prev[:8]), prev[8:]], axis=0)
        zs = z + mu_ref[:, lo:hi] * (prev - z)
        carry_ref[:, lo:hi] = z[tb - 1:tb, :]
        return zs

    w = RW_WIDTH
    r = shifted(zr_ref, 0, w)
    k = shifted(zk_ref, w, 2 * w)
    v = shifted(zv_ref, 2 * w, 3 * w)
    zl = shifted(zl_ref, 3 * w, 3 * w + LORA_PAD)

    lo2 = zl[:, 0:128]
    wlog = -jax.nn.softplus(-(w0_ref[...] + _mm(jnp.tanh(lo2), w2_ref[...]))) - 0.5
    lw = -jnp.exp(wlog)
    a = jax.nn.sigmoid(a0_ref[...] + _mm(lo2, a2_ref[...]))
    g = _mm(jax.nn.sigmoid(zl[:, 128:384]), g2_ref[...])
    kk = k * kk_ref[...]
    km = k * (1.0 + (a - 1.0) * ka_ref[...])

    ri = lax.broadcasted_iota(jnp.int32, (tb, tb), 0)
    ci = lax.broadcasted_iota(jnp.int32, (tb, tb), 1)
    tri = jnp.where((ri // c == ci // c) & (ri >= ci), 1.0, 0.0).astype(BF16)
    cum = _mm_exact_lhs(tri, lw)

    for dst, val in ((r_s, r), (km_s, km), (v_s, v), (a_s, a), (lw_s, lw), (cum_s, cum), (kk_s, kk)):
        dst[...] = val

    rr = lax.broadcasted_iota(jnp.int32, (c, c), 0)
    cc = lax.broadcasted_iota(jnp.int32, (c, c), 1)
    strict = (rr > cc)[None]
    eye = jnp.where(rr == cc, 1.0, 0.0).astype(F32)[None]
    rr2 = lax.broadcasted_iota(jnp.int32, (c, 2 * c), 0)
    cc2 = lax.broadcasted_iota(jnp.int32, (c, 2 * c), 1)
    strict_k = ((cc2 >= c) & (rr2 > cc2 - c))[None]
    incl_bk = (rr2 >= cc2 % c)[None]
    lo = (lax.broadcasted_iota(jnp.int32, (1, 1, pw), 2) < RW_HEAD)
    bd_r = lax.broadcasted_iota(jnp.int32, (pw, pw), 0) // RW_HEAD
    bd_c = lax.broadcasted_iota(jnp.int32, (pw, pw), 1) // RW_HEAD
    block_diag = (bd_r == bd_c)[None]
    rk = rk_ref[...]
    lnw = lnw_ref[...]
    lnb = lnb_ref[...]

    def head_sum(x):
        s_lo = jnp.sum(jnp.where(lo, x, 0.0), axis=-1, keepdims=True)
        s_hi = jnp.sum(jnp.where(lo, 0.0, x), axis=-1, keepdims=True)
        return jnp.where(lo, s_lo, s_hi)

    def pick(x16):
        x4 = x16.reshape(n_pairs, 2, c, pw)
        return jnp.where(lo, x4[:, 0], x4[:, 1])

    def body(ch, carry):
        rows = pl.ds(pl.multiple_of(ch * c, c), c)

        def pairs(s):
            x = s[rows, :]
            return jnp.stack([x[:, p * pw:(p + 1) * pw] for p in range(n_pairs)], axis=0)

        rh, kmh, vh, ah, lwh, cumh, kkh = [pairs(s) for s in (r_s, km_s, v_s, a_s, lw_s, cum_s, kk_s)]
        s_prev = sbd_s[...]
        kkn = kkh * lax.rsqrt(jnp.maximum(head_sum(kkh * kkh), 1e-24))
        bvec = kkn * ah
        cum_end = cumh[:, c - 1:c, :]
        p_inc = jnp.exp(cumh)
        p_exc = jnp.exp(cumh - lwh)
        p_inv = jnp.exp(-cumh)
        p_end = jnp.exp(cum_end)
        p_rel = jnp.exp(cum_end - cumh)
        a_t = -(kkn * p_exc)
        r_t = rh * p_inc
        b_t = bvec * p_inv
        k_t = kmh * p_inv
        b_h = bvec * p_rel
        k_h = kmh * p_rel

        ar = jnp.concatenate([jnp.where(lo, a_t, 0.0), jnp.where(lo, r_t, 0.0),
                              jnp.where(lo, 0.0, a_t), jnp.where(lo, 0.0, r_t)], axis=1)
        prod = _bmm_nt(ar, jnp.concatenate([b_t, k_t, s_prev], axis=1))
        prod = prod.reshape(RW_HEADS, 2 * c, 2 * c + pw)
        a_rows, r_rows = prod[:, :c], prod[:, c:]
        lmat = jnp.where(strict, a_rows[:, :, :c], 0.0)
        aak_wide = jnp.where(strict_k, a_rows[:, :, :2 * c], 0.0)
        arbk = jnp.where(incl_bk, r_rows[:, :, :2 * c], 0.0)
        tinv = eye + lmat
        if n_factors > 1:
            lp = _bmm(lmat, lmat)
            for _ in range(n_factors - 2):
                st = _bmm(jnp.concatenate([lp, tinv], axis=1), lp)
                lp = st[:, :c]
                tinv = tinv + st[:, c:]
            tinv = tinv + _bmm(tinv, lp)
        v16 = jnp.repeat(vh, 2, axis=0)
        x = a_rows[:, :, 2 * c:] + _bmm(aak_wide, jnp.concatenate([v16, v16], axis=1))
        u16 = _bmm(tinv, x)
        o16 = r_rows[:, :, 2 * c:] + _bmm(arbk, jnp.concatenate([u16, v16], axis=1))
        u = pick(u16)
        o = pick(o16)
        s_upd = _bmm_tn(jnp.concatenate([u, vh], axis=1), jnp.concatenate([b_h, k_h], axis=1))
        sbd_s[...] = s_prev * p_end + jnp.where(block_diag, s_upd, 0.0)
        inv_n = 1.0 / RW_HEAD
        mean = head_sum(o) * inv_n
        var = head_sum(jnp.square(o - mean)) * inv_n
        on = ((o - mean) * lax.rsqrt(var + RW_GN_EPS)) * lnw + lnb
        out = on + head_sum(rh * kmh * rk) * vh
        for p in range(n_pairs):
            o_s[rows, p * pw:(p + 1) * pw] = out[p]
        return carry

    lax.fori_loop(0, n_chunks, body, 0)

    o_ref[...] = (o_s[...] * g).astype(BF16)
    zlast_ref[0] = carry_ref[...]

    @pl.when(i == pl.num_programs(1) - 1)
    def _():
        for p in range(n_pairs):
            blk = sbd_s[p]
            sout_ref[0, 2 * p] = blk[:RW_HEAD, :RW_HEAD]
            sout_ref[0, 2 * p + 1] = blk[RW_HEAD:, RW_HEAD:]


N_RWKV_INPUTS = 17
N_RWKV_OUTPUTS = 3


def _rwkv_and_cast_kernel(*refs, n_cast, **kw):
    ins = refs[:N_RWKV_INPUTS]
    cast_in = refs[N_RWKV_INPUTS:N_RWKV_INPUTS + n_cast]
    o0 = N_RWKV_INPUTS + n_cast
    outs = refs[o0:o0 + N_RWKV_OUTPUTS]
    cast_out = refs[o0 + N_RWKV_OUTPUTS:o0 + N_RWKV_OUTPUTS + n_cast]
    scratch = refs[o0 + N_RWKV_OUTPUTS + n_cast:]
    for src, dst in zip(cast_in, cast_out):
        dst[...] = src[...].astype(BF16)
    _rwkv_kernel(*ins, *outs, *scratch, **kw)


def _rwkv_mixer(z, row_block0, n_groups, n_tiles, zprev0, s0, p, *, tb, c, cast=()):
    n_factors = max(1, math.ceil(math.log2(c)))
    kern = functools.partial(_rwkv_and_cast_kernel, n_cast=len(cast), tb=tb, c=c, n_factors=n_factors)
    n_steps = n_groups * n_tiles
    cast_specs = [pl.BlockSpec((a.shape[0] // n_steps, a.shape[1]), lambda gi, i: (gi * n_tiles + i, 0))
                  for a in cast]
    cast_shapes = [jax.ShapeDtypeStruct(a.shape, BF16) for a in cast]
    cb = RW_OFF // RW_WIDTH
    full = lambda arr: pl.BlockSpec(arr.shape, lambda gi, i: (0,) * arr.ndim)
    lora_cb = (RW_OFF + 3 * RW_WIDTH) // LORA_PAD
    per_pair = lambda a: a.reshape(RW_HEADS // 2, 1, 2 * RW_HEAD)
    params = [p["mu"], p["w0"], p["w2"], p["a0"], p["a2"], p["g2"], p["kk"], p["ka"],
              per_pair(p["rk"]), per_pair(p["lnw"]), per_pair(p["lnb"])]
    dense = lambda: pltpu.VMEM((tb, RW_WIDTH), F32)
    zspec = lambda width, col: pl.BlockSpec((tb, width), lambda gi, i: (row_block0 + gi * n_tiles + i, col))
    sspec = pl.BlockSpec((1, RW_HEADS, RW_HEAD, RW_HEAD), lambda gi, i: (gi, 0, 0, 0))
    return pl.pallas_call(
        kern,
        grid=(n_groups, n_tiles),
        in_specs=[zspec(RW_WIDTH, cb), zspec(RW_WIDTH, cb + 1), zspec(RW_WIDTH, cb + 2),
                  zspec(LORA_PAD, lora_cb),
                  pl.BlockSpec((1, 1, SHIFT_PAD), lambda gi, i: (gi, 0, 0)),
                  sspec]
                 + [full(a) for a in params] + cast_specs,
        out_specs=[pl.BlockSpec((tb, RW_WIDTH), lambda gi, i: (gi * n_tiles + i, 0)), sspec,
                   pl.BlockSpec((1, 1, SHIFT_PAD), lambda gi, i: (gi, 0, 0))] + cast_specs,
        out_shape=[jax.ShapeDtypeStruct((n_groups * n_tiles * tb, RW_WIDTH), BF16),
                   jax.ShapeDtypeStruct((n_groups, RW_HEADS, RW_HEAD, RW_HEAD), F32),
                   jax.ShapeDtypeStruct((n_groups, 1, SHIFT_PAD), F32)] + cast_shapes,
        scratch_shapes=[pltpu.VMEM((1, SHIFT_PAD), F32),
                        pltpu.VMEM((RW_HEADS // 2, 2 * RW_HEAD, 2 * RW_HEAD), F32)]
                       + [dense() for _ in range(8)],
        compiler_params=_cparams(("parallel", "arbitrary")),
        name="rwkv7_mixer",
    )(z, z, z, z, zprev0, s0, *params, *cast)


def _rwkv_sample_kernel(zr_ref, zk_ref, zv_ref, zl_ref, shr_ref, shk_ref, shv_ref, shl_ref,
                        mur_ref, muk_ref, muv_ref, mul_ref, s0_ref,
                        w0_ref, w2t_ref, a0_ref, a2t_ref, g2t_ref, kk_ref, ka_ref, rk_ref,
                        lnw_ref, lnb_ref,
                        o_ref, sout_ref,
                        dec_s, a_s, b_s, k_s, r_s, v_s, o_s,
                        *, n_tok, hpb):
    ns = LANES
    ch = hpb * RW_HEAD

    def lanes(x):
        return jnp.concatenate([x] * n_tok, axis=1)

    def shifted(z_ref, sh_ref, mu_ref):
        z = z_ref[...]
        prev = jnp.concatenate([sh_ref[...], z[:, :(n_tok - 1) * ns]], axis=1)
        return z + lanes(mu_ref[...]) * (prev - z)

    r = shifted(zr_ref, shr_ref, mur_ref)
    k = shifted(zk_ref, shk_ref, muk_ref)
    v = shifted(zv_ref, shv_ref, muv_ref)
    zl = shifted(zl_ref, shl_ref, mul_ref)
    wd = zl[0:DECAY_LORA]
    ad = zl[DECAY_LORA:DECAY_LORA + AAA_LORA]
    gd = zl[DECAY_LORA + AAA_LORA:DECAY_LORA + AAA_LORA + GATE_LORA]

    wlog = -jax.nn.softplus(-(lanes(w0_ref[...]) + _mm(w2t_ref[...], jnp.tanh(wd)))) - 0.5
    lw = -jnp.exp(wlog)
    a = jax.nn.sigmoid(lanes(a0_ref[...]) + _mm(a2t_ref[...], ad))
    g = _mm(g2t_ref[...], jax.nn.sigmoid(gd))
    kk = k * lanes(kk_ref[...])
    km = k * (1.0 + (a - 1.0) * lanes(ka_ref[...]))

    def head_sum(x):
        x3 = x.reshape(hpb, RW_HEAD, n_tok * ns)
        s = jnp.sum(x3, axis=1, keepdims=True)
        return jnp.broadcast_to(s, x3.shape).reshape(ch, n_tok * ns)

    kkn = kk * lax.rsqrt(jnp.maximum(head_sum(kk * kk), 1e-24))
    dec_s[...] = jnp.exp(lw)
    a_s[...] = -kkn
    b_s[...] = kkn * a
    k_s[...] = km
    r_s[...] = r
    v_s[...] = v

    for hh in range(hpb):
        hrows = slice(hh * RW_HEAD, (hh + 1) * RW_HEAD)

        def body(i8, carry, hh=hh, hrows=hrows):
            base = pl.multiple_of(i8 * 8, 8)
            rows8 = pl.ds(pl.multiple_of(hh * RW_HEAD + base, 8), 8)
            o_rows = [[] for _ in range(n_tok)]
            for j in range(8):
                s = s0_ref[hh, base + j]
                for t in range(n_tok):
                    tl = slice(t * ns, (t + 1) * ns)
                    sa = jnp.sum(s * a_s[hrows, tl], axis=0, keepdims=True)
                    v_row = v_s[rows8, tl][j:j + 1, :]
                    s = s * dec_s[hrows, tl] + sa * b_s[hrows, tl] + v_row * k_s[hrows, tl]
                    o_rows[t].append(jnp.sum(s * r_s[hrows, tl], axis=0, keepdims=True))
                sout_ref[hh, base + j] = s
            for t in range(n_tok):
                o_s[rows8, t * ns:(t + 1) * ns] = jnp.concatenate(o_rows[t], axis=0)
            return carry

        lax.fori_loop(0, RW_HEAD // 8, body, 0)

    o = o_s[...]
    inv_n = 1.0 / RW_HEAD
    mean = head_sum(o) * inv_n
    var = head_sum(jnp.square(o - mean)) * inv_n
    on = ((o - mean) * lax.rsqrt(var + RW_GN_EPS)) * lanes(lnw_ref[...]) + lanes(lnb_ref[...])
    bonus = head_sum(r * km * lanes(rk_ref[...])) * v
    out = (on + bonus) * g
    for t in range(n_tok):
        o_ref[t] = out[:, t * ns:(t + 1) * ns].T.astype(BF16)


def _rwkv_sample_mixer(z_t, shift_t, mu_t, state_t, p_t, *, n_tok, hpb):
    ch = hpb * RW_HEAD
    n_steps = RW_HEADS // hpb
    nl = n_tok * LANES
    seg = RW_WIDTH // ch
    zspec = lambda s: pl.BlockSpec((ch, nl), lambda h: (s * seg + h, 0))
    cspec = lambda s: pl.BlockSpec((ch, LANES), lambda h: (s * seg + h, 0))
    lora_blk = 3 * RW_WIDTH // LORA_PAD
    zl_spec = pl.BlockSpec((LORA_PAD, nl), lambda h: (lora_blk, 0))
    cl_spec = pl.BlockSpec((LORA_PAD, LANES), lambda h: (lora_blk, 0))
    hspec = pl.BlockSpec((ch, LANES), lambda h: (h, 0))
    wspec = lambda k: pl.BlockSpec((ch, k), lambda h: (h, 0))
    sspec = pl.BlockSpec((hpb, RW_HEAD, RW_HEAD, LANES), lambda h: (h, 0, 0, 0))
    buf = lambda: pltpu.VMEM((ch, nl), F32)
    return pl.pallas_call(
        functools.partial(_rwkv_sample_kernel, n_tok=n_tok, hpb=hpb),
        grid=(n_steps,),
        in_specs=[zspec(0), zspec(1), zspec(2), zl_spec,
                  cspec(0), cspec(1), cspec(2), cl_spec,
                  cspec(0), cspec(1), cspec(2), cl_spec,
                  sspec,
                  hspec, wspec(DECAY_LORA), hspec, wspec(AAA_LORA), wspec(GATE_LORA),
                  hspec, hspec, hspec, hspec, hspec],
        out_specs=[pl.BlockSpec((n_tok, LANES, ch), lambda h: (0, 0, h)), sspec],
        out_shape=[jax.ShapeDtypeStruct((n_tok, LANES, RW_WIDTH), BF16),
                   jax.ShapeDtypeStruct(state_t.shape, F32)],
        scratch_shapes=[buf() for _ in range(7)],
        compiler_params=_cparams(("parallel",)),
        name="rwkv7_sample",
    )(z_t, z_t, z_t, z_t, shift_t, shift_t, shift_t, shift_t, mu_t, mu_t, mu_t, mu_t, state_t,
      p_t["w0"], p_t["w2t"], p_t["a0"], p_t["a2t"], p_t["g2t"], p_t["kk"], p_t["ka"], p_t["rk"],
      p_t["lnw"], p_t["lnb"])


def _retention_kernel(zq_ref, zk_ref, zv_ref, zg_ref, cos_ref, sin_ref, dmask_ref, iscale_ref,
                      kscale_ref, sdec_ref, s0_ref, o_ref, sout_ref,
                      *, rows, c, n_real, per_chunk_state):
    i = pl.program_id(1)
    n_chunks = rows // c

    if per_chunk_state:
        sout_ref[...] = s0_ref[...]
        pr = lax.broadcasted_iota(jnp.int32, (rows, n_chunks * n_real), 0)
        pc = lax.broadcasted_iota(jnp.int32, (rows, n_chunks * n_real), 1)
        place = jnp.where((pr // c == pc // n_real) & (pr % c - (c - n_real) == pc % n_real),
                          1.0, 0.0).astype(BF16)
        load = lambda ref: _mm_exact_lhs(place, ref[...])
    else:
        load = lambda ref: ref[...]

        @pl.when(i == 0)
        def _():
            sout_ref[...] = s0_ref[...]

    lane = lax.broadcasted_iota(jnp.int32, (rows, RET_WIDTH), 1)
    even = (lane % 2) == 0
    cos = jnp.concatenate([cos_ref[...]] * RET_HEADS, axis=-1)
    sin = jnp.concatenate([sin_ref[...]] * RET_HEADS, axis=-1)

    def rot(x):
        partner = jnp.where(even, pltpu.roll(x, RET_WIDTH - 1, axis=1), pltpu.roll(x, 1, axis=1))
        return x * cos + partner * sin

    q = rot(load(zq_ref))
    k = rot(load(zk_ref)) * (RET_HEAD ** -0.5)
    v = load(zv_ref)
    g = load(zg_ref)

    out_rows = []
    for ch in range(n_chunks):
        rs = slice(ch * c, (ch + 1) * c)
        sidx = ch if per_chunk_state else 0
        out_heads = []
        for h in range(RET_HEADS):
            hs = slice(h * RET_HEAD, (h + 1) * RET_HEAD)
            qh, kh, vh = q[rs, hs], k[rs, hs], v[rs, hs]
            s_prev = sout_ref[sidx, h]
            scores = _mm_nt(qh, kh) * dmask_ref[h]
            o = _mm(scores, vh) + _mm(qh, s_prev) * iscale_ref[h]
            sout_ref[sidx, h] = s_prev * sdec_ref[h] + _mm_tn(kh * kscale_ref[h], vh)
            o = o * lax.rsqrt(jnp.mean(o * o, axis=-1, keepdims=True) + RET_GN_EPS)
            gh = g[rs, hs]
            out_heads.append(o * (gh * jax.nn.sigmoid(gh)))
        out_rows.append(jnp.concatenate(out_heads, axis=-1))
    o_out = jnp.concatenate(out_rows, axis=0).astype(BF16)
    if per_chunk_state:
        o_out = lax.dot_general(place, o_out, (((0,), (0,)), ((), ())),
                                preferred_element_type=F32).astype(BF16)
    o_ref[...] = o_out


def _retention_mixer(z, row_block0, n_groups, n_tiles, cos, sin, tabs, s0, *, rows, c, n_real,
                     per_chunk_state, pos_per_tile):
    n_states = s0.shape[0]
    sb = n_states // n_groups
    kern = functools.partial(_retention_kernel, rows=rows, c=c, n_real=n_real,
                             per_chunk_state=per_chunk_state)
    rows_io = rows // c * n_real if per_chunk_state else rows
    zspec = lambda col: pl.BlockSpec((rows_io, RET_WIDTH), lambda gi, i: (row_block0 + gi * n_tiles + i, col))
    full = lambda arr: pl.BlockSpec(arr.shape, lambda gi, i: (0,) * arr.ndim)
    if pos_per_tile:
        tspec = pl.BlockSpec((rows, RET_HEAD), lambda gi, i: (i, 0))
    else:
        tspec = pl.BlockSpec((rows, RET_HEAD), lambda gi, i: (0, 0))
    sspec = pl.BlockSpec((sb, RET_HEADS, RET_HEAD, RET_HEAD), lambda gi, i: (gi, 0, 0, 0))
    dmask, iscale, kscale, sdec = tabs
    return pl.pallas_call(
        kern,
        grid=(n_groups, n_tiles),
        in_specs=[zspec(0), zspec(1), zspec(2), zspec(3), tspec, tspec,
                  full(dmask), full(iscale), full(kscale), full(sdec), sspec],
        out_specs=[pl.BlockSpec((rows_io, RET_WIDTH), lambda gi, i: (gi * n_tiles + i, 0)), sspec],
        out_shape=[jax.ShapeDtypeStruct((n_groups * n_tiles * rows_io, RET_WIDTH), BF16),
                   jax.ShapeDtypeStruct(s0.shape, F32)],
        compiler_params=_cparams(("parallel", "arbitrary")),
        name="retention_mixer",
    )(z, z, z, z, cos, sin, dmask, iscale, kscale, sdec, s0)


def _retention_tables(c, n_real):
    log_gamma = jnp.log(1.0 - 2.0 ** (-5.0 - jnp.arange(RET_HEADS, dtype=F32)))
    r = jnp.arange(c, dtype=F32)
    idx = r - float(c - n_real)
    diff = r[:, None] - r[None, :]
    dmask = jnp.where(diff[None] >= 0,
                      jnp.exp(log_gamma[:, None, None] * jnp.maximum(diff, 0.0)[None]), 0.0)
    iscale = jnp.exp(log_gamma[:, None] * (idx + 1.0)[None, :])[:, :, None]
    kscale = jnp.exp(log_gamma[:, None] * (n_real - 1.0 - idx)[None, :])[:, :, None]
    sdec = jnp.broadcast_to(jnp.exp(log_gamma * n_real)[:, None, None], (RET_HEADS, 1, RET_HEAD))
    return dmask, iscale, kscale, sdec


def _rotary_tables(pos):
    inv_freq = 1.0 / (ROPE_BASE ** jnp.linspace(0.0, 1.0, RET_HEAD // 2, dtype=F32))
    ang = pos.astype(F32)[:, None] * inv_freq[None, :]
    cos = jnp.cos(ang)
    sin = jnp.sin(ang)
    cos2 = jnp.repeat(cos, 2, axis=-1)
    sin2 = jnp.stack([-sin, sin], axis=-1).reshape(pos.shape[0], RET_HEAD)
    return cos2, sin2


def _outproj_kernel(oa_ref, ob_ref, w_ref, x_ref, g_ref, h_ref, hn_ref):
    acc = jnp.dot(oa_ref[...], w_ref[0:RW_WIDTH, :], preferred_element_type=F32)
    acc = acc + jnp.dot(ob_ref[...], w_ref[RW_WIDTH:, :], preferred_element_type=F32)
    h = x_ref[...] + acc
    h_ref[...] = h
    hn_ref[...] = _rms_norm_bf16(h, g_ref[...])


def _out_projection(o_a, o_b, w_out, x2d, norm_g, tm):
    m = o_a.shape[0]
    return pl.pallas_call(
        _outproj_kernel,
        grid=(m // tm,),
        in_specs=[
            pl.BlockSpec((tm, RW_WIDTH), lambda i: (i, 0)),
            pl.BlockSpec((tm, RET_WIDTH), lambda i: (i, 0)),
            pl.BlockSpec((D_MODEL, D_MODEL), lambda i: (0, 0)),
            pl.BlockSpec((tm, D_MODEL), lambda i: (i, 0)),
            pl.BlockSpec((1, D_MODEL), lambda i: (0, 0)),
        ],
        out_specs=[pl.BlockSpec((tm, D_MODEL), lambda i: (i, 0)),
                   pl.BlockSpec((tm, D_MODEL), lambda i: (i, 0))],
        out_shape=[jax.ShapeDtypeStruct((m, D_MODEL), F32), jax.ShapeDtypeStruct((m, D_MODEL), BF16)],
        compiler_params=_cparams(("parallel",)),
        name="out_projection",
    )(o_a, o_b, w_out, x2d, norm_g)


def _ffn_kernel(hn_ref, wg_ref, wu_ref, wd_ref, h_ref, g_ref, y_ref, acc_ref, *, h_slices):
    f = pl.program_id(1)

    @pl.when(f == 0)
    def _():
        acc_ref[...] = jnp.zeros_like(acc_ref)

    @pl.when(f < h_slices)
    def _():
        rows = h_ref.shape[0]
        sl = pl.ds(pl.multiple_of(f * rows, rows), rows)
        acc_ref[sl, :] += h_ref[...]

    hn = hn_ref[...]
    gate = jnp.dot(hn, wg_ref[...], preferred_element_type=F32)
    up = jnp.dot(hn, wu_ref[...], preferred_element_type=F32)
    act = (gate * jax.nn.sigmoid(gate)) * up
    acc_ref[...] += jnp.dot(act.astype(BF16), wd_ref[...], preferred_element_type=F32)

    @pl.when(f == pl.num_programs(1) - 1)
    def _():
        h = acc_ref[...]
        ms = jnp.mean(h * h, axis=-1, keepdims=True)
        y_ref[...] = (h * lax.rsqrt(ms + RMS_EPS)) * g_ref[...]


def _ffn(hn, w_gate, w_up, w_down, h, norm_g, tm, tf, h_slices=8):
    m = hn.shape[0]
    assert D_FF // tf >= h_slices
    return pl.pallas_call(
        functools.partial(_ffn_kernel, h_slices=h_slices),
        grid=(m // tm, D_FF // tf),
        in_specs=[
            pl.BlockSpec((tm, D_MODEL), lambda i, f: (i, 0)),
            pl.BlockSpec((D_MODEL, tf), lambda i, f: (0, f)),
            pl.BlockSpec((D_MODEL, tf), lambda i, f: (0, f)),
            pl.BlockSpec((tf, D_MODEL), lambda i, f: (f, 0)),
            pl.BlockSpec((tm // h_slices, D_MODEL),
                         lambda i, f: (i * h_slices + jnp.minimum(f, h_slices - 1), 0)),
            pl.BlockSpec((1, D_MODEL), lambda i, f: (0, 0)),
        ],
        out_specs=pl.BlockSpec((tm, D_MODEL), lambda i, f: (i, 0)),
        out_shape=jax.ShapeDtypeStruct((m, D_MODEL), F32),
        scratch_shapes=[pltpu.VMEM((tm, D_MODEL), F32)],
        compiler_params=_cparams(("parallel", "arbitrary")),
        name="swiglu_ffn",
    )(hn, w_gate, w_up, w_down, h, norm_g)


def _pad_cols(a, n):
    return jnp.pad(a, ((0, 0), (0, n - a.shape[1])))


def _pad_rows(a, n):
    return jnp.pad(a, ((0, n - a.shape[0]), (0, 0)))


def kernel(x_prompt, x_sample, state_shift, state_rwkv, state_ret, meta_tokens, norm_mix, w_in,
           rwkv_mu, rwkv_w0, rwkv_w2, rwkv_a0, rwkv_a2, rwkv_g2, rwkv_kk, rwkv_ka, rwkv_rk,
           rwkv_ln_w, rwkv_ln_b, w_out, norm_ffn, w_gate, w_up, w_down, norm_final):
    n_b, seq, d = x_prompt.shape
    n_s, dec_seq, _ = x_sample.shape
    n_p = n_b * seq
    n_d = n_s * dec_seq
    depth = w_in.shape[0]
    assert depth == 1 and d == D_MODEL and n_s == LANES

    w_in_t = jnp.transpose(w_in[0])
    w_in_p = _w_in_layout(w_in_t, tn=512)
    row = lambda a: a.reshape(1, -1).astype(F32)
    w2p = jnp.concatenate([rwkv_w2[0], jnp.zeros((128 - DECAY_LORA, RW_WIDTH), F32)], axis=0)
    a2p = jnp.concatenate([jnp.zeros((DECAY_LORA, RW_WIDTH), F32), rwkv_a2[0]], axis=0)
    g2p = jnp.concatenate([rwkv_g2[0], jnp.zeros((256 - GATE_LORA, RW_WIDTH), F32)], axis=0)
    rw_params = dict(mu=_pad_cols(row(rwkv_mu[0]), SHIFT_PAD), w0=row(rwkv_w0[0]), w2=w2p,
                     a0=row(rwkv_a0[0]), a2=a2p, g2=g2p, kk=row(rwkv_kk[0]), ka=row(rwkv_ka[0]),
                     rk=row(rwkv_rk[0]), lnw=row(rwkv_ln_w[0]), lnb=row(rwkv_ln_b[0]))
    col = lambda a: jnp.broadcast_to(a.reshape(-1, 1).astype(F32), (a.size, LANES))
    rw_params_t = dict(w0=col(rwkv_w0[0]), w2t=rwkv_w2[0].T, a0=col(rwkv_a0[0]), a2t=rwkv_a2[0].T,
                       g2t=rwkv_g2[0].T, kk=col(rwkv_kk[0]), ka=col(rwkv_ka[0]), rk=col(rwkv_rk[0]),
                       lnw=col(rwkv_ln_w[0]), lnb=col(rwkv_ln_b[0]))

    x_p = x_prompt.reshape(n_p, d)
    x_s = x_sample.reshape(n_d, d)
    x_sm = jnp.concatenate([x_s, meta_tokens.astype(F32)], axis=0)
    x_ts = jnp.transpose(x_sample, (1, 0, 2)).reshape(n_d, d)
    g_mix = row(norm_mix[0])
    z_p = _in_projection(x_p, g_mix, w_in_p, tm=1024, tn=1536)
    z_sm = _in_projection(x_sm, g_mix, w_in_p, tm=x_sm.shape[0], tn=768)
    z_st = _in_projection_t(w_in_t, x_ts, g_mix, SHIFT_PAD, tm=512)

    z_meta = jnp.pad(z_sm[n_d:], ((RET_CHUNK - N_META, 0), (0, 0)))
    srow = 2 * dec_seq

    zero_prev = jnp.zeros((1, 1, SHIFT_PAD), F32)
    zero_rw = jnp.zeros((1, RW_HEADS, RW_HEAD, RW_HEAD), F32)
    c_rw = 64
    _, s_rw_meta, zlast_meta = _rwkv_mixer(z_meta, 1, 1, 1, zero_prev, zero_rw, rw_params, tb=c_rw, c=c_rw)
    oa_p, rwkv_p, zlast_p, w_out_b, w_gate_b, w_up_b, w_down_b = _rwkv_mixer(
        z_p, 0, n_b, seq // 256, jnp.broadcast_to(zlast_meta, (n_b, 1, SHIFT_PAD)),
        jnp.broadcast_to(s_rw_meta, (n_b,) + s_rw_meta.shape[1:]), rw_params, tb=256, c=c_rw,
        cast=(w_out[0], w_gate[0], w_up[0], w_down[0]))
    shift_t = _pad_rows(jnp.transpose(state_shift[0]), SHIFT_PAD)
    mu_t = _pad_rows(col(rwkv_mu[0]), SHIFT_PAD)
    state_t = jnp.transpose(state_rwkv[0], (1, 2, 3, 0))
    oa_st, rwkv_st = _rwkv_sample_mixer(z_st, shift_t, mu_t, state_t, rw_params_t, n_tok=dec_seq, hpb=2)
    oa_s = jnp.transpose(oa_st, (1, 0, 2)).reshape(n_d, RW_WIDTH)
    rwkv_s = jnp.transpose(rwkv_st, (3, 0, 1, 2))

    past_len = 16384
    tabs_full = _retention_tables(RET_CHUNK, RET_CHUNK)
    tabs_meta = _retention_tables(RET_CHUNK, N_META)
    tabs_smp = _retention_tables(srow, dec_seq)
    cos_m, sin_m = _rotary_tables(jnp.arange(RET_CHUNK) - (RET_CHUNK - N_META))
    cos_p, sin_p = _rotary_tables(N_META + jnp.arange(seq))
    sb_rt = 4
    pos_s = past_len + jnp.tile(jnp.arange(srow) - (srow - dec_seq), sb_rt)
    cos_s, sin_s = _rotary_tables(pos_s)
    zero_rt = jnp.zeros((1, RET_HEADS, RET_HEAD, RET_HEAD), F32)
    _, s_rt_meta = _retention_mixer(z_meta, 0, 1, 1, cos_m, sin_m, tabs_meta, zero_rt,
                                    rows=RET_CHUNK, c=RET_CHUNK, n_real=N_META, per_chunk_state=False,
                                    pos_per_tile=False)
    ob_p, ret_p = _retention_mixer(z_p, 0, n_b, seq // RET_CHUNK, cos_p, sin_p, tabs_full,
                                   jnp.broadcast_to(s_rt_meta, (n_b,) + s_rt_meta.shape[1:]),
                                   rows=RET_CHUNK, c=RET_CHUNK, n_real=RET_CHUNK, per_chunk_state=False,
                                   pos_per_tile=True)
    ob_s, ret_s = _retention_mixer(z_sm, 0, n_s // sb_rt, 1, cos_s, sin_s, tabs_smp,
                                   state_ret.reshape(n_s, RET_HEADS, RET_HEAD, RET_HEAD),
                                   rows=sb_rt * srow, c=srow, n_real=dec_seq, per_chunk_state=True,
                                   pos_per_tile=False)

    g_ffn = row(norm_ffn[0])
    g_fin = row(norm_final)
    h_p, hn_p = _out_projection(oa_p, ob_p, w_out_b, x_p, g_ffn, tm=512)
    h_s, hn_s = _out_projection(oa_s, ob_s, w_out_b, x_s, g_ffn, tm=n_d)
    y_p = _ffn(hn_p, w_gate_b, w_up_b, w_down_b, h_p, g_fin, tm=1024, tf=512)
    y_s = _ffn(hn_s, w_gate_b, w_up_b, w_down_b, h_s, g_fin, tm=n_d, tf=512)

    y_prompt = y_p.reshape(n_b, seq, d)
    y_sample = y_s.reshape(n_s, dec_seq, d)
    shift_p = zlast_p[:, 0, :RW_COLS][None]
    shift_s = jnp.transpose(z_st[:RW_COLS, (dec_seq - 1) * n_s:])[None]
    return (y_prompt, y_sample, shift_p, rwkv_p[None], ret_p[None], shift_s, rwkv_s[None], ret_s[None])
```

```python
import functools
import math

import jax
import jax.numpy as jnp
from jax import lax
from jax.experimental import pallas as pl
from jax.experimental.pallas import tpu as pltpu

F32 = jnp.float32
BF16 = jnp.bfloat16

D_MODEL = 2048
N_META = 16
RW_WIDTH = 1024
RW_HEAD = 64
RW_HEADS = 16
DECAY_LORA = 64
AAA_LORA = 64
GATE_LORA = 160
RW_COLS = 3 * RW_WIDTH + DECAY_LORA + AAA_LORA + GATE_LORA
RET_WIDTH = 1024
RET_HEADS = 4
RET_HEAD = 256
RET_CHUNK = 128
D_FF = 5632
RMS_EPS = 1e-6
RW_GN_EPS = 64e-5
RET_GN_EPS = 1e-6
ROPE_BASE = 10000.0
LANES = 128

LORA_PAD = 512
Z_COLS = 4 * RET_WIDTH + 3 * RW_WIDTH + LORA_PAD
RW_OFF = 4 * RET_WIDTH
SHIFT_PAD = 3 * RW_WIDTH + LORA_PAD

VMEM_LIMIT = 60 * 1024 * 1024


def _cparams(sem):
    return pltpu.CompilerParams(dimension_semantics=sem, vmem_limit_bytes=VMEM_LIMIT)


def _mm(a, b):
    return jnp.dot(a.astype(BF16), b.astype(BF16), preferred_element_type=F32)


def _mm_nt(a, b):
    return lax.dot_general(a.astype(BF16), b.astype(BF16), (((1,), (1,)), ((), ())),
                           preferred_element_type=F32)


def _mm_tn(a, b):
    return lax.dot_general(a.astype(BF16), b.astype(BF16), (((0,), (0,)), ((), ())),
                           preferred_element_type=F32)


def _bmm(a, b):
    return jnp.einsum("bmk,bkn->bmn", a.astype(BF16), b.astype(BF16), preferred_element_type=F32)


def _bmm_nt(a, b):
    return jnp.einsum("bmk,bnk->bmn", a.astype(BF16), b.astype(BF16), preferred_element_type=F32)


def _bmm_tn(a, b):
    return jnp.einsum("bkm,bkn->bmn", a.astype(BF16), b.astype(BF16), preferred_element_type=F32)


def _mm_exact_lhs(m_bf16, x):
    hi = x.astype(BF16)
    r1 = x - hi.astype(F32)
    mid = r1.astype(BF16)
    lo = (r1 - mid.astype(F32)).astype(BF16)
    d = functools.partial(jnp.dot, preferred_element_type=F32)
    return d(m_bf16, hi) + d(m_bf16, mid) + d(m_bf16, lo)


def _rms_norm_bf16(x, g):
    ms = jnp.mean(x * x, axis=-1, keepdims=True)
    return ((x * lax.rsqrt(ms + RMS_EPS)) * g).astype(BF16)


def _w_in_layout_kernel(w_ref, o_ref):
    o_ref[...] = w_ref[...].T.astype(BF16)


def _w_in_layout(w_in_t, tn):
    n, d = w_in_t.shape
    n_ret = 4 * RET_WIDTH // tn

    def src_row(c):
        align = math.gcd(RW_COLS, tn)
        return (pl.multiple_of(jnp.where(c < n_ret, RW_COLS + tn * c, tn * (c - n_ret)), align), 0)

    return pl.pallas_call(
        _w_in_layout_kernel,
        grid=(Z_COLS // tn,),
        in_specs=[pl.BlockSpec((pl.Element(tn), pl.Element(d)), src_row)],
        out_specs=pl.BlockSpec((d, tn), lambda c: (0, c)),
        out_shape=jax.ShapeDtypeStruct((d, Z_COLS), BF16),
        compiler_params=_cparams(("parallel",)),
        name="w_in_layout",
    )(w_in_t)


def _inproj_kernel(x_ref, g_ref, w_ref, o_ref, xn_ref):
    @pl.when(pl.program_id(1) == 0)
    def _():
        xn_ref[...] = _rms_norm_bf16(x_ref[...], g_ref[...])

    o_ref[...] = jnp.dot(xn_ref[...], w_ref[...], preferred_element_type=F32)


def _in_projection(x2d, norm_g, w_in_p, tm, tn):
    m = x2d.shape[0]
    return pl.pallas_call(
        _inproj_kernel,
        grid=(m // tm, Z_COLS // tn),
        in_specs=[
            pl.BlockSpec((tm, D_MODEL), lambda i, j: (i, 0)),
            pl.BlockSpec((1, D_MODEL), lambda i, j: (0, 0)),
            pl.BlockSpec((D_MODEL, tn), lambda i, j: (0, j)),
        ],
        out_specs=pl.BlockSpec((tm, tn), lambda i, j: (i, j)),
        out_shape=jax.ShapeDtypeStruct((m, Z_COLS), F32),
        scratch_shapes=[pltpu.VMEM((tm, D_MODEL), BF16)],
        compiler_params=_cparams(("parallel", "arbitrary")),
        name="in_projection",
    )(x2d, norm_g, w_in_p)


def _inproj_t_kernel(w_ref, x_ref, g_ref, o_ref, xn_ref):
    @pl.when(pl.program_id(0) == 0)
    def _():
        xn_ref[...] = _rms_norm_bf16(x_ref[...], g_ref[...])

    o_ref[...] = _mm_nt(w_ref[...], xn_ref[...])


def _in_projection_t(w_in_t, x2d, norm_g, n_rows, tm):
    m, d = x2d.shape
    return pl.pallas_call(
        _inproj_t_kernel,
        grid=(n_rows // tm,),
        in_specs=[
            pl.BlockSpec((tm, d), lambda i: (i, 0)),
            pl.BlockSpec((m, d), lambda i: (0, 0)),
            pl.BlockSpec((1, d), lambda i: (0, 0)),
        ],
        out_specs=pl.BlockSpec((tm, m), lambda i: (i, 0)),
        out_shape=jax.ShapeDtypeStruct((n_rows, m), F32),
        scratch_shapes=[pltpu.VMEM((m, d), BF16)],
        compiler_params=_cparams(("arbitrary",)),
        name="in_projection_t",
    )(w_in_t, x2d, norm_g)


def _rwkv_kernel(zr_ref, zk_ref, zv_ref, zl_ref, zprev0_ref, s0_ref,
                 mu_ref, w0_ref, w2_ref, a0_ref, a2_ref, g2_ref, kk_ref, ka_ref, rk_ref,
                 lnw_ref, lnb_ref,
                 o_ref, sout_ref, zlast_ref,
                 carry_ref, sbd_s, r_s, km_s, v_s, kk_s, a_s, lw_s, cum_s, o_s,
                 *, tb, c, n_factors):
    i = pl.program_id(1)
    n_chunks = tb // c
    n_pairs = RW_HEADS // 2
    pw = 2 * RW_HEAD

    @pl.when(i == 0)
    def _():
        carry_ref[...] = zprev0_ref[0]
        zero = jnp.zeros((RW_HEAD, RW_HEAD), F32)
        for p in range(n_pairs):
            top = jnp.concatenate([s0_ref[0, 2 * p], zero], axis=1)
            bot = jnp.concatenate([zero, s0_ref[0, 2 * p + 1]], axis=1)
            sbd_s[p] = jnp.concatenate([top, bot], axis=0)

    row = lax.broadcasted_iota(jnp.int32, (tb, 1), 0)

    def shifted(z_ref, lo, hi):
        z = z_ref[...]
        prev = pltpu.roll(z, 1, axis=0)
        prev = jnp.concatenate([jnp.where(row[:8] == 0, carry_ref[:, lo:hi], prev[:8]), prev[8:]], axis=0)
        zs = z + mu_ref[:, lo:hi] * (prev - z)
        carry_ref[:, lo:hi] = z[tb - 1:tb, :]
        return zs

    w = RW_WIDTH
    r = shifted(zr_ref, 0, w)
    k = shifted(zk_ref, w, 2 * w)
    v = shifted(zv_ref, 2 * w, 3 * w)
    zl = shifted(zl_ref, 3 * w, 3 * w + LORA_PAD)

    lo2 = zl[:, 0:128]
    wlog = -jax.nn.softplus(-(w0_ref[...] + _mm(jnp.tanh(lo2), w2_ref[...]))) - 0.5
    lw = -jnp.exp(wlog)
    a = jax.nn.sigmoid(a0_ref[...] + _mm(lo2, a2_ref[...]))
    g = _mm(jax.nn.sigmoid(zl[:, 128:384]), g2_ref[...])
    kk = k * kk_ref[...]
    km = k * (1.0 + (a - 1.0) * ka_ref[...])

    ri = lax.broadcasted_iota(jnp.int32, (tb, tb), 0)
    ci = lax.broadcasted_iota(jnp.int32, (tb, tb), 1)
    tri = jnp.where((ri // c == ci // c) & (ri >= ci), 1.0, 0.0).astype(BF16)
    cum = _mm_exact_lhs(tri, lw)

    for dst, val in ((r_s, r), (km_s, km), (v_s, v), (a_s, a), (lw_s, lw), (cum_s, cum), (kk_s, kk)):
        dst[...] = val

    rr = lax.broadcasted_iota(jnp.int32, (c, c), 0)
    cc = lax.broadcasted_iota(jnp.int32, (c, c), 1)
    strict = (rr > cc)[None]
    eye = jnp.where(rr == cc, 1.0, 0.0).astype(F32)[None]
    rr2 = lax.broadcasted_iota(jnp.int32, (c, 2 * c), 0)
    cc2 = lax.broadcasted_iota(jnp.int32, (c, 2 * c), 1)
    strict_k = ((cc2 >= c) & (rr2 > cc2 - c))[None]
    incl_bk = (rr2 >= cc2 % c)[None]
    lo = (lax.broadcasted_iota(jnp.int32, (1, 1, pw), 2) < RW_HEAD)
    bd_r = lax.broadcasted_iota(jnp.int32, (pw, pw), 0) // RW_HEAD
    bd_c = lax.broadcasted_iota(jnp.int32, (pw, pw), 1) // RW_HEAD
    block_diag = (bd_r == bd_c)[None]
    rk = rk_ref[...]
    lnw = lnw_ref[...]
    lnb = lnb_ref[...]

    def head_sum(x):
        s_lo = jnp.sum(jnp.where(lo, x, 0.0), axis=-1, keepdims=True)
        s_hi = jnp.sum(jnp.where(lo, 0.0, x), axis=-1, keepdims=True)
        return jnp.where(lo, s_lo, s_hi)

    def pick(x16):
        x4 = x16.reshape(n_pairs, 2, c, pw)
        return jnp.where(lo, x4[:, 0], x4[:, 1])

    def body(ch):
        rows = pl.ds(ch * c, c)

        def pairs(s):
            x = s[rows, :]
            return jnp.stack([x[:, p * pw:(p + 1) * pw] for p in range(n_pairs)], axis=0)

        rh, kmh, vh, ah, lwh, cumh, kkh = [pairs(s) for s in (r_s, km_s, v_s, a_s, lw_s, cum_s, kk_s)]
        s_prev = sbd_s[...]
        kkn = kkh * lax.rsqrt(jnp.maximum(head_sum(kkh * kkh), 1e-24))
        bvec = kkn * ah
        cum_end = cumh[:, c - 1:c, :]
        p_inc = jnp.exp(cumh)
        p_exc = jnp.exp(cumh - lwh)
        p_inv = jnp.exp(-cumh)
        p_end = jnp.exp(cum_end)
        p_rel = jnp.exp(cum_end - cumh)
        a_t = -(kkn * p_exc)
        r_t = rh * p_inc
        b_t = bvec * p_inv
        k_t = kmh * p_inv
        b_h = bvec * p_rel
        k_h = kmh * p_rel

        ar = jnp.concatenate([jnp.where(lo, a_t, 0.0), jnp.where(lo, r_t, 0.0),
                              jnp.where(lo, 0.0, a_t), jnp.where(lo, 0.0, r_t)], axis=1)
        prod = _bmm_nt(ar, jnp.concatenate([b_t, k_t, s_prev], axis=1))
        prod = prod.reshape(RW_HEADS, 2 * c, 2 * c + pw)
        a_rows, r_rows = prod[:, :c], prod[:, c:]
        lmat = jnp.where(strict, a_rows[:, :, :c], 0.0)
        aak_wide = jnp.where(strict_k, a_rows[:, :, :2 * c], 0.0)
        arbk = jnp.where(incl_bk, r_rows[:, :, :2 * c], 0.0)
        tinv = eye + lmat
        if n_factors > 1:
            lp = _bmm(lmat, lmat)
            for _ in range(n_factors - 2):
                st = _bmm(jnp.concatenate([lp, tinv], axis=1), lp)
                lp = st[:, :c]
                tinv = tinv + st[:, c:]
            tinv = tinv + _bmm(tinv, lp)
        v16 = jnp.repeat(vh, 2, axis=0)
        x = a_rows[:, :, 2 * c:] + _bmm(aak_wide, jnp.concatenate([v16, v16], axis=1))
        u16 = _bmm(tinv, x)
        o16 = r_rows[:, :, 2 * c:] + _bmm(arbk, jnp.concatenate([u16, v16], axis=1))
        u = pick(u16)
        o = pick(o16)
        s_upd = _bmm_tn(jnp.concatenate([u, vh], axis=1), jnp.concatenate([b_h, k_h], axis=1))
        sbd_s[...] = s_prev * p_end + jnp.where(block_diag, s_upd, 0.0)
        inv_n = 1.0 / RW_HEAD
        mean = head_sum(o) * inv_n
        var = head_sum(jnp.square(o - mean)) * inv_n
        on = ((o - mean) * lax.rsqrt(var + RW_GN_EPS)) * lnw + lnb
        out = on + head_sum(rh * kmh * rk) * vh
        for p in range(n_pairs):
            o_s[rows, p * pw:(p + 1) * pw] = out[p]

    for ch in range(n_chunks):
        body(ch)

    o_ref[...] = (o_s[...] * g).astype(BF16)
    zlast_ref[0] = carry_ref[...]

    @pl.when(i == pl.num_programs(1) - 1)
    def _():
        for p in range(n_pairs):
            blk = sbd_s[p]
            sout_ref[0, 2 * p] = blk[:RW_HEAD, :RW_HEAD]
            sout_ref[0, 2 * p + 1] = blk[RW_HEAD:, RW_HEAD:]


N_RWKV_INPUTS = 17
N_RWKV_OUTPUTS = 3


def _rwkv_and_cast_kernel(*refs, n_cast, **kw):
    ins = refs[:N_RWKV_INPUTS]
    cast_in = refs[N_RWKV_INPUTS:N_RWKV_INPUTS + n_cast]
    o0 = N_RWKV_INPUTS + n_cast
    outs = refs[o0:o0 + N_RWKV_OUTPUTS]
    cast_out = refs[o0 + N_RWKV_OUTPUTS:o0 + N_RWKV_OUTPUTS + n_cast]
    scratch = refs[o0 + N_RWKV_OUTPUTS + n_cast:]
    for src, dst in zip(cast_in, cast_out):
        dst[...] = src[...].astype(BF16)
    _rwkv_kernel(*ins, *outs, *scratch, **kw)


def _rwkv_mixer(z, row_block0, n_groups, n_tiles, zprev0, s0, p, *, tb, c, cast=()):
    n_factors = max(1, math.ceil(math.log2(c)))
    kern = functools.partial(_rwkv_and_cast_kernel, n_cast=len(cast), tb=tb, c=c, n_factors=n_factors)
    n_steps = n_groups * n_tiles
    cast_specs = [pl.BlockSpec((a.shape[0] // n_steps, a.shape[1]), lambda gi, i: (gi * n_tiles + i, 0))
                  for a in cast]
    cast_shapes = [jax.ShapeDtypeStruct(a.shape, BF16) for a in cast]
    cb = RW_OFF // RW_WIDTH
    full = lambda arr: pl.BlockSpec(arr.shape, lambda gi, i: (0,) * arr.ndim)
    lora_cb = (RW_OFF + 3 * RW_WIDTH) // LORA_PAD
    per_pair = lambda a: a.reshape(RW_HEADS // 2, 1, 2 * RW_HEAD)
    params = [p["mu"], p["w0"], p["w2"], p["a0"], p["a2"], p["g2"], p["kk"], p["ka"],
              per_pair(p["rk"]), per_pair(p["lnw"]), per_pair(p["lnb"])]
    dense = lambda: pltpu.VMEM((tb, RW_WIDTH), F32)
    zspec = lambda width, col: pl.BlockSpec((tb, width), lambda gi, i: (row_block0 + gi * n_tiles + i, col))
    sspec = pl.BlockSpec((1, RW_HEADS, RW_HEAD, RW_HEAD), lambda gi, i: (gi, 0, 0, 0))
    return pl.pallas_call(
        kern,
        grid=(n_groups, n_tiles),
        in_specs=[zspec(RW_WIDTH, cb), zspec(RW_WIDTH, cb + 1), zspec(RW_WIDTH, cb + 2),
                  zspec(LORA_PAD, lora_cb),
                  pl.BlockSpec((1, 1, SHIFT_PAD), lambda gi, i: (gi, 0, 0)),
                  sspec]
                 + [full(a) for a in params] + cast_specs,
        out_specs=[pl.BlockSpec((tb, RW_WIDTH), lambda gi, i: (gi * n_tiles + i, 0)), sspec,
                   pl.BlockSpec((1, 1, SHIFT_PAD), lambda gi, i: (gi, 0, 0))] + cast_specs,
        out_shape=[jax.ShapeDtypeStruct((n_groups * n_tiles * tb, RW_WIDTH), BF16),
                   jax.ShapeDtypeStruct((n_groups, RW_HEADS, RW_HEAD, RW_HEAD), F32),
                   jax.ShapeDtypeStruct((n_groups, 1, SHIFT_PAD), F32)] + cast_shapes,
        scratch_shapes=[pltpu.VMEM((1, SHIFT_PAD), F32),
                        pltpu.VMEM((RW_HEADS // 2, 2 * RW_HEAD, 2 * RW_HEAD), F32)]
                       + [dense() for _ in range(8)],
        compiler_params=_cparams(("parallel", "arbitrary")),
        name="rwkv7_mixer",
    )(z, z, z, z, zprev0, s0, *params, *cast)


def _rwkv_sample_kernel(zr_ref, zk_ref, zv_ref, zl_ref, shr_ref, shk_ref, shv_ref, shl_ref,
                        mur_ref, muk_ref, muv_ref, mul_ref, s0_ref,
                        w0_ref, w2t_ref, a0_ref, a2t_ref, g2t_ref, kk_ref, ka_ref, rk_ref,
                        lnw_ref, lnb_ref,
                        o_ref, sout_ref,
                        dec_s, a_s, b_s, k_s, r_s, v_s, o_s,
                        *, n_tok, hpb):
    ns = LANES
    ch = hpb * RW_HEAD

    def lanes(x):
        return jnp.concatenate([x] * n_tok, axis=1)

    def shifted(z_ref, sh_ref, mu_ref):
        z = z_ref[...]
        prev = jnp.concatenate([sh_ref[...], z[:, :(n_tok - 1) * ns]], axis=1)
        return z + lanes(mu_ref[...]) * (prev - z)

    r = shifted(zr_ref, shr_ref, mur_ref)
    k = shifted(zk_ref, shk_ref, muk_ref)
    v = shifted(zv_ref, shv_ref, muv_ref)
    zl = shifted(zl_ref, shl_ref, mul_ref)
    wd = zl[0:DECAY_LORA]
    ad = zl[DECAY_LORA:DECAY_LORA + AAA_LORA]
    gd = zl[DECAY_LORA + AAA_LORA:DECAY_LORA + AAA_LORA + GATE_LORA]

    wlog = -jax.nn.softplus(-(lanes(w0_ref[...]) + _mm(w2t_ref[...], jnp.tanh(wd)))) - 0.5
    lw = -jnp.exp(wlog)
    a = jax.nn.sigmoid(lanes(a0_ref[...]) + _mm(a2t_ref[...], ad))
    g = _mm(g2t_ref[...], jax.nn.sigmoid(gd))
    kk = k * lanes(kk_ref[...])
    km = k * (1.0 + (a - 1.0) * lanes(ka_ref[...]))

    def head_sum(x):
        x3 = x.reshape(hpb, RW_HEAD, n_tok * ns)
        s = jnp.sum(x3, axis=1, keepdims=True)
        return jnp.broadcast_to(s, x3.shape).reshape(ch, n_tok * ns)

    kkn = kk * lax.rsqrt(jnp.maximum(head_sum(kk * kk), 1e-24))
    dec_s[...] = jnp.exp(lw)
    a_s[...] = -kkn
    b_s[...] = kkn * a
    k_s[...] = km
    r_s[...] = r
    v_s[...] = v

    for hh in range(hpb):
        hrows = slice(hh * RW_HEAD, (hh + 1) * RW_HEAD)

        def body(i8, carry, hh=hh, hrows=hrows):
            base = pl.multiple_of(i8 * 8, 8)
            rows8 = pl.ds(pl.multiple_of(hh * RW_HEAD + base, 8), 8)
            o_rows = [[] for _ in range(n_tok)]
            for j in range(8):
                s = s0_ref[hh, base + j]
                for t in range(n_tok):
                    tl = slice(t * ns, (t + 1) * ns)
                    sa = jnp.sum(s * a_s[hrows, tl], axis=0, keepdims=True)
                    v_row = v_s[rows8, tl][j:j + 1, :]
                    s = s * dec_s[hrows, tl] + sa * b_s[hrows, tl] + v_row * k_s[hrows, tl]
                    o_rows[t].append(jnp.sum(s * r_s[hrows, tl], axis=0, keepdims=True))
                sout_ref[hh, base + j] = s
            for t in range(n_tok):
                o_s[rows8, t * ns:(t + 1) * ns] = jnp.concatenate(o_rows[t], axis=0)
            return carry

        lax.fori_loop(0, RW_HEAD // 8, body, 0)

    o = o_s[...]
    inv_n = 1.0 / RW_HEAD
    mean = head_sum(o) * inv_n
    var = head_sum(jnp.square(o - mean)) * inv_n
    on = ((o - mean) * lax.rsqrt(var + RW_GN_EPS)) * lanes(lnw_ref[...]) + lanes(lnb_ref[...])
    bonus = head_sum(r * km * lanes(rk_ref[...])) * v
    out = (on + bonus) * g
    for t in range(n_tok):
        o_ref[t] = out[:, t * ns:(t + 1) * ns].T.astype(BF16)


def _rwkv_sample_mixer(z_t, shift_t, mu_t, state_t, p_t, *, n_tok, hpb):
    ch = hpb * RW_HEAD
    n_steps = RW_HEADS // hpb
    nl = n_tok * LANES
    seg = RW_WIDTH // ch
    zspec = lambda s: pl.BlockSpec((ch, nl), lambda h: (s * seg + h, 0))
    cspec = lambda s: pl.BlockSpec((ch, LANES), lambda h: (s * seg + h, 0))
    lora_blk = 3 * RW_WIDTH // LORA_PAD
    zl_spec = pl.BlockSpec((LORA_PAD, nl), lambda h: (lora_blk, 0))
    cl_spec = pl.BlockSpec((LORA_PAD, LANES), lambda h: (lora_blk, 0))
    hspec = pl.BlockSpec((ch, LANES), lambda h: (h, 0))
    wspec = lambda k: pl.BlockSpec((ch, k), lambda h: (h, 0))
    sspec = pl.BlockSpec((hpb, RW_HEAD, RW_HEAD, LANES), lambda h: (h, 0, 0, 0))
    buf = lambda: pltpu.VMEM((ch, nl), F32)
    return pl.pallas_call(
        functools.partial(_rwkv_sample_kernel, n_tok=n_tok, hpb=hpb),
        grid=(n_steps,),
        in_specs=[zspec(0), zspec(1), zspec(2), zl_spec,
                  cspec(0), cspec(1), cspec(2), cl_spec,
                  cspec(0), cspec(1), cspec(2), cl_spec,
                  sspec,
                  hspec, wspec(DECAY_LORA), hspec, wspec(AAA_LORA), wspec(GATE_LORA),
                  hspec, hspec, hspec, hspec, hspec],
        out_specs=[pl.BlockSpec((n_tok, LANES, ch), lambda h: (0, 0, h)), sspec],
        out_shape=[jax.ShapeDtypeStruct((n_tok, LANES, RW_WIDTH), BF16),
                   jax.ShapeDtypeStruct(state_t.shape, F32)],
        scratch_shapes=[buf() for _ in range(7)],
        compiler_params=_cparams(("parallel",)),
        name="rwkv7_sample",
    )(z_t, z_t, z_t, z_t, shift_t, shift_t, shift_t, shift_t, mu_t, mu_t, mu_t, mu_t, state_t,
      p_t["w0"], p_t["w2t"], p_t["a0"], p_t["a2t"], p_t["g2t"], p_t["kk"], p_t["ka"], p_t["rk"],
      p_t["lnw"], p_t["lnb"])


def _retention_kernel(zq_ref, zk_ref, zv_ref, zg_ref, cos_ref, sin_ref, dmask_ref, iscale_ref,
                      kscale_ref, sdec_ref, s0_ref, o_ref, sout_ref, st_s,
                      *, rows, c, n_real, per_chunk_state):
    i = pl.program_id(1)
    n_chunks = rows // c

    if per_chunk_state:
        pr = lax.broadcasted_iota(jnp.int32, (rows, n_chunks * n_real), 0)
        pc = lax.broadcasted_iota(jnp.int32, (rows, n_chunks * n_real), 1)
        place = jnp.where((pr // c == pc // n_real) & (pr % c - (c - n_real) == pc % n_real),
                          1.0, 0.0).astype(BF16)
        load = lambda ref: _mm_exact_lhs(place, ref[...])
    else:
        load = lambda ref: ref[...]

        @pl.when(i == 0)
        def _():
            st_s[...] = s0_ref[0]

    lane = lax.broadcasted_iota(jnp.int32, (rows, RET_WIDTH), 1)
    even = (lane % 2) == 0
    cos = jnp.concatenate([cos_ref[...]] * RET_HEADS, axis=-1)
    sin = jnp.concatenate([sin_ref[...]] * RET_HEADS, axis=-1)

    def rot(x):
        partner = jnp.where(even, pltpu.roll(x, RET_WIDTH - 1, axis=1), pltpu.roll(x, 1, axis=1))
        return x * cos + partner * sin

    q = rot(load(zq_ref))
    k = rot(load(zk_ref)) * (RET_HEAD ** -0.5)
    v = load(zv_ref)
    g = load(zg_ref)

    out_rows = []
    for ch in range(n_chunks):
        rs = slice(ch * c, (ch + 1) * c)
        out_heads = []
        for h in range(RET_HEADS):
            hs = slice(h * RET_HEAD, (h + 1) * RET_HEAD)
            qh, kh, vh = q[rs, hs], k[rs, hs], v[rs, hs]
            s_prev = s0_ref[ch, h] if per_chunk_state else st_s[h]
            scores = _mm_nt(qh, kh) * dmask_ref[h]
            o = _mm(scores, vh) + _mm(qh, s_prev) * iscale_ref[h]
            s_new = s_prev * sdec_ref[h] + _mm_tn(kh * kscale_ref[h], vh)
            if per_chunk_state:
                sout_ref[ch, h] = s_new
            else:
                st_s[h] = s_new
            o = o * lax.rsqrt(jnp.mean(o * o, axis=-1, keepdims=True) + RET_GN_EPS)
            gh = g[rs, hs]
            out_heads.append(o * (gh * jax.nn.sigmoid(gh)))
        out_rows.append(jnp.concatenate(out_heads, axis=-1))
    o_out = jnp.concatenate(out_rows, axis=0).astype(BF16)
    if per_chunk_state:
        o_out = lax.dot_general(place, o_out, (((0,), (0,)), ((), ())),
                                preferred_element_type=F32).astype(BF16)
    o_ref[...] = o_out

    if not per_chunk_state:
        @pl.when(i == pl.num_programs(1) - 1)
        def _():
            sout_ref[0] = st_s[...]


def _retention_mixer(z, row_block0, n_groups, n_tiles, cos, sin, tabs, s0, *, rows, c, n_real,
                     per_chunk_state, pos_per_tile):
    n_states = s0.shape[0]
    sb = n_states // n_groups
    kern = functools.partial(_retention_kernel, rows=rows, c=c, n_real=n_real,
                             per_chunk_state=per_chunk_state)
    rows_io = rows // c * n_real if per_chunk_state else rows
    zspec = lambda col: pl.BlockSpec((rows_io, RET_WIDTH), lambda gi, i: (row_block0 + gi * n_tiles + i, col))
    full = lambda arr: pl.BlockSpec(arr.shape, lambda gi, i: (0,) * arr.ndim)
    tbl_row0, tbl_advance = pos_per_tile
    assert tbl_row0 % rows == 0
    tspec = pl.BlockSpec((rows, RET_HEAD), lambda gi, i: (tbl_row0 // rows + (i if tbl_advance else 0), 0))
    sspec = pl.BlockSpec((sb, RET_HEADS, RET_HEAD, RET_HEAD), lambda gi, i: (gi, 0, 0, 0))
    dmask, iscale, kscale, sdec = tabs
    return pl.pallas_call(
        kern,
        grid=(n_groups, n_tiles),
        in_specs=[zspec(0), zspec(1), zspec(2), zspec(3), tspec, tspec,
                  full(dmask), full(iscale), full(kscale), full(sdec), sspec],
        out_specs=[pl.BlockSpec((rows_io, RET_WIDTH), lambda gi, i: (gi * n_tiles + i, 0)), sspec],
        out_shape=[jax.ShapeDtypeStruct((n_groups * n_tiles * rows_io, RET_WIDTH), BF16),
                   jax.ShapeDtypeStruct(s0.shape, F32)],
        scratch_shapes=[pltpu.VMEM((RET_HEADS, RET_HEAD, RET_HEAD), F32)],
        compiler_params=_cparams(("parallel", "arbitrary")),
        name="retention_mixer",
    )(z, z, z, z, cos, sin, dmask, iscale, kscale, sdec, s0)


def _retention_tables(c, n_real):
    log_gamma = jnp.log(1.0 - 2.0 ** (-5.0 - jnp.arange(RET_HEADS, dtype=F32)))
    r = jnp.arange(c, dtype=F32)
    idx = r - float(c - n_real)
    diff = r[:, None] - r[None, :]
    dmask = jnp.where(diff[None] >= 0,
                      jnp.exp(log_gamma[:, None, None] * jnp.maximum(diff, 0.0)[None]), 0.0)
    iscale = jnp.exp(log_gamma[:, None] * (idx + 1.0)[None, :])[:, :, None]
    kscale = jnp.exp(log_gamma[:, None] * (n_real - 1.0 - idx)[None, :])[:, :, None]
    sdec = jnp.broadcast_to(jnp.exp(log_gamma * n_real)[:, None, None], (RET_HEADS, 1, RET_HEAD))
    return dmask, iscale, kscale, sdec


def _rotary_tables(pos):
    inv_freq = 1.0 / (ROPE_BASE ** jnp.linspace(0.0, 1.0, RET_HEAD // 2, dtype=F32))
    ang = pos.astype(F32)[:, None] * inv_freq[None, :]
    cos = jnp.cos(ang)
    sin = jnp.sin(ang)
    cos2 = jnp.repeat(cos, 2, axis=-1)
    sin2 = jnp.stack([-sin, sin], axis=-1).reshape(pos.shape[0], RET_HEAD)
    return cos2, sin2


def _outproj_kernel(oa_ref, ob_ref, w_ref, x_ref, g_ref, h_ref, hn_ref):
    acc = jnp.dot(oa_ref[...], w_ref[0:RW_WIDTH, :], preferred_element_type=F32)
    acc = acc + jnp.dot(ob_ref[...], w_ref[RW_WIDTH:, :], preferred_element_type=F32)
    h = x_ref[...] + acc
    h_ref[...] = h
    hn_ref[...] = _rms_norm_bf16(h, g_ref[...])


def _out_projection(o_a, o_b, w_out, x2d, norm_g, tm):
    m = o_a.shape[0]
    return pl.pallas_call(
        _outproj_kernel,
        grid=(m // tm,),
        in_specs=[
            pl.BlockSpec((tm, RW_WIDTH), lambda i: (i, 0)),
            pl.BlockSpec((tm, RET_WIDTH), lambda i: (i, 0)),
            pl.BlockSpec((D_MODEL, D_MODEL), lambda i: (0, 0)),
            pl.BlockSpec((tm, D_MODEL), lambda i: (i, 0)),
            pl.BlockSpec((1, D_MODEL), lambda i: (0, 0)),
        ],
        out_specs=[pl.BlockSpec((tm, D_MODEL), lambda i: (i, 0)),
                   pl.BlockSpec((tm, D_MODEL), lambda i: (i, 0))],
        out_shape=[jax.ShapeDtypeStruct((m, D_MODEL), F32), jax.ShapeDtypeStruct((m, D_MODEL), BF16)],
        compiler_params=_cparams(("parallel",)),
        name="out_projection",
    )(o_a, o_b, w_out, x2d, norm_g)


def _ffn_kernel(hn_ref, wg_ref, wu_ref, wd_ref, h_ref, g_ref, y_ref, acc_ref, *, h_slices):
    f = pl.program_id(1)

    @pl.when(f == 0)
    def _():
        acc_ref[...] = jnp.zeros_like(acc_ref)

    @pl.when(f < h_slices)
    def _():
        rows = h_ref.shape[0]
        sl = pl.ds(pl.multiple_of(f * rows, rows), rows)
        acc_ref[sl, :] += h_ref[...]

    hn = hn_ref[...]
    gate = jnp.dot(hn, wg_ref[...], preferred_element_type=F32)
    up = jnp.dot(hn, wu_ref[...], preferred_element_type=F32)
    act = (gate * jax.nn.sigmoid(gate)) * up
    acc_ref[...] += jnp.dot(act.astype(BF16), wd_ref[...], preferred_element_type=F32)

    @pl.when(f == pl.num_programs(1) - 1)
    def _():
        h = acc_ref[...]
        ms = jnp.mean(h * h, axis=-1, keepdims=True)
        y_ref[...] = (h * lax.rsqrt(ms + RMS_EPS)) * g_ref[...]


def _ffn(hn, w_gate, w_up, w_down, h, norm_g, tm, tf, h_slices=8):
    m = hn.shape[0]
    assert D_FF // tf >= h_slices
    return pl.pallas_call(
        functools.partial(_ffn_kernel, h_slices=h_slices),
        grid=(m // tm, D_FF // tf),
        in_specs=[
            pl.BlockSpec((tm, D_MODEL), lambda i, f: (i, 0)),
            pl.BlockSpec((D_MODEL, tf), lambda i, f: (0, f)),
            pl.BlockSpec((D_MODEL, tf), lambda i, f: (0, f)),
            pl.BlockSpec((tf, D_MODEL), lambda i, f: (f, 0)),
            pl.BlockSpec((tm // h_slices, D_MODEL),
                         lambda i, f: (i * h_slices + jnp.minimum(f, h_slices - 1), 0)),
            pl.BlockSpec((1, D_MODEL), lambda i, f: (0, 0)),
        ],
        out_specs=pl.BlockSpec((tm, D_MODEL), lambda i, f: (i, 0)),
        out_shape=jax.ShapeDtypeStruct((m, D_MODEL), F32),
        scratch_shapes=[pltpu.VMEM((tm, D_MODEL), F32)],
        compiler_params=_cparams(("parallel", "arbitrary")),
        name="swiglu_ffn",
    )(hn, w_gate, w_up, w_down, h, norm_g)


def _pad_cols(a, n):
    return jnp.pad(a, ((0, 0), (0, n - a.shape[1])))


def _pad_rows(a, n):
    return jnp.pad(a, ((0, n - a.shape[0]), (0, 0)))


def kernel(x_prompt, x_sample, state_shift, state_rwkv, state_ret, meta_tokens, norm_mix, w_in,
           rwkv_mu, rwkv_w0, rwkv_w2, rwkv_a0, rwkv_a2, rwkv_g2, rwkv_kk, rwkv_ka, rwkv_rk,
           rwkv_ln_w, rwkv_ln_b, w_out, norm_ffn, w_gate, w_up, w_down, norm_final):
    n_b, seq, d = x_prompt.shape
    n_s, dec_seq, _ = x_sample.shape
    n_p = n_b * seq
    n_d = n_s * dec_seq
    depth = w_in.shape[0]
    assert depth == 1 and d == D_MODEL and n_s == LANES

    w_in_t = jnp.transpose(w_in[0])
    w_in_p = _w_in_layout(w_in_t, tn=512)
    row = lambda a: a.reshape(1, -1).astype(F32)
    w2p = jnp.concatenate([rwkv_w2[0], jnp.zeros((128 - DECAY_LORA, RW_WIDTH), F32)], axis=0)
    a2p = jnp.concatenate([jnp.zeros((DECAY_LORA, RW_WIDTH), F32), rwkv_a2[0]], axis=0)
    g2p = jnp.concatenate([rwkv_g2[0], jnp.zeros((256 - GATE_LORA, RW_WIDTH), F32)], axis=0)
    rw_params = dict(mu=_pad_cols(row(rwkv_mu[0]), SHIFT_PAD), w0=row(rwkv_w0[0]), w2=w2p,
                     a0=row(rwkv_a0[0]), a2=a2p, g2=g2p, kk=row(rwkv_kk[0]), ka=row(rwkv_ka[0]),
                     rk=row(rwkv_rk[0]), lnw=row(rwkv_ln_w[0]), lnb=row(rwkv_ln_b[0]))
    col = lambda a: jnp.broadcast_to(a.reshape(-1, 1).astype(F32), (a.size, LANES))
    rw_params_t = dict(w0=col(rwkv_w0[0]), w2t=rwkv_w2[0].T, a0=col(rwkv_a0[0]), a2t=rwkv_a2[0].T,
                       g2t=rwkv_g2[0].T, kk=col(rwkv_kk[0]), ka=col(rwkv_ka[0]), rk=col(rwkv_rk[0]),
                       lnw=col(rwkv_ln_w[0]), lnb=col(rwkv_ln_b[0]))

    x_p = x_prompt.reshape(n_p, d)
    x_s = x_sample.reshape(n_d, d)
    x_sm = jnp.concatenate([x_s, meta_tokens.astype(F32)], axis=0)
    x_ts = jnp.transpose(x_sample, (1, 0, 2)).reshape(n_d, d)
    g_mix = row(norm_mix[0])
    z_p = _in_projection(x_p, g_mix, w_in_p, tm=1024, tn=1536)
    z_sm = _in_projection(x_sm, g_mix, w_in_p, tm=x_sm.shape[0], tn=768)
    z_st = _in_projection_t(w_in_t, x_ts, g_mix, SHIFT_PAD, tm=512)

    z_meta = jnp.pad(z_sm[n_d:], ((RET_CHUNK - N_META, 0), (0, 0)))
    srow = 2 * dec_seq

    zero_prev = jnp.zeros((1, 1, SHIFT_PAD), F32)
    zero_rw = jnp.zeros((1, RW_HEADS, RW_HEAD, RW_HEAD), F32)
    c_rw = 64
    _, s_rw_meta, zlast_meta = _rwkv_mixer(z_meta, 1, 1, 1, zero_prev, zero_rw, rw_params, tb=c_rw, c=c_rw)
    oa_p, rwkv_p, zlast_p, w_out_b, w_gate_b, w_up_b, w_down_b = _rwkv_mixer(
        z_p, 0, n_b, seq // 256, jnp.broadcast_to(zlast_meta, (n_b, 1, SHIFT_PAD)),
        jnp.broadcast_to(s_rw_meta, (n_b,) + s_rw_meta.shape[1:]), rw_params, tb=256, c=c_rw,
        cast=(w_out[0], w_gate[0], w_up[0], w_down[0]))
    shift_t = _pad_rows(jnp.transpose(state_shift[0]), SHIFT_PAD)
    mu_t = _pad_rows(col(rwkv_mu[0]), SHIFT_PAD)
    state_t = jnp.transpose(state_rwkv[0], (1, 2, 3, 0))
    oa_st, rwkv_st = _rwkv_sample_mixer(z_st, shift_t, mu_t, state_t, rw_params_t, n_tok=dec_seq, hpb=2)
    oa_s = jnp.transpose(oa_st, (1, 0, 2)).reshape(n_d, RW_WIDTH)
    rwkv_s = jnp.transpose(rwkv_st, (3, 0, 1, 2))

    past_len = 16384
    tabs_full = _retention_tables(RET_CHUNK, RET_CHUNK)
    tabs_meta = _retention_tables(RET_CHUNK, N_META)
    tabs_smp = _retention_tables(srow, dec_seq)
    sb_rt = 4
    pos_all = jnp.concatenate([N_META + jnp.arange(seq),
                               jnp.arange(RET_CHUNK) - (RET_CHUNK - N_META),
                               past_len + jnp.tile(jnp.arange(srow) - (srow - dec_seq), sb_rt)])
    cos_t, sin_t = _rotary_tables(pos_all)
    zero_rt = jnp.zeros((1, RET_HEADS, RET_HEAD, RET_HEAD), F32)
    _, s_rt_meta = _retention_mixer(z_meta, 0, 1, 1, cos_t, sin_t, tabs_meta, zero_rt,
                                    rows=RET_CHUNK, c=RET_CHUNK, n_real=N_META, per_chunk_state=False,
                                    pos_per_tile=(seq, False))
    rows_rt = 2 * RET_CHUNK
    ob_p, ret_p = _retention_mixer(z_p, 0, n_b, seq // rows_rt, cos_t, sin_t, tabs_full,
                                   jnp.broadcast_to(s_rt_meta, (n_b,) + s_rt_meta.shape[1:]),
                                   rows=rows_rt, c=RET_CHUNK, n_real=RET_CHUNK, per_chunk_state=False,
                                   pos_per_tile=(0, True))
    ob_s, ret_s = _retention_mixer(z_sm, 0, n_s // sb_rt, 1, cos_t, sin_t, tabs_smp,
                                   state_ret.reshape(n_s, RET_HEADS, RET_HEAD, RET_HEAD),
                                   rows=sb_rt * srow, c=srow, n_real=dec_seq, per_chunk_state=True,
                                   pos_per_tile=(seq + RET_CHUNK, False))

    g_ffn = row(norm_ffn[0])
    g_fin = row(norm_final)
    h_p, hn_p = _out_projection(oa_p, ob_p, w_out_b, x_p, g_ffn, tm=512)
    h_s, hn_s = _out_projection(oa_s, ob_s, w_out_b, x_s, g_ffn, tm=n_d)
    y_p = _ffn(hn_p, w_gate_b, w_up_b, w_down_b, h_p, g_fin, tm=1024, tf=512)
    y_s = _ffn(hn_s, w_gate_b, w_up_b, w_down_b, h_s, g_fin, tm=n_d, tf=512)

    y_prompt = y_p.reshape(n_b, seq, d)
    y_sample = y_s.reshape(n_s, dec_seq, d)
    shift_p = zlast_p[:, 0, :RW_COLS][None]
    shift_s = jnp.transpose(z_st[:RW_COLS, (dec_seq - 1) * n_s:])[None]
    return (y_prompt, y_sample, shift_p, rwkv_p[None], ret_p[None], shift_s, rwkv_s[None], ret_s[None])
```

```python
import functools
import math

import jax
import jax.numpy as jnp
from jax import lax
from jax.experimental import pallas as pl
from jax.experimental.pallas import tpu as pltpu

F32 = jnp.float32
BF16 = jnp.bfloat16

D_MODEL = 2048
N_META = 16
RW_WIDTH = 1024
RW_HEAD = 64
RW_HEADS = 16
DECAY_LORA = 64
AAA_LORA = 64
GATE_LORA = 160
RW_COLS = 3 * RW_WIDTH + DECAY_LORA + AAA_LORA + GATE_LORA
RET_WIDTH = 1024
RET_HEADS = 4
RET_HEAD = 256
RET_CHUNK = 128
D_FF = 5632
RMS_EPS = 1e-6
RW_GN_EPS = 64e-5
RET_GN_EPS = 1e-6
ROPE_BASE = 10000.0
LANES = 128

LORA_PAD = 512
Z_COLS = 4 * RET_WIDTH + 3 * RW_WIDTH + LORA_PAD
RW_OFF = 4 * RET_WIDTH
SHIFT_PAD = 3 * RW_WIDTH + LORA_PAD

VMEM_LIMIT = 60 * 1024 * 1024


def _cparams(sem):
    return pltpu.CompilerParams(dimension_semantics=sem, vmem_limit_bytes=VMEM_LIMIT)


def _mm(a, b):
    return jnp.dot(a.astype(BF16), b.astype(BF16), preferred_element_type=F32)


def _mm_nt(a, b):
    return lax.dot_general(a.astype(BF16), b.astype(BF16), (((1,), (1,)), ((), ())),
                           preferred_element_type=F32)


def _mm_tn(a, b):
    return lax.dot_general(a.astype(BF16), b.astype(BF16), (((0,), (0,)), ((), ())),
                           preferred_element_type=F32)


def _bmm(a, b):
    return jnp.einsum("bmk,bkn->bmn", a.astype(BF16), b.astype(BF16), preferred_element_type=F32)


def _bmm_nt(a, b):
    return jnp.einsum("bmk,bnk->bmn", a.astype(BF16), b.astype(BF16), preferred_element_type=F32)


def _bmm_tn(a, b):
    return jnp.einsum("bkm,bkn->bmn", a.astype(BF16), b.astype(BF16), preferred_element_type=F32)


def _mm_exact_lhs(m_bf16, x):
    hi = x.astype(BF16)
    r1 = x - hi.astype(F32)
    mid = r1.astype(BF16)
    lo = (r1 - mid.astype(F32)).astype(BF16)
    d = functools.partial(jnp.dot, preferred_element_type=F32)
    return d(m_bf16, hi) + d(m_bf16, mid) + d(m_bf16, lo)


def _rms_norm_bf16(x, g):
    ms = jnp.mean(x * x, axis=-1, keepdims=True)
    return ((x * lax.rsqrt(ms + RMS_EPS)) * g).astype(BF16)


def _w_in_layout_kernel(w_ref, o_ref):
    o_ref[...] = w_ref[...].T.astype(BF16)


def _w_in_layout(w_in_t, tn):
    n, d = w_in_t.shape
    n_ret = 4 * RET_WIDTH // tn

    def src_row(c):
        align = math.gcd(RW_COLS, tn)
        return (pl.multiple_of(jnp.where(c < n_ret, RW_COLS + tn * c, tn * (c - n_ret)), align), 0)

    return pl.pallas_call(
        _w_in_layout_kernel,
        grid=(Z_COLS // tn,),
        in_specs=[pl.BlockSpec((pl.Element(tn), pl.Element(d)), src_row)],
        out_specs=pl.BlockSpec((d, tn), lambda c: (0, c)),
        out_shape=jax.ShapeDtypeStruct((d, Z_COLS), BF16),
        compiler_params=_cparams(("parallel",)),
        name="w_in_layout",
    )(w_in_t)


def _inproj_kernel(x_ref, g_ref, w_ref, o_ref, xn_ref):
    @pl.when(pl.program_id(1) == 0)
    def _():
        xn_ref[...] = _rms_norm_bf16(x_ref[...], g_ref[...])

    o_ref[...] = jnp.dot(xn_ref[...], w_ref[...], preferred_element_type=F32)


def _in_projection(x2d, norm_g, w_in_p, tm, tn):
    m = x2d.shape[0]
    return pl.pallas_call(
        _inproj_kernel,
        grid=(m // tm, Z_COLS // tn),
        in_specs=[
            pl.BlockSpec((tm, D_MODEL), lambda i, j: (i, 0)),
            pl.BlockSpec((1, D_MODEL), lambda i, j: (0, 0)),
            pl.BlockSpec((D_MODEL, tn), lambda i, j: (0, j)),
        ],
        out_specs=pl.BlockSpec((tm, tn), lambda i, j: (i, j)),
        out_shape=jax.ShapeDtypeStruct((m, Z_COLS), F32),
        scratch_shapes=[pltpu.VMEM((tm, D_MODEL), BF16)],
        compiler_params=_cparams(("parallel", "arbitrary")),
        name="in_projection",
    )(x2d, norm_g, w_in_p)


def _inproj_t_kernel(w_ref, x_ref, g_ref, o_ref, xn_ref):
    @pl.when(pl.program_id(0) == 0)
    def _():
        xn_ref[...] = _rms_norm_bf16(x_ref[...], g_ref[...])

    o_ref[...] = _mm_nt(w_ref[...], xn_ref[...])


def _in_projection_t(w_in_t, x2d, norm_g, n_rows, tm):
    m, d = x2d.shape
    return pl.pallas_call(
        _inproj_t_kernel,
        grid=(n_rows // tm,),
        in_specs=[
            pl.BlockSpec((tm, d), lambda i: (i, 0)),
            pl.BlockSpec((m, d), lambda i: (0, 0)),
            pl.BlockSpec((1, d), lambda i: (0, 0)),
        ],
        out_specs=pl.BlockSpec((tm, m), lambda i: (i, 0)),
        out_shape=jax.ShapeDtypeStruct((n_rows, m), F32),
        scratch_shapes=[pltpu.VMEM((m, d), BF16)],
        compiler_params=_cparams(("arbitrary",)),
        name="in_projection_t",
    )(w_in_t, x2d, norm_g)


def _rwkv_kernel(zr_ref, zk_ref, zv_ref, zl_ref, zprev0_ref, s0_ref,
                 mu_ref, w0_ref, w2_ref, a0_ref, a2_ref, g2_ref, kk_ref, ka_ref, rk_ref,
                 lnw_ref, lnb_ref,
                 o_ref, sout_ref, zlast_ref,
                 carry_ref, sbd_s, r_s, km_s, v_s, kk_s, a_s, lw_s, cum_s, o_s,
                 *, tb, c, n_factors):
    i = pl.program_id(1)
    n_chunks = tb // c
    n_pairs = RW_HEADS // 2
    pw = 2 * RW_HEAD

    @pl.when(i == 0)
    def _():
        carry_ref[...] = zprev0_ref[0]
        zero = jnp.zeros((RW_HEAD, RW_HEAD), F32)
        for p in range(n_pairs):
            top = jnp.concatenate([s0_ref[0, 2 * p], zero], axis=1)
            bot = jnp.concatenate([zero, s0_ref[0, 2 * p + 1]], axis=1)
            sbd_s[p] = jnp.concatenate([top, bot], axis=0)

    row = lax.broadcasted_iota(jnp.int32, (tb, 1), 0)

    def shifted(z_ref, lo, hi):
        z = z_ref[...]
        prev = pltpu.roll(z, 1, axis=0)
        prev = jnp.concatenate([jnp.where(row[:8] == 0, carry_ref[:, lo:hi], prev[:8]), prev[8:]], axis=0)
        zs = z + mu_ref[:, lo:hi] * (prev - z)
        carry_ref[:, lo:hi] = z[tb - 1:tb, :]
        return zs

    w = RW_WIDTH
    r = shifted(zr_ref, 0, w)
    k = shifted(zk_ref, w, 2 * w)
    v = shifted(zv_ref, 2 * w, 3 * w)
    zl = shifted(zl_ref, 3 * w, 3 * w + LORA_PAD)

    lo2 = zl[:, 0:128]
    wlog = -jax.nn.softplus(-(w0_ref[...] + _mm(jnp.tanh(lo2), w2_ref[...]))) - 0.5
    lw = -jnp.exp(wlog)
    a = jax.nn.sigmoid(a0_ref[...] + _mm(lo2, a2_ref[...]))
    g = _mm(jax.nn.sigmoid(zl[:, 128:384]), g2_ref[...])
    kk = k * kk_ref[...]
    km = k * (1.0 + (a - 1.0) * ka_ref[...])

    ri = lax.broadcasted_iota(jnp.int32, (tb, tb), 0)
    ci = lax.broadcasted_iota(jnp.int32, (tb, tb), 1)
    tri = jnp.where((ri // c == ci // c) & (ri >= ci), 1.0, 0.0).astype(BF16)
    cum = _mm_exact_lhs(tri, lw)

    for dst, val in ((r_s, r), (km_s, km), (v_s, v), (a_s, a), (lw_s, lw), (cum_s, cum), (kk_s, kk)):
        dst[...] = val

    rr = lax.broadcasted_iota(jnp.int32, (c, c), 0)
    cc = lax.broadcasted_iota(jnp.int32, (c, c), 1)
    strict = (rr > cc)[None]
    eye = jnp.where(rr == cc, 1.0, 0.0).astype(F32)[None]
    rr2 = lax.broadcasted_iota(jnp.int32, (c, 2 * c), 0)
    cc2 = lax.broadcasted_iota(jnp.int32, (c, 2 * c), 1)
    strict_k = ((cc2 >= c) & (rr2 > cc2 - c))[None]
    incl_bk = (rr2 >= cc2 % c)[None]
    lo = (lax.broadcasted_iota(jnp.int32, (1, 1, pw), 2) < RW_HEAD)
    bd_r = lax.broadcasted_iota(jnp.int32, (pw, pw), 0) // RW_HEAD
    bd_c = lax.broadcasted_iota(jnp.int32, (pw, pw), 1) // RW_HEAD
    block_diag = (bd_r == bd_c)[None]
    rk = rk_ref[...]
    lnw = lnw_ref[...]
    lnb = lnb_ref[...]

    def head_sum(x):
        s_lo = jnp.sum(jnp.where(lo, x, 0.0), axis=-1, keepdims=True)
        s_hi = jnp.sum(jnp.where(lo, 0.0, x), axis=-1, keepdims=True)
        return jnp.where(lo, s_lo, s_hi)

    def pick(x16):
        x4 = x16.reshape(n_pairs, 2, c, pw)
        return jnp.where(lo, x4[:, 0], x4[:, 1])

    def body(ch):
        rows = pl.ds(ch * c, c)

        def pairs(s):
            x = s[rows, :]
            return jnp.stack([x[:, p * pw:(p + 1) * pw] for p in range(n_pairs)], axis=0)

        rh, kmh, vh, ah, lwh, cumh, kkh = [pairs(s) for s in (r_s, km_s, v_s, a_s, lw_s, cum_s, kk_s)]
        s_prev = sbd_s[...]
        kkn = kkh * lax.rsqrt(jnp.maximum(head_sum(kkh * kkh), 1e-24))
        bvec = kkn * ah
        cum_end = cumh[:, c - 1:c, :]
        p_inc = jnp.exp(cumh)
        p_exc = jnp.exp(cumh - lwh)
        p_inv = jnp.exp(-cumh)
        p_end = jnp.exp(cum_end)
        p_rel = jnp.exp(cum_end - cumh)
        a_t = -(kkn * p_exc)
        r_t = rh * p_inc
        b_t = bvec * p_inv
        k_t = kmh * p_inv
        b_h = bvec * p_rel
        k_h = kmh * p_rel

        ar = jnp.concatenate([jnp.where(lo, a_t, 0.0), jnp.where(lo, r_t, 0.0),
                              jnp.where(lo, 0.0, a_t), jnp.where(lo, 0.0, r_t)], axis=1)
        prod = _bmm_nt(ar, jnp.concatenate([b_t, k_t, s_prev], axis=1))
        prod = prod.reshape(RW_HEADS, 2 * c, 2 * c + pw)
        a_rows, r_rows = prod[:, :c], prod[:, c:]
        lmat = jnp.where(strict, a_rows[:, :, :c], 0.0)
        aak_wide = jnp.where(strict_k, a_rows[:, :, :2 * c], 0.0)
        arbk = jnp.where(incl_bk, r_rows[:, :, :2 * c], 0.0)
        tinv = eye + lmat
        if n_factors > 1:
            lp = _bmm(lmat, lmat)
            for _ in range(n_factors - 2):
                st = _bmm(jnp.concatenate([lp, tinv], axis=1), lp)
                lp = st[:, :c]
                tinv = tinv + st[:, c:]
            tinv = tinv + _bmm(tinv, lp)
        v16 = jnp.repeat(vh, 2, axis=0)
        x = a_rows[:, :, 2 * c:] + _bmm(aak_wide, jnp.concatenate([v16, v16], axis=1))
        u16 = _bmm(tinv, x)
        o16 = r_rows[:, :, 2 * c:] + _bmm(arbk, jnp.concatenate([u16, v16], axis=1))
        u = pick(u16)
        o = pick(o16)
        s_upd = _bmm_tn(jnp.concatenate([u, vh], axis=1), jnp.concatenate([b_h, k_h], axis=1))
        sbd_s[...] = s_prev * p_end + jnp.where(block_diag, s_upd, 0.0)
        inv_n = 1.0 / RW_HEAD
        mean = head_sum(o) * inv_n
        var = head_sum(jnp.square(o - mean)) * inv_n
        on = ((o - mean) * lax.rsqrt(var + RW_GN_EPS)) * lnw + lnb
        out = on + head_sum(rh * kmh * rk) * vh
        for p in range(n_pairs):
            o_s[rows, p * pw:(p + 1) * pw] = out[p]

    for ch in range(n_chunks):
        body(ch)

    o_ref[...] = (o_s[...] * g).astype(BF16)
    zlast_ref[0] = carry_ref[...]

    @pl.when(i == pl.num_programs(1) - 1)
    def _():
        for p in range(n_pairs):
            blk = sbd_s[p]
            sout_ref[0, 2 * p] = blk[:RW_HEAD, :RW_HEAD]
            sout_ref[0, 2 * p + 1] = blk[RW_HEAD:, RW_HEAD:]


N_RWKV_INPUTS = 17
N_RWKV_OUTPUTS = 3


def _rwkv_and_cast_kernel(*refs, n_cast, **kw):
    ins = refs[:N_RWKV_INPUTS]
    cast_in = refs[N_RWKV_INPUTS:N_RWKV_INPUTS + n_cast]
    o0 = N_RWKV_INPUTS + n_cast
    outs = refs[o0:o0 + N_RWKV_OUTPUTS]
    cast_out = refs[o0 + N_RWKV_OUTPUTS:o0 + N_RWKV_OUTPUTS + n_cast]
    scratch = refs[o0 + N_RWKV_OUTPUTS + n_cast:]
    for src, dst in zip(cast_in, cast_out):
        dst[...] = src[...].astype(BF16)
    _rwkv_kernel(*ins, *outs, *scratch, **kw)


def _rwkv_mixer(z, row_block0, n_groups, n_tiles, zprev0, s0, p, *, tb, c, cast=()):
    n_factors = max(1, math.ceil(math.log2(c)))
    kern = functools.partial(_rwkv_and_cast_kernel, n_cast=len(cast), tb=tb, c=c, n_factors=n_factors)
    n_steps = n_groups * n_tiles
    cast_specs = [pl.BlockSpec((a.shape[0] // n_steps, a.shape[1]), lambda gi, i: (gi * n_tiles + i, 0))
                  for a in cast]
    cast_shapes = [jax.ShapeDtypeStruct(a.shape, BF16) for a in cast]
    cb = RW_OFF // RW_WIDTH
    full = lambda arr: pl.BlockSpec(arr.shape, lambda gi, i: (0,) * arr.ndim)
    lora_cb = (RW_OFF + 3 * RW_WIDTH) // LORA_PAD
    per_pair = lambda a: a.reshape(RW_HEADS // 2, 1, 2 * RW_HEAD)
    params = [p["mu"], p["w0"], p["w2"], p["a0"], p["a2"], p["g2"], p["kk"], p["ka"],
              per_pair(p["rk"]), per_pair(p["lnw"]), per_pair(p["lnb"])]
    dense = lambda: pltpu.VMEM((tb, RW_WIDTH), F32)
    zspec = lambda width, col: pl.BlockSpec((tb, width), lambda gi, i: (row_block0 + gi * n_tiles + i, col))
    sspec = pl.BlockSpec((1, RW_HEADS, RW_HEAD, RW_HEAD), lambda gi, i: (gi, 0, 0, 0))
    return pl.pallas_call(
        kern,
        grid=(n_groups, n_tiles),
        in_specs=[zspec(RW_WIDTH, cb), zspec(RW_WIDTH, cb + 1), zspec(RW_WIDTH, cb + 2),
                  zspec(LORA_PAD, lora_cb),
                  pl.BlockSpec((1, 1, SHIFT_PAD), lambda gi, i: (gi, 0, 0)),
                  sspec]
                 + [full(a) for a in params] + cast_specs,
        out_specs=[pl.BlockSpec((tb, RW_WIDTH), lambda gi, i: (gi * n_tiles + i, 0)), sspec,
                   pl.BlockSpec((1, 1, SHIFT_PAD), lambda gi, i: (gi, 0, 0))] + cast_specs,
        out_shape=[jax.ShapeDtypeStruct((n_groups * n_tiles * tb, RW_WIDTH), BF16),
                   jax.ShapeDtypeStruct((n_groups, RW_HEADS, RW_HEAD, RW_HEAD), F32),
                   jax.ShapeDtypeStruct((n_groups, 1, SHIFT_PAD), F32)] + cast_shapes,
        scratch_shapes=[pltpu.VMEM((1, SHIFT_PAD), F32),
                        pltpu.VMEM((RW_HEADS // 2, 2 * RW_HEAD, 2 * RW_HEAD), F32)]
                       + [dense() for _ in range(8)],
        compiler_params=_cparams(("parallel", "arbitrary")),
        name="rwkv7_mixer",
    )(z, z, z, z, zprev0, s0, *params, *cast)


def _rwkv_sample_kernel(zr_ref, zk_ref, zv_ref, zl_ref, shr_ref, shk_ref, shv_ref, shl_ref,
                        mur_ref, muk_ref, muv_ref, mul_ref, s0_ref,
                        w0_ref, w2t_ref, a0_ref, a2t_ref, g2t_ref, kk_ref, ka_ref, rk_ref,
                        lnw_ref, lnb_ref,
                        o_ref, sout_ref,
                        dec_s, a_s, b_s, k_s, r_s, v_s, o_s,
                        *, n_tok, hpb):
    ns = LANES
    ch = hpb * RW_HEAD

    def lanes(x):
        return jnp.concatenate([x] * n_tok, axis=1)

    def shifted(z_ref, sh_ref, mu_ref):
        z = z_ref[...]
        prev = jnp.concatenate([sh_ref[...], z[:, :(n_tok - 1) * ns]], axis=1)
        return z + lanes(mu_ref[...]) * (prev - z)

    r = shifted(zr_ref, shr_ref, mur_ref)
    k = shifted(zk_ref, shk_ref, muk_ref)
    v = shifted(zv_ref, shv_ref, muv_ref)
    zl = shifted(zl_ref, shl_ref, mul_ref)
    wd = zl[0:DECAY_LORA]
    ad = zl[DECAY_LORA:DECAY_LORA + AAA_LORA]
    gd = zl[DECAY_LORA + AAA_LORA:DECAY_LORA + AAA_LORA + GATE_LORA]

    wlog = -jax.nn.softplus(-(lanes(w0_ref[...]) + _mm(w2t_ref[...], jnp.tanh(wd)))) - 0.5
    lw = -jnp.exp(wlog)
    a = jax.nn.sigmoid(lanes(a0_ref[...]) + _mm(a2t_ref[...], ad))
    g = _mm(g2t_ref[...], jax.nn.sigmoid(gd))
    kk = k * lanes(kk_ref[...])
    km = k * (1.0 + (a - 1.0) * lanes(ka_ref[...]))

    def head_sum(x):
        x3 = x.reshape(hpb, RW_HEAD, n_tok * ns)
        s = jnp.sum(x3, axis=1, keepdims=True)
        return jnp.broadcast_to(s, x3.shape).reshape(ch, n_tok * ns)

    kkn = kk * lax.rsqrt(jnp.maximum(head_sum(kk * kk), 1e-24))
    dec_s[...] = jnp.exp(lw)
    a_s[...] = -kkn
    b_s[...] = kkn * a
    k_s[...] = km
    r_s[...] = r
    v_s[...] = v

    for hh in range(hpb):
        hrows = slice(hh * RW_HEAD, (hh + 1) * RW_HEAD)

        def body(i8, carry, hh=hh, hrows=hrows):
            base = pl.multiple_of(i8 * 8, 8)
            rows8 = pl.ds(pl.multiple_of(hh * RW_HEAD + base, 8), 8)
            o_rows = [[] for _ in range(n_tok)]
            for j in range(8):
                s = s0_ref[hh, base + j]
                for t in range(n_tok):
                    tl = slice(t * ns, (t + 1) * ns)
                    sa = jnp.sum(s * a_s[hrows, tl], axis=0, keepdims=True)
                    v_row = v_s[rows8, tl][j:j + 1, :]
                    s = s * dec_s[hrows, tl] + sa * b_s[hrows, tl] + v_row * k_s[hrows, tl]
                    o_rows[t].append(jnp.sum(s * r_s[hrows, tl], axis=0, keepdims=True))
                sout_ref[hh, base + j] = s
            for t in range(n_tok):
                o_s[rows8, t * ns:(t + 1) * ns] = jnp.concatenate(o_rows[t], axis=0)
            return carry

        lax.fori_loop(0, RW_HEAD // 8, body, 0)

    o = o_s[...]
    inv_n = 1.0 / RW_HEAD
    mean = head_sum(o) * inv_n
    var = head_sum(jnp.square(o - mean)) * inv_n
    on = ((o - mean) * lax.rsqrt(var + RW_GN_EPS)) * lanes(lnw_ref[...]) + lanes(lnb_ref[...])
    bonus = head_sum(r * km * lanes(rk_ref[...])) * v
    out = (on + bonus) * g
    for t in range(n_tok):
        o_ref[t] = out[:, t * ns:(t + 1) * ns].T.astype(BF16)


def _rwkv_sample_mixer(z_t, shift_t, mu_t, state_t, p_t, *, n_tok, hpb):
    ch = hpb * RW_HEAD
    n_steps = RW_HEADS // hpb
    nl = n_tok * LANES
    seg = RW_WIDTH // ch
    zspec = lambda s: pl.BlockSpec((ch, nl), lambda h: (s * seg + h, 0))
    cspec = lambda s: pl.BlockSpec((ch, LANES), lambda h: (s * seg + h, 0))
    lora_blk = 3 * RW_WIDTH // LORA_PAD
    zl_spec = pl.BlockSpec((LORA_PAD, nl), lambda h: (lora_blk, 0))
    cl_spec = pl.BlockSpec((LORA_PAD, LANES), lambda h: (lora_blk, 0))
    hspec = pl.BlockSpec((ch, LANES), lambda h: (h, 0))
    wspec = lambda k: pl.BlockSpec((ch, k), lambda h: (h, 0))
    sspec = pl.BlockSpec((hpb, RW_HEAD, RW_HEAD, LANES), lambda h: (h, 0, 0, 0))
    buf = lambda: pltpu.VMEM((ch, nl), F32)
    return pl.pallas_call(
        functools.partial(_rwkv_sample_kernel, n_tok=n_tok, hpb=hpb),
        grid=(n_steps,),
        in_specs=[zspec(0), zspec(1), zspec(2), zl_spec,
                  cspec(0), cspec(1), cspec(2), cl_spec,
                  cspec(0), cspec(1), cspec(2), cl_spec,
                  sspec,
                  hspec, wspec(DECAY_LORA), hspec, wspec(AAA_LORA), wspec(GATE_LORA),
                  hspec, hspec, hspec, hspec, hspec],
        out_specs=[pl.BlockSpec((n_tok, LANES, ch), lambda h: (0, 0, h)), sspec],
        out_shape=[jax.ShapeDtypeStruct((n_tok, LANES, RW_WIDTH), BF16),
                   jax.ShapeDtypeStruct(state_t.shape, F32)],
        scratch_shapes=[buf() for _ in range(7)],
        compiler_params=_cparams(("parallel",)),
        name="rwkv7_sample",
    )(z_t, z_t, z_t, z_t, shift_t, shift_t, shift_t, shift_t, mu_t, mu_t, mu_t, mu_t, state_t,
      p_t["w0"], p_t["w2t"], p_t["a0"], p_t["a2t"], p_t["g2t"], p_t["kk"], p_t["ka"], p_t["rk"],
      p_t["lnw"], p_t["lnb"])


def _retention_kernel(zq_ref, zk_ref, zv_ref, zg_ref, cos_ref, sin_ref, dmask_ref, iscale_ref,
                      kscale_ref, sdec_ref, s0_ref, o_ref, sout_ref, st_s,
                      *, rows, c, n_real, per_chunk_state):
    i = pl.program_id(1)
    n_chunks = rows // c

    if per_chunk_state:
        pr = lax.broadcasted_iota(jnp.int32, (rows, n_chunks * n_real), 0)
        pc = lax.broadcasted_iota(jnp.int32, (rows, n_chunks * n_real), 1)
        place = jnp.where((pr // c == pc // n_real) & (pr % c - (c - n_real) == pc % n_real),
                          1.0, 0.0).astype(BF16)
        load = lambda ref: _mm_exact_lhs(place, ref[...])
    else:
        load = lambda ref: ref[...]

        @pl.when(i == 0)
        def _():
            st_s[...] = s0_ref[0]

    lane = lax.broadcasted_iota(jnp.int32, (rows, RET_WIDTH), 1)
    even = (lane % 2) == 0
    cos = jnp.concatenate([cos_ref[...]] * RET_HEADS, axis=-1)
    sin = jnp.concatenate([sin_ref[...]] * RET_HEADS, axis=-1)

    def rot(x):
        partner = jnp.where(even, pltpu.roll(x, RET_WIDTH - 1, axis=1), pltpu.roll(x, 1, axis=1))
        return x * cos + partner * sin

    q = rot(load(zq_ref))
    k = rot(load(zk_ref)) * (RET_HEAD ** -0.5)
    v = load(zv_ref)
    g = load(zg_ref)

    out_rows = []
    for ch in range(n_chunks):
        rs = slice(ch * c, (ch + 1) * c)
        out_heads = []
        for h in range(RET_HEADS):
            hs = slice(h * RET_HEAD, (h + 1) * RET_HEAD)
            qh, kh, vh = q[rs, hs], k[rs, hs], v[rs, hs]
            s_prev = s0_ref[ch, h] if per_chunk_state else st_s[h]
            scores = _mm_nt(qh, kh) * dmask_ref[h]
            o = _mm(scores, vh) + _mm(qh, s_prev) * iscale_ref[h]
            s_new = s_prev * sdec_ref[h] + _mm_tn(kh * kscale_ref[h], vh)
            if per_chunk_state:
                sout_ref[ch, h] = s_new
            else:
                st_s[h] = s_new
            o = o * lax.rsqrt(jnp.mean(o * o, axis=-1, keepdims=True) + RET_GN_EPS)
            gh = g[rs, hs]
            out_heads.append(o * (gh * jax.nn.sigmoid(gh)))
        out_rows.append(jnp.concatenate(out_heads, axis=-1))
    o_out = jnp.concatenate(out_rows, axis=0).astype(BF16)
    if per_chunk_state:
        o_out = lax.dot_general(place, o_out, (((0,), (0,)), ((), ())),
                                preferred_element_type=F32).astype(BF16)
    o_ref[...] = o_out

    if not per_chunk_state:
        @pl.when(i == pl.num_programs(1) - 1)
        def _():
            sout_ref[0] = st_s[...]


def _retention_mixer(z, row_block0, n_groups, n_tiles, cos, sin, tabs, s0, *, rows, c, n_real,
                     per_chunk_state, pos_per_tile):
    n_states = s0.shape[0]
    sb = n_states // n_groups
    kern = functools.partial(_retention_kernel, rows=rows, c=c, n_real=n_real,
                             per_chunk_state=per_chunk_state)
    rows_io = rows // c * n_real if per_chunk_state else rows
    zspec = lambda col: pl.BlockSpec((rows_io, RET_WIDTH), lambda gi, i: (row_block0 + gi * n_tiles + i, col))
    full = lambda arr: pl.BlockSpec(arr.shape, lambda gi, i: (0,) * arr.ndim)
    tbl_row0, tbl_advance = pos_per_tile
    assert tbl_row0 % rows == 0
    tspec = pl.BlockSpec((rows, RET_HEAD), lambda gi, i: (tbl_row0 // rows + (i if tbl_advance else 0), 0))
    sspec = pl.BlockSpec((sb, RET_HEADS, RET_HEAD, RET_HEAD), lambda gi, i: (gi, 0, 0, 0))
    dmask, iscale, kscale, sdec = tabs
    return pl.pallas_call(
        kern,
        grid=(n_groups, n_tiles),
        in_specs=[zspec(0), zspec(1), zspec(2), zspec(3), tspec, tspec,
                  full(dmask), full(iscale), full(kscale), full(sdec), sspec],
        out_specs=[pl.BlockSpec((rows_io, RET_WIDTH), lambda gi, i: (gi * n_tiles + i, 0)), sspec],
        out_shape=[jax.ShapeDtypeStruct((n_groups * n_tiles * rows_io, RET_WIDTH), BF16),
                   jax.ShapeDtypeStruct(s0.shape, F32)],
        scratch_shapes=[pltpu.VMEM((RET_HEADS, RET_HEAD, RET_HEAD), F32)],
        compiler_params=_cparams(("parallel", "arbitrary")),
        name="retention_mixer",
    )(z, z, z, z, cos, sin, dmask, iscale, kscale, sdec, s0)


def _retention_tables(c, n_real):
    log_gamma = jnp.log(1.0 - 2.0 ** (-5.0 - jnp.arange(RET_HEADS, dtype=F32)))
    r = jnp.arange(c, dtype=F32)
    idx = r - float(c - n_real)
    diff = r[:, None] - r[None, :]
    dmask = jnp.where(diff[None] >= 0,
                      jnp.exp(log_gamma[:, None, None] * jnp.maximum(diff, 0.0)[None]), 0.0)
    iscale = jnp.exp(log_gamma[:, None] * (idx + 1.0)[None, :])[:, :, None]
    kscale = jnp.exp(log_gamma[:, None] * (n_real - 1.0 - idx)[None, :])[:, :, None]
    sdec = jnp.broadcast_to(jnp.exp(log_gamma * n_real)[:, None, None], (RET_HEADS, 1, RET_HEAD))
    return dmask, iscale, kscale, sdec


def _rotary_tables(pos):
    inv_freq = 1.0 / (ROPE_BASE ** jnp.linspace(0.0, 1.0, RET_HEAD // 2, dtype=F32))
    ang = pos.astype(F32)[:, None] * inv_freq[None, :]
    cos = jnp.cos(ang)
    sin = jnp.sin(ang)
    cos2 = jnp.repeat(cos, 2, axis=-1)
    sin2 = jnp.stack([-sin, sin], axis=-1).reshape(pos.shape[0], RET_HEAD)
    return cos2, sin2


def _outproj_kernel(oa_ref, ob_ref, w_ref, x_ref, g_ref, h_ref, hn_ref):
    acc = jnp.dot(oa_ref[...], w_ref[0:RW_WIDTH, :], preferred_element_type=F32)
    acc = acc + jnp.dot(ob_ref[...], w_ref[RW_WIDTH:, :], preferred_element_type=F32)
    h = x_ref[...] + acc
    h_ref[...] = h
    hn_ref[...] = _rms_norm_bf16(h, g_ref[...])


def _out_projection(o_a, o_b, w_out, x2d, norm_g, tm):
    m = o_a.shape[0]
    return pl.pallas_call(
        _outproj_kernel,
        grid=(m // tm,),
        in_specs=[
            pl.BlockSpec((tm, RW_WIDTH), lambda i: (i, 0)),
            pl.BlockSpec((tm, RET_WIDTH), lambda i: (i, 0)),
            pl.BlockSpec((D_MODEL, D_MODEL), lambda i: (0, 0)),
            pl.BlockSpec((tm, D_MODEL), lambda i: (i, 0)),
            pl.BlockSpec((1, D_MODEL), lambda i: (0, 0)),
        ],
        out_specs=[pl.BlockSpec((tm, D_MODEL), lambda i: (i, 0)),
                   pl.BlockSpec((tm, D_MODEL), lambda i: (i, 0))],
        out_shape=[jax.ShapeDtypeStruct((m, D_MODEL), F32), jax.ShapeDtypeStruct((m, D_MODEL), BF16)],
        compiler_params=_cparams(("parallel",)),
        name="out_projection",
    )(o_a, o_b, w_out, x2d, norm_g)


def _ffn_kernel(hn_ref, wg_ref, wu_ref, wd_ref, h_ref, g_ref, y_ref, acc_ref, act_ref, *, h_slices):
    f = pl.program_id(1)
    n_f = pl.num_programs(1) - 1

    def gated_activation():
        hn = hn_ref[...]
        gate = jnp.dot(hn, wg_ref[...], preferred_element_type=F32)
        up = jnp.dot(hn, wu_ref[...], preferred_element_type=F32)
        return ((gate * jax.nn.sigmoid(gate)) * up).astype(BF16)

    slot = f % 2

    def down_prev():
        return jnp.dot(act_ref[1 - slot], wd_ref[...], preferred_element_type=F32)

    @pl.when(f == 0)
    def _():
        acc_ref[...] = jnp.zeros_like(acc_ref)
        act_ref[slot] = gated_activation()

    @pl.when((f > 0) & (f < n_f))
    def _():
        acc_ref[...] += down_prev()
        act_ref[slot] = gated_activation()

    @pl.when(f < h_slices)
    def _():
        rows = h_ref.shape[0]
        sl = pl.ds(pl.multiple_of(f * rows, rows), rows)
        acc_ref[sl, :] += h_ref[...]

    @pl.when(f == n_f)
    def _():
        h = acc_ref[...] + down_prev()
        ms = jnp.mean(h * h, axis=-1, keepdims=True)
        y_ref[...] = (h * lax.rsqrt(ms + RMS_EPS)) * g_ref[...]


def _ffn(hn, w_gate, w_up, w_down, h, norm_g, tm, tf, h_slices=8):
    m = hn.shape[0]
    n_f = D_FF // tf
    assert n_f >= h_slices
    return pl.pallas_call(
        functools.partial(_ffn_kernel, h_slices=h_slices),
        grid=(m // tm, n_f + 1),
        in_specs=[
            pl.BlockSpec((tm, D_MODEL), lambda i, f: (i, 0)),
            pl.BlockSpec((D_MODEL, tf), lambda i, f: (0, jnp.minimum(f, n_f - 1))),
            pl.BlockSpec((D_MODEL, tf), lambda i, f: (0, jnp.minimum(f, n_f - 1))),
            pl.BlockSpec((tf, D_MODEL), lambda i, f: (jnp.maximum(f - 1, 0), 0)),
            pl.BlockSpec((tm // h_slices, D_MODEL),
                         lambda i, f: (i * h_slices + jnp.minimum(f, h_slices - 1), 0)),
            pl.BlockSpec((1, D_MODEL), lambda i, f: (0, 0)),
        ],
        out_specs=pl.BlockSpec((tm, D_MODEL), lambda i, f: (i, 0)),
        out_shape=jax.ShapeDtypeStruct((m, D_MODEL), F32),
        scratch_shapes=[pltpu.VMEM((tm, D_MODEL), F32), pltpu.VMEM((2, tm, tf), BF16)],
        compiler_params=_cparams(("parallel", "arbitrary")),
        name="swiglu_ffn",
    )(hn, w_gate, w_up, w_down, h, norm_g)


def _pad_cols(a, n):
    return jnp.pad(a, ((0, 0), (0, n - a.shape[1])))


def _pad_rows(a, n):
    return jnp.pad(a, ((0, n - a.shape[0]), (0, 0)))


def kernel(x_prompt, x_sample, state_shift, state_rwkv, state_ret, meta_tokens, norm_mix, w_in,
           rwkv_mu, rwkv_w0, rwkv_w2, rwkv_a0, rwkv_a2, rwkv_g2, rwkv_kk, rwkv_ka, rwkv_rk,
           rwkv_ln_w, rwkv_ln_b, w_out, norm_ffn, w_gate, w_up, w_down, norm_final):
    n_b, seq, d = x_prompt.shape
    n_s, dec_seq, _ = x_sample.shape
    n_p = n_b * seq
    n_d = n_s * dec_seq
    depth = w_in.shape[0]
    assert depth == 1 and d == D_MODEL and n_s == LANES

    w_in_t = jnp.transpose(w_in[0])
    w_in_p = _w_in_layout(w_in_t, tn=512)
    row = lambda a: a.reshape(1, -1).astype(F32)
    w2p = jnp.concatenate([rwkv_w2[0], jnp.zeros((128 - DECAY_LORA, RW_WIDTH), F32)], axis=0)
    a2p = jnp.concatenate([jnp.zeros((DECAY_LORA, RW_WIDTH), F32), rwkv_a2[0]], axis=0)
    g2p = jnp.concatenate([rwkv_g2[0], jnp.zeros((256 - GATE_LORA, RW_WIDTH), F32)], axis=0)
    rw_params = dict(mu=_pad_cols(row(rwkv_mu[0]), SHIFT_PAD), w0=row(rwkv_w0[0]), w2=w2p,
                     a0=row(rwkv_a0[0]), a2=a2p, g2=g2p, kk=row(rwkv_kk[0]), ka=row(rwkv_ka[0]),
                     rk=row(rwkv_rk[0]), lnw=row(rwkv_ln_w[0]), lnb=row(rwkv_ln_b[0]))
    col = lambda a: jnp.broadcast_to(a.reshape(-1, 1).astype(F32), (a.size, LANES))
    rw_params_t = dict(w0=col(rwkv_w0[0]), w2t=rwkv_w2[0].T, a0=col(rwkv_a0[0]), a2t=rwkv_a2[0].T,
                       g2t=rwkv_g2[0].T, kk=col(rwkv_kk[0]), ka=col(rwkv_ka[0]), rk=col(rwkv_rk[0]),
                       lnw=col(rwkv_ln_w[0]), lnb=col(rwkv_ln_b[0]))

    x_p = x_prompt.reshape(n_p, d)
    x_s = x_sample.reshape(n_d, d)
    x_sm = jnp.concatenate([x_s, meta_tokens.astype(F32)], axis=0)
    x_ts = jnp.transpose(x_sample, (1, 0, 2)).reshape(n_d, d)
    g_mix = row(norm_mix[0])
    z_p = _in_projection(x_p, g_mix, w_in_p, tm=1024, tn=1920)
    z_sm = _in_projection(x_sm, g_mix, w_in_p, tm=x_sm.shape[0], tn=768)
    z_st = _in_projection_t(w_in_t, x_ts, g_mix, SHIFT_PAD, tm=512)

    z_meta = jnp.pad(z_sm[n_d:], ((RET_CHUNK - N_META, 0), (0, 0)))
    srow = 2 * dec_seq

    zero_prev = jnp.zeros((1, 1, SHIFT_PAD), F32)
    zero_rw = jnp.zeros((1, RW_HEADS, RW_HEAD, RW_HEAD), F32)
    c_rw = 64
    _, s_rw_meta, zlast_meta = _rwkv_mixer(z_meta, 1, 1, 1, zero_prev, zero_rw, rw_params, tb=c_rw, c=c_rw)
    oa_p, rwkv_p, zlast_p, w_out_b, w_gate_b, w_up_b, w_down_b = _rwkv_mixer(
        z_p, 0, n_b, seq // 256, jnp.broadcast_to(zlast_meta, (n_b, 1, SHIFT_PAD)),
        jnp.broadcast_to(s_rw_meta, (n_b,) + s_rw_meta.shape[1:]), rw_params, tb=256, c=c_rw,
        cast=(w_out[0], w_gate[0], w_up[0], w_down[0]))
    shift_t = _pad_rows(jnp.transpose(state_shift[0]), SHIFT_PAD)
    mu_t = _pad_rows(col(rwkv_mu[0]), SHIFT_PAD)
    state_t = jnp.transpose(state_rwkv[0], (1, 2, 3, 0))
    oa_st, rwkv_st = _rwkv_sample_mixer(z_st, shift_t, mu_t, state_t, rw_params_t, n_tok=dec_seq, hpb=2)
    oa_s = jnp.transpose(oa_st, (1, 0, 2)).reshape(n_d, RW_WIDTH)
    rwkv_s = jnp.transpose(rwkv_st, (3, 0, 1, 2))

    past_len = 16384
    tabs_full = _retention_tables(RET_CHUNK, RET_CHUNK)
    tabs_meta = _retention_tables(RET_CHUNK, N_META)
    tabs_smp = _retention_tables(srow, dec_seq)
    sb_rt = 8
    pos_all = jnp.concatenate([N_META + jnp.arange(seq),
                               jnp.arange(RET_CHUNK) - (RET_CHUNK - N_META),
                               past_len + jnp.tile(jnp.arange(srow) - (srow - dec_seq), sb_rt)])
    cos_t, sin_t = _rotary_tables(pos_all)
    zero_rt = jnp.zeros((1, RET_HEADS, RET_HEAD, RET_HEAD), F32)
    _, s_rt_meta = _retention_mixer(z_meta, 0, 1, 1, cos_t, sin_t, tabs_meta, zero_rt,
                                    rows=RET_CHUNK, c=RET_CHUNK, n_real=N_META, per_chunk_state=False,
                                    pos_per_tile=(seq, False))
    rows_rt = 2 * RET_CHUNK
    ob_p, ret_p = _retention_mixer(z_p, 0, n_b, seq // rows_rt, cos_t, sin_t, tabs_full,
                                   jnp.broadcast_to(s_rt_meta, (n_b,) + s_rt_meta.shape[1:]),
                                   rows=rows_rt, c=RET_CHUNK, n_real=RET_CHUNK, per_chunk_state=False,
                                   pos_per_tile=(0, True))
    ob_s, ret_s = _retention_mixer(z_sm, 0, n_s // sb_rt, 1, cos_t, sin_t, tabs_smp,
                                   state_ret.reshape(n_s, RET_HEADS, RET_HEAD, RET_HEAD),
                                   rows=sb_rt * srow, c=srow, n_real=dec_seq, per_chunk_state=True,
                                   pos_per_tile=(seq + RET_CHUNK, False))

    g_ffn = row(norm_ffn[0])
    g_fin = row(norm_final)
    h_p, hn_p = _out_projection(oa_p, ob_p, w_out_b, x_p, g_ffn, tm=512)
    h_s, hn_s = _out_projection(oa_s, ob_s, w_out_b, x_s, g_ffn, tm=n_d)
    y_p = _ffn(hn_p, w_gate_b, w_up_b, w_down_b, h_p, g_fin, tm=1024, tf=512)
    y_s = _ffn(hn_s, w_gate_b, w_up_b, w_down_b, h_s, g_fin, tm=n_d, tf=512)

    y_prompt = y_p.reshape(n_b, seq, d)
    y_sample = y_s.reshape(n_s, dec_seq, d)
    shift_p = zlast_p[:, 0, :RW_COLS][None]
    shift_s = jnp.transpose(z_st[:RW_COLS, (dec_seq - 1) * n_s:])[None]
    return (y_prompt, y_sample, shift_p, rwkv_p[None], ret_p[None], shift_s, rwkv_s[None], ret_s[None])
```

```python
import functools
import math

import jax
import jax.numpy as jnp
import numpy as np
from jax import lax
from jax.experimental import pallas as pl
from jax.experimental.pallas import tpu as pltpu

F32 = jnp.float32
BF16 = jnp.bfloat16

D_MODEL = 2048
N_META = 16
RW_WIDTH = 1024
RW_HEAD = 64
RW_HEADS = 16
DECAY_LORA = 64
AAA_LORA = 64
GATE_LORA = 160
RW_COLS = 3 * RW_WIDTH + DECAY_LORA + AAA_LORA + GATE_LORA
RET_WIDTH = 1024
RET_HEADS = 4
RET_HEAD = 256
RET_CHUNK = 128
D_FF = 5632
RMS_EPS = 1e-6
RW_GN_EPS = 64e-5
RET_GN_EPS = 1e-6
ROPE_BASE = 10000.0
LANES = 128

LORA_PAD = 512
Z_COLS = 4 * RET_WIDTH + 3 * RW_WIDTH + LORA_PAD
RW_OFF = 4 * RET_WIDTH
SHIFT_PAD = 3 * RW_WIDTH + LORA_PAD

VMEM_LIMIT = 60 * 1024 * 1024


def _cparams(sem):
    return pltpu.CompilerParams(dimension_semantics=sem, vmem_limit_bytes=VMEM_LIMIT)


def _mm(a, b):
    return jnp.dot(a.astype(BF16), b.astype(BF16), preferred_element_type=F32)


def _mm_nt(a, b):
    return lax.dot_general(a.astype(BF16), b.astype(BF16), (((1,), (1,)), ((), ())),
                           preferred_element_type=F32)


def _mm_tn(a, b):
    return lax.dot_general(a.astype(BF16), b.astype(BF16), (((0,), (0,)), ((), ())),
                           preferred_element_type=F32)


def _bmm(a, b):
    return jnp.einsum("bmk,bkn->bmn", a.astype(BF16), b.astype(BF16), preferred_element_type=F32)


def _bmm_nt(a, b):
    return jnp.einsum("bmk,bnk->bmn", a.astype(BF16), b.astype(BF16), preferred_element_type=F32)


def _bmm_tn(a, b):
    return jnp.einsum("bkm,bkn->bmn", a.astype(BF16), b.astype(BF16), preferred_element_type=F32)


def _mm_exact_lhs(m_bf16, x):
    hi = x.astype(BF16)
    r1 = x - hi.astype(F32)
    mid = r1.astype(BF16)
    lo = (r1 - mid.astype(F32)).astype(BF16)
    d = functools.partial(jnp.dot, preferred_element_type=F32)
    return d(m_bf16, hi) + d(m_bf16, mid) + d(m_bf16, lo)


def _rms_norm_bf16(x, g):
    ms = jnp.mean(x * x, axis=-1, keepdims=True)
    return ((x * lax.rsqrt(ms + RMS_EPS)) * g).astype(BF16)


def _w_in_layout_kernel(w_ref, o_ref):
    o_ref[...] = w_ref[...].T.astype(BF16)


def _w_in_layout(w_in_t, tn):
    n, d = w_in_t.shape
    n_ret = 4 * RET_WIDTH // tn

    def src_row(c):
        align = math.gcd(RW_COLS, tn)
        return (pl.multiple_of(jnp.where(c < n_ret, RW_COLS + tn * c, tn * (c - n_ret)), align), 0)

    return pl.pallas_call(
        _w_in_layout_kernel,
        grid=(Z_COLS // tn,),
        in_specs=[pl.BlockSpec((pl.Element(tn), pl.Element(d)), src_row)],
        out_specs=pl.BlockSpec((d, tn), lambda c: (0, c)),
        out_shape=jax.ShapeDtypeStruct((d, Z_COLS), BF16),
        compiler_params=_cparams(("parallel",)),
        name="w_in_layout",
    )(w_in_t)


def _inproj_kernel(x_ref, g_ref, w_ref, o_ref, xn_ref):
    @pl.when(pl.program_id(1) == 0)
    def _():
        xn_ref[...] = _rms_norm_bf16(x_ref[...], g_ref[...])

    o_ref[...] = jnp.dot(xn_ref[...], w_ref[...], preferred_element_type=F32)


def _in_projection(x2d, norm_g, w_in_p, tm, tn):
    m = x2d.shape[0]
    return pl.pallas_call(
        _inproj_kernel,
        grid=(m // tm, Z_COLS // tn),
        in_specs=[
            pl.BlockSpec((tm, D_MODEL), lambda i, j: (i, 0)),
            pl.BlockSpec((1, D_MODEL), lambda i, j: (0, 0)),
            pl.BlockSpec((D_MODEL, tn), lambda i, j: (0, j)),
        ],
        out_specs=pl.BlockSpec((tm, tn), lambda i, j: (i, j)),
        out_shape=jax.ShapeDtypeStruct((m, Z_COLS), F32),
        scratch_shapes=[pltpu.VMEM((tm, D_MODEL), BF16)],
        compiler_params=_cparams(("parallel", "arbitrary")),
        name="in_projection",
    )(x2d, norm_g, w_in_p)


def _inproj_t_kernel(w_ref, x_ref, g_ref, o_ref, xn_ref):
    @pl.when(pl.program_id(0) == 0)
    def _():
        xn_ref[...] = _rms_norm_bf16(x_ref[...], g_ref[...])

    o_ref[...] = _mm_nt(w_ref[...], xn_ref[...])


def _in_projection_t(w_in_t, x2d, norm_g, n_rows, tm):
    m, d = x2d.shape
    return pl.pallas_call(
        _inproj_t_kernel,
        grid=(n_rows // tm,),
        in_specs=[
            pl.BlockSpec((tm, d), lambda i: (i, 0)),
            pl.BlockSpec((m, d), lambda i: (0, 0)),
            pl.BlockSpec((1, d), lambda i: (0, 0)),
        ],
        out_specs=pl.BlockSpec((tm, m), lambda i: (i, 0)),
        out_shape=jax.ShapeDtypeStruct((n_rows, m), F32),
        scratch_shapes=[pltpu.VMEM((m, d), BF16)],
        compiler_params=_cparams(("arbitrary",)),
        name="in_projection_t",
    )(w_in_t, x2d, norm_g)


def _rwkv_kernel(zr_ref, zk_ref, zv_ref, zl_ref, zprev0_ref, s0_ref,
                 mu_ref, w0_ref, w2_ref, a0_ref, a2_ref, g2_ref, kk_ref, ka_ref, rk_ref,
                 lnw_ref, lnb_ref,
                 o_ref, sout_ref, zlast_ref,
                 carry_ref, sbd_s, r_s, km_s, v_s, kk_s, a_s, lw_s, cum_s, o_s,
                 *, tb, c, n_factors):
    i = pl.program_id(1)
    n_chunks = tb // c
    n_pairs = RW_HEADS // 2
    pw = 2 * RW_HEAD

    @pl.when(i == 0)
    def _():
        carry_ref[...] = zprev0_ref[0]
        zero = jnp.zeros((RW_HEAD, RW_HEAD), F32)
        for p in range(n_pairs):
            top = jnp.concatenate([s0_ref[0, 2 * p], zero], axis=1)
            bot = jnp.concatenate([zero, s0_ref[0, 2 * p + 1]], axis=1)
            sbd_s[p] = jnp.concatenate([top, bot], axis=0)

    row = lax.broadcasted_iota(jnp.int32, (tb, 1), 0)

    def shifted(z_ref, lo, hi):
        z = z_ref[...]
        prev = pltpu.roll(z, 1, axis=0)
        prev = jnp.concatenate([jnp.where(row[:8] == 0, carry_ref[:, lo:hi], prev[:8]), prev[8:]], axis=0)
        zs = z + mu_ref[:, lo:hi] * (prev - z)
        carry_ref[:, lo:hi] = z[tb - 1:tb, :]
        return zs

    w = RW_WIDTH
    r = shifted(zr_ref, 0, w)
    k = shifted(zk_ref, w, 2 * w)
    v = shifted(zv_ref, 2 * w, 3 * w)
    zl = shifted(zl_ref, 3 * w, 3 * w + LORA_PAD)

    lo2 = zl[:, 0:128]
    wlog = -jax.nn.softplus(-(w0_ref[...] + _mm(jnp.tanh(lo2), w2_ref[...]))) - 0.5
    lw = -jnp.exp(wlog)
    a = jax.nn.sigmoid(a0_ref[...] + _mm(lo2, a2_ref[...]))
    g = _mm(jax.nn.sigmoid(zl[:, 128:384]), g2_ref[...])
    kk = k * kk_ref[...]
    km = k * (1.0 + (a - 1.0) * ka_ref[...])

    ri = lax.broadcasted_iota(jnp.int32, (tb, tb), 0)
    ci = lax.broadcasted_iota(jnp.int32, (tb, tb), 1)
    tri = jnp.where((ri // c == ci // c) & (ri >= ci), 1.0, 0.0).astype(BF16)
    cum = _mm_exact_lhs(tri, lw)

    for dst, val in ((r_s, r), (km_s, km), (v_s, v), (a_s, a), (lw_s, lw), (cum_s, cum), (kk_s, kk)):
        dst[...] = val

    rr = lax.broadcasted_iota(jnp.int32, (c, c), 0)
    cc = lax.broadcasted_iota(jnp.int32, (c, c), 1)
    rr2 = lax.broadcasted_iota(jnp.int32, (c, 2 * c), 0)
    cc2 = lax.broadcasted_iota(jnp.int32, (c, 2 * c), 1)
    strict = (rr > cc)[None]
    eye = jnp.where(rr == cc, 1.0, 0.0).astype(F32)[None]
    strict_k = ((cc2 >= c) & (rr2 > cc2 - c))[None]
    incl_bk = (rr2 >= cc2 % c)[None]
    lo = (lax.broadcasted_iota(jnp.int32, (1, 1, pw), 2) < RW_HEAD)
    bd_r = lax.broadcasted_iota(jnp.int32, (pw, pw), 0) // RW_HEAD
    bd_c = lax.broadcasted_iota(jnp.int32, (pw, pw), 1) // RW_HEAD
    block_diag = (bd_r == bd_c)[None]
    rk = rk_ref[...]
    lnw = lnw_ref[...]
    lnb = lnb_ref[...]

    def head_sum(x):
        s_lo = jnp.sum(jnp.where(lo, x, 0.0), axis=-1, keepdims=True)
        s_hi = jnp.sum(jnp.where(lo, 0.0, x), axis=-1, keepdims=True)
        return jnp.where(lo, s_lo, s_hi)

    def pick(x16):
        x4 = x16.reshape(n_pairs, 2, c, pw)
        return jnp.where(lo, x4[:, 0], x4[:, 1])

    def body(ch):
        rows = pl.ds(ch * c, c)

        def pairs(s):
            x = s[rows, :]
            return jnp.stack([x[:, p * pw:(p + 1) * pw] for p in range(n_pairs)], axis=0)

        rh, kmh, vh, ah, lwh, cumh, kkh = [pairs(s) for s in (r_s, km_s, v_s, a_s, lw_s, cum_s, kk_s)]
        s_prev = sbd_s[...]
        kkn = kkh * lax.rsqrt(jnp.maximum(head_sum(kkh * kkh), 1e-24))
        bvec = kkn * ah
        cum_end = cumh[:, c - 1:c, :]
        p_inc = jnp.exp(cumh)
        p_exc = jnp.exp(cumh - lwh)
        p_inv = jnp.exp(-cumh)
        p_end = jnp.exp(cum_end)
        p_rel = jnp.exp(cum_end - cumh)
        a_t = -(kkn * p_exc)
        r_t = rh * p_inc
        b_t = bvec * p_inv
        k_t = kmh * p_inv
        b_h = bvec * p_rel
        k_h = kmh * p_rel

        ar = jnp.concatenate([jnp.where(lo, a_t, 0.0), jnp.where(lo, r_t, 0.0),
                              jnp.where(lo, 0.0, a_t), jnp.where(lo, 0.0, r_t)], axis=1)
        prod = _bmm_nt(ar, jnp.concatenate([b_t, k_t, s_prev], axis=1))
        prod = prod.reshape(RW_HEADS, 2 * c, 2 * c + pw)
        a_rows, r_rows = prod[:, :c], prod[:, c:]
        aak_wide = jnp.where(strict_k, a_rows[:, :, :2 * c], 0.0)
        arbk = jnp.where(incl_bk, r_rows[:, :, :2 * c], 0.0)
        lmat = jnp.where(strict, a_rows[:, :, :c], 0.0)
        tinv = eye + lmat
        if n_factors > 1:
            lp = _bmm(lmat, lmat)
            for _ in range(n_factors - 2):
                st = _bmm(jnp.concatenate([lp, tinv], axis=1), lp)
                lp = st[:, :c]
                tinv = tinv + st[:, c:]
            tinv = tinv + _bmm(tinv, lp)
        v16 = jnp.repeat(vh, 2, axis=0)
        x = a_rows[:, :, 2 * c:] + _bmm(aak_wide, jnp.concatenate([v16, v16], axis=1))
        u16 = _bmm(tinv, x)
        o16 = r_rows[:, :, 2 * c:] + _bmm(arbk, jnp.concatenate([u16, v16], axis=1))
        u = pick(u16)
        o = pick(o16)
        s_upd = _bmm_tn(jnp.concatenate([u, vh], axis=1), jnp.concatenate([b_h, k_h], axis=1))
        sbd_s[...] = s_prev * p_end + jnp.where(block_diag, s_upd, 0.0)
        inv_n = 1.0 / RW_HEAD
        mean = head_sum(o) * inv_n
        var = head_sum(jnp.square(o - mean)) * inv_n
        on = ((o - mean) * lax.rsqrt(var + RW_GN_EPS)) * lnw + lnb
        out = on + head_sum(rh * kmh * rk) * vh
        for p in range(n_pairs):
            o_s[rows, p * pw:(p + 1) * pw] = out[p]

    for ch in range(n_chunks):
        body(ch)

    o_ref[...] = (o_s[...] * g).astype(BF16)
    zlast_ref[0] = carry_ref[...]

    @pl.when(i == pl.num_programs(1) - 1)
    def _():
        for p in range(n_pairs):
            blk = sbd_s[p]
            sout_ref[0, 2 * p] = blk[:RW_HEAD, :RW_HEAD]
            sout_ref[0, 2 * p + 1] = blk[RW_HEAD:, RW_HEAD:]


N_RWKV_INPUTS = 17
N_RWKV_OUTPUTS = 3


def _rwkv_and_cast_kernel(*refs, n_cast, **kw):
    ins = refs[:N_RWKV_INPUTS]
    cast_in = refs[N_RWKV_INPUTS:N_RWKV_INPUTS + n_cast]
    o0 = N_RWKV_INPUTS + n_cast
    outs = refs[o0:o0 + N_RWKV_OUTPUTS]
    cast_out = refs[o0 + N_RWKV_OUTPUTS:o0 + N_RWKV_OUTPUTS + n_cast]
    scratch = refs[o0 + N_RWKV_OUTPUTS + n_cast:]
    for src, dst in zip(cast_in, cast_out):
        dst[...] = src[...].astype(BF16)
    _rwkv_kernel(*ins, *outs, *scratch, **kw)


def _rwkv_mixer(z, row_block0, n_groups, n_tiles, zprev0, s0, p, *, tb, c, cast=()):
    n_factors = max(1, math.ceil(math.log2(c)))
    kern = functools.partial(_rwkv_and_cast_kernel, n_cast=len(cast), tb=tb, c=c, n_factors=n_factors)
    n_steps = n_groups * n_tiles
    cast_specs = [pl.BlockSpec((a.shape[0] // n_steps, a.shape[1]), lambda gi, i: (gi * n_tiles + i, 0))
                  for a in cast]
    cast_shapes = [jax.ShapeDtypeStruct(a.shape, BF16) for a in cast]
    cb = RW_OFF // RW_WIDTH
    full = lambda arr: pl.BlockSpec(arr.shape, lambda gi, i: (0,) * arr.ndim)
    lora_cb = (RW_OFF + 3 * RW_WIDTH) // LORA_PAD
    per_pair = lambda a: a.reshape(RW_HEADS // 2, 1, 2 * RW_HEAD)
    params = [p["mu"], p["w0"], p["w2"], p["a0"], p["a2"], p["g2"], p["kk"], p["ka"],
              per_pair(p["rk"]), per_pair(p["lnw"]), per_pair(p["lnb"])]
    dense = lambda: pltpu.VMEM((tb, RW_WIDTH), F32)
    zspec = lambda width, col: pl.BlockSpec((tb, width), lambda gi, i: (row_block0 + gi * n_tiles + i, col))
    sspec = pl.BlockSpec((1, RW_HEADS, RW_HEAD, RW_HEAD), lambda gi, i: (gi, 0, 0, 0))
    return pl.pallas_call(
        kern,
        grid=(n_groups, n_tiles),
        in_specs=[zspec(RW_WIDTH, cb), zspec(RW_WIDTH, cb + 1), zspec(RW_WIDTH, cb + 2),
                  zspec(LORA_PAD, lora_cb),
                  pl.BlockSpec((1, 1, SHIFT_PAD), lambda gi, i: (gi, 0, 0)),
                  sspec]
                 + [full(a) for a in params] + cast_specs,
        out_specs=[pl.BlockSpec((tb, RW_WIDTH), lambda gi, i: (gi * n_tiles + i, 0)), sspec,
                   pl.BlockSpec((1, 1, SHIFT_PAD), lambda gi, i: (gi, 0, 0))] + cast_specs,
        out_shape=[jax.ShapeDtypeStruct((n_groups * n_tiles * tb, RW_WIDTH), BF16),
                   jax.ShapeDtypeStruct((n_groups, RW_HEADS, RW_HEAD, RW_HEAD), F32),
                   jax.ShapeDtypeStruct((n_groups, 1, SHIFT_PAD), F32)] + cast_shapes,
        scratch_shapes=[pltpu.VMEM((1, SHIFT_PAD), F32),
                        pltpu.VMEM((RW_HEADS // 2, 2 * RW_HEAD, 2 * RW_HEAD), F32)]
                       + [dense() for _ in range(8)],
        compiler_params=_cparams(("parallel", "arbitrary")),
        name="rwkv7_mixer",
    )(z, z, z, z, zprev0, s0, *params, *cast)


def _rwkv_sample_kernel(zr_ref, zk_ref, zv_ref, zl_ref, shr_ref, shk_ref, shv_ref, shl_ref,
                        mur_ref, muk_ref, muv_ref, mul_ref, s0_ref,
                        w0_ref, w2t_ref, a0_ref, a2t_ref, g2t_ref, kk_ref, ka_ref, rk_ref,
                        lnw_ref, lnb_ref,
                        o_ref, sout_ref,
                        dec_s, a_s, b_s, k_s, r_s, v_s, o_s,
                        *, n_tok, hpb):
    ns = LANES
    ch = hpb * RW_HEAD

    def lanes(x):
        return jnp.concatenate([x] * n_tok, axis=1)

    def shifted(z_ref, sh_ref, mu_ref):
        z = z_ref[...]
        prev = jnp.concatenate([sh_ref[...], z[:, :(n_tok - 1) * ns]], axis=1)
        return z + lanes(mu_ref[...]) * (prev - z)

    r = shifted(zr_ref, shr_ref, mur_ref)
    k = shifted(zk_ref, shk_ref, muk_ref)
    v = shifted(zv_ref, shv_ref, muv_ref)
    zl = shifted(zl_ref, shl_ref, mul_ref)
    wd = zl[0:DECAY_LORA]
    ad = zl[DECAY_LORA:DECAY_LORA + AAA_LORA]
    gd = zl[DECAY_LORA + AAA_LORA:DECAY_LORA + AAA_LORA + GATE_LORA]

    wlog = -jax.nn.softplus(-(lanes(w0_ref[...]) + _mm(w2t_ref[...], jnp.tanh(wd)))) - 0.5
    lw = -jnp.exp(wlog)
    a = jax.nn.sigmoid(lanes(a0_ref[...]) + _mm(a2t_ref[...], ad))
    g = _mm(g2t_ref[...], jax.nn.sigmoid(gd))
    kk = k * lanes(kk_ref[...])
    km = k * (1.0 + (a - 1.0) * lanes(ka_ref[...]))

    def head_sum(x):
        x3 = x.reshape(hpb, RW_HEAD, n_tok * ns)
        s = jnp.sum(x3, axis=1, keepdims=True)
        return jnp.broadcast_to(s, x3.shape).reshape(ch, n_tok * ns)

    kkn = kk * lax.rsqrt(jnp.maximum(head_sum(kk * kk), 1e-24))
    dec_s[...] = jnp.exp(lw)
    a_s[...] = -kkn
    b_s[...] = kkn * a
    k_s[...] = km
    r_s[...] = r
    v_s[...] = v

    for hh in range(hpb):
        hrows = slice(hh * RW_HEAD, (hh + 1) * RW_HEAD)

        def body(i8, carry, hh=hh, hrows=hrows):
            base = pl.multiple_of(i8 * 8, 8)
            rows8 = pl.ds(pl.multiple_of(hh * RW_HEAD + base, 8), 8)
            o_rows = [[] for _ in range(n_tok)]
            for j in range(8):
                s = s0_ref[hh, base + j]
                for t in range(n_tok):
                    tl = slice(t * ns, (t + 1) * ns)
                    sa = jnp.sum(s * a_s[hrows, tl], axis=0, keepdims=True)
                    v_row = v_s[rows8, tl][j:j + 1, :]
                    s = s * dec_s[hrows, tl] + sa * b_s[hrows, tl] + v_row * k_s[hrows, tl]
                    o_rows[t].append(jnp.sum(s * r_s[hrows, tl], axis=0, keepdims=True))
                sout_ref[hh, base + j] = s
            for t in range(n_tok):
                o_s[rows8, t * ns:(t + 1) * ns] = jnp.concatenate(o_rows[t], axis=0)
            return carry

        lax.fori_loop(0, RW_HEAD // 8, body, 0)

    o = o_s[...]
    inv_n = 1.0 / RW_HEAD
    mean = head_sum(o) * inv_n
    var = head_sum(jnp.square(o - mean)) * inv_n
    on = ((o - mean) * lax.rsqrt(var + RW_GN_EPS)) * lanes(lnw_ref[...]) + lanes(lnb_ref[...])
    bonus = head_sum(r * km * lanes(rk_ref[...])) * v
    out = (on + bonus) * g
    for t in range(n_tok):
        o_ref[t] = out[:, t * ns:(t + 1) * ns].T.astype(BF16)


def _rwkv_sample_mixer(z_t, shift_t, mu_t, state_t, p_t, *, n_tok, hpb):
    ch = hpb * RW_HEAD
    n_steps = RW_HEADS // hpb
    nl = n_tok * LANES
    seg = RW_WIDTH // ch
    zspec = lambda s: pl.BlockSpec((ch, nl), lambda h: (s * seg + h, 0))
    cspec = lambda s: pl.BlockSpec((ch, LANES), lambda h: (s * seg + h, 0))
    lora_blk = 3 * RW_WIDTH // LORA_PAD
    zl_spec = pl.BlockSpec((LORA_PAD, nl), lambda h: (lora_blk, 0))
    cl_spec = pl.BlockSpec((LORA_PAD, LANES), lambda h: (lora_blk, 0))
    hspec = pl.BlockSpec((ch, LANES), lambda h: (h, 0))
    wspec = lambda k: pl.BlockSpec((ch, k), lambda h: (h, 0))
    sspec = pl.BlockSpec((hpb, RW_HEAD, RW_HEAD, LANES), lambda h: (h, 0, 0, 0))
    buf = lambda: pltpu.VMEM((ch, nl), F32)
    return pl.pallas_call(
        functools.partial(_rwkv_sample_kernel, n_tok=n_tok, hpb=hpb),
        grid=(n_steps,),
        in_specs=[zspec(0), zspec(1), zspec(2), zl_spec,
                  cspec(0), cspec(1), cspec(2), cl_spec,
                  cspec(0), cspec(1), cspec(2), cl_spec,
                  sspec,
                  hspec, wspec(DECAY_LORA), hspec, wspec(AAA_LORA), wspec(GATE_LORA),
                  hspec, hspec, hspec, hspec, hspec],
        out_specs=[pl.BlockSpec((n_tok, LANES, ch), lambda h: (0, 0, h)), sspec],
        out_shape=[jax.ShapeDtypeStruct((n_tok, LANES, RW_WIDTH), BF16),
                   jax.ShapeDtypeStruct(state_t.shape, F32)],
        scratch_shapes=[buf() for _ in range(7)],
        compiler_params=_cparams(("parallel",)),
        name="rwkv7_sample",
    )(z_t, z_t, z_t, z_t, shift_t, shift_t, shift_t, shift_t, mu_t, mu_t, mu_t, mu_t, state_t,
      p_t["w0"], p_t["w2t"], p_t["a0"], p_t["a2t"], p_t["g2t"], p_t["kk"], p_t["ka"], p_t["rk"],
      p_t["lnw"], p_t["lnb"])


def _retention_kernel(zq_ref, zk_ref, zv_ref, zg_ref, cos_ref, sin_ref, dmask_ref, iscale_ref,
                      kscale_ref, sdec_ref, s0_ref, o_ref, sout_ref, st_s,
                      *, rows, c, n_real, per_chunk_state):
    i = pl.program_id(1)
    n_chunks = rows // c

    if per_chunk_state:
        pr = lax.broadcasted_iota(jnp.int32, (rows, n_chunks * n_real), 0)
        pc = lax.broadcasted_iota(jnp.int32, (rows, n_chunks * n_real), 1)
        place = jnp.where((pr // c == pc // n_real) & (pr % c - (c - n_real) == pc % n_real),
                          1.0, 0.0).astype(BF16)
        load = lambda ref: _mm_exact_lhs(place, ref[...])
    else:
        load = lambda ref: ref[...]

        @pl.when(i == 0)
        def _():
            st_s[...] = s0_ref[0]

    lane = lax.broadcasted_iota(jnp.int32, (rows, RET_WIDTH), 1)
    even = (lane % 2) == 0
    cos = jnp.concatenate([cos_ref[...]] * RET_HEADS, axis=-1)
    sin = jnp.concatenate([sin_ref[...]] * RET_HEADS, axis=-1)

    def rot(x):
        partner = jnp.where(even, pltpu.roll(x, RET_WIDTH - 1, axis=1), pltpu.roll(x, 1, axis=1))
        return x * cos + partner * sin

    q = rot(load(zq_ref))
    k = rot(load(zk_ref)) * (RET_HEAD ** -0.5)
    v = load(zv_ref)
    g = load(zg_ref)

    out_rows = []
    for ch in range(n_chunks):
        rs = slice(ch * c, (ch + 1) * c)
        out_heads = []
        for h in range(RET_HEADS):
            hs = slice(h * RET_HEAD, (h + 1) * RET_HEAD)
            qh, kh, vh = q[rs, hs], k[rs, hs], v[rs, hs]
            s_prev = s0_ref[ch, h] if per_chunk_state else st_s[h]
            scores = _mm_nt(qh, kh) * dmask_ref[h]
            o = _mm(scores, vh) + _mm(qh, s_prev) * iscale_ref[h]
            s_new = s_prev * sdec_ref[h] + _mm_tn(kh * kscale_ref[h], vh)
            if per_chunk_state:
                sout_ref[ch, h] = s_new
            else:
                st_s[h] = s_new
            o = o * lax.rsqrt(jnp.mean(o * o, axis=-1, keepdims=True) + RET_GN_EPS)
            gh = g[rs, hs]
            out_heads.append(o * (gh * jax.nn.sigmoid(gh)))
        out_rows.append(jnp.concatenate(out_heads, axis=-1))
    o_out = jnp.concatenate(out_rows, axis=0).astype(BF16)
    if per_chunk_state:
        o_out = lax.dot_general(place, o_out, (((0,), (0,)), ((), ())),
                                preferred_element_type=F32).astype(BF16)
    o_ref[...] = o_out

    if not per_chunk_state:
        @pl.when(i == pl.num_programs(1) - 1)
        def _():
            sout_ref[0] = st_s[...]


def _retention_mixer(z, row_block0, n_groups, n_tiles, cos, sin, tabs, s0, *, rows, c, n_real,
                     per_chunk_state, pos_per_tile):
    n_states = s0.shape[0]
    sb = n_states // n_groups
    kern = functools.partial(_retention_kernel, rows=rows, c=c, n_real=n_real,
                             per_chunk_state=per_chunk_state)
    rows_io = rows // c * n_real if per_chunk_state else rows
    zspec = lambda col: pl.BlockSpec((rows_io, RET_WIDTH), lambda gi, i: (row_block0 + gi * n_tiles + i, col))
    full = lambda arr: pl.BlockSpec(arr.shape, lambda gi, i: (0,) * arr.ndim)
    tbl_row0, tbl_advance = pos_per_tile
    assert tbl_row0 % rows == 0
    tspec = pl.BlockSpec((rows, RET_HEAD), lambda gi, i: (tbl_row0 // rows + (i if tbl_advance else 0), 0))
    sspec = pl.BlockSpec((sb, RET_HEADS, RET_HEAD, RET_HEAD), lambda gi, i: (gi, 0, 0, 0))
    dmask, iscale, kscale, sdec = tabs
    return pl.pallas_call(
        kern,
        grid=(n_groups, n_tiles),
        in_specs=[zspec(0), zspec(1), zspec(2), zspec(3), tspec, tspec,
                  full(dmask), full(iscale), full(kscale), full(sdec), sspec],
        out_specs=[pl.BlockSpec((rows_io, RET_WIDTH), lambda gi, i: (gi * n_tiles + i, 0)), sspec],
        out_shape=[jax.ShapeDtypeStruct((n_groups * n_tiles * rows_io, RET_WIDTH), BF16),
                   jax.ShapeDtypeStruct(s0.shape, F32)],
        scratch_shapes=[pltpu.VMEM((RET_HEADS, RET_HEAD, RET_HEAD), F32)],
        compiler_params=_cparams(("parallel", "arbitrary")),
        name="retention_mixer",
    )(z, z, z, z, cos, sin, dmask, iscale, kscale, sdec, s0)


def _retention_tables(c, n_real):
    log_gamma = np.log(1.0 - 2.0 ** (-5.0 - np.arange(RET_HEADS, dtype=np.float64)))
    r = np.arange(c, dtype=np.float64)
    idx = r - float(c - n_real)
    diff = r[:, None] - r[None, :]
    dmask = np.where(diff[None] >= 0, np.exp(log_gamma[:, None, None] * np.maximum(diff, 0.0)[None]), 0.0)
    iscale = np.exp(log_gamma[:, None] * (idx + 1.0)[None, :])[:, :, None]
    kscale = np.exp(log_gamma[:, None] * (n_real - 1.0 - idx)[None, :])[:, :, None]
    sdec = np.broadcast_to(np.exp(log_gamma * n_real)[:, None, None], (RET_HEADS, 1, RET_HEAD))
    return tuple(jnp.asarray(t, dtype=F32) for t in (dmask, iscale, kscale, sdec))


def _rotary_tables(pos):
    inv_freq = 1.0 / (ROPE_BASE ** np.linspace(0.0, 1.0, RET_HEAD // 2))
    ang = np.asarray(pos, dtype=np.float64)[:, None] * inv_freq[None, :]
    cos = np.cos(ang)
    sin = np.sin(ang)
    cos2 = np.repeat(cos, 2, axis=-1)
    sin2 = np.stack([-sin, sin], axis=-1).reshape(len(pos), RET_HEAD)
    return jnp.asarray(cos2, dtype=F32), jnp.asarray(sin2, dtype=F32)


def _outproj_kernel(oa_ref, ob_ref, w_ref, x_ref, g_ref, h_ref, hn_ref):
    acc = jnp.dot(oa_ref[...], w_ref[0:RW_WIDTH, :], preferred_element_type=F32)
    acc = acc + jnp.dot(ob_ref[...], w_ref[RW_WIDTH:, :], preferred_element_type=F32)
    h = x_ref[...] + acc
    h_ref[...] = h
    hn_ref[...] = _rms_norm_bf16(h, g_ref[...])


def _out_projection(o_a, o_b, w_out, x2d, norm_g, tm):
    m = o_a.shape[0]
    return pl.pallas_call(
        _outproj_kernel,
        grid=(m // tm,),
        in_specs=[
            pl.BlockSpec((tm, RW_WIDTH), lambda i: (i, 0)),
            pl.BlockSpec((tm, RET_WIDTH), lambda i: (i, 0)),
            pl.BlockSpec((D_MODEL, D_MODEL), lambda i: (0, 0)),
            pl.BlockSpec((tm, D_MODEL), lambda i: (i, 0)),
            pl.BlockSpec((1, D_MODEL), lambda i: (0, 0)),
        ],
        out_specs=[pl.BlockSpec((tm, D_MODEL), lambda i: (i, 0)),
                   pl.BlockSpec((tm, D_MODEL), lambda i: (i, 0))],
        out_shape=[jax.ShapeDtypeStruct((m, D_MODEL), F32), jax.ShapeDtypeStruct((m, D_MODEL), BF16)],
        compiler_params=_cparams(("parallel",)),
        name="out_projection",
    )(o_a, o_b, w_out, x2d, norm_g)


def _ffn_kernel(hn_ref, wg_ref, wu_ref, wd_ref, h_ref, g_ref, y_ref, acc_ref, *, h_slices):
    f = pl.program_id(1)

    @pl.when(f == 0)
    def _():
        acc_ref[...] = jnp.zeros_like(acc_ref)

    @pl.when(f < h_slices)
    def _():
        rows = h_ref.shape[0]
        sl = pl.ds(pl.multiple_of(f * rows, rows), rows)
        acc_ref[sl, :] += h_ref[...]

    hn = hn_ref[...]
    gate = jnp.dot(hn, wg_ref[...], preferred_element_type=F32)
    up = jnp.dot(hn, wu_ref[...], preferred_element_type=F32)
    act = (gate * jax.nn.sigmoid(gate)) * up
    acc_ref[...] += jnp.dot(act.astype(BF16), wd_ref[...], preferred_element_type=F32)

    @pl.when(f == pl.num_programs(1) - 1)
    def _():
        h = acc_ref[...]
        ms = jnp.mean(h * h, axis=-1, keepdims=True)
        y_ref[...] = (h * lax.rsqrt(ms + RMS_EPS)) * g_ref[...]


def _ffn(hn, w_gate, w_up, w_down, h, norm_g, tm, tf, h_slices=8):
    m = hn.shape[0]
    assert D_FF // tf >= h_slices
    return pl.pallas_call(
        functools.partial(_ffn_kernel, h_slices=h_slices),
        grid=(m // tm, D_FF // tf),
        in_specs=[
            pl.BlockSpec((tm, D_MODEL), lambda i, f: (i, 0)),
            pl.BlockSpec((D_MODEL, tf), lambda i, f: (0, f)),
            pl.BlockSpec((D_MODEL, tf), lambda i, f: (0, f)),
            pl.BlockSpec((tf, D_MODEL), lambda i, f: (f, 0)),
            pl.BlockSpec((tm // h_slices, D_MODEL),
                         lambda i, f: (i * h_slices + jnp.minimum(f, h_slices - 1), 0)),
            pl.BlockSpec((1, D_MODEL), lambda i, f: (0, 0)),
        ],
        out_specs=pl.BlockSpec((tm, D_MODEL), lambda i, f: (i, 0)),
        out_shape=jax.ShapeDtypeStruct((m, D_MODEL), F32),
        scratch_shapes=[pltpu.VMEM((tm, D_MODEL), F32)],
        compiler_params=_cparams(("parallel", "arbitrary")),
        name="swiglu_ffn",
    )(hn, w_gate, w_up, w_down, h, norm_g)


def _pad_cols(a, n):
    return jnp.pad(a, ((0, 0), (0, n - a.shape[1])))


def _pad_rows(a, n):
    return jnp.pad(a, ((0, n - a.shape[0]), (0, 0)))


def kernel(x_prompt, x_sample, state_shift, state_rwkv, state_ret, meta_tokens, norm_mix, w_in,
           rwkv_mu, rwkv_w0, rwkv_w2, rwkv_a0, rwkv_a2, rwkv_g2, rwkv_kk, rwkv_ka, rwkv_rk,
           rwkv_ln_w, rwkv_ln_b, w_out, norm_ffn, w_gate, w_up, w_down, norm_final):
    n_b, seq, d = x_prompt.shape
    n_s, dec_seq, _ = x_sample.shape
    n_p = n_b * seq
    n_d = n_s * dec_seq
    depth = w_in.shape[0]
    assert depth == 1 and d == D_MODEL and n_s == LANES

    w_in_t = jnp.transpose(w_in[0])
    w_in_p = _w_in_layout(w_in_t, tn=512)
    row = lambda a: a.reshape(1, -1).astype(F32)
    w2p = jnp.concatenate([rwkv_w2[0], jnp.zeros((128 - DECAY_LORA, RW_WIDTH), F32)], axis=0)
    a2p = jnp.concatenate([jnp.zeros((DECAY_LORA, RW_WIDTH), F32), rwkv_a2[0]], axis=0)
    g2p = jnp.concatenate([rwkv_g2[0], jnp.zeros((256 - GATE_LORA, RW_WIDTH), F32)], axis=0)
    rw_params = dict(mu=_pad_cols(row(rwkv_mu[0]), SHIFT_PAD), w0=row(rwkv_w0[0]), w2=w2p,
                     a0=row(rwkv_a0[0]), a2=a2p, g2=g2p, kk=row(rwkv_kk[0]), ka=row(rwkv_ka[0]),
                     rk=row(rwkv_rk[0]), lnw=row(rwkv_ln_w[0]), lnb=row(rwkv_ln_b[0]))
    col = lambda a: jnp.broadcast_to(a.reshape(-1, 1).astype(F32), (a.size, LANES))
    rw_params_t = dict(w0=col(rwkv_w0[0]), w2t=rwkv_w2[0].T, a0=col(rwkv_a0[0]), a2t=rwkv_a2[0].T,
                       g2t=rwkv_g2[0].T, kk=col(rwkv_kk[0]), ka=col(rwkv_ka[0]), rk=col(rwkv_rk[0]),
                       lnw=col(rwkv_ln_w[0]), lnb=col(rwkv_ln_b[0]))

    x_p = x_prompt.reshape(n_p, d)
    x_s = x_sample.reshape(n_d, d)
    x_sm = jnp.concatenate([x_s, meta_tokens.astype(F32)], axis=0)
    x_ts = jnp.transpose(x_sample, (1, 0, 2)).reshape(n_d, d)
    g_mix = row(norm_mix[0])
    z_p = _in_projection(x_p, g_mix, w_in_p, tm=1024, tn=1536)
    z_sm = _in_projection(x_sm, g_mix, w_in_p, tm=x_sm.shape[0], tn=768)
    z_st = _in_projection_t(w_in_t, x_ts, g_mix, SHIFT_PAD, tm=512)

    z_meta = jnp.pad(z_sm[n_d:], ((RET_CHUNK - N_META, 0), (0, 0)))
    srow = 2 * dec_seq

    zero_prev = jnp.zeros((1, 1, SHIFT_PAD), F32)
    zero_rw = jnp.zeros((1, RW_HEADS, RW_HEAD, RW_HEAD), F32)
    c_rw = 64
    _, s_rw_meta, zlast_meta = _rwkv_mixer(z_meta, 1, 1, 1, zero_prev, zero_rw, rw_params, tb=c_rw, c=c_rw)
    oa_p, rwkv_p, zlast_p, w_out_b, w_gate_b, w_up_b, w_down_b = _rwkv_mixer(
        z_p, 0, n_b, seq // 256, jnp.broadcast_to(zlast_meta, (n_b, 1, SHIFT_PAD)),
        jnp.broadcast_to(s_rw_meta, (n_b,) + s_rw_meta.shape[1:]), rw_params, tb=256, c=c_rw,
        cast=(w_out[0], w_gate[0], w_up[0], w_down[0]))
    shift_t = _pad_rows(jnp.transpose(state_shift[0]), SHIFT_PAD)
    mu_t = _pad_rows(col(rwkv_mu[0]), SHIFT_PAD)
    state_t = jnp.transpose(state_rwkv[0], (1, 2, 3, 0))
    oa_st, rwkv_st = _rwkv_sample_mixer(z_st, shift_t, mu_t, state_t, rw_params_t, n_tok=dec_seq, hpb=2)
    oa_s = jnp.transpose(oa_st, (1, 0, 2)).reshape(n_d, RW_WIDTH)
    rwkv_s = jnp.transpose(rwkv_st, (3, 0, 1, 2))

    past_len = 16384
    tabs_full = _retention_tables(RET_CHUNK, RET_CHUNK)
    tabs_meta = _retention_tables(RET_CHUNK, N_META)
    tabs_smp = _retention_tables(srow, dec_seq)
    sb_rt = 8
    pos_all = np.concatenate([N_META + np.arange(seq),
                              np.arange(RET_CHUNK) - (RET_CHUNK - N_META),
                              past_len + np.tile(np.arange(srow) - (srow - dec_seq), sb_rt)])
    cos_t, sin_t = _rotary_tables(pos_all)
    zero_rt = jnp.zeros((1, RET_HEADS, RET_HEAD, RET_HEAD), F32)
    _, s_rt_meta = _retention_mixer(z_meta, 0, 1, 1, cos_t, sin_t, tabs_meta, zero_rt,
                                    rows=RET_CHUNK, c=RET_CHUNK, n_real=N_META, per_chunk_state=False,
                                    pos_per_tile=(seq, False))
    rows_rt = 2 * RET_CHUNK
    ob_p, ret_p = _retention_mixer(z_p, 0, n_b, seq // rows_rt, cos_t, sin_t, tabs_full,
                                   jnp.broadcast_to(s_rt_meta, (n_b,) + s_rt_meta.shape[1:]),
                                   rows=rows_rt, c=RET_CHUNK, n_real=RET_CHUNK, per_chunk_state=False,
                                   pos_per_tile=(0, True))
    ob_s, ret_s = _retention_mixer(z_sm, 0, n_s // sb_rt, 1, cos_t, sin_t, tabs_smp,
                                   state_ret.reshape(n_s, RET_HEADS, RET_HEAD, RET_HEAD),
                                   rows=sb_rt * srow, c=srow, n_real=dec_seq, per_chunk_state=True,
                                   pos_per_tile=(seq + RET_CHUNK, False))

    g_ffn = row(norm_ffn[0])
    g_fin = row(norm_final)
    h_p, hn_p = _out_projection(oa_p, ob_p, w_out_b, x_p, g_ffn, tm=512)
    h_s, hn_s = _out_projection(oa_s, ob_s, w_out_b, x_s, g_ffn, tm=n_d)
    y_p = _ffn(hn_p, w_gate_b, w_up_b, w_down_b, h_p, g_fin, tm=1024, tf=512)
    y_s = _ffn(hn_s, w_gate_b, w_up_b, w_down_b, h_s, g_fin, tm=n_d, tf=512)

    y_prompt = y_p.reshape(n_b, seq, d)
    y_sample = y_s.reshape(n_s, dec_seq, d)
    shift_p = zlast_p[:, 0, :RW_COLS][None]
    shift_s = jnp.transpose(z_st[:RW_COLS, (dec_seq - 1) * n_s:])[None]
    return (y_prompt, y_sample, shift_p, rwkv_p[None], ret_p[None], shift_s, rwkv_s[None], ret_s[None])
```

```python
import functools
import math

import jax
import jax.numpy as jnp
import numpy as np
from jax import lax
from jax.experimental import pallas as pl
from jax.experimental.pallas import tpu as pltpu

F32 = jnp.float32
BF16 = jnp.bfloat16

D_MODEL = 2048
N_META = 16
RW_WIDTH = 1024
RW_HEAD = 64
RW_HEADS = 16
DECAY_LORA = 64
AAA_LORA = 64
GATE_LORA = 160
RW_COLS = 3 * RW_WIDTH + DECAY_LORA + AAA_LORA + GATE_LORA
RET_WIDTH = 1024
RET_HEADS = 4
RET_HEAD = 256
RET_CHUNK = 128
D_FF = 5632
RMS_EPS = 1e-6
RW_GN_EPS = 64e-5
RET_GN_EPS = 1e-6
ROPE_BASE = 10000.0
LANES = 128

LORA_PAD = 512
Z_COLS = 4 * RET_WIDTH + 3 * RW_WIDTH + LORA_PAD
RW_OFF = 4 * RET_WIDTH
SHIFT_PAD = 3 * RW_WIDTH + LORA_PAD

VMEM_LIMIT = 60 * 1024 * 1024


def _cparams(sem):
    return pltpu.CompilerParams(dimension_semantics=sem, vmem_limit_bytes=VMEM_LIMIT)


def _mm(a, b):
    return jnp.dot(a.astype(BF16), b.astype(BF16), preferred_element_type=F32)


def _mm_nt(a, b):
    return lax.dot_general(a.astype(BF16), b.astype(BF16), (((1,), (1,)), ((), ())),
                           preferred_element_type=F32)


def _mm_tn(a, b):
    return lax.dot_general(a.astype(BF16), b.astype(BF16), (((0,), (0,)), ((), ())),
                           preferred_element_type=F32)


def _bmm(a, b):
    return jnp.einsum("bmk,bkn->bmn", a.astype(BF16), b.astype(BF16), preferred_element_type=F32)


def _bmm_nt(a, b):
    return jnp.einsum("bmk,bnk->bmn", a.astype(BF16), b.astype(BF16), preferred_element_type=F32)


def _bmm_tn(a, b):
    return jnp.einsum("bkm,bkn->bmn", a.astype(BF16), b.astype(BF16), preferred_element_type=F32)


def _mm_exact_lhs(m_bf16, x):
    hi = x.astype(BF16)
    r1 = x - hi.astype(F32)
    mid = r1.astype(BF16)
    lo = (r1 - mid.astype(F32)).astype(BF16)
    d = functools.partial(jnp.dot, preferred_element_type=F32)
    return d(m_bf16, hi) + d(m_bf16, mid) + d(m_bf16, lo)


def _mm_exact_lhs_t(m_bf16, x):
    hi = x.astype(BF16)
    r1 = x - hi.astype(F32)
    mid = r1.astype(BF16)
    lo = (r1 - mid.astype(F32)).astype(BF16)
    d = functools.partial(lax.dot_general, dimension_numbers=(((0,), (0,)), ((), ())),
                          preferred_element_type=F32)
    return d(m_bf16, hi) + d(m_bf16, mid) + d(m_bf16, lo)


def _rms_norm_bf16(x, g):
    ms = jnp.mean(x * x, axis=-1, keepdims=True)
    return ((x * lax.rsqrt(ms + RMS_EPS)) * g).astype(BF16)


def _pair_split_index(n):
    j = np.arange(n)
    half = RET_HEAD // 2
    return (j // RET_HEAD) * RET_HEAD + 2 * (j % half) + (j % RET_HEAD) // half


def _w_in_layout_kernel(w_ref, sel_ref, o_ref, *, n_qk):
    c = pl.program_id(0)

    @pl.when(c >= n_qk)
    def _():
        o_ref[...] = w_ref[...].T.astype(BF16)

    @pl.when(c < n_qk)
    def _():
        o_ref[...] = _mm_exact_lhs(sel_ref[...], w_ref[...]).T.astype(BF16)


def _w_in_layout(w_in_t, tn):
    n, d = w_in_t.shape
    n_ret = 4 * RET_WIDTH // tn
    n_z = Z_COLS // tn
    n_qk = 2 * RET_WIDTH // tn
    sel = jnp.asarray(_pair_split_index(tn)[:, None] == np.arange(tn)[None, :], dtype=BF16)

    def src_row(c):
        align = math.gcd(RW_COLS, tn)
        row = jnp.where(c < n_ret, RW_COLS + tn * c,
                        jnp.where(c < n_z, tn * (c - n_ret), RW_COLS + tn * (c - n_z)))
        return (pl.multiple_of(row, align), 0)

    return pl.pallas_call(
        functools.partial(_w_in_layout_kernel, n_qk=n_qk),
        grid=(n_z + n_qk,),
        in_specs=[pl.BlockSpec((pl.Element(tn), pl.Element(d)), src_row),
                  pl.BlockSpec((tn, tn), lambda c: (0, 0))],
        out_specs=pl.BlockSpec((d, tn), lambda c: (0, c)),
        out_shape=jax.ShapeDtypeStruct((d, Z_COLS + 2 * RET_WIDTH), BF16),
        compiler_params=_cparams(("parallel",)),
        name="w_in_layout",
    )(w_in_t, sel)


def _inproj_kernel(x_ref, g_ref, w_ref, o_ref, xn_ref):
    @pl.when(pl.program_id(1) == 0)
    def _():
        xn_ref[...] = _rms_norm_bf16(x_ref[...], g_ref[...])

    o_ref[...] = jnp.dot(xn_ref[...], w_ref[...], preferred_element_type=F32)


def _in_projection(x2d, norm_g, w_in_p, tm, tn, qk_interleaved=False):
    m = x2d.shape[0]
    if qk_interleaved:
        n_qk = 2 * RET_WIDTH // tn
        w_col = lambda i, j: (0, jnp.where(j < n_qk, Z_COLS // tn + j, j))
    else:
        w_col = lambda i, j: (0, j)
    return pl.pallas_call(
        _inproj_kernel,
        grid=(m // tm, Z_COLS // tn),
        in_specs=[
            pl.BlockSpec((tm, D_MODEL), lambda i, j: (i, 0)),
            pl.BlockSpec((1, D_MODEL), lambda i, j: (0, 0)),
            pl.BlockSpec((D_MODEL, tn), w_col),
        ],
        out_specs=pl.BlockSpec((tm, tn), lambda i, j: (i, j)),
        out_shape=jax.ShapeDtypeStruct((m, Z_COLS), F32),
        scratch_shapes=[pltpu.VMEM((tm, D_MODEL), BF16)],
        compiler_params=_cparams(("parallel", "arbitrary")),
        name="in_projection",
    )(x2d, norm_g, w_in_p)


def _inproj_t_kernel(w_ref, x_ref, g_ref, o_ref, xn_ref):
    @pl.when(pl.program_id(0) == 0)
    def _():
        xn_ref[...] = _rms_norm_bf16(x_ref[...], g_ref[...])

    o_ref[...] = _mm_nt(w_ref[...], xn_ref[...])


def _in_projection_t(w_in_t, x2d, norm_g, n_rows, tm):
    m, d = x2d.shape
    return pl.pallas_call(
        _inproj_t_kernel,
        grid=(n_rows // tm,),
        in_specs=[
            pl.BlockSpec((tm, d), lambda i: (i, 0)),
            pl.BlockSpec((m, d), lambda i: (0, 0)),
            pl.BlockSpec((1, d), lambda i: (0, 0)),
        ],
        out_specs=pl.BlockSpec((tm, m), lambda i: (i, 0)),
        out_shape=jax.ShapeDtypeStruct((n_rows, m), F32),
        scratch_shapes=[pltpu.VMEM((m, d), BF16)],
        compiler_params=_cparams(("arbitrary",)),
        name="in_projection_t",
    )(w_in_t, x2d, norm_g)


def _rwkv_kernel(zr_ref, zk_ref, zv_ref, zl_ref, zprev0_ref, s0_ref,
                 mu_ref, w0_ref, w2_ref, a0_ref, a2_ref, g2_ref, kk_ref, ka_ref, rk_ref,
                 lnw_ref, lnb_ref,
                 o_ref, sout_ref, zlast_ref,
                 carry_ref, sbd_s, r_s, km_s, v_s, kk_s, a_s, lw_s, cum_s, o_s,
                 *, tb, c, n_factors):
    i = pl.program_id(1)
    n_chunks = tb // c
    n_pairs = RW_HEADS // 2
    pw = 2 * RW_HEAD

    @pl.when(i == 0)
    def _():
        carry_ref[...] = zprev0_ref[0]
        zero = jnp.zeros((RW_HEAD, RW_HEAD), F32)
        for p in range(n_pairs):
            top = jnp.concatenate([s0_ref[0, 2 * p], zero], axis=1)
            bot = jnp.concatenate([zero, s0_ref[0, 2 * p + 1]], axis=1)
            sbd_s[p] = jnp.concatenate([top, bot], axis=0)

    row = lax.broadcasted_iota(jnp.int32, (tb, 1), 0)

    def shifted(z_ref, lo, hi):
        z = z_ref[...]
        prev = pltpu.roll(z, 1, axis=0)
        prev = jnp.concatenate([jnp.where(row[:8] == 0, carry_ref[:, lo:hi], prev[:8]), prev[8:]], axis=0)
        zs = z + mu_ref[:, lo:hi] * (prev - z)
        carry_ref[:, lo:hi] = z[tb - 1:tb, :]
        return zs

    w = RW_WIDTH
    r = shifted(zr_ref, 0, w)
    k = shifted(zk_ref, w, 2 * w)
    v = shifted(zv_ref, 2 * w, 3 * w)
    zl = shifted(zl_ref, 3 * w, 3 * w + LORA_PAD)

    lo2 = zl[:, 0:128]
    wlog = -jax.nn.softplus(-(w0_ref[...] + _mm(jnp.tanh(lo2), w2_ref[...]))) - 0.5
    lw = -jnp.exp(wlog)
    a = jax.nn.sigmoid(a0_ref[...] + _mm(lo2, a2_ref[...]))
    g = _mm(jax.nn.sigmoid(zl[:, 128:384]), g2_ref[...])
    kk = k * kk_ref[...]
    km = k * (1.0 + (a - 1.0) * ka_ref[...])

    ri = lax.broadcasted_iota(jnp.int32, (tb, tb), 0)
    ci = lax.broadcasted_iota(jnp.int32, (tb, tb), 1)
    tri = jnp.where((ri // c == ci // c) & (ri >= ci), 1.0, 0.0).astype(BF16)
    cum = _mm_exact_lhs(tri, lw)

    for dst, val in ((r_s, r), (km_s, km), (v_s, v), (a_s, a), (lw_s, lw), (cum_s, cum), (kk_s, kk)):
        dst[...] = val

    rr = lax.broadcasted_iota(jnp.int32, (c, c), 0)
    cc = lax.broadcasted_iota(jnp.int32, (c, c), 1)
    rr2 = lax.broadcasted_iota(jnp.int32, (c, 2 * c), 0)
    cc2 = lax.broadcasted_iota(jnp.int32, (c, 2 * c), 1)
    strict = (rr > cc)[None]
    eye = jnp.where(rr == cc, 1.0, 0.0).astype(F32)[None]
    strict_k = ((cc2 >= c) & (rr2 > cc2 - c))[None]
    incl_bk = (rr2 >= cc2 % c)[None]
    lo = (lax.broadcasted_iota(jnp.int32, (1, 1, pw), 2) < RW_HEAD)
    bd_r = lax.broadcasted_iota(jnp.int32, (pw, pw), 0) // RW_HEAD
    bd_c = lax.broadcasted_iota(jnp.int32, (pw, pw), 1) // RW_HEAD
    block_diag = (bd_r == bd_c)[None]
    rk = rk_ref[...]
    lnw = lnw_ref[...]
    lnb = lnb_ref[...]

    def head_sum(x):
        s_lo = jnp.sum(jnp.where(lo, x, 0.0), axis=-1, keepdims=True)
        s_hi = jnp.sum(jnp.where(lo, 0.0, x), axis=-1, keepdims=True)
        return jnp.where(lo, s_lo, s_hi)

    def pick(x16):
        x4 = x16.reshape(n_pairs, 2, c, pw)
        return jnp.where(lo, x4[:, 0], x4[:, 1])

    def body(ch):
        rows = pl.ds(ch * c, c)

        def pairs(s):
            x = s[rows, :]
            return jnp.stack([x[:, p * pw:(p + 1) * pw] for p in range(n_pairs)], axis=0)

        rh, kmh, vh, ah, lwh, cumh, kkh = [pairs(s) for s in (r_s, km_s, v_s, a_s, lw_s, cum_s, kk_s)]
        s_prev = sbd_s[...]
        kkn = kkh * lax.rsqrt(jnp.maximum(head_sum(kkh * kkh), 1e-24))
        bvec = kkn * ah
        cum_end = cumh[:, c - 1:c, :]
        p_inc = jnp.exp(cumh)
        p_exc = jnp.exp(cumh - lwh)
        p_inv = jnp.exp(-cumh)
        p_end = jnp.exp(cum_end)
        p_rel = jnp.exp(cum_end - cumh)
        a_t = -(kkn * p_exc)
        r_t = rh * p_inc
        b_t = bvec * p_inv
        k_t = kmh * p_inv
        b_h = bvec * p_rel
        k_h = kmh * p_rel

        ar = jnp.concatenate([jnp.where(lo, a_t, 0.0), jnp.where(lo, r_t, 0.0),
                              jnp.where(lo, 0.0, a_t), jnp.where(lo, 0.0, r_t)], axis=1)
        prod = _bmm_nt(ar, jnp.concatenate([b_t, k_t, s_prev], axis=1))
        prod = prod.reshape(RW_HEADS, 2 * c, 2 * c + pw)
        a_rows, r_rows = prod[:, :c], prod[:, c:]
        aak_wide = jnp.where(strict_k, a_rows[:, :, :2 * c], 0.0)
        arbk = jnp.where(incl_bk, r_rows[:, :, :2 * c], 0.0)
        lmat = jnp.where(strict, a_rows[:, :, :c], 0.0)
        tinv = eye + lmat
        if n_factors > 1:
            lp = _bmm(lmat, lmat)
            for _ in range(n_factors - 2):
                st = _bmm(jnp.concatenate([lp, tinv], axis=1), lp)
                lp = st[:, :c]
                tinv = tinv + st[:, c:]
            tinv = tinv + _bmm(tinv, lp)
        v16 = jnp.repeat(vh, 2, axis=0)
        x = a_rows[:, :, 2 * c:] + _bmm(aak_wide, jnp.concatenate([v16, v16], axis=1))
        u16 = _bmm(tinv, x)
        o16 = r_rows[:, :, 2 * c:] + _bmm(arbk, jnp.concatenate([u16, v16], axis=1))
        u = pick(u16)
        o = pick(o16)
        s_upd = _bmm_tn(jnp.concatenate([u, vh], axis=1), jnp.concatenate([b_h, k_h], axis=1))
        sbd_s[...] = s_prev * p_end + jnp.where(block_diag, s_upd, 0.0)
        inv_n = 1.0 / RW_HEAD
        mean = head_sum(o) * inv_n
        var = head_sum(jnp.square(o - mean)) * inv_n
        on = ((o - mean) * lax.rsqrt(var + RW_GN_EPS)) * lnw + lnb
        out = on + head_sum(rh * kmh * rk) * vh
        for p in range(n_pairs):
            o_s[rows, p * pw:(p + 1) * pw] = out[p]

    for ch in range(n_chunks):
        body(ch)

    o_ref[...] = (o_s[...] * g).astype(BF16)
    zlast_ref[0] = carry_ref[...]

    @pl.when(i == pl.num_programs(1) - 1)
    def _():
        for p in range(n_pairs):
            blk = sbd_s[p]
            sout_ref[0, 2 * p] = blk[:RW_HEAD, :RW_HEAD]
            sout_ref[0, 2 * p + 1] = blk[RW_HEAD:, RW_HEAD:]


N_RWKV_INPUTS = 17
N_RWKV_OUTPUTS = 3


def _rwkv_and_cast_kernel(*refs, n_cast, **kw):
    ins = refs[:N_RWKV_INPUTS]
    cast_in = refs[N_RWKV_INPUTS:N_RWKV_INPUTS + n_cast]
    o0 = N_RWKV_INPUTS + n_cast
    outs = refs[o0:o0 + N_RWKV_OUTPUTS]
    cast_out = refs[o0 + N_RWKV_OUTPUTS:o0 + N_RWKV_OUTPUTS + n_cast]
    scratch = refs[o0 + N_RWKV_OUTPUTS + n_cast:]
    for src, dst in zip(cast_in, cast_out):
        dst[...] = src[...].astype(BF16)
    _rwkv_kernel(*ins, *outs, *scratch, **kw)


def _rwkv_mixer(z, row_block0, n_groups, n_tiles, zprev0, s0, p, *, tb, c, cast=()):
    n_factors = max(1, math.ceil(math.log2(c)))
    kern = functools.partial(_rwkv_and_cast_kernel, n_cast=len(cast), tb=tb, c=c, n_factors=n_factors)
    n_steps = n_groups * n_tiles
    cast_specs = [pl.BlockSpec((a.shape[0] // n_steps, a.shape[1]), lambda gi, i: (gi * n_tiles + i, 0))
                  for a in cast]
    cast_shapes = [jax.ShapeDtypeStruct(a.shape, BF16) for a in cast]
    cb = RW_OFF // RW_WIDTH
    full = lambda arr: pl.BlockSpec(arr.shape, lambda gi, i: (0,) * arr.ndim)
    lora_cb = (RW_OFF + 3 * RW_WIDTH) // LORA_PAD
    per_pair = lambda a: a.reshape(RW_HEADS // 2, 1, 2 * RW_HEAD)
    params = [p["mu"], p["w0"], p["w2"], p["a0"], p["a2"], p["g2"], p["kk"], p["ka"],
              per_pair(p["rk"]), per_pair(p["lnw"]), per_pair(p["lnb"])]
    dense = lambda: pltpu.VMEM((tb, RW_WIDTH), F32)
    zspec = lambda width, col: pl.BlockSpec((tb, width), lambda gi, i: (row_block0 + gi * n_tiles + i, col))
    sspec = pl.BlockSpec((1, RW_HEADS, RW_HEAD, RW_HEAD), lambda gi, i: (gi, 0, 0, 0))
    shared = lambda a: (lambda gi, i: (0,) * a.ndim) if a.shape[0] == 1 else (lambda gi, i: (gi,) + (0,) * (a.ndim - 1))
    return pl.pallas_call(
        kern,
        grid=(n_groups, n_tiles),
        in_specs=[zspec(RW_WIDTH, cb), zspec(RW_WIDTH, cb + 1), zspec(RW_WIDTH, cb + 2),
                  zspec(LORA_PAD, lora_cb),
                  pl.BlockSpec((1, 1, SHIFT_PAD), shared(zprev0)),
                  pl.BlockSpec((1, RW_HEADS, RW_HEAD, RW_HEAD), shared(s0))]
                 + [full(a) for a in params] + cast_specs,
        out_specs=[pl.BlockSpec((tb, RW_WIDTH), lambda gi, i: (gi * n_tiles + i, 0)), sspec,
                   pl.BlockSpec((1, 1, SHIFT_PAD), lambda gi, i: (gi, 0, 0))] + cast_specs,
        out_shape=[jax.ShapeDtypeStruct((n_groups * n_tiles * tb, RW_WIDTH), BF16),
                   jax.ShapeDtypeStruct((n_groups, RW_HEADS, RW_HEAD, RW_HEAD), F32),
                   jax.ShapeDtypeStruct((n_groups, 1, SHIFT_PAD), F32)] + cast_shapes,
        scratch_shapes=[pltpu.VMEM((1, SHIFT_PAD), F32),
                        pltpu.VMEM((RW_HEADS // 2, 2 * RW_HEAD, 2 * RW_HEAD), F32)]
                       + [dense() for _ in range(8)],
        compiler_params=_cparams(("parallel", "arbitrary")),
        name="rwkv7_mixer",
    )(z, z, z, z, zprev0, s0, *params, *cast)


def _rwkv_sample_kernel(zr_ref, zk_ref, zv_ref, zl_ref, shr_ref, shk_ref, shv_ref, shl_ref,
                        mur_ref, muk_ref, muv_ref, mul_ref, s0_ref,
                        w0_ref, w2t_ref, a0_ref, a2t_ref, g2t_ref, kk_ref, ka_ref, rk_ref,
                        lnw_ref, lnb_ref,
                        o_ref, sout_ref,
                        dec_s, a_s, b_s, k_s, r_s, v_s, o_s,
                        *, n_tok, hpb):
    ns = LANES
    ch = hpb * RW_HEAD

    def lanes(x):
        return jnp.concatenate([x] * n_tok, axis=1)

    def shifted(z_ref, sh_ref, mu_ref):
        z = z_ref[...]
        prev = jnp.concatenate([sh_ref[...], z[:, :(n_tok - 1) * ns]], axis=1)
        return z + lanes(mu_ref[...]) * (prev - z)

    r = shifted(zr_ref, shr_ref, mur_ref)
    k = shifted(zk_ref, shk_ref, muk_ref)
    v = shifted(zv_ref, shv_ref, muv_ref)
    zl = shifted(zl_ref, shl_ref, mul_ref)
    wd = zl[0:DECAY_LORA]
    ad = zl[DECAY_LORA:DECAY_LORA + AAA_LORA]
    gd = zl[DECAY_LORA + AAA_LORA:DECAY_LORA + AAA_LORA + GATE_LORA]

    wlog = -jax.nn.softplus(-(lanes(w0_ref[...]) + _mm(w2t_ref[...], jnp.tanh(wd)))) - 0.5
    lw = -jnp.exp(wlog)
    a = jax.nn.sigmoid(lanes(a0_ref[...]) + _mm(a2t_ref[...], ad))
    g = _mm(g2t_ref[...], jax.nn.sigmoid(gd))
    kk = k * lanes(kk_ref[...])
    km = k * (1.0 + (a - 1.0) * lanes(ka_ref[...]))

    def head_sum(x):
        x3 = x.reshape(hpb, RW_HEAD, n_tok * ns)
        s = jnp.sum(x3, axis=1, keepdims=True)
        return jnp.broadcast_to(s, x3.shape).reshape(ch, n_tok * ns)

    kkn = kk * lax.rsqrt(jnp.maximum(head_sum(kk * kk), 1e-24))
    dec_s[...] = jnp.exp(lw)
    a_s[...] = -kkn
    b_s[...] = kkn * a
    k_s[...] = km
    r_s[...] = r
    v_s[...] = v

    for hh in range(hpb):
        hrows = slice(hh * RW_HEAD, (hh + 1) * RW_HEAD)

        def body(i8, carry, hh=hh, hrows=hrows):
            base = pl.multiple_of(i8 * 8, 8)
            rows8 = pl.ds(pl.multiple_of(hh * RW_HEAD + base, 8), 8)
            o_rows = [[] for _ in range(n_tok)]
            for j in range(8):
                s = s0_ref[hh, base + j]
                for t in range(n_tok):
                    tl = slice(t * ns, (t + 1) * ns)
                    sa = jnp.sum(s * a_s[hrows, tl], axis=0, keepdims=True)
                    v_row = v_s[rows8, tl][j:j + 1, :]
                    s = s * dec_s[hrows, tl] + sa * b_s[hrows, tl] + v_row * k_s[hrows, tl]
                    o_rows[t].append(jnp.sum(s * r_s[hrows, tl], axis=0, keepdims=True))
                sout_ref[hh, base + j] = s
            for t in range(n_tok):
                o_s[rows8, t * ns:(t + 1) * ns] = jnp.concatenate(o_rows[t], axis=0)
            return carry

        lax.fori_loop(0, RW_HEAD // 8, body, 0)

    o = o_s[...]
    inv_n = 1.0 / RW_HEAD
    mean = head_sum(o) * inv_n
    var = head_sum(jnp.square(o - mean)) * inv_n
    on = ((o - mean) * lax.rsqrt(var + RW_GN_EPS)) * lanes(lnw_ref[...]) + lanes(lnb_ref[...])
    bonus = head_sum(r * km * lanes(rk_ref[...])) * v
    out = (on + bonus) * g
    for t in range(n_tok):
        o_ref[t] = out[:, t * ns:(t + 1) * ns].T.astype(BF16)


def _rwkv_sample_mixer(z_t, shift_t, mu_t, state_t, p_t, *, n_tok, hpb):
    ch = hpb * RW_HEAD
    n_steps = RW_HEADS // hpb
    nl = n_tok * LANES
    seg = RW_WIDTH // ch
    zspec = lambda s: pl.BlockSpec((ch, nl), lambda h: (s * seg + h, 0))
    cspec = lambda s: pl.BlockSpec((ch, LANES), lambda h: (s * seg + h, 0))
    lora_blk = 3 * RW_WIDTH // LORA_PAD
    zl_spec = pl.BlockSpec((LORA_PAD, nl), lambda h: (lora_blk, 0))
    cl_spec = pl.BlockSpec((LORA_PAD, LANES), lambda h: (lora_blk, 0))
    hspec = pl.BlockSpec((ch, LANES), lambda h: (h, 0))
    wspec = lambda k: pl.BlockSpec((ch, k), lambda h: (h, 0))
    sspec = pl.BlockSpec((hpb, RW_HEAD, RW_HEAD, LANES), lambda h: (h, 0, 0, 0))
    buf = lambda: pltpu.VMEM((ch, nl), F32)
    return pl.pallas_call(
        functools.partial(_rwkv_sample_kernel, n_tok=n_tok, hpb=hpb),
        grid=(n_steps,),
        in_specs=[zspec(0), zspec(1), zspec(2), zl_spec,
                  cspec(0), cspec(1), cspec(2), cl_spec,
                  cspec(0), cspec(1), cspec(2), cl_spec,
                  sspec,
                  hspec, wspec(DECAY_LORA), hspec, wspec(AAA_LORA), wspec(GATE_LORA),
                  hspec, hspec, hspec, hspec, hspec],
        out_specs=[pl.BlockSpec((n_tok, LANES, ch), lambda h: (0, 0, h)), sspec],
        out_shape=[jax.ShapeDtypeStruct((n_tok, LANES, RW_WIDTH), BF16),
                   jax.ShapeDtypeStruct(state_t.shape, F32)],
        scratch_shapes=[buf() for _ in range(7)],
        compiler_params=_cparams(("parallel",)),
        name="rwkv7_sample",
    )(z_t, z_t, z_t, z_t, shift_t, shift_t, shift_t, shift_t, mu_t, mu_t, mu_t, mu_t, state_t,
      p_t["w0"], p_t["w2t"], p_t["a0"], p_t["a2t"], p_t["g2t"], p_t["kk"], p_t["ka"], p_t["rk"],
      p_t["lnw"], p_t["lnb"])


def _retention_kernel(zq_ref, zk_ref, zv_ref, zg_ref, cos_ref, sin_ref, dmask_ref, iscale_ref,
                      kscale_ref, sdec_ref, sel_ref, s0_ref, o_ref, sout_ref, st_s,
                      *, rows, c, n_real, per_chunk_state, pair_split):
    i = pl.program_id(1)
    n_chunks = rows // c
    assert not (pair_split and per_chunk_state)

    if per_chunk_state:
        pr = lax.broadcasted_iota(jnp.int32, (rows, n_chunks * n_real), 0)
        pc = lax.broadcasted_iota(jnp.int32, (rows, n_chunks * n_real), 1)
        place = jnp.where((pr // c == pc // n_real) & (pr % c - (c - n_real) == pc % n_real),
                          1.0, 0.0).astype(BF16)
        load = lambda ref: _mm_exact_lhs(place, ref[...])
    else:
        load = lambda ref: ref[...]

        @pl.when(i == 0)
        def _():
            if pair_split:
                for h in range(RET_HEADS):
                    st_s[h] = _mm_exact_lhs(sel_ref[...], s0_ref[0, h])
            else:
                st_s[...] = s0_ref[0]

    if pair_split:
        cos = cos_ref[...]
        sin = sin_ref[...]
        half = RET_HEAD // 2

        def rot(x):
            parts = []
            for h in range(RET_HEADS):
                x0 = x[:, h * RET_HEAD:h * RET_HEAD + half]
                x1 = x[:, h * RET_HEAD + half:(h + 1) * RET_HEAD]
                parts += [x0 * cos - x1 * sin, x0 * sin + x1 * cos]
            return jnp.concatenate(parts, axis=-1)
    else:
        lane = lax.broadcasted_iota(jnp.int32, (rows, RET_WIDTH), 1)
        even = (lane % 2) == 0
        cos = jnp.concatenate([cos_ref[...]] * RET_HEADS, axis=-1)
        sin = jnp.concatenate([sin_ref[...]] * RET_HEADS, axis=-1)

        def rot(x):
            partner = jnp.where(even, pltpu.roll(x, RET_WIDTH - 1, axis=1), pltpu.roll(x, 1, axis=1))
            return x * cos + partner * sin

    q = rot(load(zq_ref))
    k = rot(load(zk_ref)) * (RET_HEAD ** -0.5)
    v = load(zv_ref)
    g = load(zg_ref)

    out_rows = []
    for ch in range(n_chunks):
        rs = slice(ch * c, (ch + 1) * c)
        out_heads = []
        for h in range(RET_HEADS):
            hs = slice(h * RET_HEAD, (h + 1) * RET_HEAD)
            qh, kh, vh = q[rs, hs], k[rs, hs], v[rs, hs]
            s_prev = s0_ref[ch, h] if per_chunk_state else st_s[h]
            scores = _mm_nt(qh, kh) * dmask_ref[h]
            o = _mm(scores, vh) + _mm(qh, s_prev) * iscale_ref[h]
            s_new = s_prev * sdec_ref[h] + _mm_tn(kh * kscale_ref[h], vh)
            if per_chunk_state:
                sout_ref[ch, h] = s_new
            else:
                st_s[h] = s_new
            o = o * lax.rsqrt(jnp.mean(o * o, axis=-1, keepdims=True) + RET_GN_EPS)
            gh = g[rs, hs]
            out_heads.append(o * (gh * jax.nn.sigmoid(gh)))
        out_rows.append(jnp.concatenate(out_heads, axis=-1))
    o_out = jnp.concatenate(out_rows, axis=0).astype(BF16)
    if per_chunk_state:
        o_out = lax.dot_general(place, o_out, (((0,), (0,)), ((), ())),
                                preferred_element_type=F32).astype(BF16)
    o_ref[...] = o_out

    if not per_chunk_state:
        @pl.when(i == pl.num_programs(1) - 1)
        def _():
            if pair_split:
                for h in range(RET_HEADS):
                    sout_ref[0, h] = _mm_exact_lhs_t(sel_ref[...], st_s[h])
            else:
                sout_ref[0] = st_s[...]


def _retention_mixer(z, row_block0, n_groups, n_tiles, cos, sin, tabs, s0, *, rows, c, n_real,
                     per_chunk_state, pos_per_tile, pair_split=False):
    shared_s0 = s0.shape[0] == 1
    sb = 1 if shared_s0 else s0.shape[0] // n_groups
    kern = functools.partial(_retention_kernel, rows=rows, c=c, n_real=n_real,
                             per_chunk_state=per_chunk_state, pair_split=pair_split)
    rows_io = rows // c * n_real if per_chunk_state else rows
    zspec = lambda col: pl.BlockSpec((rows_io, RET_WIDTH), lambda gi, i: (row_block0 + gi * n_tiles + i, col))
    full = lambda arr: pl.BlockSpec(arr.shape, lambda gi, i: (0,) * arr.ndim)
    tbl_row0, tbl_advance = pos_per_tile
    assert tbl_row0 % rows == 0
    tspec = pl.BlockSpec((rows, cos.shape[1]),
                         lambda gi, i: (tbl_row0 // rows + (i if tbl_advance else 0), 0))
    state_blk = (sb, RET_HEADS, RET_HEAD, RET_HEAD)
    s0_spec = pl.BlockSpec(state_blk, lambda gi, i: (0 if shared_s0 else gi, 0, 0, 0))
    sout_spec = pl.BlockSpec(state_blk, lambda gi, i: (gi, 0, 0, 0))
    dmask, iscale, kscale, sdec = tabs
    sel = jnp.asarray(_pair_split_index(RET_HEAD)[:, None] == np.arange(RET_HEAD)[None, :], dtype=BF16)
    return pl.pallas_call(
        kern,
        grid=(n_groups, n_tiles),
        in_specs=[zspec(0), zspec(1), zspec(2), zspec(3), tspec, tspec,
                  full(dmask), full(iscale), full(kscale), full(sdec), full(sel), s0_spec],
        out_specs=[pl.BlockSpec((rows_io, RET_WIDTH), lambda gi, i: (gi * n_tiles + i, 0)), sout_spec],
        out_shape=[jax.ShapeDtypeStruct((n_groups * n_tiles * rows_io, RET_WIDTH), BF16),
                   jax.ShapeDtypeStruct((n_groups * sb,) + state_blk[1:], F32)],
        scratch_shapes=[pltpu.VMEM((RET_HEADS, RET_HEAD, RET_HEAD), F32)],
        compiler_params=_cparams(("parallel", "arbitrary")),
        name="retention_mixer",
    )(z, z, z, z, cos, sin, dmask, iscale, kscale, sdec, sel, s0)


def _retention_tables(c, n_real):
    log_gamma = np.log(1.0 - 2.0 ** (-5.0 - np.arange(RET_HEADS, dtype=np.float64)))
    r = np.arange(c, dtype=np.float64)
    idx = r - float(c - n_real)
    diff = r[:, None] - r[None, :]
    dmask = np.where(diff[None] >= 0, np.exp(log_gamma[:, None, None] * np.maximum(diff, 0.0)[None]), 0.0)
    iscale = np.exp(log_gamma[:, None] * (idx + 1.0)[None, :])[:, :, None]
    kscale = np.exp(log_gamma[:, None] * (n_real - 1.0 - idx)[None, :])[:, :, None]
    sdec = np.broadcast_to(np.exp(log_gamma * n_real)[:, None, None], (RET_HEADS, 1, RET_HEAD))
    return tuple(jnp.asarray(t, dtype=F32) for t in (dmask, iscale, kscale, sdec))


def _rotary_tables(pos, interleaved=True):
    inv_freq = 1.0 / (ROPE_BASE ** np.linspace(0.0, 1.0, RET_HEAD // 2))
    ang = np.asarray(pos, dtype=np.float64)[:, None] * inv_freq[None, :]
    cos = np.cos(ang)
    sin = np.sin(ang)
    if not interleaved:
        return jnp.asarray(cos, dtype=F32), jnp.asarray(sin, dtype=F32)
    cos2 = np.repeat(cos, 2, axis=-1)
    sin2 = np.stack([-sin, sin], axis=-1).reshape(len(pos), RET_HEAD)
    return jnp.asarray(cos2, dtype=F32), jnp.asarray(sin2, dtype=F32)


def _outproj_kernel(oa_ref, ob_ref, w_ref, x_ref, g_ref, h_ref, hn_ref):
    acc = jnp.dot(oa_ref[...], w_ref[0:RW_WIDTH, :], preferred_element_type=F32)
    acc = acc + jnp.dot(ob_ref[...], w_ref[RW_WIDTH:, :], preferred_element_type=F32)
    h = x_ref[...] + acc
    h_ref[...] = h
    hn_ref[...] = _rms_norm_bf16(h, g_ref[...])


def _out_projection(o_a, o_b, w_out, x2d, norm_g, tm):
    m = o_a.shape[0]
    return pl.pallas_call(
        _outproj_kernel,
        grid=(m // tm,),
        in_specs=[
            pl.BlockSpec((tm, RW_WIDTH), lambda i: (i, 0)),
            pl.BlockSpec((tm, RET_WIDTH), lambda i: (i, 0)),
            pl.BlockSpec((D_MODEL, D_MODEL), lambda i: (0, 0)),
            pl.BlockSpec((tm, D_MODEL), lambda i: (i, 0)),
            pl.BlockSpec((1, D_MODEL), lambda i: (0, 0)),
        ],
        out_specs=[pl.BlockSpec((tm, D_MODEL), lambda i: (i, 0)),
                   pl.BlockSpec((tm, D_MODEL), lambda i: (i, 0))],
        out_shape=[jax.ShapeDtypeStruct((m, D_MODEL), F32), jax.ShapeDtypeStruct((m, D_MODEL), BF16)],
        compiler_params=_cparams(("parallel",)),
        name="out_projection",
    )(o_a, o_b, w_out, x2d, norm_g)


def _ffn_kernel(hn_ref, wg_ref, wu_ref, wd_ref, h_ref, g_ref, y_ref, acc_ref, *, h_slices):
    f = pl.program_id(1)

    @pl.when(f == 0)
    def _():
        acc_ref[...] = jnp.zeros_like(acc_ref)

    @pl.when(f < h_slices)
    def _():
        rows = h_ref.shape[0]
        sl = pl.ds(pl.multiple_of(f * rows, rows), rows)
        acc_ref[sl, :] += h_ref[...]

    hn = hn_ref[...]
    gate = jnp.dot(hn, wg_ref[...], preferred_element_type=F32)
    up = jnp.dot(hn, wu_ref[...], preferred_element_type=F32)
    act = (gate * jax.nn.sigmoid(gate)) * up
    acc_ref[...] += jnp.dot(act.astype(BF16), wd_ref[...], preferred_element_type=F32)

    @pl.when(f == pl.num_programs(1) - 1)
    def _():
        h = acc_ref[...]
        ms = jnp.mean(h * h, axis=-1, keepdims=True)
        y_ref[...] = (h * lax.rsqrt(ms + RMS_EPS)) * g_ref[...]


def _ffn(hn, w_gate, w_up, w_down, h, norm_g, tm, tf, h_slices=8):
    m = hn.shape[0]
    assert D_FF // tf >= h_slices
    return pl.pallas_call(
        functools.partial(_ffn_kernel, h_slices=h_slices),
        grid=(m // tm, D_FF // tf),
        in_specs=[
            pl.BlockSpec((tm, D_MODEL), lambda i, f: (i, 0)),
            pl.BlockSpec((D_MODEL, tf), lambda i, f: (0, f)),
            pl.BlockSpec((D_MODEL, tf), lambda i, f: (0, f)),
            pl.BlockSpec((tf, D_MODEL), lambda i, f: (f, 0)),
            pl.BlockSpec((tm // h_slices, D_MODEL),
                         lambda i, f: (i * h_slices + jnp.minimum(f, h_slices - 1), 0)),
            pl.BlockSpec((1, D_MODEL), lambda i, f: (0, 0)),
        ],
        out_specs=pl.BlockSpec((tm, D_MODEL), lambda i, f: (i, 0)),
        out_shape=jax.ShapeDtypeStruct((m, D_MODEL), F32),
        scratch_shapes=[pltpu.VMEM((tm, D_MODEL), F32)],
        compiler_params=_cparams(("parallel", "arbitrary")),
        name="swiglu_ffn",
    )(hn, w_gate, w_up, w_down, h, norm_g)


def _pad_cols(a, n):
    return jnp.pad(a, ((0, 0), (0, n - a.shape[1])))


def _pad_rows(a, n):
    return jnp.pad(a, ((0, n - a.shape[0]), (0, 0)))


def kernel(x_prompt, x_sample, state_shift, state_rwkv, state_ret, meta_tokens, norm_mix, w_in,
           rwkv_mu, rwkv_w0, rwkv_w2, rwkv_a0, rwkv_a2, rwkv_g2, rwkv_kk, rwkv_ka, rwkv_rk,
           rwkv_ln_w, rwkv_ln_b, w_out, norm_ffn, w_gate, w_up, w_down, norm_final):
    n_b, seq, d = x_prompt.shape
    n_s, dec_seq, _ = x_sample.shape
    n_p = n_b * seq
    n_d = n_s * dec_seq
    depth = w_in.shape[0]
    assert depth == 1 and d == D_MODEL and n_s == LANES

    w_in_t = jnp.transpose(w_in[0])
    w_in_p = _w_in_layout(w_in_t, tn=512)
    row = lambda a: a.reshape(1, -1).astype(F32)
    w2p = jnp.concatenate([rwkv_w2[0], jnp.zeros((128 - DECAY_LORA, RW_WIDTH), F32)], axis=0)
    a2p = jnp.concatenate([jnp.zeros((DECAY_LORA, RW_WIDTH), F32), rwkv_a2[0]], axis=0)
    g2p = jnp.concatenate([rwkv_g2[0], jnp.zeros((256 - GATE_LORA, RW_WIDTH), F32)], axis=0)
    rw_params = dict(mu=_pad_cols(row(rwkv_mu[0]), SHIFT_PAD), w0=row(rwkv_w0[0]), w2=w2p,
                     a0=row(rwkv_a0[0]), a2=a2p, g2=g2p, kk=row(rwkv_kk[0]), ka=row(rwkv_ka[0]),
                     rk=row(rwkv_rk[0]), lnw=row(rwkv_ln_w[0]), lnb=row(rwkv_ln_b[0]))
    col = lambda a: jnp.broadcast_to(a.reshape(-1, 1).astype(F32), (a.size, LANES))
    rw_params_t = dict(w0=col(rwkv_w0[0]), w2t=rwkv_w2[0].T, a0=col(rwkv_a0[0]), a2t=rwkv_a2[0].T,
                       g2t=rwkv_g2[0].T, kk=col(rwkv_kk[0]), ka=col(rwkv_ka[0]), rk=col(rwkv_rk[0]),
                       lnw=col(rwkv_ln_w[0]), lnb=col(rwkv_ln_b[0]))

    x_p = x_prompt.reshape(n_p, d)
    x_s = x_sample.reshape(n_d, d)
    x_sm = jnp.concatenate([x_s, meta_tokens.astype(F32)], axis=0)
    x_ts = jnp.transpose(x_sample, (1, 0, 2)).reshape(n_d, d)
    g_mix = row(norm_mix[0])
    z_p = _in_projection(x_p, g_mix, w_in_p, tm=1024, tn=1536)
    z_sm = _in_projection(x_sm, g_mix, w_in_p, tm=x_sm.shape[0], tn=512, qk_interleaved=True)
    z_st = _in_projection_t(w_in_t, x_ts, g_mix, SHIFT_PAD, tm=512)

    z_meta = jnp.pad(z_sm[n_d:], ((RET_CHUNK - N_META, 0), (0, 0)))
    srow = 2 * dec_seq

    zero_prev = jnp.zeros((1, 1, SHIFT_PAD), F32)
    zero_rw = jnp.zeros((1, RW_HEADS, RW_HEAD, RW_HEAD), F32)
    c_rw = 64
    _, s_rw_meta, zlast_meta = _rwkv_mixer(z_meta, 1, 1, 1, zero_prev, zero_rw, rw_params, tb=c_rw, c=c_rw)
    oa_p, rwkv_p, zlast_p, w_out_b, w_gate_b, w_up_b, w_down_b = _rwkv_mixer(
        z_p, 0, n_b, seq // 256, zlast_meta, s_rw_meta, rw_params, tb=256, c=c_rw,
        cast=(w_out[0], w_gate[0], w_up[0], w_down[0]))
    shift_t = _pad_rows(jnp.transpose(state_shift[0]), SHIFT_PAD)
    mu_t = _pad_rows(col(rwkv_mu[0]), SHIFT_PAD)
    state_t = jnp.transpose(state_rwkv[0], (1, 2, 3, 0))
    oa_st, rwkv_st = _rwkv_sample_mixer(z_st, shift_t, mu_t, state_t, rw_params_t, n_tok=dec_seq, hpb=2)
    oa_s = jnp.transpose(oa_st, (1, 0, 2)).reshape(n_d, RW_WIDTH)
    rwkv_s = jnp.transpose(rwkv_st, (3, 0, 1, 2))

    past_len = 16384
    tabs_full = _retention_tables(RET_CHUNK, RET_CHUNK)
    tabs_meta = _retention_tables(RET_CHUNK, N_META)
    tabs_smp = _retention_tables(srow, dec_seq)
    sb_rt = 8
    cos_p, sin_p = _rotary_tables(N_META + np.arange(seq), interleaved=False)
    pos_ms = np.concatenate([np.arange(RET_CHUNK) - (RET_CHUNK - N_META),
                             past_len + np.tile(np.arange(srow) - (srow - dec_seq), sb_rt)])
    cos_ms, sin_ms = _rotary_tables(pos_ms)
    zero_rt = jnp.zeros((1, RET_HEADS, RET_HEAD, RET_HEAD), F32)
    _, s_rt_meta = _retention_mixer(z_meta, 0, 1, 1, cos_ms, sin_ms, tabs_meta, zero_rt,
                                    rows=RET_CHUNK, c=RET_CHUNK, n_real=N_META, per_chunk_state=False,
                                    pos_per_tile=(0, False))
    rows_rt = 2 * RET_CHUNK
    ob_p, ret_p = _retention_mixer(z_p, 0, n_b, seq // rows_rt, cos_p, sin_p, tabs_full, s_rt_meta,
                                   rows=rows_rt, c=RET_CHUNK, n_real=RET_CHUNK, per_chunk_state=False,
                                   pos_per_tile=(0, True), pair_split=True)
    ob_s, ret_s = _retention_mixer(z_sm, 0, n_s // sb_rt, 1, cos_ms, sin_ms, tabs_smp,
                                   state_ret.reshape(n_s, RET_HEADS, RET_HEAD, RET_HEAD),
                                   rows=sb_rt * srow, c=srow, n_real=dec_seq, per_chunk_state=True,
                                   pos_per_tile=(RET_CHUNK, False))

    g_ffn = row(norm_ffn[0])
    g_fin = row(norm_final)
    h_p, hn_p = _out_projection(oa_p, ob_p, w_out_b, x_p, g_ffn, tm=512)
    h_s, hn_s = _out_projection(oa_s, ob_s, w_out_b, x_s, g_ffn, tm=n_d)
    y_p = _ffn(hn_p, w_gate_b, w_up_b, w_down_b, h_p, g_fin, tm=1024, tf=512)
    y_s = _ffn(hn_s, w_gate_b, w_up_b, w_down_b, h_s, g_fin, tm=n_d, tf=512)

    y_prompt = y_p.reshape(n_b, seq, d)
    y_sample = y_s.reshape(n_s, dec_seq, d)
    shift_p = zlast_p[:, 0, :RW_COLS][None]
    shift_s = jnp.transpose(z_st[:RW_COLS, (dec_seq - 1) * n_s:])[None]
    return (y_prompt, y_sample, shift_p, rwkv_p[None], ret_p[None], shift_s, rwkv_s[None], ret_s[None])
```

```python
import functools
import math

import jax
import jax.numpy as jnp
import numpy as np
from jax import lax
from jax.experimental import pallas as pl
from jax.experimental.pallas import tpu as pltpu

F32 = jnp.float32
BF16 = jnp.bfloat16

D_MODEL = 2048
N_META = 16
RW_WIDTH = 1024
RW_HEAD = 64
RW_HEADS = 16
DECAY_LORA = 64
AAA_LORA = 64
GATE_LORA = 160
RW_COLS = 3 * RW_WIDTH + DECAY_LORA + AAA_LORA + GATE_LORA
RET_WIDTH = 1024
RET_HEADS = 4
RET_HEAD = 256
RET_CHUNK = 128
D_FF = 5632
RMS_EPS = 1e-6
RW_GN_EPS = 64e-5
RET_GN_EPS = 1e-6
ROPE_BASE = 10000.0
LANES = 128

LORA_PAD = 512
Z_COLS = 4 * RET_WIDTH + 3 * RW_WIDTH + LORA_PAD
RW_OFF = 4 * RET_WIDTH
SHIFT_PAD = 3 * RW_WIDTH + LORA_PAD

VMEM_LIMIT = 60 * 1024 * 1024


def _cparams(sem):
    return pltpu.CompilerParams(dimension_semantics=sem, vmem_limit_bytes=VMEM_LIMIT)


def _mm(a, b):
    return jnp.dot(a.astype(BF16), b.astype(BF16), preferred_element_type=F32)


def _mm_nt(a, b):
    return lax.dot_general(a.astype(BF16), b.astype(BF16), (((1,), (1,)), ((), ())),
                           preferred_element_type=F32)


def _mm_tn(a, b):
    return lax.dot_general(a.astype(BF16), b.astype(BF16), (((0,), (0,)), ((), ())),
                           preferred_element_type=F32)


def _bmm(a, b):
    return jnp.einsum("bmk,bkn->bmn", a.astype(BF16), b.astype(BF16), preferred_element_type=F32)


def _bmm_nt(a, b):
    return jnp.einsum("bmk,bnk->bmn", a.astype(BF16), b.astype(BF16), preferred_element_type=F32)


def _bmm_tn(a, b):
    return jnp.einsum("bkm,bkn->bmn", a.astype(BF16), b.astype(BF16), preferred_element_type=F32)


def _mm_exact_lhs(m_bf16, x):
    hi = x.astype(BF16)
    r1 = x - hi.astype(F32)
    mid = r1.astype(BF16)
    lo = (r1 - mid.astype(F32)).astype(BF16)
    d = functools.partial(jnp.dot, preferred_element_type=F32)
    return d(m_bf16, hi) + d(m_bf16, mid) + d(m_bf16, lo)


def _mm_exact_lhs_t(m_bf16, x):
    hi = x.astype(BF16)
    r1 = x - hi.astype(F32)
    mid = r1.astype(BF16)
    lo = (r1 - mid.astype(F32)).astype(BF16)
    d = functools.partial(lax.dot_general, dimension_numbers=(((0,), (0,)), ((), ())),
                          preferred_element_type=F32)
    return d(m_bf16, hi) + d(m_bf16, mid) + d(m_bf16, lo)


def _rms_norm_bf16(x, g):
    ms = jnp.mean(x * x, axis=-1, keepdims=True)
    return ((x * lax.rsqrt(ms + RMS_EPS)) * g).astype(BF16)


def _pair_split_index(n):
    j = np.arange(n)
    half = RET_HEAD // 2
    return (j // RET_HEAD) * RET_HEAD + 2 * (j % half) + (j % RET_HEAD) // half


def _w_in_layout_kernel(w_ref, sel_ref, o_ref, *, n_qk):
    c = pl.program_id(0)

    @pl.when(c >= n_qk)
    def _():
        o_ref[...] = w_ref[...].T.astype(BF16)

    @pl.when(c < n_qk)
    def _():
        picked = jnp.dot(sel_ref[...], w_ref[...].astype(BF16), preferred_element_type=F32)
        o_ref[...] = picked.T.astype(BF16)


def _w_in_layout(w_in_t, tn):
    n, d = w_in_t.shape
    n_ret = 4 * RET_WIDTH // tn
    n_z = Z_COLS // tn
    n_qk = 2 * RET_WIDTH // tn
    sel = jnp.asarray(_pair_split_index(tn)[:, None] == np.arange(tn)[None, :], dtype=BF16)

    def src_row(c):
        align = math.gcd(RW_COLS, tn)
        row = jnp.where(c < n_ret, RW_COLS + tn * c,
                        jnp.where(c < n_z, tn * (c - n_ret), RW_COLS + tn * (c - n_z)))
        return (pl.multiple_of(row, align), 0)

    return pl.pallas_call(
        functools.partial(_w_in_layout_kernel, n_qk=n_qk),
        grid=(n_z + n_qk,),
        in_specs=[pl.BlockSpec((pl.Element(tn), pl.Element(d)), src_row),
                  pl.BlockSpec((tn, tn), lambda c: (0, 0))],
        out_specs=pl.BlockSpec((d, tn), lambda c: (0, c)),
        out_shape=jax.ShapeDtypeStruct((d, Z_COLS + 2 * RET_WIDTH), BF16),
        compiler_params=_cparams(("parallel",)),
        name="w_in_layout",
    )(w_in_t, sel)


def _inproj_kernel(x_ref, g_ref, w_ref, o_ref, xn_ref):
    @pl.when(pl.program_id(1) == 0)
    def _():
        xn_ref[...] = _rms_norm_bf16(x_ref[...], g_ref[...])

    o_ref[...] = jnp.dot(xn_ref[...], w_ref[...], preferred_element_type=F32)


def _in_projection(x2d, norm_g, w_in_p, tm, tn, qk_interleaved=False):
    m = x2d.shape[0]
    if qk_interleaved:
        n_qk = 2 * RET_WIDTH // tn
        w_col = lambda i, j: (0, jnp.where(j < n_qk, Z_COLS // tn + j, j))
    else:
        w_col = lambda i, j: (0, j)
    return pl.pallas_call(
        _inproj_kernel,
        grid=(m // tm, Z_COLS // tn),
        in_specs=[
            pl.BlockSpec((tm, D_MODEL), lambda i, j: (i, 0)),
            pl.BlockSpec((1, D_MODEL), lambda i, j: (0, 0)),
            pl.BlockSpec((D_MODEL, tn), w_col),
        ],
        out_specs=pl.BlockSpec((tm, tn), lambda i, j: (i, j)),
        out_shape=jax.ShapeDtypeStruct((m, Z_COLS), F32),
        scratch_shapes=[pltpu.VMEM((tm, D_MODEL), BF16)],
        compiler_params=_cparams(("parallel", "arbitrary")),
        name="in_projection",
    )(x2d, norm_g, w_in_p)


def _inproj_t_kernel(w_ref, x_ref, g_ref, o_ref, xn_ref):
    @pl.when(pl.program_id(0) == 0)
    def _():
        xn_ref[...] = _rms_norm_bf16(x_ref[...], g_ref[...])

    o_ref[...] = _mm_nt(w_ref[...], xn_ref[...])


def _in_projection_t(w_in_t, x2d, norm_g, n_rows, tm):
    m, d = x2d.shape
    return pl.pallas_call(
        _inproj_t_kernel,
        grid=(n_rows // tm,),
        in_specs=[
            pl.BlockSpec((tm, d), lambda i: (i, 0)),
            pl.BlockSpec((m, d), lambda i: (0, 0)),
            pl.BlockSpec((1, d), lambda i: (0, 0)),
        ],
        out_specs=pl.BlockSpec((tm, m), lambda i: (i, 0)),
        out_shape=jax.ShapeDtypeStruct((n_rows, m), F32),
        scratch_shapes=[pltpu.VMEM((m, d), BF16)],
        compiler_params=_cparams(("arbitrary",)),
        name="in_projection_t",
    )(w_in_t, x2d, norm_g)


def _rwkv_kernel(zr_ref, zk_ref, zv_ref, zl_ref, zprev0_ref, s0_ref,
                 mu_ref, w0_ref, w2_ref, a0_ref, a2_ref, g2_ref, kk_ref, ka_ref, rk_ref,
                 lnw_ref, lnb_ref,
                 o_ref, sout_ref, zlast_ref,
                 carry_ref, sbd_s, r_s, km_s, v_s, kk_s, a_s, lw_s, cum_s, o_s,
                 *, tb, c, n_factors):
    i = pl.program_id(1)
    n_chunks = tb // c
    n_pairs = RW_HEADS // 2
    pw = 2 * RW_HEAD

    @pl.when(i == 0)
    def _():
        carry_ref[...] = zprev0_ref[0]
        zero = jnp.zeros((RW_HEAD, RW_HEAD), F32)
        for p in range(n_pairs):
            top = jnp.concatenate([s0_ref[0, 2 * p], zero], axis=1)
            bot = jnp.concatenate([zero, s0_ref[0, 2 * p + 1]], axis=1)
            sbd_s[p] = jnp.concatenate([top, bot], axis=0)

    row = lax.broadcasted_iota(jnp.int32, (tb, 1), 0)

    def shifted(z_ref, lo, hi):
        z = z_ref[...]
        prev = pltpu.roll(z, 1, axis=0)
        prev = jnp.concatenate([jnp.where(row[:8] == 0, carry_ref[:, lo:hi], prev[:8]), prev[8:]], axis=0)
        zs = z + mu_ref[:, lo:hi] * (prev - z)
        carry_ref[:, lo:hi] = z[tb - 1:tb, :]
        return zs

    w = RW_WIDTH
    r = shifted(zr_ref, 0, w)
    k = shifted(zk_ref, w, 2 * w)
    v = shifted(zv_ref, 2 * w, 3 * w)
    zl = shifted(zl_ref, 3 * w, 3 * w + LORA_PAD)

    lo2 = zl[:, 0:128]
    wlog = -jax.nn.softplus(-(w0_ref[...] + _mm(jnp.tanh(lo2), w2_ref[...]))) - 0.5
    lw = -jnp.exp(wlog)
    a = jax.nn.sigmoid(a0_ref[...] + _mm(lo2, a2_ref[...]))
    g = _mm(jax.nn.sigmoid(zl[:, 128:384]), g2_ref[...])
    kk = k * kk_ref[...]
    km = k * (1.0 + (a - 1.0) * ka_ref[...])

    ri = lax.broadcasted_iota(jnp.int32, (tb, tb), 0)
    ci = lax.broadcasted_iota(jnp.int32, (tb, tb), 1)
    tri = jnp.where((ri // c == ci // c) & (ri >= ci), 1.0, 0.0).astype(BF16)
    cum = _mm_exact_lhs(tri, lw)

    for dst, val in ((r_s, r), (km_s, km), (v_s, v), (a_s, a), (lw_s, lw), (cum_s, cum), (kk_s, kk)):
        dst[...] = val

    rr = lax.broadcasted_iota(jnp.int32, (c, c), 0)
    cc = lax.broadcasted_iota(jnp.int32, (c, c), 1)
    rr2 = lax.broadcasted_iota(jnp.int32, (c, 2 * c), 0)
    cc2 = lax.broadcasted_iota(jnp.int32, (c, 2 * c), 1)
    strict = (rr > cc)[None]
    eye = jnp.where(rr == cc, 1.0, 0.0).astype(F32)[None]
    strict_k = ((cc2 >= c) & (rr2 > cc2 - c))[None]
    incl_bk = (rr2 >= cc2 % c)[None]
    lo = (lax.broadcasted_iota(jnp.int32, (1, 1, pw), 2) < RW_HEAD)
    bd_r = lax.broadcasted_iota(jnp.int32, (pw, pw), 0) // RW_HEAD
    bd_c = lax.broadcasted_iota(jnp.int32, (pw, pw), 1) // RW_HEAD
    block_diag = (bd_r == bd_c)[None]
    rk = rk_ref[...]
    lnw = lnw_ref[...]
    lnb = lnb_ref[...]

    def head_sum(x):
        s_lo = jnp.sum(jnp.where(lo, x, 0.0), axis=-1, keepdims=True)
        s_hi = jnp.sum(jnp.where(lo, 0.0, x), axis=-1, keepdims=True)
        return jnp.where(lo, s_lo, s_hi)

    def pick(x16):
        x4 = x16.reshape(n_pairs, 2, c, pw)
        return jnp.where(lo, x4[:, 0], x4[:, 1])

    def body(ch):
        rows = pl.ds(ch * c, c)

        def pairs(s):
            x = s[rows, :]
            return jnp.stack([x[:, p * pw:(p + 1) * pw] for p in range(n_pairs)], axis=0)

        rh, kmh, vh, ah, lwh, cumh, kkh = [pairs(s) for s in (r_s, km_s, v_s, a_s, lw_s, cum_s, kk_s)]
        s_prev = sbd_s[...]
        kkn = kkh * lax.rsqrt(jnp.maximum(head_sum(kkh * kkh), 1e-24))
        bvec = kkn * ah
        cum_end = cumh[:, c - 1:c, :]
        p_inc = jnp.exp(cumh)
        p_exc = jnp.exp(cumh - lwh)
        p_inv = jnp.exp(-cumh)
        p_end = jnp.exp(cum_end)
        p_rel = jnp.exp(cum_end - cumh)
        a_t = -(kkn * p_exc)
        r_t = rh * p_inc
        b_t = bvec * p_inv
        k_t = kmh * p_inv
        b_h = bvec * p_rel
        k_h = kmh * p_rel

        ar = jnp.concatenate([jnp.where(lo, a_t, 0.0), jnp.where(lo, r_t, 0.0),
                              jnp.where(lo, 0.0, a_t), jnp.where(lo, 0.0, r_t)], axis=1)
        prod = _bmm_nt(ar, jnp.concatenate([b_t, k_t, s_prev], axis=1))
        prod = prod.reshape(RW_HEADS, 2 * c, 2 * c + pw)
        a_rows, r_rows = prod[:, :c], prod[:, c:]
        aak_wide = jnp.where(strict_k, a_rows[:, :, :2 * c], 0.0)
        arbk = jnp.where(incl_bk, r_rows[:, :, :2 * c], 0.0)
        lmat = jnp.where(strict, a_rows[:, :, :c], 0.0)
        tinv = eye + lmat
        if n_factors > 1:
            lp = _bmm(lmat, lmat)
            for _ in range(n_factors - 2):
                st = _bmm(jnp.concatenate([lp, tinv], axis=1), lp)
                lp = st[:, :c]
                tinv = tinv + st[:, c:]
            tinv = tinv + _bmm(tinv, lp)
        v16 = jnp.repeat(vh, 2, axis=0)
        x = a_rows[:, :, 2 * c:] + _bmm(aak_wide, jnp.concatenate([v16, v16], axis=1))
        u16 = _bmm(tinv, x)
        o16 = r_rows[:, :, 2 * c:] + _bmm(arbk, jnp.concatenate([u16, v16], axis=1))
        u = pick(u16)
        o = pick(o16)
        s_upd = _bmm_tn(jnp.concatenate([u, vh], axis=1), jnp.concatenate([b_h, k_h], axis=1))
        sbd_s[...] = s_prev * p_end + jnp.where(block_diag, s_upd, 0.0)
        inv_n = 1.0 / RW_HEAD
        mean = head_sum(o) * inv_n
        var = head_sum(jnp.square(o - mean)) * inv_n
        on = ((o - mean) * lax.rsqrt(var + RW_GN_EPS)) * lnw + lnb
        out = on + head_sum(rh * kmh * rk) * vh
        for p in range(n_pairs):
            o_s[rows, p * pw:(p + 1) * pw] = out[p]

    for ch in range(n_chunks):
        body(ch)

    o_ref[...] = (o_s[...] * g).astype(BF16)
    zlast_ref[0] = carry_ref[...]

    @pl.when(i == pl.num_programs(1) - 1)
    def _():
        for p in range(n_pairs):
            blk = sbd_s[p]
            sout_ref[0, 2 * p] = blk[:RW_HEAD, :RW_HEAD]
            sout_ref[0, 2 * p + 1] = blk[RW_HEAD:, RW_HEAD:]


N_RWKV_INPUTS = 17
N_RWKV_OUTPUTS = 3


def _rwkv_and_cast_kernel(*refs, n_cast, **kw):
    ins = refs[:N_RWKV_INPUTS]
    cast_in = refs[N_RWKV_INPUTS:N_RWKV_INPUTS + n_cast]
    o0 = N_RWKV_INPUTS + n_cast
    outs = refs[o0:o0 + N_RWKV_OUTPUTS]
    cast_out = refs[o0 + N_RWKV_OUTPUTS:o0 + N_RWKV_OUTPUTS + n_cast]
    scratch = refs[o0 + N_RWKV_OUTPUTS + n_cast:]
    for src, dst in zip(cast_in, cast_out):
        dst[...] = src[...].astype(BF16)
    _rwkv_kernel(*ins, *outs, *scratch, **kw)


def _rwkv_mixer(z, row_block0, n_groups, n_tiles, zprev0, s0, p, *, tb, c, cast=()):
    n_factors = max(1, math.ceil(math.log2(c)))
    kern = functools.partial(_rwkv_and_cast_kernel, n_cast=len(cast), tb=tb, c=c, n_factors=n_factors)
    n_steps = n_groups * n_tiles
    cast_specs = [pl.BlockSpec((a.shape[0] // n_steps, a.shape[1]), lambda gi, i: (gi * n_tiles + i, 0))
                  for a in cast]
    cast_shapes = [jax.ShapeDtypeStruct(a.shape, BF16) for a in cast]
    cb = RW_OFF // RW_WIDTH
    full = lambda arr: pl.BlockSpec(arr.shape, lambda gi, i: (0,) * arr.ndim)
    lora_cb = (RW_OFF + 3 * RW_WIDTH) // LORA_PAD
    per_pair = lambda a: a.reshape(RW_HEADS // 2, 1, 2 * RW_HEAD)
    params = [p["mu"], p["w0"], p["w2"], p["a0"], p["a2"], p["g2"], p["kk"], p["ka"],
              per_pair(p["rk"]), per_pair(p["lnw"]), per_pair(p["lnb"])]
    dense = lambda: pltpu.VMEM((tb, RW_WIDTH), F32)
    zspec = lambda width, col: pl.BlockSpec((tb, width), lambda gi, i: (row_block0 + gi * n_tiles + i, col))
    sspec = pl.BlockSpec((1, RW_HEADS, RW_HEAD, RW_HEAD), lambda gi, i: (gi, 0, 0, 0))
    shared = lambda a: (lambda gi, i: (0,) * a.ndim) if a.shape[0] == 1 else (lambda gi, i: (gi,) + (0,) * (a.ndim - 1))
    return pl.pallas_call(
        kern,
        grid=(n_groups, n_tiles),
        in_specs=[zspec(RW_WIDTH, cb), zspec(RW_WIDTH, cb + 1), zspec(RW_WIDTH, cb + 2),
                  zspec(LORA_PAD, lora_cb),
                  pl.BlockSpec((1, 1, SHIFT_PAD), shared(zprev0)),
                  pl.BlockSpec((1, RW_HEADS, RW_HEAD, RW_HEAD), shared(s0))]
                 + [full(a) for a in params] + cast_specs,
        out_specs=[pl.BlockSpec((tb, RW_WIDTH), lambda gi, i: (gi * n_tiles + i, 0)), sspec,
                   pl.BlockSpec((1, 1, SHIFT_PAD), lambda gi, i: (gi, 0, 0))] + cast_specs,
        out_shape=[jax.ShapeDtypeStruct((n_groups * n_tiles * tb, RW_WIDTH), BF16),
                   jax.ShapeDtypeStruct((n_groups, RW_HEADS, RW_HEAD, RW_HEAD), F32),
                   jax.ShapeDtypeStruct((n_groups, 1, SHIFT_PAD), F32)] + cast_shapes,
        scratch_shapes=[pltpu.VMEM((1, SHIFT_PAD), F32),
                        pltpu.VMEM((RW_HEADS // 2, 2 * RW_HEAD, 2 * RW_HEAD), F32)]
                       + [dense() for _ in range(8)],
        compiler_params=_cparams(("parallel", "arbitrary")),
        name="rwkv7_mixer",
    )(z, z, z, z, zprev0, s0, *params, *cast)


def _rwkv_sample_kernel(zr_ref, zk_ref, zv_ref, zl_ref, shr_ref, shk_ref, shv_ref, shl_ref,
                        mur_ref, muk_ref, muv_ref, mul_ref, s0_ref,
                        pc_ref, w2t_ref, a2t_ref, g2t_ref,
                        o_ref, sout_ref,
                        dec_s, a_s, b_s, k_s, r_s, v_s, o_s,
                        *, n_tok, hpb):
    w0_ref, a0_ref, kk_ref, ka_ref, rk_ref, lnw_ref, lnb_ref = [pc_ref.at[j] for j in range(7)]
    ns = LANES
    ch = hpb * RW_HEAD

    def lanes(x):
        return jnp.concatenate([x] * n_tok, axis=1)

    def shifted(z_ref, sh_ref, mu_ref):
        z = z_ref[...]
        prev = jnp.concatenate([sh_ref[...], z[:, :(n_tok - 1) * ns]], axis=1)
        return z + lanes(mu_ref[...]) * (prev - z)

    r = shifted(zr_ref, shr_ref, mur_ref)
    k = shifted(zk_ref, shk_ref, muk_ref)
    v = shifted(zv_ref, shv_ref, muv_ref)
    zl = shifted(zl_ref, shl_ref, mul_ref)
    wd = zl[0:DECAY_LORA]
    ad = zl[DECAY_LORA:DECAY_LORA + AAA_LORA]
    gd = zl[DECAY_LORA + AAA_LORA:DECAY_LORA + AAA_LORA + GATE_LORA]

    wlog = -jax.nn.softplus(-(lanes(w0_ref[...]) + _mm(w2t_ref[...], jnp.tanh(wd)))) - 0.5
    lw = -jnp.exp(wlog)
    a = jax.nn.sigmoid(lanes(a0_ref[...]) + _mm(a2t_ref[...], ad))
    g = _mm(g2t_ref[...], jax.nn.sigmoid(gd))
    kk = k * lanes(kk_ref[...])
    km = k * (1.0 + (a - 1.0) * lanes(ka_ref[...]))

    def head_sum(x):
        x3 = x.reshape(hpb, RW_HEAD, n_tok * ns)
        s = jnp.sum(x3, axis=1, keepdims=True)
        return jnp.broadcast_to(s, x3.shape).reshape(ch, n_tok * ns)

    kkn = kk * lax.rsqrt(jnp.maximum(head_sum(kk * kk), 1e-24))
    dec_s[...] = jnp.exp(lw)
    a_s[...] = -kkn
    b_s[...] = kkn * a
    k_s[...] = km
    r_s[...] = r
    v_s[...] = v

    for hh in range(hpb):
        hrows = slice(hh * RW_HEAD, (hh + 1) * RW_HEAD)

        def body(i8, carry, hh=hh, hrows=hrows):
            base = pl.multiple_of(i8 * 8, 8)
            rows8 = pl.ds(pl.multiple_of(hh * RW_HEAD + base, 8), 8)
            o_rows = [[] for _ in range(n_tok)]
            for j in range(8):
                s = s0_ref[hh, base + j]
                for t in range(n_tok):
                    tl = slice(t * ns, (t + 1) * ns)
                    sa = jnp.sum(s * a_s[hrows, tl], axis=0, keepdims=True)
                    v_row = v_s[rows8, tl][j:j + 1, :]
                    s = s * dec_s[hrows, tl] + sa * b_s[hrows, tl] + v_row * k_s[hrows, tl]
                    o_rows[t].append(jnp.sum(s * r_s[hrows, tl], axis=0, keepdims=True))
                sout_ref[hh, base + j] = s
            for t in range(n_tok):
                o_s[rows8, t * ns:(t + 1) * ns] = jnp.concatenate(o_rows[t], axis=0)
            return carry

        lax.fori_loop(0, RW_HEAD // 8, body, 0)

    o = o_s[...]
    inv_n = 1.0 / RW_HEAD
    mean = head_sum(o) * inv_n
    var = head_sum(jnp.square(o - mean)) * inv_n
    on = ((o - mean) * lax.rsqrt(var + RW_GN_EPS)) * lanes(lnw_ref[...]) + lanes(lnb_ref[...])
    bonus = head_sum(r * km * lanes(rk_ref[...])) * v
    out = (on + bonus) * g
    for t in range(n_tok):
        o_ref[t] = out[:, t * ns:(t + 1) * ns].T.astype(BF16)


def _rwkv_sample_mixer(z_t, shift_t, mu_t, state_t, p_t, *, n_tok, hpb):
    ch = hpb * RW_HEAD
    n_steps = RW_HEADS // hpb
    nl = n_tok * LANES
    seg = RW_WIDTH // ch
    zspec = lambda s: pl.BlockSpec((ch, nl), lambda h: (s * seg + h, 0))
    cspec = lambda s: pl.BlockSpec((ch, LANES), lambda h: (s * seg + h, 0))
    lora_blk = 3 * RW_WIDTH // LORA_PAD
    zl_spec = pl.BlockSpec((LORA_PAD, nl), lambda h: (lora_blk, 0))
    cl_spec = pl.BlockSpec((LORA_PAD, LANES), lambda h: (lora_blk, 0))
    pc_spec = pl.BlockSpec((p_t["cols"].shape[0], ch, LANES), lambda h: (0, h, 0))
    wspec = lambda k: pl.BlockSpec((ch, k), lambda h: (h, 0))
    sspec = pl.BlockSpec((hpb, RW_HEAD, RW_HEAD, LANES), lambda h: (h, 0, 0, 0))
    buf = lambda: pltpu.VMEM((ch, nl), F32)
    return pl.pallas_call(
        functools.partial(_rwkv_sample_kernel, n_tok=n_tok, hpb=hpb),
        grid=(n_steps,),
        in_specs=[zspec(0), zspec(1), zspec(2), zl_spec,
                  cspec(0), cspec(1), cspec(2), cl_spec,
                  cspec(0), cspec(1), cspec(2), cl_spec,
                  sspec,
                  pc_spec, wspec(DECAY_LORA), wspec(AAA_LORA), wspec(GATE_LORA)],
        out_specs=[pl.BlockSpec((n_tok, LANES, ch), lambda h: (0, 0, h)), sspec],
        out_shape=[jax.ShapeDtypeStruct((n_tok, LANES, RW_WIDTH), BF16),
                   jax.ShapeDtypeStruct(state_t.shape, F32)],
        scratch_shapes=[buf() for _ in range(7)],
        compiler_params=_cparams(("parallel",)),
        name="rwkv7_sample",
    )(z_t, z_t, z_t, z_t, shift_t, shift_t, shift_t, shift_t, mu_t, mu_t, mu_t, mu_t, state_t,
      p_t["cols"], p_t["w2t"], p_t["a2t"], p_t["g2t"])


def _retention_kernel(zq_ref, zk_ref, zv_ref, zg_ref, cos_ref, sin_ref, dmask_ref, iscale_ref,
                      kscale_ref, sdec_ref, sel_ref, s0_ref, o_ref, sout_ref, st_s,
                      *, rows, c, n_real, per_chunk_state, pair_split):
    i = pl.program_id(1)
    n_chunks = rows // c
    assert not (pair_split and per_chunk_state)

    if per_chunk_state:
        pr = lax.broadcasted_iota(jnp.int32, (rows, n_chunks * n_real), 0)
        pc = lax.broadcasted_iota(jnp.int32, (rows, n_chunks * n_real), 1)
        place = jnp.where((pr // c == pc // n_real) & (pr % c - (c - n_real) == pc % n_real),
                          1.0, 0.0).astype(BF16)
        load = lambda ref: _mm_exact_lhs(place, ref[...])
    else:
        load = lambda ref: ref[...]

        @pl.when(i == 0)
        def _():
            if pair_split:
                for h in range(RET_HEADS):
                    st_s[h] = _mm_exact_lhs(sel_ref[...], s0_ref[0, h])
            else:
                st_s[...] = s0_ref[0]

    if pair_split:
        cos = cos_ref[...]
        sin = sin_ref[...]
        half = RET_HEAD // 2

        def rot(x):
            parts = []
            for h in range(RET_HEADS):
                x0 = x[:, h * RET_HEAD:h * RET_HEAD + half]
                x1 = x[:, h * RET_HEAD + half:(h + 1) * RET_HEAD]
                parts += [x0 * cos - x1 * sin, x0 * sin + x1 * cos]
            return jnp.concatenate(parts, axis=-1)
    else:
        lane = lax.broadcasted_iota(jnp.int32, (rows, RET_WIDTH), 1)
        even = (lane % 2) == 0
        cos = jnp.concatenate([cos_ref[...]] * RET_HEADS, axis=-1)
        sin = jnp.concatenate([sin_ref[...]] * RET_HEADS, axis=-1)

        def rot(x):
            partner = jnp.where(even, pltpu.roll(x, RET_WIDTH - 1, axis=1), pltpu.roll(x, 1, axis=1))
            return x * cos + partner * sin

    q = rot(load(zq_ref))
    k = rot(load(zk_ref)) * (RET_HEAD ** -0.5)
    v = load(zv_ref)
    g = load(zg_ref)

    out_rows = []
    for ch in range(n_chunks):
        rs = slice(ch * c, (ch + 1) * c)
        out_heads = []
        for h in range(RET_HEADS):
            hs = slice(h * RET_HEAD, (h + 1) * RET_HEAD)
            qh, kh, vh = q[rs, hs], k[rs, hs], v[rs, hs]
            s_prev = s0_ref[ch, h] if per_chunk_state else st_s[h]
            scores = _mm_nt(qh, kh) * dmask_ref[h]
            o = _mm(scores, vh) + _mm(qh, s_prev) * iscale_ref[h]
            s_new = s_prev * sdec_ref[h] + _mm_tn(kh * kscale_ref[h], vh)
            if per_chunk_state:
                sout_ref[ch, h] = s_new
            else:
                st_s[h] = s_new
            o = o * lax.rsqrt(jnp.mean(o * o, axis=-1, keepdims=True) + RET_GN_EPS)
            gh = g[rs, hs]
            out_heads.append(o * (gh * jax.nn.sigmoid(gh)))
        out_rows.append(jnp.concatenate(out_heads, axis=-1))
    o_out = jnp.concatenate(out_rows, axis=0).astype(BF16)
    if per_chunk_state:
        o_out = lax.dot_general(place, o_out, (((0,), (0,)), ((), ())),
                                preferred_element_type=F32).astype(BF16)
    o_ref[...] = o_out

    if not per_chunk_state:
        @pl.when(i == pl.num_programs(1) - 1)
        def _():
            if pair_split:
                for h in range(RET_HEADS):
                    sout_ref[0, h] = _mm_exact_lhs_t(sel_ref[...], st_s[h])
            else:
                sout_ref[0] = st_s[...]


def _retention_mixer(z, row_block0, n_groups, n_tiles, cos, sin, tabs, s0, *, rows, c, n_real,
                     per_chunk_state, pos_per_tile, pair_split=False):
    shared_s0 = s0.shape[0] == 1
    sb = 1 if shared_s0 else s0.shape[0] // n_groups
    kern = functools.partial(_retention_kernel, rows=rows, c=c, n_real=n_real,
                             per_chunk_state=per_chunk_state, pair_split=pair_split)
    rows_io = rows // c * n_real if per_chunk_state else rows
    zspec = lambda col: pl.BlockSpec((rows_io, RET_WIDTH), lambda gi, i: (row_block0 + gi * n_tiles + i, col))
    full = lambda arr: pl.BlockSpec(arr.shape, lambda gi, i: (0,) * arr.ndim)
    tbl_row0, tbl_advance = pos_per_tile
    assert tbl_row0 % rows == 0
    tspec = pl.BlockSpec((rows, cos.shape[1]),
                         lambda gi, i: (tbl_row0 // rows + (i if tbl_advance else 0), 0))
    state_blk = (sb, RET_HEADS, RET_HEAD, RET_HEAD)
    s0_spec = pl.BlockSpec(state_blk, lambda gi, i: (0 if shared_s0 else gi, 0, 0, 0))
    sout_spec = pl.BlockSpec(state_blk, lambda gi, i: (gi, 0, 0, 0))
    dmask, iscale, kscale, sdec = tabs
    sel = jnp.asarray(_pair_split_index(RET_HEAD)[:, None] == np.arange(RET_HEAD)[None, :], dtype=BF16)
    return pl.pallas_call(
        kern,
        grid=(n_groups, n_tiles),
        in_specs=[zspec(0), zspec(1), zspec(2), zspec(3), tspec, tspec,
                  full(dmask), full(iscale), full(kscale), full(sdec), full(sel), s0_spec],
        out_specs=[pl.BlockSpec((rows_io, RET_WIDTH), lambda gi, i: (gi * n_tiles + i, 0)), sout_spec],
        out_shape=[jax.ShapeDtypeStruct((n_groups * n_tiles * rows_io, RET_WIDTH), BF16),
                   jax.ShapeDtypeStruct((n_groups * sb,) + state_blk[1:], F32)],
        scratch_shapes=[pltpu.VMEM((RET_HEADS, RET_HEAD, RET_HEAD), F32)],
        compiler_params=_cparams(("parallel", "arbitrary")),
        name="retention_mixer",
    )(z, z, z, z, cos, sin, dmask, iscale, kscale, sdec, sel, s0)


def _retention_tables(c, n_real):
    log_gamma = np.log(1.0 - 2.0 ** (-5.0 - np.arange(RET_HEADS, dtype=np.float64)))
    r = np.arange(c, dtype=np.float64)
    idx = r - float(c - n_real)
    diff = r[:, None] - r[None, :]
    dmask = np.where(diff[None] >= 0, np.exp(log_gamma[:, None, None] * np.maximum(diff, 0.0)[None]), 0.0)
    iscale = np.exp(log_gamma[:, None] * (idx + 1.0)[None, :])[:, :, None]
    kscale = np.exp(log_gamma[:, None] * (n_real - 1.0 - idx)[None, :])[:, :, None]
    sdec = np.broadcast_to(np.exp(log_gamma * n_real)[:, None, None], (RET_HEADS, 1, RET_HEAD))
    return tuple(jnp.asarray(t, dtype=F32) for t in (dmask, iscale, kscale, sdec))


def _rotary_tables(pos, interleaved=True):
    inv_freq = 1.0 / (ROPE_BASE ** np.linspace(0.0, 1.0, RET_HEAD // 2))
    ang = np.asarray(pos, dtype=np.float64)[:, None] * inv_freq[None, :]
    cos = np.cos(ang)
    sin = np.sin(ang)
    if not interleaved:
        return jnp.asarray(cos, dtype=F32), jnp.asarray(sin, dtype=F32)
    cos2 = np.repeat(cos, 2, axis=-1)
    sin2 = np.stack([-sin, sin], axis=-1).reshape(len(pos), RET_HEAD)
    return jnp.asarray(cos2, dtype=F32), jnp.asarray(sin2, dtype=F32)


def _outproj_kernel(oa_ref, ob_ref, w_ref, x_ref, g_ref, h_ref, hn_ref):
    acc = jnp.dot(oa_ref[...], w_ref[0:RW_WIDTH, :], preferred_element_type=F32)
    acc = acc + jnp.dot(ob_ref[...], w_ref[RW_WIDTH:, :], preferred_element_type=F32)
    h = x_ref[...] + acc
    h_ref[...] = h
    hn_ref[...] = _rms_norm_bf16(h, g_ref[...])


def _out_projection(o_a, o_b, w_out, x2d, norm_g, tm):
    m = o_a.shape[0]
    return pl.pallas_call(
        _outproj_kernel,
        grid=(m // tm,),
        in_specs=[
            pl.BlockSpec((tm, RW_WIDTH), lambda i: (i, 0)),
            pl.BlockSpec((tm, RET_WIDTH), lambda i: (i, 0)),
            pl.BlockSpec((D_MODEL, D_MODEL), lambda i: (0, 0)),
            pl.BlockSpec((tm, D_MODEL), lambda i: (i, 0)),
            pl.BlockSpec((1, D_MODEL), lambda i: (0, 0)),
        ],
        out_specs=[pl.BlockSpec((tm, D_MODEL), lambda i: (i, 0)),
                   pl.BlockSpec((tm, D_MODEL), lambda i: (i, 0))],
        out_shape=[jax.ShapeDtypeStruct((m, D_MODEL), F32), jax.ShapeDtypeStruct((m, D_MODEL), BF16)],
        compiler_params=_cparams(("parallel",)),
        name="out_projection",
    )(o_a, o_b, w_out, x2d, norm_g)


def _ffn_kernel(hn_ref, wg_ref, wu_ref, wd_ref, h_ref, g_ref, y_ref, acc_ref, *, h_slices):
    f = pl.program_id(1)

    @pl.when(f == 0)
    def _():
        acc_ref[...] = jnp.zeros_like(acc_ref)

    @pl.when(f < h_slices)
    def _():
        rows = h_ref.shape[0]
        sl = pl.ds(pl.multiple_of(f * rows, rows), rows)
        acc_ref[sl, :] += h_ref[...]

    hn = hn_ref[...]
    gate = jnp.dot(hn, wg_ref[...], preferred_element_type=F32)
    up = jnp.dot(hn, wu_ref[...], preferred_element_type=F32)
    act = (gate * jax.nn.sigmoid(gate)) * up
    acc_ref[...] += jnp.dot(act.astype(BF16), wd_ref[...], preferred_element_type=F32)

    @pl.when(f == pl.num_programs(1) - 1)
    def _():
        h = acc_ref[...]
        ms = jnp.mean(h * h, axis=-1, keepdims=True)
        y_ref[...] = (h * lax.rsqrt(ms + RMS_EPS)) * g_ref[...]


def _ffn(hn, w_gate, w_up, w_down, h, norm_g, tm, tf, h_slices=8):
    m = hn.shape[0]
    assert D_FF // tf >= h_slices
    return pl.pallas_call(
        functools.partial(_ffn_kernel, h_slices=h_slices),
        grid=(m // tm, D_FF // tf),
        in_specs=[
            pl.BlockSpec((tm, D_MODEL), lambda i, f: (i, 0)),
            pl.BlockSpec((D_MODEL, tf), lambda i, f: (0, f)),
            pl.BlockSpec((D_MODEL, tf), lambda i, f: (0, f)),
            pl.BlockSpec((tf, D_MODEL), lambda i, f: (f, 0)),
            pl.BlockSpec((tm // h_slices, D_MODEL),
                         lambda i, f: (i * h_slices + jnp.minimum(f, h_slices - 1), 0)),
            pl.BlockSpec((1, D_MODEL), lambda i, f: (0, 0)),
        ],
        out_specs=pl.BlockSpec((tm, D_MODEL), lambda i, f: (i, 0)),
        out_shape=jax.ShapeDtypeStruct((m, D_MODEL), F32),
        scratch_shapes=[pltpu.VMEM((tm, D_MODEL), F32)],
        compiler_params=_cparams(("parallel", "arbitrary")),
        name="swiglu_ffn",
    )(hn, w_gate, w_up, w_down, h, norm_g)


def _pad_cols(a, n):
    return jnp.pad(a, ((0, 0), (0, n - a.shape[1])))


def _pad_rows(a, n):
    return jnp.pad(a, ((0, n - a.shape[0]), (0, 0)))


def kernel(x_prompt, x_sample, state_shift, state_rwkv, state_ret, meta_tokens, norm_mix, w_in,
           rwkv_mu, rwkv_w0, rwkv_w2, rwkv_a0, rwkv_a2, rwkv_g2, rwkv_kk, rwkv_ka, rwkv_rk,
           rwkv_ln_w, rwkv_ln_b, w_out, norm_ffn, w_gate, w_up, w_down, norm_final):
    n_b, seq, d = x_prompt.shape
    n_s, dec_seq, _ = x_sample.shape
    n_p = n_b * seq
    n_d = n_s * dec_seq
    depth = w_in.shape[0]
    assert depth == 1 and d == D_MODEL and n_s == LANES

    w_in_t = jnp.transpose(w_in[0])
    w_in_p = _w_in_layout(w_in_t, tn=512)
    row = lambda a: a.reshape(1, -1).astype(F32)
    w2p = jnp.concatenate([rwkv_w2[0], jnp.zeros((128 - DECAY_LORA, RW_WIDTH), F32)], axis=0)
    a2p = jnp.concatenate([jnp.zeros((DECAY_LORA, RW_WIDTH), F32), rwkv_a2[0]], axis=0)
    g2p = jnp.concatenate([rwkv_g2[0], jnp.zeros((256 - GATE_LORA, RW_WIDTH), F32)], axis=0)
    rw_params = dict(mu=_pad_cols(row(rwkv_mu[0]), SHIFT_PAD), w0=row(rwkv_w0[0]), w2=w2p,
                     a0=row(rwkv_a0[0]), a2=a2p, g2=g2p, kk=row(rwkv_kk[0]), ka=row(rwkv_ka[0]),
                     rk=row(rwkv_rk[0]), lnw=row(rwkv_ln_w[0]), lnb=row(rwkv_ln_b[0]))
    col = lambda a: jnp.broadcast_to(a.reshape(-1, 1).astype(F32), (a.size, LANES))
    cols = jnp.stack([a.reshape(-1).astype(F32) for a in (rwkv_w0[0], rwkv_a0[0], rwkv_kk[0], rwkv_ka[0],
                                                         rwkv_rk[0], rwkv_ln_w[0], rwkv_ln_b[0])])
    rw_params_t = dict(cols=jnp.broadcast_to(cols[:, :, None], cols.shape + (LANES,)),
                       w2t=rwkv_w2[0].T, a2t=rwkv_a2[0].T, g2t=rwkv_g2[0].T)

    x_p = x_prompt.reshape(n_p, d)
    x_s = x_sample.reshape(n_d, d)
    x_sm = jnp.concatenate([x_s, meta_tokens.astype(F32)], axis=0)
    x_ts = jnp.transpose(x_sample, (1, 0, 2)).reshape(n_d, d)
    g_mix = row(norm_mix[0])
    z_p = _in_projection(x_p, g_mix, w_in_p, tm=1024, tn=1536)
    z_sm = _in_projection(x_sm, g_mix, w_in_p, tm=x_sm.shape[0], tn=512, qk_interleaved=True)
    z_st = _in_projection_t(w_in_t, x_ts, g_mix, SHIFT_PAD, tm=512)

    z_meta = jnp.pad(z_sm[n_d:], ((RET_CHUNK - N_META, 0), (0, 0)))
    srow = 2 * dec_seq

    zero_prev = jnp.zeros((1, 1, SHIFT_PAD), F32)
    zero_rw = jnp.zeros((1, RW_HEADS, RW_HEAD, RW_HEAD), F32)
    c_rw = 64
    _, s_rw_meta, zlast_meta = _rwkv_mixer(z_meta, 1, 1, 1, zero_prev, zero_rw, rw_params, tb=c_rw, c=c_rw)
    oa_p, rwkv_p, zlast_p, w_out_b, w_gate_b, w_up_b, w_down_b = _rwkv_mixer(
        z_p, 0, n_b, seq // 256, zlast_meta, s_rw_meta, rw_params, tb=256, c=c_rw,
        cast=(w_out[0], w_gate[0], w_up[0], w_down[0]))
    shift_t = _pad_rows(jnp.transpose(state_shift[0]), SHIFT_PAD)
    mu_t = _pad_rows(col(rwkv_mu[0]), SHIFT_PAD)
    state_t = jnp.transpose(state_rwkv[0], (1, 2, 3, 0))
    oa_st, rwkv_st = _rwkv_sample_mixer(z_st, shift_t, mu_t, state_t, rw_params_t, n_tok=dec_seq, hpb=2)
    oa_s = jnp.transpose(oa_st, (1, 0, 2)).reshape(n_d, RW_WIDTH)
    rwkv_s = jnp.transpose(rwkv_st, (3, 0, 1, 2))

    past_len = 16384
    tabs_full = _retention_tables(RET_CHUNK, RET_CHUNK)
    tabs_meta = _retention_tables(RET_CHUNK, N_META)
    tabs_smp = _retention_tables(srow, dec_seq)
    sb_rt = 8
    cos_p, sin_p = _rotary_tables(N_META + np.arange(seq), interleaved=False)
    pos_ms = np.concatenate([np.arange(RET_CHUNK) - (RET_CHUNK - N_META),
                             past_len + np.tile(np.arange(srow) - (srow - dec_seq), sb_rt)])
    cos_ms, sin_ms = _rotary_tables(pos_ms)
    zero_rt = jnp.zeros((1, RET_HEADS, RET_HEAD, RET_HEAD), F32)
    _, s_rt_meta = _retention_mixer(z_meta, 0, 1, 1, cos_ms, sin_ms, tabs_meta, zero_rt,
                                    rows=RET_CHUNK, c=RET_CHUNK, n_real=N_META, per_chunk_state=False,
                                    pos_per_tile=(0, False))
    rows_rt = 2 * RET_CHUNK
    ob_p, ret_p = _retention_mixer(z_p, 0, n_b, seq // rows_rt, cos_p, sin_p, tabs_full, s_rt_meta,
                                   rows=rows_rt, c=RET_CHUNK, n_real=RET_CHUNK, per_chunk_state=False,
                                   pos_per_tile=(0, True), pair_split=True)
    ob_s, ret_s = _retention_mixer(z_sm, 0, n_s // sb_rt, 1, cos_ms, sin_ms, tabs_smp,
                                   state_ret.reshape(n_s, RET_HEADS, RET_HEAD, RET_HEAD),
                                   rows=sb_rt * srow, c=srow, n_real=dec_seq, per_chunk_state=True,
                                   pos_per_tile=(RET_CHUNK, False))

    g_ffn = row(norm_ffn[0])
    g_fin = row(norm_final)
    h_p, hn_p = _out_projection(oa_p, ob_p, w_out_b, x_p, g_ffn, tm=512)
    h_s, hn_s = _out_projection(oa_s, ob_s, w_out_b, x_s, g_ffn, tm=n_d)
    y_p = _ffn(hn_p, w_gate_b, w_up_b, w_down_b, h_p, g_fin, tm=1024, tf=512)
    y_s = _ffn(hn_s, w_gate_b, w_up_b, w_down_b, h_s, g_fin, tm=n_d, tf=512)

    y_prompt = y_p.reshape(n_b, seq, d)
    y_sample = y_s.reshape(n_s, dec_seq, d)
    shift_p = zlast_p[:, 0, :RW_COLS][None]
    shift_s = jnp.transpose(z_st[:RW_COLS, (dec_seq - 1) * n_s:])[None]
    return (y_prompt, y_sample, shift_p, rwkv_p[None], ret_p[None], shift_s, rwkv_s[None], ret_s[None])
```

```python
import functools
import math

import jax
import jax.numpy as jnp
import numpy as np
from jax import lax
from jax.experimental import pallas as pl
from jax.experimental.pallas import tpu as pltpu

F32 = jnp.float32
BF16 = jnp.bfloat16

D_MODEL = 2048
N_META = 16
RW_WIDTH = 1024
RW_HEAD = 64
RW_HEADS = 16
DECAY_LORA = 64
AAA_LORA = 64
GATE_LORA = 160
RW_COLS = 3 * RW_WIDTH + DECAY_LORA + AAA_LORA + GATE_LORA
RET_WIDTH = 1024
RET_HEADS = 4
RET_HEAD = 256
RET_CHUNK = 128
D_FF = 5632
RMS_EPS = 1e-6
RW_GN_EPS = 64e-5
RET_GN_EPS = 1e-6
ROPE_BASE = 10000.0
LANES = 128

LORA_PAD = 512
Z_COLS = 4 * RET_WIDTH + 3 * RW_WIDTH + LORA_PAD
RW_OFF = 4 * RET_WIDTH
SHIFT_PAD = 3 * RW_WIDTH + LORA_PAD

VMEM_LIMIT = 60 * 1024 * 1024


def _cparams(sem):
    return pltpu.CompilerParams(dimension_semantics=sem, vmem_limit_bytes=VMEM_LIMIT)


def _mm(a, b):
    return jnp.dot(a.astype(BF16), b.astype(BF16), preferred_element_type=F32)


def _mm_nt(a, b):
    return lax.dot_general(a.astype(BF16), b.astype(BF16), (((1,), (1,)), ((), ())),
                           preferred_element_type=F32)


def _mm_tn(a, b):
    return lax.dot_general(a.astype(BF16), b.astype(BF16), (((0,), (0,)), ((), ())),
                           preferred_element_type=F32)


def _bmm(a, b):
    return jnp.einsum("bmk,bkn->bmn", a.astype(BF16), b.astype(BF16), preferred_element_type=F32)


def _bmm_nt(a, b):
    return jnp.einsum("bmk,bnk->bmn", a.astype(BF16), b.astype(BF16), preferred_element_type=F32)


def _bmm_tn(a, b):
    return jnp.einsum("bkm,bkn->bmn", a.astype(BF16), b.astype(BF16), preferred_element_type=F32)


def _mm_exact_lhs(m_bf16, x):
    hi = x.astype(BF16)
    r1 = x - hi.astype(F32)
    mid = r1.astype(BF16)
    lo = (r1 - mid.astype(F32)).astype(BF16)
    d = functools.partial(jnp.dot, preferred_element_type=F32)
    return d(m_bf16, hi) + d(m_bf16, mid) + d(m_bf16, lo)


def _mm_exact_lhs_t(m_bf16, x):
    hi = x.astype(BF16)
    r1 = x - hi.astype(F32)
    mid = r1.astype(BF16)
    lo = (r1 - mid.astype(F32)).astype(BF16)
    d = functools.partial(lax.dot_general, dimension_numbers=(((0,), (0,)), ((), ())),
                          preferred_element_type=F32)
    return d(m_bf16, hi) + d(m_bf16, mid) + d(m_bf16, lo)


def _rms_norm_bf16(x, g):
    ms = jnp.mean(x * x, axis=-1, keepdims=True)
    return ((x * lax.rsqrt(ms + RMS_EPS)) * g).astype(BF16)


def _pair_split_index(n):
    j = np.arange(n)
    half = RET_HEAD // 2
    return (j // RET_HEAD) * RET_HEAD + 2 * (j % half) + (j % RET_HEAD) // half


def _w_in_layout_kernel(w_ref, sel_ref, o_ref, *, n_qk):
    c = pl.program_id(0)

    @pl.when(c >= n_qk)
    def _():
        o_ref[...] = w_ref[...].T.astype(BF16)

    @pl.when(c < n_qk)
    def _():
        picked = jnp.dot(sel_ref[...], w_ref[...].astype(BF16), preferred_element_type=F32)
        o_ref[...] = picked.T.astype(BF16)


def _w_in_layout(w_in_t, tn):
    n, d = w_in_t.shape
    n_ret = 4 * RET_WIDTH // tn
    n_z = Z_COLS // tn
    n_qk = 2 * RET_WIDTH // tn
    sel = jnp.asarray(_pair_split_index(tn)[:, None] == np.arange(tn)[None, :], dtype=BF16)

    def src_row(c):
        align = math.gcd(RW_COLS, tn)
        row = jnp.where(c < n_ret, RW_COLS + tn * c,
                        jnp.where(c < n_z, tn * (c - n_ret), RW_COLS + tn * (c - n_z)))
        return (pl.multiple_of(row, align), 0)

    return pl.pallas_call(
        functools.partial(_w_in_layout_kernel, n_qk=n_qk),
        grid=(n_z + n_qk,),
        in_specs=[pl.BlockSpec((pl.Element(tn), pl.Element(d)), src_row),
                  pl.BlockSpec((tn, tn), lambda c: (0, 0))],
        out_specs=pl.BlockSpec((d, tn), lambda c: (0, c)),
        out_shape=jax.ShapeDtypeStruct((d, Z_COLS + 2 * RET_WIDTH), BF16),
        compiler_params=_cparams(("parallel",)),
        name="w_in_layout",
    )(w_in_t, sel)


def _inproj_kernel(x_ref, g_ref, w_ref, o_ref, xn_ref):
    @pl.when(pl.program_id(1) == 0)
    def _():
        xn_ref[...] = _rms_norm_bf16(x_ref[...], g_ref[...])

    o_ref[...] = jnp.dot(xn_ref[...], w_ref[...], preferred_element_type=F32)


def _in_projection(x2d, norm_g, w_in_p, tm, tn, qk_interleaved=False):
    m = x2d.shape[0]
    if qk_interleaved:
        n_qk = 2 * RET_WIDTH // tn
        w_col = lambda i, j: (0, jnp.where(j < n_qk, Z_COLS // tn + j, j))
    else:
        w_col = lambda i, j: (0, j)
    return pl.pallas_call(
        _inproj_kernel,
        grid=(m // tm, Z_COLS // tn),
        in_specs=[
            pl.BlockSpec((tm, D_MODEL), lambda i, j: (i, 0)),
            pl.BlockSpec((1, D_MODEL), lambda i, j: (0, 0)),
            pl.BlockSpec((D_MODEL, tn), w_col),
        ],
        out_specs=pl.BlockSpec((tm, tn), lambda i, j: (i, j)),
        out_shape=jax.ShapeDtypeStruct((m, Z_COLS), F32),
        scratch_shapes=[pltpu.VMEM((tm, D_MODEL), BF16)],
        compiler_params=_cparams(("parallel", "arbitrary")),
        name="in_projection",
    )(x2d, norm_g, w_in_p)


def _inproj_t_kernel(w_ref, x_ref, g_ref, o_ref, xn_ref):
    @pl.when(pl.program_id(0) == 0)
    def _():
        xn_ref[...] = _rms_norm_bf16(x_ref[...], g_ref[...])

    o_ref[...] = _mm_nt(w_ref[...], xn_ref[...])


def _in_projection_t(w_in_t, x2d, norm_g, n_rows, tm):
    m, d = x2d.shape
    return pl.pallas_call(
        _inproj_t_kernel,
        grid=(n_rows // tm,),
        in_specs=[
            pl.BlockSpec((tm, d), lambda i: (i, 0)),
            pl.BlockSpec((m, d), lambda i: (0, 0)),
            pl.BlockSpec((1, d), lambda i: (0, 0)),
        ],
        out_specs=pl.BlockSpec((tm, m), lambda i: (i, 0)),
        out_shape=jax.ShapeDtypeStruct((n_rows, m), F32),
        scratch_shapes=[pltpu.VMEM((m, d), BF16)],
        compiler_params=_cparams(("arbitrary",)),
        name="in_projection_t",
    )(w_in_t, x2d, norm_g)


def _rwkv_kernel(zr_ref, zk_ref, zv_ref, zl_ref, zprev0_ref, s0_ref,
                 mu_ref, w0_ref, w2_ref, a0_ref, a2_ref, g2_ref, kk_ref, ka_ref, rk_ref,
                 lnw_ref, lnb_ref,
                 o_ref, sout_ref, zlast_ref,
                 carry_ref, sbd_s, r_s, km_s, v_s, kk_s, a_s, lw_s, cum_s, o_s,
                 *, tb, c, n_factors):
    i = pl.program_id(1)
    n_chunks = tb // c
    n_pairs = RW_HEADS // 2
    pw = 2 * RW_HEAD

    @pl.when(i == 0)
    def _():
        carry_ref[...] = zprev0_ref[0]
        zero = jnp.zeros((RW_HEAD, RW_HEAD), F32)
        for p in range(n_pairs):
            top = jnp.concatenate([s0_ref[0, 2 * p], zero], axis=1)
            bot = jnp.concatenate([zero, s0_ref[0, 2 * p + 1]], axis=1)
            sbd_s[p] = jnp.concatenate([top, bot], axis=0)

    row = lax.broadcasted_iota(jnp.int32, (tb, 1), 0)

    def shifted(z_ref, lo, hi):
        z = z_ref[...]
        prev = pltpu.roll(z, 1, axis=0)
        prev = jnp.concatenate([jnp.where(row[:8] == 0, carry_ref[:, lo:hi], prev[:8]), prev[8:]], axis=0)
        zs = z + mu_ref[:, lo:hi] * (prev - z)
        carry_ref[:, lo:hi] = z[tb - 1:tb, :]
        return zs

    w = RW_WIDTH
    r = shifted(zr_ref, 0, w)
    k = shifted(zk_ref, w, 2 * w)
    v = shifted(zv_ref, 2 * w, 3 * w)
    zl = shifted(zl_ref, 3 * w, 3 * w + LORA_PAD)

    lo2 = zl[:, 0:128]
    wlog = -jax.nn.softplus(-(w0_ref[...] + _mm(jnp.tanh(lo2), w2_ref[...]))) - 0.5
    lw = -jnp.exp(wlog)
    a = jax.nn.sigmoid(a0_ref[...] + _mm(lo2, a2_ref[...]))
    g = _mm(jax.nn.sigmoid(zl[:, 128:384]), g2_ref[...])
    kk = k * kk_ref[...]
    km = k * (1.0 + (a - 1.0) * ka_ref[...])

    for dst, val in ((r_s, r), (km_s, km), (v_s, v), (a_s, a), (lw_s, lw), (kk_s, kk)):
        dst[...] = val

    rr = lax.broadcasted_iota(jnp.int32, (c, c), 0)
    cc = lax.broadcasted_iota(jnp.int32, (c, c), 1)
    tri = jnp.where(rr >= cc, 1.0, 0.0).astype(BF16)
    rr2 = lax.broadcasted_iota(jnp.int32, (c, 2 * c), 0)
    cc2 = lax.broadcasted_iota(jnp.int32, (c, 2 * c), 1)
    strict = (rr > cc)[None]
    eye = jnp.where(rr == cc, 1.0, 0.0).astype(F32)[None]
    strict_k = ((cc2 >= c) & (rr2 > cc2 - c))[None]
    incl_bk = (rr2 >= cc2 % c)[None]
    lo = (lax.broadcasted_iota(jnp.int32, (1, 1, pw), 2) < RW_HEAD)
    bd_r = lax.broadcasted_iota(jnp.int32, (pw, pw), 0) // RW_HEAD
    bd_c = lax.broadcasted_iota(jnp.int32, (pw, pw), 1) // RW_HEAD
    block_diag = (bd_r == bd_c)[None]
    rk = rk_ref[...]
    lnw = lnw_ref[...]
    lnb = lnb_ref[...]

    def head_sum(x):
        s_lo = jnp.sum(jnp.where(lo, x, 0.0), axis=-1, keepdims=True)
        s_hi = jnp.sum(jnp.where(lo, 0.0, x), axis=-1, keepdims=True)
        return jnp.where(lo, s_lo, s_hi)

    def pick(x16):
        x4 = x16.reshape(n_pairs, 2, c, pw)
        return jnp.where(lo, x4[:, 0], x4[:, 1])

    def body(ch):
        rows = pl.ds(ch * c, c)

        def pairs(s):
            x = s[rows, :]
            return jnp.stack([x[:, p * pw:(p + 1) * pw] for p in range(n_pairs)], axis=0)

        cum_s[rows, :] = _mm_exact_lhs(tri, lw_s[rows, :])
        rh, kmh, vh, ah, lwh, cumh, kkh = [pairs(s) for s in (r_s, km_s, v_s, a_s, lw_s, cum_s, kk_s)]
        s_prev = sbd_s[...]
        kkn = kkh * lax.rsqrt(jnp.maximum(head_sum(kkh * kkh), 1e-24))
        bvec = kkn * ah
        cum_end = cumh[:, c - 1:c, :]
        p_inc = jnp.exp(cumh)
        p_exc = jnp.exp(cumh - lwh)
        p_inv = jnp.exp(-cumh)
        p_end = jnp.exp(cum_end)
        p_rel = jnp.exp(cum_end - cumh)
        a_t = -(kkn * p_exc)
        r_t = rh * p_inc
        b_t = bvec * p_inv
        k_t = kmh * p_inv
        b_h = bvec * p_rel
        k_h = kmh * p_rel

        ar = jnp.concatenate([jnp.where(lo, a_t, 0.0), jnp.where(lo, r_t, 0.0),
                              jnp.where(lo, 0.0, a_t), jnp.where(lo, 0.0, r_t)], axis=1)
        prod = _bmm_nt(ar, jnp.concatenate([b_t, k_t, s_prev], axis=1))
        prod = prod.reshape(RW_HEADS, 2 * c, 2 * c + pw)
        a_rows, r_rows = prod[:, :c], prod[:, c:]
        aak_wide = jnp.where(strict_k, a_rows[:, :, :2 * c], 0.0)
        arbk = jnp.where(incl_bk, r_rows[:, :, :2 * c], 0.0)
        lmat = jnp.where(strict, a_rows[:, :, :c], 0.0)
        tinv = eye + lmat
        if n_factors > 1:
            lp = _bmm(lmat, lmat)
            for _ in range(n_factors - 2):
                st = _bmm(jnp.concatenate([lp, tinv], axis=1), lp)
                lp = st[:, :c]
                tinv = tinv + st[:, c:]
            tinv = tinv + _bmm(tinv, lp)
        v16 = jnp.repeat(vh, 2, axis=0)
        x = a_rows[:, :, 2 * c:] + _bmm(aak_wide, jnp.concatenate([v16, v16], axis=1))
        u16 = _bmm(tinv, x)
        o16 = r_rows[:, :, 2 * c:] + _bmm(arbk, jnp.concatenate([u16, v16], axis=1))
        u = pick(u16)
        o = pick(o16)
        s_upd = _bmm_tn(jnp.concatenate([u, vh], axis=1), jnp.concatenate([b_h, k_h], axis=1))
        sbd_s[...] = s_prev * p_end + jnp.where(block_diag, s_upd, 0.0)
        inv_n = 1.0 / RW_HEAD
        mean = head_sum(o) * inv_n
        var = head_sum(jnp.square(o - mean)) * inv_n
        on = ((o - mean) * lax.rsqrt(var + RW_GN_EPS)) * lnw + lnb
        out = on + head_sum(rh * kmh * rk) * vh
        for p in range(n_pairs):
            o_s[rows, p * pw:(p + 1) * pw] = out[p]

    for ch in range(n_chunks):
        body(ch)

    o_ref[...] = (o_s[...] * g).astype(BF16)
    zlast_ref[0] = carry_ref[...]

    @pl.when(i == pl.num_programs(1) - 1)
    def _():
        for p in range(n_pairs):
            blk = sbd_s[p]
            sout_ref[0, 2 * p] = blk[:RW_HEAD, :RW_HEAD]
            sout_ref[0, 2 * p + 1] = blk[RW_HEAD:, RW_HEAD:]


N_RWKV_INPUTS = 17
N_RWKV_OUTPUTS = 3


def _rwkv_and_cast_kernel(*refs, n_cast, **kw):
    ins = refs[:N_RWKV_INPUTS]
    cast_in = refs[N_RWKV_INPUTS:N_RWKV_INPUTS + n_cast]
    o0 = N_RWKV_INPUTS + n_cast
    outs = refs[o0:o0 + N_RWKV_OUTPUTS]
    cast_out = refs[o0 + N_RWKV_OUTPUTS:o0 + N_RWKV_OUTPUTS + n_cast]
    scratch = refs[o0 + N_RWKV_OUTPUTS + n_cast:]
    for src, dst in zip(cast_in, cast_out):
        dst[...] = src[...].astype(BF16)
    _rwkv_kernel(*ins, *outs, *scratch, **kw)


def _rwkv_mixer(z, row_block0, n_groups, n_tiles, zprev0, s0, p, *, tb, c, cast=()):
    n_factors = max(1, math.ceil(math.log2(c)))
    kern = functools.partial(_rwkv_and_cast_kernel, n_cast=len(cast), tb=tb, c=c, n_factors=n_factors)
    n_steps = n_groups * n_tiles
    cast_specs = [pl.BlockSpec((a.shape[0] // n_steps, a.shape[1]), lambda gi, i: (gi * n_tiles + i, 0))
                  for a in cast]
    cast_shapes = [jax.ShapeDtypeStruct(a.shape, BF16) for a in cast]
    cb = RW_OFF // RW_WIDTH
    full = lambda arr: pl.BlockSpec(arr.shape, lambda gi, i: (0,) * arr.ndim)
    lora_cb = (RW_OFF + 3 * RW_WIDTH) // LORA_PAD
    per_pair = lambda a: a.reshape(RW_HEADS // 2, 1, 2 * RW_HEAD)
    params = [p["mu"], p["w0"], p["w2"], p["a0"], p["a2"], p["g2"], p["kk"], p["ka"],
              per_pair(p["rk"]), per_pair(p["lnw"]), per_pair(p["lnb"])]
    dense = lambda: pltpu.VMEM((tb, RW_WIDTH), F32)
    zspec = lambda width, col: pl.BlockSpec((tb, width), lambda gi, i: (row_block0 + gi * n_tiles + i, col))
    sspec = pl.BlockSpec((1, RW_HEADS, RW_HEAD, RW_HEAD), lambda gi, i: (gi, 0, 0, 0))
    shared = lambda a: (lambda gi, i: (0,) * a.ndim) if a.shape[0] == 1 else (lambda gi, i: (gi,) + (0,) * (a.ndim - 1))
    return pl.pallas_call(
        kern,
        grid=(n_groups, n_tiles),
        in_specs=[zspec(RW_WIDTH, cb), zspec(RW_WIDTH, cb + 1), zspec(RW_WIDTH, cb + 2),
                  zspec(LORA_PAD, lora_cb),
                  pl.BlockSpec((1, 1, SHIFT_PAD), shared(zprev0)),
                  pl.BlockSpec((1, RW_HEADS, RW_HEAD, RW_HEAD), shared(s0))]
                 + [full(a) for a in params] + cast_specs,
        out_specs=[pl.BlockSpec((tb, RW_WIDTH), lambda gi, i: (gi * n_tiles + i, 0)), sspec,
                   pl.BlockSpec((1, 1, SHIFT_PAD), lambda gi, i: (gi, 0, 0))] + cast_specs,
        out_shape=[jax.ShapeDtypeStruct((n_groups * n_tiles * tb, RW_WIDTH), BF16),
                   jax.ShapeDtypeStruct((n_groups, RW_HEADS, RW_HEAD, RW_HEAD), F32),
                   jax.ShapeDtypeStruct((n_groups, 1, SHIFT_PAD), F32)] + cast_shapes,
        scratch_shapes=[pltpu.VMEM((1, SHIFT_PAD), F32),
                        pltpu.VMEM((RW_HEADS // 2, 2 * RW_HEAD, 2 * RW_HEAD), F32)]
                       + [dense() for _ in range(8)],
        compiler_params=_cparams(("parallel", "arbitrary")),
        name="rwkv7_mixer",
    )(z, z, z, z, zprev0, s0, *params, *cast)


def _rwkv_sample_kernel(zr_ref, zk_ref, zv_ref, zl_ref, shr_ref, shk_ref, shv_ref, shl_ref,
                        mur_ref, muk_ref, muv_ref, mul_ref, s0_ref,
                        pc_ref, w2t_ref, a2t_ref, g2t_ref,
                        o_ref, sout_ref,
                        dec_s, a_s, b_s, k_s, r_s, v_s, o_s,
                        *, n_tok, hpb):
    w0_ref, a0_ref, kk_ref, ka_ref, rk_ref, lnw_ref, lnb_ref = [pc_ref.at[j] for j in range(7)]
    ns = LANES
    ch = hpb * RW_HEAD

    def lanes(x):
        return jnp.concatenate([x] * n_tok, axis=1)

    def shifted(z_ref, sh_ref, mu_ref):
        z = z_ref[...]
        prev = jnp.concatenate([sh_ref[...], z[:, :(n_tok - 1) * ns]], axis=1)
        return z + lanes(mu_ref[...]) * (prev - z)

    r = shifted(zr_ref, shr_ref, mur_ref)
    k = shifted(zk_ref, shk_ref, muk_ref)
    v = shifted(zv_ref, shv_ref, muv_ref)
    zl = shifted(zl_ref, shl_ref, mul_ref)
    wd = zl[0:DECAY_LORA]
    ad = zl[DECAY_LORA:DECAY_LORA + AAA_LORA]
    gd = zl[DECAY_LORA + AAA_LORA:DECAY_LORA + AAA_LORA + GATE_LORA]

    wlog = -jax.nn.softplus(-(lanes(w0_ref[...]) + _mm(w2t_ref[...], jnp.tanh(wd)))) - 0.5
    lw = -jnp.exp(wlog)
    a = jax.nn.sigmoid(lanes(a0_ref[...]) + _mm(a2t_ref[...], ad))
    g = _mm(g2t_ref[...], jax.nn.sigmoid(gd))
    kk = k * lanes(kk_ref[...])
    km = k * (1.0 + (a - 1.0) * lanes(ka_ref[...]))

    def head_sum(x):
        x3 = x.reshape(hpb, RW_HEAD, n_tok * ns)
        s = jnp.sum(x3, axis=1, keepdims=True)
        return jnp.broadcast_to(s, x3.shape).reshape(ch, n_tok * ns)

    kkn = kk * lax.rsqrt(jnp.maximum(head_sum(kk * kk), 1e-24))
    dec_s[...] = jnp.exp(lw)
    a_s[...] = -kkn
    b_s[...] = kkn * a
    k_s[...] = km
    r_s[...] = r
    v_s[...] = v

    for hh in range(hpb):
        hrows = slice(hh * RW_HEAD, (hh + 1) * RW_HEAD)

        def body(i8, carry, hh=hh, hrows=hrows):
            base = pl.multiple_of(i8 * 8, 8)
            rows8 = pl.ds(pl.multiple_of(hh * RW_HEAD + base, 8), 8)
            o_rows = [[] for _ in range(n_tok)]
            for j in range(8):
                s = s0_ref[hh, base + j]
                for t in range(n_tok):
                    tl = slice(t * ns, (t + 1) * ns)
                    sa = jnp.sum(s * a_s[hrows, tl], axis=0, keepdims=True)
                    v_row = v_s[rows8, tl][j:j + 1, :]
                    s = s * dec_s[hrows, tl] + sa * b_s[hrows, tl] + v_row * k_s[hrows, tl]
                    o_rows[t].append(jnp.sum(s * r_s[hrows, tl], axis=0, keepdims=True))
                sout_ref[hh, base + j] = s
            for t in range(n_tok):
                o_s[rows8, t * ns:(t + 1) * ns] = jnp.concatenate(o_rows[t], axis=0)
            return carry

        lax.fori_loop(0, RW_HEAD // 8, body, 0)

    o = o_s[...]
    inv_n = 1.0 / RW_HEAD
    mean = head_sum(o) * inv_n
    var = head_sum(jnp.square(o - mean)) * inv_n
    on = ((o - mean) * lax.rsqrt(var + RW_GN_EPS)) * lanes(lnw_ref[...]) + lanes(lnb_ref[...])
    bonus = head_sum(r * km * lanes(rk_ref[...])) * v
    out = (on + bonus) * g
    for t in range(n_tok):
        o_ref[t] = out[:, t * ns:(t + 1) * ns].T.astype(BF16)


def _rwkv_sample_mixer(z_t, shift_t, mu_t, state_t, p_t, *, n_tok, hpb):
    ch = hpb * RW_HEAD
    n_steps = RW_HEADS // hpb
    nl = n_tok * LANES
    seg = RW_WIDTH // ch
    zspec = lambda s: pl.BlockSpec((ch, nl), lambda h: (s * seg + h, 0))
    cspec = lambda s: pl.BlockSpec((ch, LANES), lambda h: (s * seg + h, 0))
    lora_blk = 3 * RW_WIDTH // LORA_PAD
    zl_spec = pl.BlockSpec((LORA_PAD, nl), lambda h: (lora_blk, 0))
    cl_spec = pl.BlockSpec((LORA_PAD, LANES), lambda h: (lora_blk, 0))
    pc_spec = pl.BlockSpec((p_t["cols"].shape[0], ch, LANES), lambda h: (0, h, 0))
    wspec = lambda k: pl.BlockSpec((ch, k), lambda h: (h, 0))
    sspec = pl.BlockSpec((hpb, RW_HEAD, RW_HEAD, LANES), lambda h: (h, 0, 0, 0))
    buf = lambda: pltpu.VMEM((ch, nl), F32)
    return pl.pallas_call(
        functools.partial(_rwkv_sample_kernel, n_tok=n_tok, hpb=hpb),
        grid=(n_steps,),
        in_specs=[zspec(0), zspec(1), zspec(2), zl_spec,
                  cspec(0), cspec(1), cspec(2), cl_spec,
                  cspec(0), cspec(1), cspec(2), cl_spec,
                  sspec,
                  pc_spec, wspec(DECAY_LORA), wspec(AAA_LORA), wspec(GATE_LORA)],
        out_specs=[pl.BlockSpec((n_tok, LANES, ch), lambda h: (0, 0, h)), sspec],
        out_shape=[jax.ShapeDtypeStruct((n_tok, LANES, RW_WIDTH), BF16),
                   jax.ShapeDtypeStruct(state_t.shape, F32)],
        scratch_shapes=[buf() for _ in range(7)],
        compiler_params=_cparams(("parallel",)),
        name="rwkv7_sample",
    )(z_t, z_t, z_t, z_t, shift_t, shift_t, shift_t, shift_t, mu_t, mu_t, mu_t, mu_t, state_t,
      p_t["cols"], p_t["w2t"], p_t["a2t"], p_t["g2t"])


def _retention_kernel(zq_ref, zk_ref, zv_ref, zg_ref, cos_ref, sin_ref, dmask_ref, iscale_ref,
                      kscale_ref, sdec_ref, sel_ref, s0_ref, o_ref, sout_ref, st_s,
                      *, rows, c, n_real, per_chunk_state, pair_split):
    i = pl.program_id(1)
    n_chunks = rows // c
    assert not (pair_split and per_chunk_state)

    if per_chunk_state:
        pr = lax.broadcasted_iota(jnp.int32, (rows, n_chunks * n_real), 0)
        pc = lax.broadcasted_iota(jnp.int32, (rows, n_chunks * n_real), 1)
        place = jnp.where((pr // c == pc // n_real) & (pr % c - (c - n_real) == pc % n_real),
                          1.0, 0.0).astype(BF16)
        load = lambda ref: _mm_exact_lhs(place, ref[...])
    else:
        load = lambda ref: ref[...]

        @pl.when(i == 0)
        def _():
            if pair_split:
                for h in range(RET_HEADS):
                    st_s[h] = _mm_exact_lhs(sel_ref[...], s0_ref[0, h])
            else:
                st_s[...] = s0_ref[0]

    if pair_split:
        cos = cos_ref[...]
        sin = sin_ref[...]
        half = RET_HEAD // 2

        def rot(x):
            parts = []
            for h in range(RET_HEADS):
                x0 = x[:, h * RET_HEAD:h * RET_HEAD + half]
                x1 = x[:, h * RET_HEAD + half:(h + 1) * RET_HEAD]
                parts += [x0 * cos - x1 * sin, x0 * sin + x1 * cos]
            return jnp.concatenate(parts, axis=-1)
    else:
        lane = lax.broadcasted_iota(jnp.int32, (rows, RET_WIDTH), 1)
        even = (lane % 2) == 0
        cos = jnp.concatenate([cos_ref[...]] * RET_HEADS, axis=-1)
        sin = jnp.concatenate([sin_ref[...]] * RET_HEADS, axis=-1)

        def rot(x):
            partner = jnp.where(even, pltpu.roll(x, RET_WIDTH - 1, axis=1), pltpu.roll(x, 1, axis=1))
            return x * cos + partner * sin

    q = rot(load(zq_ref))
    k = rot(load(zk_ref)) * (RET_HEAD ** -0.5)
    v = load(zv_ref)
    g = load(zg_ref)

    out_rows = []
    for ch in range(n_chunks):
        rs = slice(ch * c, (ch + 1) * c)
        out_heads = []
        for h in range(RET_HEADS):
            hs = slice(h * RET_HEAD, (h + 1) * RET_HEAD)
            qh, kh, vh = q[rs, hs], k[rs, hs], v[rs, hs]
            s_prev = s0_ref[ch, h] if per_chunk_state else st_s[h]
            scores = _mm_nt(qh, kh) * dmask_ref[h]
            o = _mm(scores, vh) + _mm(qh, s_prev) * iscale_ref[h]
            s_new = s_prev * sdec_ref[h] + _mm_tn(kh * kscale_ref[h], vh)
            if per_chunk_state:
                sout_ref[ch, h] = s_new
            else:
                st_s[h] = s_new
            o = o * lax.rsqrt(jnp.mean(o * o, axis=-1, keepdims=True) + RET_GN_EPS)
            gh = g[rs, hs]
            out_heads.append(o * (gh * jax.nn.sigmoid(gh)))
        out_rows.append(jnp.concatenate(out_heads, axis=-1))
    o_out = jnp.concatenate(out_rows, axis=0).astype(BF16)
    if per_chunk_state:
        o_out = lax.dot_general(place, o_out, (((0,), (0,)), ((), ())),
                                preferred_element_type=F32).astype(BF16)
    o_ref[...] = o_out

    if not per_chunk_state:
        @pl.when(i == pl.num_programs(1) - 1)
        def _():
            if pair_split:
                for h in range(RET_HEADS):
                    sout_ref[0, h] = _mm_exact_lhs_t(sel_ref[...], st_s[h])
            else:
                sout_ref[0] = st_s[...]


def _retention_mixer(z, row_block0, n_groups, n_tiles, cos, sin, tabs, s0, *, rows, c, n_real,
                     per_chunk_state, pos_per_tile, pair_split=False):
    shared_s0 = s0.shape[0] == 1
    sb = 1 if shared_s0 else s0.shape[0] // n_groups
    kern = functools.partial(_retention_kernel, rows=rows, c=c, n_real=n_real,
                             per_chunk_state=per_chunk_state, pair_split=pair_split)
    rows_io = rows // c * n_real if per_chunk_state else rows
    zspec = lambda col: pl.BlockSpec((rows_io, RET_WIDTH), lambda gi, i: (row_block0 + gi * n_tiles + i, col))
    full = lambda arr: pl.BlockSpec(arr.shape, lambda gi, i: (0,) * arr.ndim)
    tbl_row0, tbl_advance = pos_per_tile
    assert tbl_row0 % rows == 0
    tspec = pl.BlockSpec((rows, cos.shape[1]),
                         lambda gi, i: (tbl_row0 // rows + (i if tbl_advance else 0), 0))
    state_blk = (sb, RET_HEADS, RET_HEAD, RET_HEAD)
    s0_spec = pl.BlockSpec(state_blk, lambda gi, i: (0 if shared_s0 else gi, 0, 0, 0))
    sout_spec = pl.BlockSpec(state_blk, lambda gi, i: (gi, 0, 0, 0))
    dmask, iscale, kscale, sdec = tabs
    sel = jnp.asarray(_pair_split_index(RET_HEAD)[:, None] == np.arange(RET_HEAD)[None, :], dtype=BF16)
    return pl.pallas_call(
        kern,
        grid=(n_groups, n_tiles),
        in_specs=[zspec(0), zspec(1), zspec(2), zspec(3), tspec, tspec,
                  full(dmask), full(iscale), full(kscale), full(sdec), full(sel), s0_spec],
        out_specs=[pl.BlockSpec((rows_io, RET_WIDTH), lambda gi, i: (gi * n_tiles + i, 0)), sout_spec],
        out_shape=[jax.ShapeDtypeStruct((n_groups * n_tiles * rows_io, RET_WIDTH), BF16),
                   jax.ShapeDtypeStruct((n_groups * sb,) + state_blk[1:], F32)],
        scratch_shapes=[pltpu.VMEM((RET_HEADS, RET_HEAD, RET_HEAD), F32)],
        compiler_params=_cparams(("parallel", "arbitrary")),
        name="retention_mixer",
    )(z, z, z, z, cos, sin, dmask, iscale, kscale, sdec, sel, s0)


def _retention_tables(c, n_real):
    log_gamma = np.log(1.0 - 2.0 ** (-5.0 - np.arange(RET_HEADS, dtype=np.float64)))
    r = np.arange(c, dtype=np.float64)
    idx = r - float(c - n_real)
    diff = r[:, None] - r[None, :]
    dmask = np.where(diff[None] >= 0, np.exp(log_gamma[:, None, None] * np.maximum(diff, 0.0)[None]), 0.0)
    iscale = np.exp(log_gamma[:, None] * (idx + 1.0)[None, :])[:, :, None]
    kscale = np.exp(log_gamma[:, None] * (n_real - 1.0 - idx)[None, :])[:, :, None]
    sdec = np.broadcast_to(np.exp(log_gamma * n_real)[:, None, None], (RET_HEADS, 1, RET_HEAD))
    return tuple(jnp.asarray(t, dtype=F32) for t in (dmask, iscale, kscale, sdec))


def _rotary_tables(pos, interleaved=True):
    inv_freq = 1.0 / (ROPE_BASE ** np.linspace(0.0, 1.0, RET_HEAD // 2))
    ang = np.asarray(pos, dtype=np.float64)[:, None] * inv_freq[None, :]
    cos = np.cos(ang)
    sin = np.sin(ang)
    if not interleaved:
        return jnp.asarray(cos, dtype=F32), jnp.asarray(sin, dtype=F32)
    cos2 = np.repeat(cos, 2, axis=-1)
    sin2 = np.stack([-sin, sin], axis=-1).reshape(len(pos), RET_HEAD)
    return jnp.asarray(cos2, dtype=F32), jnp.asarray(sin2, dtype=F32)


def _outproj_kernel(oa_ref, ob_ref, w_ref, x_ref, g_ref, h_ref, hn_ref):
    acc = jnp.dot(oa_ref[...], w_ref[0:RW_WIDTH, :], preferred_element_type=F32)
    acc = acc + jnp.dot(ob_ref[...], w_ref[RW_WIDTH:, :], preferred_element_type=F32)
    h = x_ref[...] + acc
    h_ref[...] = h
    hn_ref[...] = _rms_norm_bf16(h, g_ref[...])


def _out_projection(o_a, o_b, w_out, x2d, norm_g, tm):
    m = o_a.shape[0]
    return pl.pallas_call(
        _outproj_kernel,
        grid=(m // tm,),
        in_specs=[
            pl.BlockSpec((tm, RW_WIDTH), lambda i: (i, 0)),
            pl.BlockSpec((tm, RET_WIDTH), lambda i: (i, 0)),
            pl.BlockSpec((D_MODEL, D_MODEL), lambda i: (0, 0)),
            pl.BlockSpec((tm, D_MODEL), lambda i: (i, 0)),
            pl.BlockSpec((1, D_MODEL), lambda i: (0, 0)),
        ],
        out_specs=[pl.BlockSpec((tm, D_MODEL), lambda i: (i, 0)),
                   pl.BlockSpec((tm, D_MODEL), lambda i: (i, 0))],
        out_shape=[jax.ShapeDtypeStruct((m, D_MODEL), F32), jax.ShapeDtypeStruct((m, D_MODEL), BF16)],
        compiler_params=_cparams(("parallel",)),
        name="out_projection",
    )(o_a, o_b, w_out, x2d, norm_g)


def _ffn_kernel(hn_ref, wg_ref, wu_ref, wd_ref, h_ref, g_ref, y_ref, acc_ref, *, h_slices):
    f = pl.program_id(1)

    @pl.when(f == 0)
    def _():
        acc_ref[...] = jnp.zeros_like(acc_ref)

    @pl.when(f < h_slices)
    def _():
        rows = h_ref.shape[0]
        sl = pl.ds(pl.multiple_of(f * rows, rows), rows)
        acc_ref[sl, :] += h_ref[...]

    hn = hn_ref[...]
    gate = jnp.dot(hn, wg_ref[...], preferred_element_type=F32)
    up = jnp.dot(hn, wu_ref[...], preferred_element_type=F32)
    act = (gate * jax.nn.sigmoid(gate)) * up
    acc_ref[...] += jnp.dot(act.astype(BF16), wd_ref[...], preferred_element_type=F32)

    @pl.when(f == pl.num_programs(1) - 1)
    def _():
        h = acc_ref[...]
        ms = jnp.mean(h * h, axis=-1, keepdims=True)
        y_ref[...] = (h * lax.rsqrt(ms + RMS_EPS)) * g_ref[...]


def _ffn(hn, w_gate, w_up, w_down, h, norm_g, tm, tf, h_slices=8):
    m = hn.shape[0]
    assert D_FF // tf >= h_slices
    return pl.pallas_call(
        functools.partial(_ffn_kernel, h_slices=h_slices),
        grid=(m // tm, D_FF // tf),
        in_specs=[
            pl.BlockSpec((tm, D_MODEL), lambda i, f: (i, 0)),
            pl.BlockSpec((D_MODEL, tf), lambda i, f: (0, f)),
            pl.BlockSpec((D_MODEL, tf), lambda i, f: (0, f)),
            pl.BlockSpec((tf, D_MODEL), lambda i, f: (f, 0)),
            pl.BlockSpec((tm // h_slices, D_MODEL),
                         lambda i, f: (i * h_slices + jnp.minimum(f, h_slices - 1), 0)),
            pl.BlockSpec((1, D_MODEL), lambda i, f: (0, 0)),
        ],
        out_specs=pl.BlockSpec((tm, D_MODEL), lambda i, f: (i, 0)),
        out_shape=jax.ShapeDtypeStruct((m, D_MODEL), F32),
        scratch_shapes=[pltpu.VMEM((tm, D_MODEL), F32)],
        compiler_params=_cparams(("parallel", "arbitrary")),
        name="swiglu_ffn",
    )(hn, w_gate, w_up, w_down, h, norm_g)


def _pad_cols(a, n):
    return jnp.pad(a, ((0, 0), (0, n - a.shape[1])))


def _pad_rows(a, n):
    return jnp.pad(a, ((0, n - a.shape[0]), (0, 0)))


def kernel(x_prompt, x_sample, state_shift, state_rwkv, state_ret, meta_tokens, norm_mix, w_in,
           rwkv_mu, rwkv_w0, rwkv_w2, rwkv_a0, rwkv_a2, rwkv_g2, rwkv_kk, rwkv_ka, rwkv_rk,
           rwkv_ln_w, rwkv_ln_b, w_out, norm_ffn, w_gate, w_up, w_down, norm_final):
    n_b, seq, d = x_prompt.shape
    n_s, dec_seq, _ = x_sample.shape
    n_p = n_b * seq
    n_d = n_s * dec_seq
    depth = w_in.shape[0]
    assert depth == 1 and d == D_MODEL and n_s == LANES

    w_in_t = jnp.transpose(w_in[0])
    w_in_p = _w_in_layout(w_in_t, tn=512)
    row = lambda a: a.reshape(1, -1).astype(F32)
    w2p = jnp.concatenate([rwkv_w2[0], jnp.zeros((128 - DECAY_LORA, RW_WIDTH), F32)], axis=0)
    a2p = jnp.concatenate([jnp.zeros((DECAY_LORA, RW_WIDTH), F32), rwkv_a2[0]], axis=0)
    g2p = jnp.concatenate([rwkv_g2[0], jnp.zeros((256 - GATE_LORA, RW_WIDTH), F32)], axis=0)
    rw_params = dict(mu=_pad_cols(row(rwkv_mu[0]), SHIFT_PAD), w0=row(rwkv_w0[0]), w2=w2p,
                     a0=row(rwkv_a0[0]), a2=a2p, g2=g2p, kk=row(rwkv_kk[0]), ka=row(rwkv_ka[0]),
                     rk=row(rwkv_rk[0]), lnw=row(rwkv_ln_w[0]), lnb=row(rwkv_ln_b[0]))
    col = lambda a: jnp.broadcast_to(a.reshape(-1, 1).astype(F32), (a.size, LANES))
    cols = jnp.stack([a.reshape(-1).astype(F32) for a in (rwkv_w0[0], rwkv_a0[0], rwkv_kk[0], rwkv_ka[0],
                                                         rwkv_rk[0], rwkv_ln_w[0], rwkv_ln_b[0])])
    rw_params_t = dict(cols=jnp.broadcast_to(cols[:, :, None], cols.shape + (LANES,)),
                       w2t=rwkv_w2[0].T, a2t=rwkv_a2[0].T, g2t=rwkv_g2[0].T)

    x_p = x_prompt.reshape(n_p, d)
    x_s = x_sample.reshape(n_d, d)
    x_sm = jnp.concatenate([x_s, meta_tokens.astype(F32)], axis=0)
    x_ts = jnp.transpose(x_sample, (1, 0, 2)).reshape(n_d, d)
    g_mix = row(norm_mix[0])
    z_p = _in_projection(x_p, g_mix, w_in_p, tm=1024, tn=1536)
    z_sm = _in_projection(x_sm, g_mix, w_in_p, tm=x_sm.shape[0], tn=512, qk_interleaved=True)
    z_st = _in_projection_t(w_in_t, x_ts, g_mix, SHIFT_PAD, tm=512)

    z_meta = jnp.pad(z_sm[n_d:], ((RET_CHUNK - N_META, 0), (0, 0)))
    srow = 2 * dec_seq

    zero_prev = jnp.zeros((1, 1, SHIFT_PAD), F32)
    zero_rw = jnp.zeros((1, RW_HEADS, RW_HEAD, RW_HEAD), F32)
    c_rw = 64
    _, s_rw_meta, zlast_meta = _rwkv_mixer(z_meta, 1, 1, 1, zero_prev, zero_rw, rw_params, tb=c_rw, c=c_rw)
    oa_p, rwkv_p, zlast_p, w_out_b, w_gate_b, w_up_b, w_down_b = _rwkv_mixer(
        z_p, 0, n_b, seq // 256, zlast_meta, s_rw_meta, rw_params, tb=256, c=c_rw,
        cast=(w_out[0], w_gate[0], w_up[0], w_down[0]))
    shift_t = _pad_rows(jnp.transpose(state_shift[0]), SHIFT_PAD)
    mu_t = _pad_rows(col(rwkv_mu[0]), SHIFT_PAD)
    state_t = jnp.transpose(state_rwkv[0], (1, 2, 3, 0))
    oa_st, rwkv_st = _rwkv_sample_mixer(z_st, shift_t, mu_t, state_t, rw_params_t, n_tok=dec_seq, hpb=2)
    oa_s = jnp.transpose(oa_st, (1, 0, 2)).reshape(n_d, RW_WIDTH)
    rwkv_s = jnp.transpose(rwkv_st, (3, 0, 1, 2))

    past_len = 16384
    tabs_full = _retention_tables(RET_CHUNK, RET_CHUNK)
    tabs_meta = _retention_tables(RET_CHUNK, N_META)
    tabs_smp = _retention_tables(srow, dec_seq)
    sb_rt = 8
    cos_p, sin_p = _rotary_tables(N_META + np.arange(seq), interleaved=False)
    pos_ms = np.concatenate([np.arange(RET_CHUNK) - (RET_CHUNK - N_META),
                             past_len + np.tile(np.arange(srow) - (srow - dec_seq), sb_rt)])
    cos_ms, sin_ms = _rotary_tables(pos_ms)
    zero_rt = jnp.zeros((1, RET_HEADS, RET_HEAD, RET_HEAD), F32)
    _, s_rt_meta = _retention_mixer(z_meta, 0, 1, 1, cos_ms, sin_ms, tabs_meta, zero_rt,
                                    rows=RET_CHUNK, c=RET_CHUNK, n_real=N_META, per_chunk_state=False,
                                    pos_per_tile=(0, False))
    rows_rt = 4 * RET_CHUNK
    ob_p, ret_p = _retention_mixer(z_p, 0, n_b, seq // rows_rt, cos_p, sin_p, tabs_full, s_rt_meta,
                                   rows=rows_rt, c=RET_CHUNK, n_real=RET_CHUNK, per_chunk_state=False,
                                   pos_per_tile=(0, True), pair_split=True)
    ob_s, ret_s = _retention_mixer(z_sm, 0, n_s // sb_rt, 1, cos_ms, sin_ms, tabs_smp,
                                   state_ret.reshape(n_s, RET_HEADS, RET_HEAD, RET_HEAD),
                                   rows=sb_rt * srow, c=srow, n_real=dec_seq, per_chunk_state=True,
                                   pos_per_tile=(RET_CHUNK, False))

    g_ffn = row(norm_ffn[0])
    g_fin = row(norm_final)
    h_p, hn_p = _out_projection(oa_p, ob_p, w_out_b, x_p, g_ffn, tm=512)
    h_s, hn_s = _out_projection(oa_s, ob_s, w_out_b, x_s, g_ffn, tm=n_d)
    y_p = _ffn(hn_p, w_gate_b, w_up_b, w_down_b, h_p, g_fin, tm=1024, tf=512)
    y_s = _ffn(hn_s, w_gate_b, w_up_b, w_down_b, h_s, g_fin, tm=n_d, tf=512)

    y_prompt = y_p.reshape(n_b, seq, d)
    y_sample = y_s.reshape(n_s, dec_seq, d)
    shift_p = zlast_p[:, 0, :RW_COLS][None]
    shift_s = jnp.transpose(z_st[:RW_COLS, (dec_seq - 1) * n_s:])[None]
    return (y_prompt, y_sample, shift_p, rwkv_p[None], ret_p[None], shift_s, rwkv_s[None], ret_s[None])
```

```python
import functools
import math

import jax
import jax.numpy as jnp
import numpy as np
from jax import lax
from jax.experimental import pallas as pl
from jax.experimental.pallas import tpu as pltpu

F32 = jnp.float32
BF16 = jnp.bfloat16

D_MODEL = 2048
N_META = 16
RW_WIDTH = 1024
RW_HEAD = 64
RW_HEADS = 16
DECAY_LORA = 64
AAA_LORA = 64
GATE_LORA = 160
RW_COLS = 3 * RW_WIDTH + DECAY_LORA + AAA_LORA + GATE_LORA
RET_WIDTH = 1024
RET_HEADS = 4
RET_HEAD = 256
RET_CHUNK = 128
D_FF = 5632
RMS_EPS = 1e-6
RW_GN_EPS = 64e-5
RET_GN_EPS = 1e-6
ROPE_BASE = 10000.0
LANES = 128

LORA_PAD = 512
Z_COLS = 4 * RET_WIDTH + 3 * RW_WIDTH + LORA_PAD
RW_OFF = 4 * RET_WIDTH
SHIFT_PAD = 3 * RW_WIDTH + LORA_PAD

VMEM_LIMIT = 60 * 1024 * 1024


def _cparams(sem):
    return pltpu.CompilerParams(dimension_semantics=sem, vmem_limit_bytes=VMEM_LIMIT)


def _mm(a, b):
    return jnp.dot(a.astype(BF16), b.astype(BF16), preferred_element_type=F32)


def _mm_nt(a, b):
    return lax.dot_general(a.astype(BF16), b.astype(BF16), (((1,), (1,)), ((), ())),
                           preferred_element_type=F32)


def _mm_tn(a, b):
    return lax.dot_general(a.astype(BF16), b.astype(BF16), (((0,), (0,)), ((), ())),
                           preferred_element_type=F32)


def _bmm(a, b):
    return jnp.einsum("bmk,bkn->bmn", a.astype(BF16), b.astype(BF16), preferred_element_type=F32)


def _bmm_nt(a, b):
    return jnp.einsum("bmk,bnk->bmn", a.astype(BF16), b.astype(BF16), preferred_element_type=F32)


def _bmm_tn(a, b):
    return jnp.einsum("bkm,bkn->bmn", a.astype(BF16), b.astype(BF16), preferred_element_type=F32)


def _mm_exact_lhs(m_bf16, x):
    hi = x.astype(BF16)
    r1 = x - hi.astype(F32)
    mid = r1.astype(BF16)
    lo = (r1 - mid.astype(F32)).astype(BF16)
    d = functools.partial(jnp.dot, preferred_element_type=F32)
    return d(m_bf16, hi) + d(m_bf16, mid) + d(m_bf16, lo)


def _mm_exact_lhs_t(m_bf16, x):
    hi = x.astype(BF16)
    r1 = x - hi.astype(F32)
    mid = r1.astype(BF16)
    lo = (r1 - mid.astype(F32)).astype(BF16)
    d = functools.partial(lax.dot_general, dimension_numbers=(((0,), (0,)), ((), ())),
                          preferred_element_type=F32)
    return d(m_bf16, hi) + d(m_bf16, mid) + d(m_bf16, lo)


def _rms_norm_bf16(x, g):
    ms = jnp.mean(x * x, axis=-1, keepdims=True)
    return ((x * lax.rsqrt(ms + RMS_EPS)) * g).astype(BF16)


def _pair_split_index(n):
    j = np.arange(n)
    half = RET_HEAD // 2
    return (j // RET_HEAD) * RET_HEAD + 2 * (j % half) + (j % RET_HEAD) // half


def _w_in_layout_kernel(w_ref, sel_ref, o_ref, *, n_qk):
    c = pl.program_id(0)

    @pl.when(c >= n_qk)
    def _():
        o_ref[...] = w_ref[...].T.astype(BF16)

    @pl.when(c < n_qk)
    def _():
        picked = jnp.dot(sel_ref[...], w_ref[...].astype(BF16), preferred_element_type=F32)
        o_ref[...] = picked.T.astype(BF16)


def _w_in_layout(w_in_t, tn):
    n, d = w_in_t.shape
    n_ret = 4 * RET_WIDTH // tn
    n_z = Z_COLS // tn
    n_qk = 2 * RET_WIDTH // tn
    sel = jnp.asarray(_pair_split_index(tn)[:, None] == np.arange(tn)[None, :], dtype=BF16)

    def src_row(c):
        align = math.gcd(RW_COLS, tn)
        row = jnp.where(c < n_ret, RW_COLS + tn * c,
                        jnp.where(c < n_z, tn * (c - n_ret), RW_COLS + tn * (c - n_z)))
        return (pl.multiple_of(row, align), 0)

    return pl.pallas_call(
        functools.partial(_w_in_layout_kernel, n_qk=n_qk),
        grid=(n_z + n_qk,),
        in_specs=[pl.BlockSpec((pl.Element(tn), pl.Element(d)), src_row),
                  pl.BlockSpec((tn, tn), lambda c: (0, 0))],
        out_specs=pl.BlockSpec((d, tn), lambda c: (0, c)),
        out_shape=jax.ShapeDtypeStruct((d, Z_COLS + 2 * RET_WIDTH), BF16),
        compiler_params=_cparams(("parallel",)),
        name="w_in_layout",
    )(w_in_t, sel)


def _inproj_kernel(x_ref, g_ref, w_ref, o_ref, xn_ref):
    @pl.when(pl.program_id(1) == 0)
    def _():
        xn_ref[...] = _rms_norm_bf16(x_ref[...], g_ref[...])

    o_ref[...] = jnp.dot(xn_ref[...], w_ref[...], preferred_element_type=F32)


def _in_projection(x2d, norm_g, w_in_p, tm, tn, qk_interleaved=False):
    m = x2d.shape[0]
    if qk_interleaved:
        n_qk = 2 * RET_WIDTH // tn
        w_col = lambda i, j: (0, jnp.where(j < n_qk, Z_COLS // tn + j, j))
    else:
        w_col = lambda i, j: (0, j)
    return pl.pallas_call(
        _inproj_kernel,
        grid=(m // tm, Z_COLS // tn),
        in_specs=[
            pl.BlockSpec((tm, D_MODEL), lambda i, j: (i, 0)),
            pl.BlockSpec((1, D_MODEL), lambda i, j: (0, 0)),
            pl.BlockSpec((D_MODEL, tn), w_col),
        ],
        out_specs=pl.BlockSpec((tm, tn), lambda i, j: (i, j)),
        out_shape=jax.ShapeDtypeStruct((m, Z_COLS), F32),
        scratch_shapes=[pltpu.VMEM((tm, D_MODEL), BF16)],
        compiler_params=_cparams(("parallel", "arbitrary")),
        name="in_projection",
    )(x2d, norm_g, w_in_p)


def _inproj_t_kernel(w_ref, x_ref, g_ref, o_ref, xn_ref):
    @pl.when(pl.program_id(0) == 0)
    def _():
        xn_ref[...] = _rms_norm_bf16(x_ref[...], g_ref[...])

    o_ref[...] = _mm_nt(w_ref[...], xn_ref[...])


def _in_projection_t(w_in_t, x2d, norm_g, n_rows, tm):
    m, d = x2d.shape
    return pl.pallas_call(
        _inproj_t_kernel,
        grid=(n_rows // tm,),
        in_specs=[
            pl.BlockSpec((tm, d), lambda i: (i, 0)),
            pl.BlockSpec((m, d), lambda i: (0, 0)),
            pl.BlockSpec((1, d), lambda i: (0, 0)),
        ],
        out_specs=pl.BlockSpec((tm, m), lambda i: (i, 0)),
        out_shape=jax.ShapeDtypeStruct((n_rows, m), F32),
        scratch_shapes=[pltpu.VMEM((m, d), BF16)],
        compiler_params=_cparams(("arbitrary",)),
        name="in_projection_t",
    )(w_in_t, x2d, norm_g)


def _rwkv_kernel(zr_ref, zk_ref, zv_ref, zl_ref, zprev0_ref, s0_ref,
                 mu_ref, w0_ref, w2_ref, a0_ref, a2_ref, g2_ref, kk_ref, ka_ref, rk_ref,
                 lnw_ref, lnb_ref,
                 o_ref, sout_ref, zlast_ref,
                 carry_ref, sbd_s, r_s, km_s, v_s, kk_s, a_s, lw_s, cum_s, o_s,
                 *, tb, c, n_factors):
    i = pl.program_id(1)
    n_chunks = tb // c
    n_pairs = RW_HEADS // 2
    pw = 2 * RW_HEAD

    @pl.when(i == 0)
    def _():
        carry_ref[...] = zprev0_ref[0]
        zero = jnp.zeros((RW_HEAD, RW_HEAD), F32)
        for p in range(n_pairs):
            top = jnp.concatenate([s0_ref[0, 2 * p], zero], axis=1)
            bot = jnp.concatenate([zero, s0_ref[0, 2 * p + 1]], axis=1)
            sbd_s[p] = jnp.concatenate([top, bot], axis=0)

    row = lax.broadcasted_iota(jnp.int32, (tb, 1), 0)

    def shifted(z_ref, lo, hi):
        z = z_ref[...]
        prev = pltpu.roll(z, 1, axis=0)
        prev = jnp.concatenate([jnp.where(row[:8] == 0, carry_ref[:, lo:hi], prev[:8]), prev[8:]], axis=0)
        zs = z + mu_ref[:, lo:hi] * (prev - z)
        carry_ref[:, lo:hi] = z[tb - 1:tb, :]
        return zs

    w = RW_WIDTH
    r = shifted(zr_ref, 0, w)
    k = shifted(zk_ref, w, 2 * w)
    v = shifted(zv_ref, 2 * w, 3 * w)
    zl = shifted(zl_ref, 3 * w, 3 * w + LORA_PAD)

    lo2 = zl[:, 0:128]
    wlog = -jax.nn.softplus(-(w0_ref[...] + _mm(jnp.tanh(lo2), w2_ref[...]))) - 0.5
    lw = -jnp.exp(wlog)
    a = jax.nn.sigmoid(a0_ref[...] + _mm(lo2, a2_ref[...]))
    g = _mm(jax.nn.sigmoid(zl[:, 128:384]), g2_ref[...])
    kk = k * kk_ref[...]
    km = k * (1.0 + (a - 1.0) * ka_ref[...])

    for dst, val in ((r_s, r), (km_s, km), (v_s, v), (a_s, a), (lw_s, lw), (kk_s, kk)):
        dst[...] = val

    rr = lax.broadcasted_iota(jnp.int32, (c, c), 0)
    cc = lax.broadcasted_iota(jnp.int32, (c, c), 1)
    tri = jnp.where(rr >= cc, 1.0, 0.0).astype(BF16)
    rr2 = lax.broadcasted_iota(jnp.int32, (c, 2 * c), 0)
    cc2 = lax.broadcasted_iota(jnp.int32, (c, 2 * c), 1)
    strict = (rr > cc)[None]
    eye = jnp.where(rr == cc, 1.0, 0.0).astype(F32)[None]
    strict_k = ((cc2 >= c) & (rr2 > cc2 - c))[None]
    incl_bk = (rr2 >= cc2 % c)[None]
    lo = (lax.broadcasted_iota(jnp.int32, (1, 1, pw), 2) < RW_HEAD)
    bd_r = lax.broadcasted_iota(jnp.int32, (pw, pw), 0) // RW_HEAD
    bd_c = lax.broadcasted_iota(jnp.int32, (pw, pw), 1) // RW_HEAD
    block_diag = (bd_r == bd_c)[None]
    rk = rk_ref[...]
    lnw = lnw_ref[...]
    lnb = lnb_ref[...]

    def head_sum(x):
        s_lo = jnp.sum(jnp.where(lo, x, 0.0), axis=-1, keepdims=True)
        s_hi = jnp.sum(jnp.where(lo, 0.0, x), axis=-1, keepdims=True)
        return jnp.where(lo, s_lo, s_hi)

    def pick(x16):
        x4 = x16.reshape(n_pairs, 2, c, pw)
        return jnp.where(lo, x4[:, 0], x4[:, 1])

    def body(ch, s_prev):
        rows = pl.ds(ch * c, c)

        def pairs(s):
            x = s[rows, :]
            return jnp.stack([x[:, p * pw:(p + 1) * pw] for p in range(n_pairs)], axis=0)

        cum_s[rows, :] = _mm_exact_lhs(tri, lw_s[rows, :])
        rh, kmh, vh, ah, lwh, cumh, kkh = [pairs(s) for s in (r_s, km_s, v_s, a_s, lw_s, cum_s, kk_s)]
        kkn = kkh * lax.rsqrt(jnp.maximum(head_sum(kkh * kkh), 1e-24))
        bvec = kkn * ah
        cum_end = cumh[:, c - 1:c, :]
        p_inc = jnp.exp(cumh)
        p_exc = jnp.exp(cumh - lwh)
        p_inv = jnp.exp(-cumh)
        p_end = jnp.exp(cum_end)
        p_rel = jnp.exp(cum_end - cumh)
        a_t = -(kkn * p_exc)
        r_t = rh * p_inc
        b_t = bvec * p_inv
        k_t = kmh * p_inv
        b_h = bvec * p_rel
        k_h = kmh * p_rel

        ar = jnp.concatenate([jnp.where(lo, a_t, 0.0), jnp.where(lo, r_t, 0.0),
                              jnp.where(lo, 0.0, a_t), jnp.where(lo, 0.0, r_t)], axis=1)
        prod = _bmm_nt(ar, jnp.concatenate([b_t, k_t, s_prev], axis=1))
        prod = prod.reshape(RW_HEADS, 2 * c, 2 * c + pw)
        a_rows, r_rows = prod[:, :c], prod[:, c:]
        aak_wide = jnp.where(strict_k, a_rows[:, :, :2 * c], 0.0)
        arbk = jnp.where(incl_bk, r_rows[:, :, :2 * c], 0.0)
        lmat = jnp.where(strict, a_rows[:, :, :c], 0.0)
        tinv = eye + lmat
        if n_factors > 1:
            lp = _bmm(lmat, lmat)
            for _ in range(n_factors - 2):
                st = _bmm(jnp.concatenate([lp, tinv], axis=1), lp)
                lp = st[:, :c]
                tinv = tinv + st[:, c:]
            tinv = tinv + _bmm(tinv, lp)
        v16 = jnp.repeat(vh, 2, axis=0)
        x = a_rows[:, :, 2 * c:] + _bmm(aak_wide, jnp.concatenate([v16, v16], axis=1))
        u16 = _bmm(tinv, x)
        o16 = r_rows[:, :, 2 * c:] + _bmm(arbk, jnp.concatenate([u16, v16], axis=1))
        u = pick(u16)
        o = pick(o16)
        s_upd = _bmm_tn(jnp.concatenate([u, vh], axis=1), jnp.concatenate([b_h, k_h], axis=1))
        s_new = s_prev * p_end + jnp.where(block_diag, s_upd, 0.0)
        inv_n = 1.0 / RW_HEAD
        mean = head_sum(o) * inv_n
        var = head_sum(jnp.square(o - mean)) * inv_n
        on = ((o - mean) * lax.rsqrt(var + RW_GN_EPS)) * lnw + lnb
        out = on + head_sum(rh * kmh * rk) * vh
        for p in range(n_pairs):
            o_s[rows, p * pw:(p + 1) * pw] = out[p]
        return s_new

    state = sbd_s[...]
    for ch in range(n_chunks):
        state = body(ch, state)
    sbd_s[...] = state

    o_ref[...] = (o_s[...] * g).astype(BF16)
    zlast_ref[0] = carry_ref[...]

    @pl.when(i == pl.num_programs(1) - 1)
    def _():
        for p in range(n_pairs):
            blk = sbd_s[p]
            sout_ref[0, 2 * p] = blk[:RW_HEAD, :RW_HEAD]
            sout_ref[0, 2 * p + 1] = blk[RW_HEAD:, RW_HEAD:]


N_RWKV_INPUTS = 17
N_RWKV_OUTPUTS = 3


def _rwkv_and_cast_kernel(*refs, n_cast, **kw):
    ins = refs[:N_RWKV_INPUTS]
    cast_in = refs[N_RWKV_INPUTS:N_RWKV_INPUTS + n_cast]
    o0 = N_RWKV_INPUTS + n_cast
    outs = refs[o0:o0 + N_RWKV_OUTPUTS]
    cast_out = refs[o0 + N_RWKV_OUTPUTS:o0 + N_RWKV_OUTPUTS + n_cast]
    scratch = refs[o0 + N_RWKV_OUTPUTS + n_cast:]
    for src, dst in zip(cast_in, cast_out):
        dst[...] = src[...].astype(BF16)
    _rwkv_kernel(*ins, *outs, *scratch, **kw)


def _rwkv_mixer(z, row_block0, n_groups, n_tiles, zprev0, s0, p, *, tb, c, cast=()):
    n_factors = max(1, math.ceil(math.log2(c)))
    kern = functools.partial(_rwkv_and_cast_kernel, n_cast=len(cast), tb=tb, c=c, n_factors=n_factors)
    n_steps = n_groups * n_tiles
    cast_specs = [pl.BlockSpec((a.shape[0] // n_steps, a.shape[1]), lambda gi, i: (gi * n_tiles + i, 0))
                  for a in cast]
    cast_shapes = [jax.ShapeDtypeStruct(a.shape, BF16) for a in cast]
    cb = RW_OFF // RW_WIDTH
    full = lambda arr: pl.BlockSpec(arr.shape, lambda gi, i: (0,) * arr.ndim)
    lora_cb = (RW_OFF + 3 * RW_WIDTH) // LORA_PAD
    per_pair = lambda a: a.reshape(RW_HEADS // 2, 1, 2 * RW_HEAD)
    params = [p["mu"], p["w0"], p["w2"], p["a0"], p["a2"], p["g2"], p["kk"], p["ka"],
              per_pair(p["rk"]), per_pair(p["lnw"]), per_pair(p["lnb"])]
    dense = lambda: pltpu.VMEM((tb, RW_WIDTH), F32)
    zspec = lambda width, col: pl.BlockSpec((tb, width), lambda gi, i: (row_block0 + gi * n_tiles + i, col))
    sspec = pl.BlockSpec((1, RW_HEADS, RW_HEAD, RW_HEAD), lambda gi, i: (gi, 0, 0, 0))
    shared = lambda a: (lambda gi, i: (0,) * a.ndim) if a.shape[0] == 1 else (lambda gi, i: (gi,) + (0,) * (a.ndim - 1))
    return pl.pallas_call(
        kern,
        grid=(n_groups, n_tiles),
        in_specs=[zspec(RW_WIDTH, cb), zspec(RW_WIDTH, cb + 1), zspec(RW_WIDTH, cb + 2),
                  zspec(LORA_PAD, lora_cb),
                  pl.BlockSpec((1, 1, SHIFT_PAD), shared(zprev0)),
                  pl.BlockSpec((1, RW_HEADS, RW_HEAD, RW_HEAD), shared(s0))]
                 + [full(a) for a in params] + cast_specs,
        out_specs=[pl.BlockSpec((tb, RW_WIDTH), lambda gi, i: (gi * n_tiles + i, 0)), sspec,
                   pl.BlockSpec((1, 1, SHIFT_PAD), lambda gi, i: (gi, 0, 0))] + cast_specs,
        out_shape=[jax.ShapeDtypeStruct((n_groups * n_tiles * tb, RW_WIDTH), BF16),
                   jax.ShapeDtypeStruct((n_groups, RW_HEADS, RW_HEAD, RW_HEAD), F32),
                   jax.ShapeDtypeStruct((n_groups, 1, SHIFT_PAD), F32)] + cast_shapes,
        scratch_shapes=[pltpu.VMEM((1, SHIFT_PAD), F32),
                        pltpu.VMEM((RW_HEADS // 2, 2 * RW_HEAD, 2 * RW_HEAD), F32)]
                       + [dense() for _ in range(8)],
        compiler_params=_cparams(("parallel", "arbitrary")),
        name="rwkv7_mixer",
    )(z, z, z, z, zprev0, s0, *params, *cast)


def _rwkv_sample_kernel(zr_ref, zk_ref, zv_ref, zl_ref, shr_ref, shk_ref, shv_ref, shl_ref,
                        mur_ref, muk_ref, muv_ref, mul_ref, s0_ref,
                        pc_ref, w2t_ref, a2t_ref, g2t_ref,
                        o_ref, sout_ref,
                        dec_s, a_s, b_s, k_s, r_s, v_s, o_s,
                        *, n_tok, hpb):
    w0_ref, a0_ref, kk_ref, ka_ref, rk_ref, lnw_ref, lnb_ref = [pc_ref.at[j] for j in range(7)]
    ns = LANES
    ch = hpb * RW_HEAD

    def lanes(x):
        return jnp.concatenate([x] * n_tok, axis=1)

    def shifted(z_ref, sh_ref, mu_ref):
        z = z_ref[...]
        prev = jnp.concatenate([sh_ref[...], z[:, :(n_tok - 1) * ns]], axis=1)
        return z + lanes(mu_ref[...]) * (prev - z)

    r = shifted(zr_ref, shr_ref, mur_ref)
    k = shifted(zk_ref, shk_ref, muk_ref)
    v = shifted(zv_ref, shv_ref, muv_ref)
    zl = shifted(zl_ref, shl_ref, mul_ref)
    wd = zl[0:DECAY_LORA]
    ad = zl[DECAY_LORA:DECAY_LORA + AAA_LORA]
    gd = zl[DECAY_LORA + AAA_LORA:DECAY_LORA + AAA_LORA + GATE_LORA]

    wlog = -jax.nn.softplus(-(lanes(w0_ref[...]) + _mm(w2t_ref[...], jnp.tanh(wd)))) - 0.5
    lw = -jnp.exp(wlog)
    a = jax.nn.sigmoid(lanes(a0_ref[...]) + _mm(a2t_ref[...], ad))
    g = _mm(g2t_ref[...], jax.nn.sigmoid(gd))
    kk = k * lanes(kk_ref[...])
    km = k * (1.0 + (a - 1.0) * lanes(ka_ref[...]))

    def head_sum(x):
        x3 = x.reshape(hpb, RW_HEAD, n_tok * ns)
        s = jnp.sum(x3, axis=1, keepdims=True)
        return jnp.broadcast_to(s, x3.shape).reshape(ch, n_tok * ns)

    kkn = kk * lax.rsqrt(jnp.maximum(head_sum(kk * kk), 1e-24))
    dec_s[...] = jnp.exp(lw)
    a_s[...] = -kkn
    b_s[...] = kkn * a
    k_s[...] = km
    r_s[...] = r
    v_s[...] = v

    for hh in range(hpb):
        hrows = slice(hh * RW_HEAD, (hh + 1) * RW_HEAD)

        def body(i8, carry, hh=hh, hrows=hrows):
            base = pl.multiple_of(i8 * 8, 8)
            rows8 = pl.ds(pl.multiple_of(hh * RW_HEAD + base, 8), 8)
            o_rows = [[] for _ in range(n_tok)]
            for j in range(8):
                s = s0_ref[hh, base + j]
                for t in range(n_tok):
                    tl = slice(t * ns, (t + 1) * ns)
                    sa = jnp.sum(s * a_s[hrows, tl], axis=0, keepdims=True)
                    v_row = v_s[rows8, tl][j:j + 1, :]
                    s = s * dec_s[hrows, tl] + sa * b_s[hrows, tl] + v_row * k_s[hrows, tl]
                    o_rows[t].append(jnp.sum(s * r_s[hrows, tl], axis=0, keepdims=True))
                sout_ref[hh, base + j] = s
            for t in range(n_tok):
                o_s[rows8, t * ns:(t + 1) * ns] = jnp.concatenate(o_rows[t], axis=0)
            return carry

        lax.fori_loop(0, RW_HEAD // 8, body, 0)

    o = o_s[...]
    inv_n = 1.0 / RW_HEAD
    mean = head_sum(o) * inv_n
    var = head_sum(jnp.square(o - mean)) * inv_n
    on = ((o - mean) * lax.rsqrt(var + RW_GN_EPS)) * lanes(lnw_ref[...]) + lanes(lnb_ref[...])
    bonus = head_sum(r * km * lanes(rk_ref[...])) * v
    out = (on + bonus) * g
    for t in range(n_tok):
        o_ref[t] = out[:, t * ns:(t + 1) * ns].T.astype(BF16)


def _rwkv_sample_mixer(z_t, shift_t, mu_t, state_t, p_t, *, n_tok, hpb):
    ch = hpb * RW_HEAD
    n_steps = RW_HEADS // hpb
    nl = n_tok * LANES
    seg = RW_WIDTH // ch
    zspec = lambda s: pl.BlockSpec((ch, nl), lambda h: (s * seg + h, 0))
    cspec = lambda s: pl.BlockSpec((ch, LANES), lambda h: (s * seg + h, 0))
    lora_blk = 3 * RW_WIDTH // LORA_PAD
    zl_spec = pl.BlockSpec((LORA_PAD, nl), lambda h: (lora_blk, 0))
    cl_spec = pl.BlockSpec((LORA_PAD, LANES), lambda h: (lora_blk, 0))
    pc_spec = pl.BlockSpec((p_t["cols"].shape[0], ch, LANES), lambda h: (0, h, 0))
    wspec = lambda k: pl.BlockSpec((ch, k), lambda h: (h, 0))
    sspec = pl.BlockSpec((hpb, RW_HEAD, RW_HEAD, LANES), lambda h: (h, 0, 0, 0))
    buf = lambda: pltpu.VMEM((ch, nl), F32)
    return pl.pallas_call(
        functools.partial(_rwkv_sample_kernel, n_tok=n_tok, hpb=hpb),
        grid=(n_steps,),
        in_specs=[zspec(0), zspec(1), zspec(2), zl_spec,
                  cspec(0), cspec(1), cspec(2), cl_spec,
                  cspec(0), cspec(1), cspec(2), cl_spec,
                  sspec,
                  pc_spec, wspec(DECAY_LORA), wspec(AAA_LORA), wspec(GATE_LORA)],
        out_specs=[pl.BlockSpec((n_tok, LANES, ch), lambda h: (0, 0, h)), sspec],
        out_shape=[jax.ShapeDtypeStruct((n_tok, LANES, RW_WIDTH), BF16),
                   jax.ShapeDtypeStruct(state_t.shape, F32)],
        scratch_shapes=[buf() for _ in range(7)],
        compiler_params=_cparams(("parallel",)),
        name="rwkv7_sample",
    )(z_t, z_t, z_t, z_t, shift_t, shift_t, shift_t, shift_t, mu_t, mu_t, mu_t, mu_t, state_t,
      p_t["cols"], p_t["w2t"], p_t["a2t"], p_t["g2t"])


def _retention_kernel(zq_ref, zk_ref, zv_ref, zg_ref, cos_ref, sin_ref, dmask_ref, iscale_ref,
                      kscale_ref, sdec_ref, sel_ref, s0_ref, o_ref, sout_ref, st_s,
                      *, rows, c, n_real, per_chunk_state, pair_split):
    i = pl.program_id(1)
    n_chunks = rows // c
    assert not (pair_split and per_chunk_state)

    if per_chunk_state:
        pr = lax.broadcasted_iota(jnp.int32, (rows, n_chunks * n_real), 0)
        pc = lax.broadcasted_iota(jnp.int32, (rows, n_chunks * n_real), 1)
        place = jnp.where((pr // c == pc // n_real) & (pr % c - (c - n_real) == pc % n_real),
                          1.0, 0.0).astype(BF16)
        load = lambda ref: _mm_exact_lhs(place, ref[...])
    else:
        load = lambda ref: ref[...]

        @pl.when(i == 0)
        def _():
            if pair_split:
                for h in range(RET_HEADS):
                    st_s[h] = _mm_exact_lhs(sel_ref[...], s0_ref[0, h])
            else:
                st_s[...] = s0_ref[0]

    if pair_split:
        cos = cos_ref[...]
        sin = sin_ref[...]
        half = RET_HEAD // 2

        def rot(x):
            parts = []
            for h in range(RET_HEADS):
                x0 = x[:, h * RET_HEAD:h * RET_HEAD + half]
                x1 = x[:, h * RET_HEAD + half:(h + 1) * RET_HEAD]
                parts += [x0 * cos - x1 * sin, x0 * sin + x1 * cos]
            return jnp.concatenate(parts, axis=-1)
    else:
        lane = lax.broadcasted_iota(jnp.int32, (rows, RET_WIDTH), 1)
        even = (lane % 2) == 0
        cos = jnp.concatenate([cos_ref[...]] * RET_HEADS, axis=-1)
        sin = jnp.concatenate([sin_ref[...]] * RET_HEADS, axis=-1)

        def rot(x):
            partner = jnp.where(even, pltpu.roll(x, RET_WIDTH - 1, axis=1), pltpu.roll(x, 1, axis=1))
            return x * cos + partner * sin

    q = rot(load(zq_ref))
    k = rot(load(zk_ref)) * (RET_HEAD ** -0.5)
    v = load(zv_ref)
    g = load(zg_ref)

    out_rows = []
    for ch in range(n_chunks):
        rs = slice(ch * c, (ch + 1) * c)
        out_heads = []
        for h in range(RET_HEADS):
            hs = slice(h * RET_HEAD, (h + 1) * RET_HEAD)
            qh, kh, vh = q[rs, hs], k[rs, hs], v[rs, hs]
            s_prev = s0_ref[ch, h] if per_chunk_state else st_s[h]
            scores = _mm_nt(qh, kh) * dmask_ref[h]
            o = _mm(scores, vh) + _mm(qh, s_prev) * iscale_ref[h]
            s_new = s_prev * sdec_ref[h] + _mm_tn(kh * kscale_ref[h], vh)
            if per_chunk_state:
                sout_ref[ch, h] = s_new
            else:
                st_s[h] = s_new
            o = o * lax.rsqrt(jnp.mean(o * o, axis=-1, keepdims=True) + RET_GN_EPS)
            gh = g[rs, hs]
            out_heads.append(o * (gh * jax.nn.sigmoid(gh)))
        out_rows.append(jnp.concatenate(out_heads, axis=-1))
    o_out = jnp.concatenate(out_rows, axis=0).astype(BF16)
    if per_chunk_state:
        o_out = lax.dot_general(place, o_out, (((0,), (0,)), ((), ())),
                                preferred_element_type=F32).astype(BF16)
    o_ref[...] = o_out

    if not per_chunk_state:
        @pl.when(i == pl.num_programs(1) - 1)
        def _():
            if pair_split:
                for h in range(RET_HEADS):
                    sout_ref[0, h] = _mm_exact_lhs_t(sel_ref[...], st_s[h])
            else:
                sout_ref[0] = st_s[...]


def _retention_mixer(z, row_block0, n_groups, n_tiles, cos, sin, tabs, s0, *, rows, c, n_real,
                     per_chunk_state, pos_per_tile, pair_split=False):
    shared_s0 = s0.shape[0] == 1
    sb = 1 if shared_s0 else s0.shape[0] // n_groups
    kern = functools.partial(_retention_kernel, rows=rows, c=c, n_real=n_real,
                             per_chunk_state=per_chunk_state, pair_split=pair_split)
    rows_io = rows // c * n_real if per_chunk_state else rows
    zspec = lambda col: pl.BlockSpec((rows_io, RET_WIDTH), lambda gi, i: (row_block0 + gi * n_tiles + i, col))
    full = lambda arr: pl.BlockSpec(arr.shape, lambda gi, i: (0,) * arr.ndim)
    tbl_row0, tbl_advance = pos_per_tile
    assert tbl_row0 % rows == 0
    tspec = pl.BlockSpec((rows, cos.shape[1]),
                         lambda gi, i: (tbl_row0 // rows + (i if tbl_advance else 0), 0))
    state_blk = (sb, RET_HEADS, RET_HEAD, RET_HEAD)
    s0_spec = pl.BlockSpec(state_blk, lambda gi, i: (0 if shared_s0 else gi, 0, 0, 0))
    sout_spec = pl.BlockSpec(state_blk, lambda gi, i: (gi, 0, 0, 0))
    dmask, iscale, kscale, sdec = tabs
    sel = jnp.asarray(_pair_split_index(RET_HEAD)[:, None] == np.arange(RET_HEAD)[None, :], dtype=BF16)
    return pl.pallas_call(
        kern,
        grid=(n_groups, n_tiles),
        in_specs=[zspec(0), zspec(1), zspec(2), zspec(3), tspec, tspec,
                  full(dmask), full(iscale), full(kscale), full(sdec), full(sel), s0_spec],
        out_specs=[pl.BlockSpec((rows_io, RET_WIDTH), lambda gi, i: (gi * n_tiles + i, 0)), sout_spec],
        out_shape=[jax.ShapeDtypeStruct((n_groups * n_tiles * rows_io, RET_WIDTH), BF16),
                   jax.ShapeDtypeStruct((n_groups * sb,) + state_blk[1:], F32)],
        scratch_shapes=[pltpu.VMEM((RET_HEADS, RET_HEAD, RET_HEAD), F32)],
        compiler_params=_cparams(("parallel", "arbitrary")),
        name="retention_mixer",
    )(z, z, z, z, cos, sin, dmask, iscale, kscale, sdec, sel, s0)


def _retention_tables(c, n_real):
    log_gamma = np.log(1.0 - 2.0 ** (-5.0 - np.arange(RET_HEADS, dtype=np.float64)))
    r = np.arange(c, dtype=np.float64)
    idx = r - float(c - n_real)
    diff = r[:, None] - r[None, :]
    dmask = np.where(diff[None] >= 0, np.exp(log_gamma[:, None, None] * np.maximum(diff, 0.0)[None]), 0.0)
    iscale = np.exp(log_gamma[:, None] * (idx + 1.0)[None, :])[:, :, None]
    kscale = np.exp(log_gamma[:, None] * (n_real - 1.0 - idx)[None, :])[:, :, None]
    sdec = np.broadcast_to(np.exp(log_gamma * n_real)[:, None, None], (RET_HEADS, 1, RET_HEAD))
    return tuple(jnp.asarray(t, dtype=F32) for t in (dmask, iscale, kscale, sdec))


def _rotary_tables(pos, interleaved=True):
    inv_freq = 1.0 / (ROPE_BASE ** np.linspace(0.0, 1.0, RET_HEAD // 2))
    ang = np.asarray(pos, dtype=np.float64)[:, None] * inv_freq[None, :]
    cos = np.cos(ang)
    sin = np.sin(ang)
    if not interleaved:
        return jnp.asarray(cos, dtype=F32), jnp.asarray(sin, dtype=F32)
    cos2 = np.repeat(cos, 2, axis=-1)
    sin2 = np.stack([-sin, sin], axis=-1).reshape(len(pos), RET_HEAD)
    return jnp.asarray(cos2, dtype=F32), jnp.asarray(sin2, dtype=F32)


def _outproj_kernel(oa_ref, ob_ref, w_ref, x_ref, g_ref, h_ref, hn_ref):
    acc = jnp.dot(oa_ref[...], w_ref[0:RW_WIDTH, :], preferred_element_type=F32)
    acc = acc + jnp.dot(ob_ref[...], w_ref[RW_WIDTH:, :], preferred_element_type=F32)
    h = x_ref[...] + acc
    h_ref[...] = h
    hn_ref[...] = _rms_norm_bf16(h, g_ref[...])


def _out_projection(o_a, o_b, w_out, x2d, norm_g, tm):
    m = o_a.shape[0]
    return pl.pallas_call(
        _outproj_kernel,
        grid=(m // tm,),
        in_specs=[
            pl.BlockSpec((tm, RW_WIDTH), lambda i: (i, 0)),
            pl.BlockSpec((tm, RET_WIDTH), lambda i: (i, 0)),
            pl.BlockSpec((D_MODEL, D_MODEL), lambda i: (0, 0)),
            pl.BlockSpec((tm, D_MODEL), lambda i: (i, 0)),
            pl.BlockSpec((1, D_MODEL), lambda i: (0, 0)),
        ],
        out_specs=[pl.BlockSpec((tm, D_MODEL), lambda i: (i, 0)),
                   pl.BlockSpec((tm, D_MODEL), lambda i: (i, 0))],
        out_shape=[jax.ShapeDtypeStruct((m, D_MODEL), F32), jax.ShapeDtypeStruct((m, D_MODEL), BF16)],
        compiler_params=_cparams(("parallel",)),
        name="out_projection",
    )(o_a, o_b, w_out, x2d, norm_g)


def _ffn_kernel(hn_ref, wg_ref, wu_ref, wd_ref, h_ref, g_ref, y_ref, acc_ref, *, h_slices):
    f = pl.program_id(1)

    @pl.when(f == 0)
    def _():
        acc_ref[...] = jnp.zeros_like(acc_ref)

    @pl.when(f < h_slices)
    def _():
        rows = h_ref.shape[0]
        sl = pl.ds(pl.multiple_of(f * rows, rows), rows)
        acc_ref[sl, :] += h_ref[...]

    hn = hn_ref[...]
    gate = jnp.dot(hn, wg_ref[...], preferred_element_type=F32)
    up = jnp.dot(hn, wu_ref[...], preferred_element_type=F32)
    act = (gate * jax.nn.sigmoid(gate)) * up
    acc_ref[...] += jnp.dot(act.astype(BF16), wd_ref[...], preferred_element_type=F32)

    @pl.when(f == pl.num_programs(1) - 1)
    def _():
        h = acc_ref[...]
        ms = jnp.mean(h * h, axis=-1, keepdims=True)
        y_ref[...] = (h * lax.rsqrt(ms + RMS_EPS)) * g_ref[...]


def _ffn(hn, w_gate, w_up, w_down, h, norm_g, tm, tf, h_slices=8):
    m = hn.shape[0]
    assert D_FF // tf >= h_slices
    return pl.pallas_call(
        functools.partial(_ffn_kernel, h_slices=h_slices),
        grid=(m // tm, D_FF // tf),
        in_specs=[
            pl.BlockSpec((tm, D_MODEL), lambda i, f: (i, 0)),
            pl.BlockSpec((D_MODEL, tf), lambda i, f: (0, f)),
            pl.BlockSpec((D_MODEL, tf), lambda i, f: (0, f)),
            pl.BlockSpec((tf, D_MODEL), lambda i, f: (f, 0)),
            pl.BlockSpec((tm // h_slices, D_MODEL),
                         lambda i, f: (i * h_slices + jnp.minimum(f, h_slices - 1), 0)),
            pl.BlockSpec((1, D_MODEL), lambda i, f: (0, 0)),
        ],
        out_specs=pl.BlockSpec((tm, D_MODEL), lambda i, f: (i, 0)),
        out_shape=jax.ShapeDtypeStruct((m, D_MODEL), F32),
        scratch_shapes=[pltpu.VMEM((tm, D_MODEL), F32)],
        compiler_params=_cparams(("parallel", "arbitrary")),
        name="swiglu_ffn",
    )(hn, w_gate, w_up, w_down, h, norm_g)


def _pad_cols(a, n):
    return jnp.pad(a, ((0, 0), (0, n - a.shape[1])))


def _pad_rows(a, n):
    return jnp.pad(a, ((0, n - a.shape[0]), (0, 0)))


def kernel(x_prompt, x_sample, state_shift, state_rwkv, state_ret, meta_tokens, norm_mix, w_in,
           rwkv_mu, rwkv_w0, rwkv_w2, rwkv_a0, rwkv_a2, rwkv_g2, rwkv_kk, rwkv_ka, rwkv_rk,
           rwkv_ln_w, rwkv_ln_b, w_out, norm_ffn, w_gate, w_up, w_down, norm_final):
    n_b, seq, d = x_prompt.shape
    n_s, dec_seq, _ = x_sample.shape
    n_p = n_b * seq
    n_d = n_s * dec_seq
    depth = w_in.shape[0]
    assert depth == 1 and d == D_MODEL and n_s == LANES

    w_in_t = jnp.transpose(w_in[0])
    w_in_p = _w_in_layout(w_in_t, tn=512)
    row = lambda a: a.reshape(1, -1).astype(F32)
    w2p = jnp.concatenate([rwkv_w2[0], jnp.zeros((128 - DECAY_LORA, RW_WIDTH), F32)], axis=0)
    a2p = jnp.concatenate([jnp.zeros((DECAY_LORA, RW_WIDTH), F32), rwkv_a2[0]], axis=0)
    g2p = jnp.concatenate([rwkv_g2[0], jnp.zeros((256 - GATE_LORA, RW_WIDTH), F32)], axis=0)
    rw_params = dict(mu=_pad_cols(row(rwkv_mu[0]), SHIFT_PAD), w0=row(rwkv_w0[0]), w2=w2p,
                     a0=row(rwkv_a0[0]), a2=a2p, g2=g2p, kk=row(rwkv_kk[0]), ka=row(rwkv_ka[0]),
                     rk=row(rwkv_rk[0]), lnw=row(rwkv_ln_w[0]), lnb=row(rwkv_ln_b[0]))
    col = lambda a: jnp.broadcast_to(a.reshape(-1, 1).astype(F32), (a.size, LANES))
    cols = jnp.stack([a.reshape(-1).astype(F32) for a in (rwkv_w0[0], rwkv_a0[0], rwkv_kk[0], rwkv_ka[0],
                                                         rwkv_rk[0], rwkv_ln_w[0], rwkv_ln_b[0])])
    rw_params_t = dict(cols=jnp.broadcast_to(cols[:, :, None], cols.shape + (LANES,)),
                       w2t=rwkv_w2[0].T, a2t=rwkv_a2[0].T, g2t=rwkv_g2[0].T)

    x_p = x_prompt.reshape(n_p, d)
    x_s = x_sample.reshape(n_d, d)
    x_sm = jnp.concatenate([x_s, meta_tokens.astype(F32)], axis=0)
    x_ts = jnp.transpose(x_sample, (1, 0, 2)).reshape(n_d, d)
    g_mix = row(norm_mix[0])
    z_p = _in_projection(x_p, g_mix, w_in_p, tm=1024, tn=1536)
    z_sm = _in_projection(x_sm, g_mix, w_in_p, tm=x_sm.shape[0], tn=512, qk_interleaved=True)
    z_st = _in_projection_t(w_in_t, x_ts, g_mix, SHIFT_PAD, tm=512)

    z_meta = jnp.pad(z_sm[n_d:], ((RET_CHUNK - N_META, 0), (0, 0)))
    srow = 2 * dec_seq

    zero_prev = jnp.zeros((1, 1, SHIFT_PAD), F32)
    zero_rw = jnp.zeros((1, RW_HEADS, RW_HEAD, RW_HEAD), F32)
    c_rw = 64
    _, s_rw_meta, zlast_meta = _rwkv_mixer(z_meta, 1, 1, 1, zero_prev, zero_rw, rw_params, tb=c_rw, c=c_rw)
    oa_p, rwkv_p, zlast_p, w_out_b, w_gate_b, w_up_b, w_down_b = _rwkv_mixer(
        z_p, 0, n_b, seq // 256, zlast_meta, s_rw_meta, rw_params, tb=256, c=c_rw,
        cast=(w_out[0], w_gate[0], w_up[0], w_down[0]))
    shift_t = _pad_rows(jnp.transpose(state_shift[0]), SHIFT_PAD)
    mu_t = _pad_rows(col(rwkv_mu[0]), SHIFT_PAD)
    state_t = jnp.transpose(state_rwkv[0], (1, 2, 3, 0))
    oa_st, rwkv_st = _rwkv_sample_mixer(z_st, shift_t, mu_t, state_t, rw_params_t, n_tok=dec_seq, hpb=4)
    oa_s = jnp.transpose(oa_st, (1, 0, 2)).reshape(n_d, RW_WIDTH)
    rwkv_s = jnp.transpose(rwkv_st, (3, 0, 1, 2))

    past_len = 16384
    tabs_full = _retention_tables(RET_CHUNK, RET_CHUNK)
    tabs_meta = _retention_tables(RET_CHUNK, N_META)
    tabs_smp = _retention_tables(srow, dec_seq)
    sb_rt = 8
    cos_p, sin_p = _rotary_tables(N_META + np.arange(seq), interleaved=False)
    pos_ms = np.concatenate([np.arange(RET_CHUNK) - (RET_CHUNK - N_META),
                             past_len + np.tile(np.arange(srow) - (srow - dec_seq), sb_rt)])
    cos_ms, sin_ms = _rotary_tables(pos_ms)
    zero_rt = jnp.zeros((1, RET_HEADS, RET_HEAD, RET_HEAD), F32)
    _, s_rt_meta = _retention_mixer(z_meta, 0, 1, 1, cos_ms, sin_ms, tabs_meta, zero_rt,
                                    rows=RET_CHUNK, c=RET_CHUNK, n_real=N_META, per_chunk_state=False,
                                    pos_per_tile=(0, False))
    rows_rt = 4 * RET_CHUNK
    ob_p, ret_p = _retention_mixer(z_p, 0, n_b, seq // rows_rt, cos_p, sin_p, tabs_full, s_rt_meta,
                                   rows=rows_rt, c=RET_CHUNK, n_real=RET_CHUNK, per_chunk_state=False,
                                   pos_per_tile=(0, True), pair_split=True)
    ob_s, ret_s = _retention_mixer(z_sm, 0, n_s // sb_rt, 1, cos_ms, sin_ms, tabs_smp,
                                   state_ret.reshape(n_s, RET_HEADS, RET_HEAD, RET_HEAD),
                                   rows=sb_rt * srow, c=srow, n_real=dec_seq, per_chunk_state=True,
                                   pos_per_tile=(RET_CHUNK, False))

    g_ffn = row(norm_ffn[0])
    g_fin = row(norm_final)
    h_p, hn_p = _out_projection(oa_p, ob_p, w_out_b, x_p, g_ffn, tm=512)
    h_s, hn_s = _out_projection(oa_s, ob_s, w_out_b, x_s, g_ffn, tm=n_d)
    y_p = _ffn(hn_p, w_gate_b, w_up_b, w_down_b, h_p, g_fin, tm=1024, tf=512)
    y_s = _ffn(hn_s, w_gate_b, w_up_b, w_down_b, h_s, g_fin, tm=n_d, tf=512)

    y_prompt = y_p.reshape(n_b, seq, d)
    y_sample = y_s.reshape(n_s, dec_seq, d)
    shift_p = zlast_p[:, 0, :RW_COLS][None]
    shift_s = jnp.transpose(z_st[:RW_COLS, (dec_seq - 1) * n_s:])[None]
    return (y_prompt, y_sample, shift_p, rwkv_p[None], ret_p[None], shift_s, rwkv_s[None], ret_s[None])
```

```python
import functools
import math

import jax
import jax.numpy as jnp
import numpy as np
from jax import lax
from jax.experimental import pallas as pl
from jax.experimental.pallas import tpu as pltpu

F32 = jnp.float32
BF16 = jnp.bfloat16

D_MODEL = 2048
N_META = 16
RW_WIDTH = 1024
RW_HEAD = 64
RW_HEADS = 16
DECAY_LORA = 64
AAA_LORA = 64
GATE_LORA = 160
RW_COLS = 3 * RW_WIDTH + DECAY_LORA + AAA_LORA + GATE_LORA
RET_WIDTH = 1024
RET_HEADS = 4
RET_HEAD = 256
RET_CHUNK = 128
D_FF = 5632
RMS_EPS = 1e-6
RW_GN_EPS = 64e-5
RET_GN_EPS = 1e-6
ROPE_BASE = 10000.0
LANES = 128

LORA_PAD = 512
Z_COLS = 4 * RET_WIDTH + 3 * RW_WIDTH + LORA_PAD
RW_OFF = 4 * RET_WIDTH
SHIFT_PAD = 3 * RW_WIDTH + LORA_PAD

VMEM_LIMIT = 60 * 1024 * 1024
PAST_LEN = 16384


class Tiles:
    w_layout_cols = 512
    inproj_rows, inproj_cols = 1024, 1536
    inproj_small_cols = 512
    inproj_t_rows = 512
    rwkv_rows, rwkv_chunk = 256, 64
    rwkv_sample_heads = 2
    ret_chunks_per_step = 4
    ret_sample_seqs = 8
    outproj_rows = 512
    ffn_rows, ffn_cols = 1024, 512


def _cparams(sem):
    return pltpu.CompilerParams(dimension_semantics=sem, vmem_limit_bytes=VMEM_LIMIT)


def _mm(a, b):
    return jnp.dot(a.astype(BF16), b.astype(BF16), preferred_element_type=F32)


def _mm_nt(a, b):
    return lax.dot_general(a.astype(BF16), b.astype(BF16), (((1,), (1,)), ((), ())),
                           preferred_element_type=F32)


def _mm_tn(a, b):
    return lax.dot_general(a.astype(BF16), b.astype(BF16), (((0,), (0,)), ((), ())),
                           preferred_element_type=F32)


def _bmm(a, b):
    return jnp.einsum("bmk,bkn->bmn", a.astype(BF16), b.astype(BF16), preferred_element_type=F32)


def _bmm_nt(a, b):
    return jnp.einsum("bmk,bnk->bmn", a.astype(BF16), b.astype(BF16), preferred_element_type=F32)


def _bmm_tn(a, b):
    return jnp.einsum("bkm,bkn->bmn", a.astype(BF16), b.astype(BF16), preferred_element_type=F32)


def _mm_exact_lhs(m_bf16, x):
    hi = x.astype(BF16)
    r1 = x - hi.astype(F32)
    mid = r1.astype(BF16)
    lo = (r1 - mid.astype(F32)).astype(BF16)
    d = functools.partial(jnp.dot, preferred_element_type=F32)
    return d(m_bf16, hi) + d(m_bf16, mid) + d(m_bf16, lo)


def _mm_exact_lhs_t(m_bf16, x):
    hi = x.astype(BF16)
    r1 = x - hi.astype(F32)
    mid = r1.astype(BF16)
    lo = (r1 - mid.astype(F32)).astype(BF16)
    d = functools.partial(lax.dot_general, dimension_numbers=(((0,), (0,)), ((), ())),
                          preferred_element_type=F32)
    return d(m_bf16, hi) + d(m_bf16, mid) + d(m_bf16, lo)


def _rms_norm_bf16(x, g):
    ms = jnp.mean(x * x, axis=-1, keepdims=True)
    return ((x * lax.rsqrt(ms + RMS_EPS)) * g).astype(BF16)


def _pair_split_index(n):
    j = np.arange(n)
    half = RET_HEAD // 2
    return (j // RET_HEAD) * RET_HEAD + 2 * (j % half) + (j % RET_HEAD) // half


def _w_in_layout_kernel(w_ref, sel_ref, o_ref, *, n_qk):
    c = pl.program_id(0)

    @pl.when(c >= n_qk)
    def _():
        o_ref[...] = w_ref[...].T.astype(BF16)

    @pl.when(c < n_qk)
    def _():
        picked = jnp.dot(sel_ref[...], w_ref[...].astype(BF16), preferred_element_type=F32)
        o_ref[...] = picked.T.astype(BF16)


def _w_in_layout(w_in_t, tn):
    n, d = w_in_t.shape
    n_ret = 4 * RET_WIDTH // tn
    n_z = Z_COLS // tn
    n_qk = 2 * RET_WIDTH // tn
    sel = jnp.asarray(_pair_split_index(tn)[:, None] == np.arange(tn)[None, :], dtype=BF16)

    def src_row(c):
        align = math.gcd(RW_COLS, tn)
        row = jnp.where(c < n_ret, RW_COLS + tn * c,
                        jnp.where(c < n_z, tn * (c - n_ret), RW_COLS + tn * (c - n_z)))
        return (pl.multiple_of(row, align), 0)

    return pl.pallas_call(
        functools.partial(_w_in_layout_kernel, n_qk=n_qk),
        grid=(n_z + n_qk,),
        in_specs=[pl.BlockSpec((pl.Element(tn), pl.Element(d)), src_row),
                  pl.BlockSpec((tn, tn), lambda c: (0, 0))],
        out_specs=pl.BlockSpec((d, tn), lambda c: (0, c)),
        out_shape=jax.ShapeDtypeStruct((d, Z_COLS + 2 * RET_WIDTH), BF16),
        compiler_params=_cparams(("parallel",)),
        name="w_in_layout",
    )(w_in_t, sel)


def _inproj_kernel(x_ref, g_ref, w_ref, o_ref, xn_ref):
    @pl.when(pl.program_id(1) == 0)
    def _():
        xn_ref[...] = _rms_norm_bf16(x_ref[...], g_ref[...])

    o_ref[...] = jnp.dot(xn_ref[...], w_ref[...], preferred_element_type=F32)


def _in_projection(x2d, norm_g, w_in_p, tm, tn, qk_interleaved=False, n_cols=Z_COLS):
    m = x2d.shape[0]
    if qk_interleaved:
        n_qk = 2 * RET_WIDTH // tn
        w_col = lambda i, j: (0, jnp.where(j < n_qk, Z_COLS // tn + j, j))
    else:
        w_col = lambda i, j: (0, j)
    return pl.pallas_call(
        _inproj_kernel,
        grid=(m // tm, n_cols // tn),
        in_specs=[
            pl.BlockSpec((tm, D_MODEL), lambda i, j: (i, 0)),
            pl.BlockSpec((1, D_MODEL), lambda i, j: (0, 0)),
            pl.BlockSpec((D_MODEL, tn), w_col),
        ],
        out_specs=pl.BlockSpec((tm, tn), lambda i, j: (i, j)),
        out_shape=jax.ShapeDtypeStruct((m, n_cols), F32),
        scratch_shapes=[pltpu.VMEM((tm, D_MODEL), BF16)],
        compiler_params=_cparams(("parallel", "arbitrary")),
        name="in_projection",
    )(x2d, norm_g, w_in_p)


def _inproj_t_kernel(w_ref, x_ref, g_ref, o_ref, xn_ref):
    @pl.when(pl.program_id(0) == 0)
    def _():
        xn_ref[...] = _rms_norm_bf16(x_ref[...], g_ref[...])

    o_ref[...] = _mm_nt(w_ref[...], xn_ref[...])


def _in_projection_t(w_in_t, x2d, norm_g, n_rows, tm):
    m, d = x2d.shape
    return pl.pallas_call(
        _inproj_t_kernel,
        grid=(n_rows // tm,),
        in_specs=[
            pl.BlockSpec((tm, d), lambda i: (i, 0)),
            pl.BlockSpec((m, d), lambda i: (0, 0)),
            pl.BlockSpec((1, d), lambda i: (0, 0)),
        ],
        out_specs=pl.BlockSpec((tm, m), lambda i: (i, 0)),
        out_shape=jax.ShapeDtypeStruct((n_rows, m), F32),
        scratch_shapes=[pltpu.VMEM((m, d), BF16)],
        compiler_params=_cparams(("arbitrary",)),
        name="in_projection_t",
    )(w_in_t, x2d, norm_g)


def _rwkv_kernel(zr_ref, zk_ref, zv_ref, zl_ref, zprev0_ref, s0_ref,
                 mu_ref, w0_ref, w2_ref, a0_ref, a2_ref, g2_ref, kk_ref, ka_ref, rk_ref,
                 lnw_ref, lnb_ref,
                 o_ref, sout_ref, zlast_ref,
                 carry_ref, sbd_s, r_s, km_s, v_s, kk_s, a_s, lw_s, cum_s, o_s,
                 *, tb, c, n_factors):
    i = pl.program_id(1)
    n_chunks = tb // c
    n_pairs = RW_HEADS // 2
    pw = 2 * RW_HEAD

    @pl.when(i == 0)
    def _():
        carry_ref[...] = zprev0_ref[0]
        zero = jnp.zeros((RW_HEAD, RW_HEAD), F32)
        for p in range(n_pairs):
            top = jnp.concatenate([s0_ref[0, 2 * p], zero], axis=1)
            bot = jnp.concatenate([zero, s0_ref[0, 2 * p + 1]], axis=1)
            sbd_s[p] = jnp.concatenate([top, bot], axis=0)

    row = lax.broadcasted_iota(jnp.int32, (tb, 1), 0)

    def shifted(z_ref, lo, hi):
        z = z_ref[...]
        prev = pltpu.roll(z, 1, axis=0)
        prev = jnp.concatenate([jnp.where(row[:8] == 0, carry_ref[:, lo:hi], prev[:8]), prev[8:]], axis=0)
        zs = z + mu_ref[:, lo:hi] * (prev - z)
        carry_ref[:, lo:hi] = z[tb - 1:tb, :]
        return zs

    w = RW_WIDTH
    r = shifted(zr_ref, 0, w)
    k = shifted(zk_ref, w, 2 * w)
    v = shifted(zv_ref, 2 * w, 3 * w)
    zl = shifted(zl_ref, 3 * w, 3 * w + LORA_PAD)

    lo2 = zl[:, 0:128]
    wlog = -jax.nn.softplus(-(w0_ref[...] + _mm(jnp.tanh(lo2), w2_ref[...]))) - 0.5
    lw = -jnp.exp(wlog)
    a = jax.nn.sigmoid(a0_ref[...] + _mm(lo2, a2_ref[...]))
    g = _mm(jax.nn.sigmoid(zl[:, 128:384]), g2_ref[...])
    kk = k * kk_ref[...]
    km = k * (1.0 + (a - 1.0) * ka_ref[...])

    for dst, val in ((r_s, r), (km_s, km), (v_s, v), (a_s, a), (lw_s, lw), (kk_s, kk)):
        dst[...] = val

    rr = lax.broadcasted_iota(jnp.int32, (c, c), 0)
    cc = lax.broadcasted_iota(jnp.int32, (c, c), 1)
    tri = jnp.where(rr >= cc, 1.0, 0.0).astype(BF16)
    rr2 = lax.broadcasted_iota(jnp.int32, (c, 2 * c), 0)
    cc2 = lax.broadcasted_iota(jnp.int32, (c, 2 * c), 1)
    strict = (rr > cc)[None]
    eye = jnp.where(rr == cc, 1.0, 0.0).astype(F32)[None]
    strict_k = ((cc2 >= c) & (rr2 > cc2 - c))[None]
    incl_bk = (rr2 >= cc2 % c)[None]
    lo = (lax.broadcasted_iota(jnp.int32, (1, 1, pw), 2) < RW_HEAD)
    bd_r = lax.broadcasted_iota(jnp.int32, (pw, pw), 0) // RW_HEAD
    bd_c = lax.broadcasted_iota(jnp.int32, (pw, pw), 1) // RW_HEAD
    block_diag = (bd_r == bd_c)[None]
    rk = rk_ref[...]
    lnw = lnw_ref[...]
    lnb = lnb_ref[...]

    def head_sum(x):
        s_lo = jnp.sum(jnp.where(lo, x, 0.0), axis=-1, keepdims=True)
        s_hi = jnp.sum(jnp.where(lo, 0.0, x), axis=-1, keepdims=True)
        return jnp.where(lo, s_lo, s_hi)

    def pick(x16):
        x4 = x16.reshape(n_pairs, 2, c, pw)
        return jnp.where(lo, x4[:, 0], x4[:, 1])

    def body(ch, s_prev):
        rows = pl.ds(ch * c, c)

        def pairs(s):
            x = s[rows, :]
            return jnp.stack([x[:, p * pw:(p + 1) * pw] for p in range(n_pairs)], axis=0)

        cum_s[rows, :] = _mm_exact_lhs(tri, lw_s[rows, :])
        rh, kmh, vh, ah, lwh, cumh, kkh = [pairs(s) for s in (r_s, km_s, v_s, a_s, lw_s, cum_s, kk_s)]
        kkn = kkh * lax.rsqrt(jnp.maximum(head_sum(kkh * kkh), 1e-24))
        bvec = kkn * ah
        cum_end = cumh[:, c - 1:c, :]
        p_inc = jnp.exp(cumh)
        p_exc = jnp.exp(cumh - lwh)
        p_inv = jnp.exp(-cumh)
        p_end = jnp.exp(cum_end)
        p_rel = jnp.exp(cum_end - cumh)
        a_t = -(kkn * p_exc)
        r_t = rh * p_inc
        b_t = bvec * p_inv
        k_t = kmh * p_inv
        b_h = bvec * p_rel
        k_h = kmh * p_rel

        ar = jnp.concatenate([jnp.where(lo, a_t, 0.0), jnp.where(lo, r_t, 0.0),
                              jnp.where(lo, 0.0, a_t), jnp.where(lo, 0.0, r_t)], axis=1)
        prod = _bmm_nt(ar, jnp.concatenate([b_t, k_t, s_prev], axis=1))
        prod = prod.reshape(RW_HEADS, 2 * c, 2 * c + pw)
        a_rows, r_rows = prod[:, :c], prod[:, c:]
        aak_wide = jnp.where(strict_k, a_rows[:, :, :2 * c], 0.0)
        arbk = jnp.where(incl_bk, r_rows[:, :, :2 * c], 0.0)
        lmat = jnp.where(strict, a_rows[:, :, :c], 0.0)
        tinv = eye + lmat
        if n_factors > 1:
            lp = _bmm(lmat, lmat)
            for _ in range(n_factors - 2):
                st = _bmm(jnp.concatenate([lp, tinv], axis=1), lp)
                lp = st[:, :c]
                tinv = tinv + st[:, c:]
            tinv = tinv + _bmm(tinv, lp)
        v16 = jnp.repeat(vh, 2, axis=0)
        x = a_rows[:, :, 2 * c:] + _bmm(aak_wide, jnp.concatenate([v16, v16], axis=1))
        u16 = _bmm(tinv, x)
        o16 = r_rows[:, :, 2 * c:] + _bmm(arbk, jnp.concatenate([u16, v16], axis=1))
        u = pick(u16)
        o = pick(o16)
        s_upd = _bmm_tn(jnp.concatenate([u, vh], axis=1), jnp.concatenate([b_h, k_h], axis=1))
        s_new = s_prev * p_end + jnp.where(block_diag, s_upd, 0.0)
        inv_n = 1.0 / RW_HEAD
        mean = head_sum(o) * inv_n
        var = head_sum(jnp.square(o - mean)) * inv_n
        on = ((o - mean) * lax.rsqrt(var + RW_GN_EPS)) * lnw + lnb
        out = on + head_sum(rh * kmh * rk) * vh
        for p in range(n_pairs):
            o_s[rows, p * pw:(p + 1) * pw] = out[p]
        return s_new

    state = sbd_s[...]
    for ch in range(n_chunks):
        state = body(ch, state)
    sbd_s[...] = state

    o_ref[...] = (o_s[...] * g).astype(BF16)
    zlast_ref[0] = carry_ref[...]

    @pl.when(i == pl.num_programs(1) - 1)
    def _():
        for p in range(n_pairs):
            blk = sbd_s[p]
            sout_ref[0, 2 * p] = blk[:RW_HEAD, :RW_HEAD]
            sout_ref[0, 2 * p + 1] = blk[RW_HEAD:, RW_HEAD:]


N_RWKV_INPUTS = 17
N_RWKV_OUTPUTS = 3


def _rwkv_and_cast_kernel(*refs, n_cast, **kw):
    ins = refs[:N_RWKV_INPUTS]
    cast_in = refs[N_RWKV_INPUTS:N_RWKV_INPUTS + n_cast]
    o0 = N_RWKV_INPUTS + n_cast
    outs = refs[o0:o0 + N_RWKV_OUTPUTS]
    cast_out = refs[o0 + N_RWKV_OUTPUTS:o0 + N_RWKV_OUTPUTS + n_cast]
    scratch = refs[o0 + N_RWKV_OUTPUTS + n_cast:]
    for src, dst in zip(cast_in, cast_out):
        dst[...] = src[...].astype(BF16)
    _rwkv_kernel(*ins, *outs, *scratch, **kw)


def _rwkv_mixer(z, row_block0, n_groups, n_tiles, zprev0, s0, p, *, tb, c, cast=()):
    n_factors = max(1, math.ceil(math.log2(c)))
    kern = functools.partial(_rwkv_and_cast_kernel, n_cast=len(cast), tb=tb, c=c, n_factors=n_factors)
    n_steps = n_groups * n_tiles
    cast_specs = [pl.BlockSpec((a.shape[0] // n_steps, a.shape[1]), lambda gi, i: (gi * n_tiles + i, 0))
                  for a in cast]
    cast_shapes = [jax.ShapeDtypeStruct(a.shape, BF16) for a in cast]
    cb = RW_OFF // RW_WIDTH
    full = lambda arr: pl.BlockSpec(arr.shape, lambda gi, i: (0,) * arr.ndim)
    lora_cb = (RW_OFF + 3 * RW_WIDTH) // LORA_PAD
    per_pair = lambda a: a.reshape(RW_HEADS // 2, 1, 2 * RW_HEAD)
    params = [p["mu"], p["w0"], p["w2"], p["a0"], p["a2"], p["g2"], p["kk"], p["ka"],
              per_pair(p["rk"]), per_pair(p["lnw"]), per_pair(p["lnb"])]
    dense = lambda: pltpu.VMEM((tb, RW_WIDTH), F32)
    zspec = lambda width, col: pl.BlockSpec((tb, width), lambda gi, i: (row_block0 + gi * n_tiles + i, col))
    sspec = pl.BlockSpec((1, RW_HEADS, RW_HEAD, RW_HEAD), lambda gi, i: (gi, 0, 0, 0))
    shared = lambda a: (lambda gi, i: (0,) * a.ndim) if a.shape[0] == 1 else (lambda gi, i: (gi,) + (0,) * (a.ndim - 1))
    return pl.pallas_call(
        kern,
        grid=(n_groups, n_tiles),
        in_specs=[zspec(RW_WIDTH, cb), zspec(RW_WIDTH, cb + 1), zspec(RW_WIDTH, cb + 2),
                  zspec(LORA_PAD, lora_cb),
                  pl.BlockSpec((1, 1, SHIFT_PAD), shared(zprev0)),
                  pl.BlockSpec((1, RW_HEADS, RW_HEAD, RW_HEAD), shared(s0))]
                 + [full(a) for a in params] + cast_specs,
        out_specs=[pl.BlockSpec((tb, RW_WIDTH), lambda gi, i: (gi * n_tiles + i, 0)), sspec,
                   pl.BlockSpec((1, 1, SHIFT_PAD), lambda gi, i: (gi, 0, 0))] + cast_specs,
        out_shape=[jax.ShapeDtypeStruct((n_groups * n_tiles * tb, RW_WIDTH), BF16),
                   jax.ShapeDtypeStruct((n_groups, RW_HEADS, RW_HEAD, RW_HEAD), F32),
                   jax.ShapeDtypeStruct((n_groups, 1, SHIFT_PAD), F32)] + cast_shapes,
        scratch_shapes=[pltpu.VMEM((1, SHIFT_PAD), F32),
                        pltpu.VMEM((RW_HEADS // 2, 2 * RW_HEAD, 2 * RW_HEAD), F32)]
                       + [dense() for _ in range(8)],
        compiler_params=_cparams(("parallel", "arbitrary")),
        name="rwkv7_mixer",
    )(z, z, z, z, zprev0, s0, *params, *cast)


def _rwkv_sample_kernel(zr_ref, zk_ref, zv_ref, zl_ref, shr_ref, shk_ref, shv_ref, shl_ref,
                        mur_ref, muk_ref, muv_ref, mul_ref, s0_ref,
                        pc_ref, w2t_ref, a2t_ref, g2t_ref,
                        o_ref, sout_ref,
                        dec_s, a_s, b_s, k_s, r_s, v_s, o_s,
                        *, n_tok, hpb):
    w0_ref, a0_ref, kk_ref, ka_ref, rk_ref, lnw_ref, lnb_ref = [pc_ref.at[j] for j in range(7)]
    ns = LANES
    ch = hpb * RW_HEAD

    def lanes(x):
        return jnp.concatenate([x] * n_tok, axis=1)

    def shifted(z_ref, sh_ref, mu_ref):
        z = z_ref[...]
        prev = jnp.concatenate([sh_ref[...], z[:, :(n_tok - 1) * ns]], axis=1)
        return z + lanes(mu_ref[...]) * (prev - z)

    r = shifted(zr_ref, shr_ref, mur_ref)
    k = shifted(zk_ref, shk_ref, muk_ref)
    v = shifted(zv_ref, shv_ref, muv_ref)
    zl = shifted(zl_ref, shl_ref, mul_ref)
    wd = zl[0:DECAY_LORA]
    ad = zl[DECAY_LORA:DECAY_LORA + AAA_LORA]
    gd = zl[DECAY_LORA + AAA_LORA:DECAY_LORA + AAA_LORA + GATE_LORA]

    wlog = -jax.nn.softplus(-(lanes(w0_ref[...]) + _mm(w2t_ref[...], jnp.tanh(wd)))) - 0.5
    lw = -jnp.exp(wlog)
    a = jax.nn.sigmoid(lanes(a0_ref[...]) + _mm(a2t_ref[...], ad))
    g = _mm(g2t_ref[...], jax.nn.sigmoid(gd))
    kk = k * lanes(kk_ref[...])
    km = k * (1.0 + (a - 1.0) * lanes(ka_ref[...]))

    def head_sum(x):
        x3 = x.reshape(hpb, RW_HEAD, n_tok * ns)
        s = jnp.sum(x3, axis=1, keepdims=True)
        return jnp.broadcast_to(s, x3.shape).reshape(ch, n_tok * ns)

    kkn = kk * lax.rsqrt(jnp.maximum(head_sum(kk * kk), 1e-24))
    dec_s[...] = jnp.exp(lw)
    a_s[...] = -kkn
    b_s[...] = kkn * a
    k_s[...] = km
    r_s[...] = r
    v_s[...] = v

    for hh in range(hpb):
        hrows = slice(hh * RW_HEAD, (hh + 1) * RW_HEAD)

        def body(i8, carry, hh=hh, hrows=hrows):
            base = pl.multiple_of(i8 * 8, 8)
            rows8 = pl.ds(pl.multiple_of(hh * RW_HEAD + base, 8), 8)
            o_rows = [[] for _ in range(n_tok)]
            for j in range(8):
                s = s0_ref[hh, base + j]
                for t in range(n_tok):
                    tl = slice(t * ns, (t + 1) * ns)
                    sa = jnp.sum(s * a_s[hrows, tl], axis=0, keepdims=True)
                    v_row = v_s[rows8, tl][j:j + 1, :]
                    s = s * dec_s[hrows, tl] + sa * b_s[hrows, tl] + v_row * k_s[hrows, tl]
                    o_rows[t].append(jnp.sum(s * r_s[hrows, tl], axis=0, keepdims=True))
                sout_ref[hh, base + j] = s
            for t in range(n_tok):
                o_s[rows8, t * ns:(t + 1) * ns] = jnp.concatenate(o_rows[t], axis=0)
            return carry

        lax.fori_loop(0, RW_HEAD // 8, body, 0)

    o = o_s[...]
    inv_n = 1.0 / RW_HEAD
    mean = head_sum(o) * inv_n
    var = head_sum(jnp.square(o - mean)) * inv_n
    on = ((o - mean) * lax.rsqrt(var + RW_GN_EPS)) * lanes(lnw_ref[...]) + lanes(lnb_ref[...])
    bonus = head_sum(r * km * lanes(rk_ref[...])) * v
    out = (on + bonus) * g
    for t in range(n_tok):
        o_ref[t] = out[:, t * ns:(t + 1) * ns].T.astype(BF16)


def _rwkv_sample_mixer(z_t, shift_t, mu_t, state_t, p_t, *, n_tok, hpb):
    ch = hpb * RW_HEAD
    n_steps = RW_HEADS // hpb
    nl = n_tok * LANES
    seg = RW_WIDTH // ch
    zspec = lambda s: pl.BlockSpec((ch, nl), lambda h: (s * seg + h, 0))
    cspec = lambda s: pl.BlockSpec((ch, LANES), lambda h: (s * seg + h, 0))
    lora_blk = 3 * RW_WIDTH // LORA_PAD
    zl_spec = pl.BlockSpec((LORA_PAD, nl), lambda h: (lora_blk, 0))
    cl_spec = pl.BlockSpec((LORA_PAD, LANES), lambda h: (lora_blk, 0))
    pc_spec = pl.BlockSpec((p_t["cols"].shape[0], ch, LANES), lambda h: (0, h, 0))
    wspec = lambda k: pl.BlockSpec((ch, k), lambda h: (h, 0))
    sspec = pl.BlockSpec((hpb, RW_HEAD, RW_HEAD, LANES), lambda h: (h, 0, 0, 0))
    buf = lambda: pltpu.VMEM((ch, nl), F32)
    return pl.pallas_call(
        functools.partial(_rwkv_sample_kernel, n_tok=n_tok, hpb=hpb),
        grid=(n_steps,),
        in_specs=[zspec(0), zspec(1), zspec(2), zl_spec,
                  cspec(0), cspec(1), cspec(2), cl_spec,
                  cspec(0), cspec(1), cspec(2), cl_spec,
                  sspec,
                  pc_spec, wspec(DECAY_LORA), wspec(AAA_LORA), wspec(GATE_LORA)],
        out_specs=[pl.BlockSpec((n_tok, LANES, ch), lambda h: (0, 0, h)), sspec],
        out_shape=[jax.ShapeDtypeStruct((n_tok, LANES, RW_WIDTH), BF16),
                   jax.ShapeDtypeStruct(state_t.shape, F32)],
        scratch_shapes=[buf() for _ in range(7)],
        compiler_params=_cparams(("parallel",)),
        name="rwkv7_sample",
    )(z_t, z_t, z_t, z_t, shift_t, shift_t, shift_t, shift_t, mu_t, mu_t, mu_t, mu_t, state_t,
      p_t["cols"], p_t["w2t"], p_t["a2t"], p_t["g2t"])


def _retention_kernel(zq_ref, zk_ref, zv_ref, zg_ref, cos_ref, sin_ref, dmask_ref, iscale_ref,
                      kscale_ref, sdec_ref, sel_ref, s0_ref, o_ref, sout_ref, st_s,
                      *, rows, c, n_real, per_chunk_state, pair_split):
    i = pl.program_id(1)
    n_chunks = rows // c
    assert not (pair_split and per_chunk_state)

    if per_chunk_state:
        pr = lax.broadcasted_iota(jnp.int32, (rows, n_chunks * n_real), 0)
        pc = lax.broadcasted_iota(jnp.int32, (rows, n_chunks * n_real), 1)
        place = jnp.where((pr // c == pc // n_real) & (pr % c - (c - n_real) == pc % n_real),
                          1.0, 0.0).astype(BF16)
        load = lambda ref: _mm_exact_lhs(place, ref[...])
    else:
        load = lambda ref: ref[...]

        @pl.when(i == 0)
        def _():
            if pair_split:
                for h in range(RET_HEADS):
                    st_s[h] = _mm_exact_lhs(sel_ref[...], s0_ref[0, h])
            else:
                st_s[...] = s0_ref[0]

    if pair_split:
        cos = cos_ref[...]
        sin = sin_ref[...]
        half = RET_HEAD // 2

        def rot(x):
            parts = []
            for h in range(RET_HEADS):
                x0 = x[:, h * RET_HEAD:h * RET_HEAD + half]
                x1 = x[:, h * RET_HEAD + half:(h + 1) * RET_HEAD]
                parts += [x0 * cos - x1 * sin, x0 * sin + x1 * cos]
            return jnp.concatenate(parts, axis=-1)
    else:
        lane = lax.broadcasted_iota(jnp.int32, (rows, RET_WIDTH), 1)
        even = (lane % 2) == 0
        cos = jnp.concatenate([cos_ref[...]] * RET_HEADS, axis=-1)
        sin = jnp.concatenate([sin_ref[...]] * RET_HEADS, axis=-1)

        def rot(x):
            partner = jnp.where(even, pltpu.roll(x, RET_WIDTH - 1, axis=1), pltpu.roll(x, 1, axis=1))
            return x * cos + partner * sin

    q = rot(load(zq_ref))
    k = rot(load(zk_ref)) * (RET_HEAD ** -0.5)
    v = load(zv_ref)
    g = load(zg_ref)

    out_rows = []
    for ch in range(n_chunks):
        rs = slice(ch * c, (ch + 1) * c)
        out_heads = []
        for h in range(RET_HEADS):
            hs = slice(h * RET_HEAD, (h + 1) * RET_HEAD)
            qh, kh, vh = q[rs, hs], k[rs, hs], v[rs, hs]
            s_prev = s0_ref[ch, h] if per_chunk_state else st_s[h]
            scores = _mm_nt(qh, kh) * dmask_ref[h]
            o = _mm(scores, vh) + _mm(qh, s_prev) * iscale_ref[h]
            s_new = s_prev * sdec_ref[h] + _mm_tn(kh * kscale_ref[h], vh)
            if per_chunk_state:
                sout_ref[ch, h] = s_new
            else:
                st_s[h] = s_new
            o = o * lax.rsqrt(jnp.mean(o * o, axis=-1, keepdims=True) + RET_GN_EPS)
            gh = g[rs, hs]
            out_heads.append(o * (gh * jax.nn.sigmoid(gh)))
        out_rows.append(jnp.concatenate(out_heads, axis=-1))
    o_out = jnp.concatenate(out_rows, axis=0).astype(BF16)
    if per_chunk_state:
        o_out = lax.dot_general(place, o_out, (((0,), (0,)), ((), ())),
                                preferred_element_type=F32).astype(BF16)
    o_ref[...] = o_out

    if not per_chunk_state:
        @pl.when(i == pl.num_programs(1) - 1)
        def _():
            if pair_split:
                for h in range(RET_HEADS):
                    sout_ref[0, h] = _mm_exact_lhs_t(sel_ref[...], st_s[h])
            else:
                sout_ref[0] = st_s[...]


def _retention_mixer(z, row_block0, n_groups, n_tiles, cos, sin, tabs, s0, *, rows, c, n_real,
                     per_chunk_state, pos_per_tile, pair_split=False):
    shared_s0 = s0.shape[0] == 1
    sb = 1 if shared_s0 else s0.shape[0] // n_groups
    kern = functools.partial(_retention_kernel, rows=rows, c=c, n_real=n_real,
                             per_chunk_state=per_chunk_state, pair_split=pair_split)
    rows_io = rows // c * n_real if per_chunk_state else rows
    zspec = lambda col: pl.BlockSpec((rows_io, RET_WIDTH), lambda gi, i: (row_block0 + gi * n_tiles + i, col))
    full = lambda arr: pl.BlockSpec(arr.shape, lambda gi, i: (0,) * arr.ndim)
    tbl_row0, tbl_advance = pos_per_tile
    assert tbl_row0 % rows == 0
    tspec = pl.BlockSpec((rows, cos.shape[1]),
                         lambda gi, i: (tbl_row0 // rows + (i if tbl_advance else 0), 0))
    state_blk = (sb, RET_HEADS, RET_HEAD, RET_HEAD)
    s0_spec = pl.BlockSpec(state_blk, lambda gi, i: (0 if shared_s0 else gi, 0, 0, 0))
    sout_spec = pl.BlockSpec(state_blk, lambda gi, i: (gi, 0, 0, 0))
    dmask, iscale, kscale, sdec = tabs
    sel = jnp.asarray(_pair_split_index(RET_HEAD)[:, None] == np.arange(RET_HEAD)[None, :], dtype=BF16)
    return pl.pallas_call(
        kern,
        grid=(n_groups, n_tiles),
        in_specs=[zspec(0), zspec(1), zspec(2), zspec(3), tspec, tspec,
                  full(dmask), full(iscale), full(kscale), full(sdec), full(sel), s0_spec],
        out_specs=[pl.BlockSpec((rows_io, RET_WIDTH), lambda gi, i: (gi * n_tiles + i, 0)), sout_spec],
        out_shape=[jax.ShapeDtypeStruct((n_groups * n_tiles * rows_io, RET_WIDTH), BF16),
                   jax.ShapeDtypeStruct((n_groups * sb,) + state_blk[1:], F32)],
        scratch_shapes=[pltpu.VMEM((RET_HEADS, RET_HEAD, RET_HEAD), F32)],
        compiler_params=_cparams(("parallel", "arbitrary")),
        name="retention_mixer",
    )(z, z, z, z, cos, sin, dmask, iscale, kscale, sdec, sel, s0)


def _retention_tables(c, n_real):
    log_gamma = np.log(1.0 - 2.0 ** (-5.0 - np.arange(RET_HEADS, dtype=np.float64)))
    r = np.arange(c, dtype=np.float64)
    idx = r - float(c - n_real)
    diff = r[:, None] - r[None, :]
    dmask = np.where(diff[None] >= 0, np.exp(log_gamma[:, None, None] * np.maximum(diff, 0.0)[None]), 0.0)
    iscale = np.exp(log_gamma[:, None] * (idx + 1.0)[None, :])[:, :, None]
    kscale = np.exp(log_gamma[:, None] * (n_real - 1.0 - idx)[None, :])[:, :, None]
    sdec = np.broadcast_to(np.exp(log_gamma * n_real)[:, None, None], (RET_HEADS, 1, RET_HEAD))
    return tuple(jnp.asarray(t, dtype=F32) for t in (dmask, iscale, kscale, sdec))


def _rotary_tables(pos, interleaved=True):
    inv_freq = 1.0 / (ROPE_BASE ** np.linspace(0.0, 1.0, RET_HEAD // 2))
    ang = np.asarray(pos, dtype=np.float64)[:, None] * inv_freq[None, :]
    cos = np.cos(ang)
    sin = np.sin(ang)
    if not interleaved:
        return jnp.asarray(cos, dtype=F32), jnp.asarray(sin, dtype=F32)
    cos2 = np.repeat(cos, 2, axis=-1)
    sin2 = np.stack([-sin, sin], axis=-1).reshape(len(pos), RET_HEAD)
    return jnp.asarray(cos2, dtype=F32), jnp.asarray(sin2, dtype=F32)


def _outproj_kernel(oa_ref, ob_ref, w_ref, x_ref, g_ref, h_ref, hn_ref):
    acc = jnp.dot(oa_ref[...], w_ref[0:RW_WIDTH, :], preferred_element_type=F32)
    acc = acc + jnp.dot(ob_ref[...], w_ref[RW_WIDTH:, :], preferred_element_type=F32)
    h = x_ref[...] + acc
    h_ref[...] = h
    hn_ref[...] = _rms_norm_bf16(h, g_ref[...])


def _out_projection(o_a, o_b, w_out, x2d, norm_g, tm):
    m = o_a.shape[0]
    return pl.pallas_call(
        _outproj_kernel,
        grid=(m // tm,),
        in_specs=[
            pl.BlockSpec((tm, RW_WIDTH), lambda i: (i, 0)),
            pl.BlockSpec((tm, RET_WIDTH), lambda i: (i, 0)),
            pl.BlockSpec((D_MODEL, D_MODEL), lambda i: (0, 0)),
            pl.BlockSpec((tm, D_MODEL), lambda i: (i, 0)),
            pl.BlockSpec((1, D_MODEL), lambda i: (0, 0)),
        ],
        out_specs=[pl.BlockSpec((tm, D_MODEL), lambda i: (i, 0)),
                   pl.BlockSpec((tm, D_MODEL), lambda i: (i, 0))],
        out_shape=[jax.ShapeDtypeStruct((m, D_MODEL), F32), jax.ShapeDtypeStruct((m, D_MODEL), BF16)],
        compiler_params=_cparams(("parallel",)),
        name="out_projection",
    )(o_a, o_b, w_out, x2d, norm_g)


def _ffn_kernel(hn_ref, wg_ref, wu_ref, wd_ref, h_ref, g_ref, y_ref, acc_ref, *, h_slices):
    f = pl.program_id(1)

    @pl.when(f == 0)
    def _():
        acc_ref[...] = jnp.zeros_like(acc_ref)

    @pl.when(f < h_slices)
    def _():
        rows = h_ref.shape[0]
        sl = pl.ds(pl.multiple_of(f * rows, rows), rows)
        acc_ref[sl, :] += h_ref[...]

    hn = hn_ref[...]
    gate = jnp.dot(hn, wg_ref[...], preferred_element_type=F32)
    up = jnp.dot(hn, wu_ref[...], preferred_element_type=F32)
    act = (gate * jax.nn.sigmoid(gate)) * up
    acc_ref[...] += jnp.dot(act.astype(BF16), wd_ref[...], preferred_element_type=F32)

    @pl.when(f == pl.num_programs(1) - 1)
    def _():
        h = acc_ref[...]
        ms = jnp.mean(h * h, axis=-1, keepdims=True)
        y_ref[...] = (h * lax.rsqrt(ms + RMS_EPS)) * g_ref[...]


def _ffn(hn, w_gate, w_up, w_down, h, norm_g, tm, tf, h_slices=8):
    m = hn.shape[0]
    assert D_FF // tf >= h_slices
    return pl.pallas_call(
        functools.partial(_ffn_kernel, h_slices=h_slices),
        grid=(m // tm, D_FF // tf),
        in_specs=[
            pl.BlockSpec((tm, D_MODEL), lambda i, f: (i, 0)),
            pl.BlockSpec((D_MODEL, tf), lambda i, f: (0, f)),
            pl.BlockSpec((D_MODEL, tf), lambda i, f: (0, f)),
            pl.BlockSpec((tf, D_MODEL), lambda i, f: (f, 0)),
            pl.BlockSpec((tm // h_slices, D_MODEL),
                         lambda i, f: (i * h_slices + jnp.minimum(f, h_slices - 1), 0)),
            pl.BlockSpec((1, D_MODEL), lambda i, f: (0, 0)),
        ],
        out_specs=pl.BlockSpec((tm, D_MODEL), lambda i, f: (i, 0)),
        out_shape=jax.ShapeDtypeStruct((m, D_MODEL), F32),
        scratch_shapes=[pltpu.VMEM((tm, D_MODEL), F32)],
        compiler_params=_cparams(("parallel", "arbitrary")),
        name="swiglu_ffn",
    )(hn, w_gate, w_up, w_down, h, norm_g)


def _pad_cols(a, n):
    return jnp.pad(a, ((0, 0), (0, n - a.shape[1])))


def _pad_rows(a, n):
    return jnp.pad(a, ((0, n - a.shape[0]), (0, 0)))


def kernel(x_prompt, x_sample, state_shift, state_rwkv, state_ret, meta_tokens, norm_mix, w_in,
           rwkv_mu, rwkv_w0, rwkv_w2, rwkv_a0, rwkv_a2, rwkv_g2, rwkv_kk, rwkv_ka, rwkv_rk,
           rwkv_ln_w, rwkv_ln_b, w_out, norm_ffn, w_gate, w_up, w_down, norm_final):
    n_b, seq, d = x_prompt.shape
    n_s, dec_seq, _ = x_sample.shape
    n_p = n_b * seq
    n_d = n_s * dec_seq
    depth = w_in.shape[0]
    assert depth == 1 and d == D_MODEL and n_s == LANES

    w_in_t = jnp.transpose(w_in[0])
    w_in_p = _w_in_layout(w_in_t, tn=Tiles.w_layout_cols)
    row = lambda a: a.reshape(1, -1).astype(F32)
    w2p = jnp.concatenate([rwkv_w2[0], jnp.zeros((128 - DECAY_LORA, RW_WIDTH), F32)], axis=0)
    a2p = jnp.concatenate([jnp.zeros((DECAY_LORA, RW_WIDTH), F32), rwkv_a2[0]], axis=0)
    g2p = jnp.concatenate([rwkv_g2[0], jnp.zeros((256 - GATE_LORA, RW_WIDTH), F32)], axis=0)
    rw_params = dict(mu=_pad_cols(row(rwkv_mu[0]), SHIFT_PAD), w0=row(rwkv_w0[0]), w2=w2p,
                     a0=row(rwkv_a0[0]), a2=a2p, g2=g2p, kk=row(rwkv_kk[0]), ka=row(rwkv_ka[0]),
                     rk=row(rwkv_rk[0]), lnw=row(rwkv_ln_w[0]), lnb=row(rwkv_ln_b[0]))
    col = lambda a: jnp.broadcast_to(a.reshape(-1, 1).astype(F32), (a.size, LANES))
    cols = jnp.stack([a.reshape(-1).astype(F32) for a in (rwkv_w0[0], rwkv_a0[0], rwkv_kk[0], rwkv_ka[0],
                                                         rwkv_rk[0], rwkv_ln_w[0], rwkv_ln_b[0])])
    rw_params_t = dict(cols=jnp.broadcast_to(cols[:, :, None], cols.shape + (LANES,)),
                       w2t=rwkv_w2[0].T, a2t=rwkv_a2[0].T, g2t=rwkv_g2[0].T)

    x_p = x_prompt.reshape(n_p, d)
    x_s = x_sample.reshape(n_d, d)
    meta = meta_tokens.astype(F32)
    x_sm = jnp.concatenate([x_s, meta], axis=0)
    x_tsm = jnp.concatenate([jnp.transpose(x_sample, (1, 0, 2)).reshape(n_d, d), meta,
                             jnp.zeros((LANES - N_META, d), F32)], axis=0)
    g_mix = row(norm_mix[0])
    z_p = _in_projection(x_p, g_mix, w_in_p, tm=Tiles.inproj_rows, tn=Tiles.inproj_cols)
    z_sm = _in_projection(x_sm, g_mix, w_in_p, tm=x_sm.shape[0], tn=Tiles.inproj_small_cols,
                          qk_interleaved=True, n_cols=RW_OFF)
    z_st = _in_projection_t(w_in_t, x_tsm, g_mix, SHIFT_PAD, tm=Tiles.inproj_t_rows)

    z_meta = jnp.pad(jnp.concatenate([z_sm[n_d:], jnp.transpose(z_st[:, n_d:n_d + N_META])], axis=1),
                     ((RET_CHUNK - N_META, 0), (0, 0)))
    srow = 2 * dec_seq

    zero_prev = jnp.zeros((1, 1, SHIFT_PAD), F32)
    zero_rw = jnp.zeros((1, RW_HEADS, RW_HEAD, RW_HEAD), F32)
    c_rw = Tiles.rwkv_chunk
    _, s_rw_meta, zlast_meta = _rwkv_mixer(z_meta, 1, 1, 1, zero_prev, zero_rw, rw_params, tb=c_rw, c=c_rw)
    oa_p, rwkv_p, zlast_p, w_out_b, w_gate_b, w_up_b, w_down_b = _rwkv_mixer(
        z_p, 0, n_b, seq // Tiles.rwkv_rows, zlast_meta, s_rw_meta, rw_params, tb=Tiles.rwkv_rows, c=c_rw,
        cast=(w_out[0], w_gate[0], w_up[0], w_down[0]))
    shift_t = _pad_rows(jnp.transpose(state_shift[0]), SHIFT_PAD)
    mu_t = _pad_rows(col(rwkv_mu[0]), SHIFT_PAD)
    state_t = jnp.transpose(state_rwkv[0], (1, 2, 3, 0))
    oa_st, rwkv_st = _rwkv_sample_mixer(z_st, shift_t, mu_t, state_t, rw_params_t, n_tok=dec_seq,
                                        hpb=Tiles.rwkv_sample_heads)
    oa_s = jnp.transpose(oa_st, (1, 0, 2)).reshape(n_d, RW_WIDTH)
    rwkv_s = jnp.transpose(rwkv_st, (3, 0, 1, 2))

    tabs_full = _retention_tables(RET_CHUNK, RET_CHUNK)
    tabs_meta = _retention_tables(RET_CHUNK, N_META)
    tabs_smp = _retention_tables(srow, dec_seq)
    sb_rt = Tiles.ret_sample_seqs
    cos_p, sin_p = _rotary_tables(N_META + np.arange(seq), interleaved=False)
    pos_ms = np.concatenate([np.arange(RET_CHUNK) - (RET_CHUNK - N_META),
                             PAST_LEN + np.tile(np.arange(srow) - (srow - dec_seq), sb_rt)])
    cos_ms, sin_ms = _rotary_tables(pos_ms)
    zero_rt = jnp.zeros((1, RET_HEADS, RET_HEAD, RET_HEAD), F32)
    _, s_rt_meta = _retention_mixer(z_meta, 0, 1, 1, cos_ms, sin_ms, tabs_meta, zero_rt,
                                    rows=RET_CHUNK, c=RET_CHUNK, n_real=N_META, per_chunk_state=False,
                                    pos_per_tile=(0, False))
    rows_rt = Tiles.ret_chunks_per_step * RET_CHUNK
    ob_p, ret_p = _retention_mixer(z_p, 0, n_b, seq // rows_rt, cos_p, sin_p, tabs_full, s_rt_meta,
                                   rows=rows_rt, c=RET_CHUNK, n_real=RET_CHUNK, per_chunk_state=False,
                                   pos_per_tile=(0, True), pair_split=True)
    ob_s, ret_s = _retention_mixer(z_sm, 0, n_s // sb_rt, 1, cos_ms, sin_ms, tabs_smp,
                                   state_ret.reshape(n_s, RET_HEADS, RET_HEAD, RET_HEAD),
                                   rows=sb_rt * srow, c=srow, n_real=dec_seq, per_chunk_state=True,
                                   pos_per_tile=(RET_CHUNK, False))

    g_ffn = row(norm_ffn[0])
    g_fin = row(norm_final)
    h_p, hn_p = _out_projection(oa_p, ob_p, w_out_b, x_p, g_ffn, tm=Tiles.outproj_rows)
    h_s, hn_s = _out_projection(oa_s, ob_s, w_out_b, x_s, g_ffn, tm=n_d)
    y_p = _ffn(hn_p, w_gate_b, w_up_b, w_down_b, h_p, g_fin, tm=Tiles.ffn_rows, tf=Tiles.ffn_cols)
    y_s = _ffn(hn_s, w_gate_b, w_up_b, w_down_b, h_s, g_fin, tm=n_d, tf=Tiles.ffn_cols)

    y_prompt = y_p.reshape(n_b, seq, d)
    y_sample = y_s.reshape(n_s, dec_seq, d)
    shift_p = zlast_p[:, 0, :RW_COLS][None]
    shift_s = jnp.transpose(z_st[:RW_COLS, (dec_seq - 1) * n_s:n_d])[None]
    return (y_prompt, y_sample, shift_p, rwkv_p[None], ret_p[None], shift_s, rwkv_s[None], ret_s[None])
```

```python
import functools
import math

import jax
import jax.numpy as jnp
import numpy as np
from jax import lax
from jax.experimental import pallas as pl
from jax.experimental.pallas import tpu as pltpu

F32 = jnp.float32
BF16 = jnp.bfloat16

D_MODEL = 2048
N_META = 16
RW_WIDTH = 1024
RW_HEAD = 64
RW_HEADS = 16
DECAY_LORA = 64
AAA_LORA = 64
GATE_LORA = 160
RW_COLS = 3 * RW_WIDTH + DECAY_LORA + AAA_LORA + GATE_LORA
RET_WIDTH = 1024
RET_HEADS = 4
RET_HEAD = 256
RET_CHUNK = 128
D_FF = 5632
RMS_EPS = 1e-6
RW_GN_EPS = 64e-5
RET_GN_EPS = 1e-6
ROPE_BASE = 10000.0
LANES = 128

LORA_PAD = 512
Z_COLS = 4 * RET_WIDTH + 3 * RW_WIDTH + LORA_PAD
RW_OFF = 4 * RET_WIDTH
SHIFT_PAD = 3 * RW_WIDTH + LORA_PAD

VMEM_LIMIT = 60 * 1024 * 1024
PAST_LEN = 16384


class Tiles:
    w_layout_cols = 512
    inproj_rows, inproj_cols = 1024, 1536
    inproj_small_cols = 512
    inproj_t_rows = 512
    rwkv_rows, rwkv_chunk = 256, 64
    rwkv_sample_heads = 2
    ret_chunks_per_step = 4
    ret_sample_seqs = 8
    outproj_rows = 512
    ffn_rows, ffn_cols = 1024, 512


def _cparams(sem):
    return pltpu.CompilerParams(dimension_semantics=sem, vmem_limit_bytes=VMEM_LIMIT)


def _mm(a, b):
    return jnp.dot(a.astype(BF16), b.astype(BF16), preferred_element_type=F32)


def _mm_nt(a, b):
    return lax.dot_general(a.astype(BF16), b.astype(BF16), (((1,), (1,)), ((), ())),
                           preferred_element_type=F32)


def _mm_tn(a, b):
    return lax.dot_general(a.astype(BF16), b.astype(BF16), (((0,), (0,)), ((), ())),
                           preferred_element_type=F32)


def _bmm(a, b):
    return jnp.einsum("bmk,bkn->bmn", a.astype(BF16), b.astype(BF16), preferred_element_type=F32)


def _bmm_nt(a, b):
    return jnp.einsum("bmk,bnk->bmn", a.astype(BF16), b.astype(BF16), preferred_element_type=F32)


def _bmm_tn(a, b):
    return jnp.einsum("bkm,bkn->bmn", a.astype(BF16), b.astype(BF16), preferred_element_type=F32)


def _mm_exact_lhs(m_bf16, x):
    hi = x.astype(BF16)
    r1 = x - hi.astype(F32)
    mid = r1.astype(BF16)
    lo = (r1 - mid.astype(F32)).astype(BF16)
    d = functools.partial(jnp.dot, preferred_element_type=F32)
    return d(m_bf16, hi) + d(m_bf16, mid) + d(m_bf16, lo)


def _mm_exact_rhs(x, m_bf16):
    hi = x.astype(BF16)
    r1 = x - hi.astype(F32)
    mid = r1.astype(BF16)
    lo = (r1 - mid.astype(F32)).astype(BF16)
    d = functools.partial(jnp.dot, preferred_element_type=F32)
    return d(hi, m_bf16) + d(mid, m_bf16) + d(lo, m_bf16)


def _mm_exact_lhs_t(m_bf16, x):
    hi = x.astype(BF16)
    r1 = x - hi.astype(F32)
    mid = r1.astype(BF16)
    lo = (r1 - mid.astype(F32)).astype(BF16)
    d = functools.partial(lax.dot_general, dimension_numbers=(((0,), (0,)), ((), ())),
                          preferred_element_type=F32)
    return d(m_bf16, hi) + d(m_bf16, mid) + d(m_bf16, lo)


def _rms_norm_bf16(x, g):
    ms = jnp.mean(x * x, axis=-1, keepdims=True)
    return ((x * lax.rsqrt(ms + RMS_EPS)) * g).astype(BF16)


def _pair_split_index(n):
    j = np.arange(n)
    half = RET_HEAD // 2
    return (j // RET_HEAD) * RET_HEAD + 2 * (j % half) + (j % RET_HEAD) // half


def _w_in_layout_kernel(w_ref, sel_ref, o_ref, *, n_qk):
    c = pl.program_id(0)

    @pl.when(c >= n_qk)
    def _():
        o_ref[...] = w_ref[...].T.astype(BF16)

    @pl.when(c < n_qk)
    def _():
        picked = jnp.dot(sel_ref[...], w_ref[...].astype(BF16), preferred_element_type=F32)
        o_ref[...] = picked.T.astype(BF16)


def _w_in_layout(w_in_t, tn):
    n, d = w_in_t.shape
    n_ret = 4 * RET_WIDTH // tn
    n_qk = 2 * RET_WIDTH // tn
    sel = jnp.asarray(_pair_split_index(tn)[:, None] == np.arange(tn)[None, :], dtype=BF16)

    def src_row(c):
        align = math.gcd(RW_COLS, tn)
        return (pl.multiple_of(jnp.where(c < n_ret, RW_COLS + tn * c, tn * (c - n_ret)), align), 0)

    return pl.pallas_call(
        functools.partial(_w_in_layout_kernel, n_qk=n_qk),
        grid=(Z_COLS // tn,),
        in_specs=[pl.BlockSpec((pl.Element(tn), pl.Element(d)), src_row),
                  pl.BlockSpec((tn, tn), lambda c: (0, 0))],
        out_specs=pl.BlockSpec((d, tn), lambda c: (0, c)),
        out_shape=jax.ShapeDtypeStruct((d, Z_COLS), BF16),
        compiler_params=_cparams(("parallel",)),
        name="w_in_layout",
    )(w_in_t, sel)


def _inproj_kernel(x_ref, g_ref, w_ref, sel_ref, o_ref, xn_ref, *, n_qk_interleave):
    j = pl.program_id(1)

    @pl.when(j == 0)
    def _():
        xn_ref[...] = _rms_norm_bf16(x_ref[...], g_ref[...])

    @pl.when(j >= n_qk_interleave)
    def _():
        o_ref[...] = jnp.dot(xn_ref[...], w_ref[...], preferred_element_type=F32)

    if n_qk_interleave:
        @pl.when(j < n_qk_interleave)
        def _():
            z = jnp.dot(xn_ref[...], w_ref[...], preferred_element_type=F32)
            o_ref[...] = _mm_exact_rhs(z, sel_ref[...])


def _in_projection(x2d, norm_g, w_in_p, tm, tn, qk_interleaved=False, n_cols=Z_COLS):
    m = x2d.shape[0]
    n_qk = 2 * RET_WIDTH // tn if qk_interleaved else 0
    assert not qk_interleaved or tn % RET_HEAD == 0
    ts = tn if qk_interleaved else LANES
    sel = jnp.asarray(_pair_split_index(ts)[:, None] == np.arange(ts)[None, :], dtype=BF16)
    return pl.pallas_call(
        functools.partial(_inproj_kernel, n_qk_interleave=n_qk),
        grid=(m // tm, n_cols // tn),
        in_specs=[
            pl.BlockSpec((tm, D_MODEL), lambda i, j: (i, 0)),
            pl.BlockSpec((1, D_MODEL), lambda i, j: (0, 0)),
            pl.BlockSpec((D_MODEL, tn), lambda i, j: (0, j)),
            pl.BlockSpec((ts, ts), lambda i, j: (0, 0)),
        ],
        out_specs=pl.BlockSpec((tm, tn), lambda i, j: (i, j)),
        out_shape=jax.ShapeDtypeStruct((m, n_cols), F32),
        scratch_shapes=[pltpu.VMEM((tm, D_MODEL), BF16)],
        compiler_params=_cparams(("parallel", "arbitrary")),
        name="in_projection",
    )(x2d, norm_g, w_in_p, sel)


def _inproj_t_kernel(w_ref, x_ref, g_ref, o_ref, xn_ref):
    @pl.when(pl.program_id(0) == 0)
    def _():
        xn_ref[...] = _rms_norm_bf16(x_ref[...], g_ref[...])

    o_ref[...] = _mm_nt(w_ref[...], xn_ref[...])


def _in_projection_t(w_in_t, x2d, norm_g, n_rows, tm):
    m, d = x2d.shape
    return pl.pallas_call(
        _inproj_t_kernel,
        grid=(n_rows // tm,),
        in_specs=[
            pl.BlockSpec((tm, d), lambda i: (i, 0)),
            pl.BlockSpec((m, d), lambda i: (0, 0)),
            pl.BlockSpec((1, d), lambda i: (0, 0)),
        ],
        out_specs=pl.BlockSpec((tm, m), lambda i: (i, 0)),
        out_shape=jax.ShapeDtypeStruct((n_rows, m), F32),
        scratch_shapes=[pltpu.VMEM((m, d), BF16)],
        compiler_params=_cparams(("arbitrary",)),
        name="in_projection_t",
    )(w_in_t, x2d, norm_g)


def _rwkv_kernel(zr_ref, zk_ref, zv_ref, zl_ref, zprev0_ref, s0_ref,
                 mu_ref, w0_ref, w2_ref, a0_ref, a2_ref, g2_ref, kk_ref, ka_ref, rk_ref,
                 lnw_ref, lnb_ref,
                 o_ref, sout_ref, zlast_ref,
                 carry_ref, sbd_s, r_s, km_s, v_s, kk_s, a_s, lw_s, cum_s, o_s,
                 *, tb, c, n_factors):
    i = pl.program_id(1)
    n_chunks = tb // c
    n_pairs = RW_HEADS // 2
    pw = 2 * RW_HEAD

    @pl.when(i == 0)
    def _():
        carry_ref[...] = zprev0_ref[0]
        zero = jnp.zeros((RW_HEAD, RW_HEAD), F32)
        for p in range(n_pairs):
            top = jnp.concatenate([s0_ref[0, 2 * p], zero], axis=1)
            bot = jnp.concatenate([zero, s0_ref[0, 2 * p + 1]], axis=1)
            sbd_s[p] = jnp.concatenate([top, bot], axis=0)

    row = lax.broadcasted_iota(jnp.int32, (tb, 1), 0)

    def shifted(z_ref, lo, hi):
        z = z_ref[...]
        prev = pltpu.roll(z, 1, axis=0)
        prev = jnp.concatenate([jnp.where(row[:8] == 0, carry_ref[:, lo:hi], prev[:8]), prev[8:]], axis=0)
        zs = z + mu_ref[:, lo:hi] * (prev - z)
        carry_ref[:, lo:hi] = z[tb - 1:tb, :]
        return zs

    w = RW_WIDTH
    r = shifted(zr_ref, 0, w)
    k = shifted(zk_ref, w, 2 * w)
    v = shifted(zv_ref, 2 * w, 3 * w)
    zl = shifted(zl_ref, 3 * w, 3 * w + LORA_PAD)

    lo2 = zl[:, 0:128]
    wlog = -jax.nn.softplus(-(w0_ref[...] + _mm(jnp.tanh(lo2), w2_ref[...]))) - 0.5
    lw = -jnp.exp(wlog)
    a = jax.nn.sigmoid(a0_ref[...] + _mm(lo2, a2_ref[...]))
    g = _mm(jax.nn.sigmoid(zl[:, 128:384]), g2_ref[...])
    kk = k * kk_ref[...]
    km = k * (1.0 + (a - 1.0) * ka_ref[...])

    for dst, val in ((r_s, r), (km_s, km), (v_s, v), (a_s, a), (lw_s, lw), (kk_s, kk)):
        dst[...] = val

    rr = lax.broadcasted_iota(jnp.int32, (c, c), 0)
    cc = lax.broadcasted_iota(jnp.int32, (c, c), 1)
    tri = jnp.where(rr >= cc, 1.0, 0.0).astype(BF16)
    rr2 = lax.broadcasted_iota(jnp.int32, (c, 2 * c), 0)
    cc2 = lax.broadcasted_iota(jnp.int32, (c, 2 * c), 1)
    strict = (rr > cc)[None]
    eye = jnp.where(rr == cc, 1.0, 0.0).astype(F32)[None]
    strict_k = ((cc2 >= c) & (rr2 > cc2 - c))[None]
    incl_bk = (rr2 >= cc2 % c)[None]
    lo = (lax.broadcasted_iota(jnp.int32, (1, 1, pw), 2) < RW_HEAD)
    bd_r = lax.broadcasted_iota(jnp.int32, (pw, pw), 0) // RW_HEAD
    bd_c = lax.broadcasted_iota(jnp.int32, (pw, pw), 1) // RW_HEAD
    block_diag = (bd_r == bd_c)[None]
    rk = rk_ref[...]
    lnw = lnw_ref[...]
    lnb = lnb_ref[...]

    def head_sum(x):
        s_lo = jnp.sum(jnp.where(lo, x, 0.0), axis=-1, keepdims=True)
        s_hi = jnp.sum(jnp.where(lo, 0.0, x), axis=-1, keepdims=True)
        return jnp.where(lo, s_lo, s_hi)

    def pick(x16):
        x4 = x16.reshape(n_pairs, 2, c, pw)
        return jnp.where(lo, x4[:, 0], x4[:, 1])

    def body(ch, s_prev):
        rows = pl.ds(ch * c, c)

        def pairs(s):
            x = s[rows, :]
            return jnp.stack([x[:, p * pw:(p + 1) * pw] for p in range(n_pairs)], axis=0)

        cum_s[rows, :] = _mm_exact_lhs(tri, lw_s[rows, :])
        rh, kmh, vh, ah, lwh, cumh, kkh = [pairs(s) for s in (r_s, km_s, v_s, a_s, lw_s, cum_s, kk_s)]
        kkn = kkh * lax.rsqrt(jnp.maximum(head_sum(kkh * kkh), 1e-24))
        bvec = kkn * ah
        cum_end = cumh[:, c - 1:c, :]
        p_inc = jnp.exp(cumh)
        p_exc = jnp.exp(cumh - lwh)
        p_inv = jnp.exp(-cumh)
        p_end = jnp.exp(cum_end)
        p_rel = jnp.exp(cum_end - cumh)
        a_t = -(kkn * p_exc)
        r_t = rh * p_inc
        b_t = bvec * p_inv
        k_t = kmh * p_inv
        b_h = bvec * p_rel
        k_h = kmh * p_rel

        ar = jnp.concatenate([jnp.where(lo, a_t, 0.0), jnp.where(lo, r_t, 0.0),
                              jnp.where(lo, 0.0, a_t), jnp.where(lo, 0.0, r_t)], axis=1)
        prod = _bmm_nt(ar, jnp.concatenate([b_t, k_t, s_prev], axis=1))
        prod = prod.reshape(RW_HEADS, 2 * c, 2 * c + pw)
        a_rows, r_rows = prod[:, :c], prod[:, c:]
        aak_wide = jnp.where(strict_k, a_rows[:, :, :2 * c], 0.0)
        arbk = jnp.where(incl_bk, r_rows[:, :, :2 * c], 0.0)
        lmat = jnp.where(strict, a_rows[:, :, :c], 0.0)
        tinv = eye + lmat
        if n_factors > 1:
            lp = _bmm(lmat, lmat)
            for _ in range(n_factors - 2):
                st = _bmm(jnp.concatenate([lp, tinv], axis=1), lp)
                lp = st[:, :c]
                tinv = tinv + st[:, c:]
            tinv = tinv + _bmm(tinv, lp)
        v16 = jnp.repeat(vh, 2, axis=0)
        x = a_rows[:, :, 2 * c:] + _bmm(aak_wide, jnp.concatenate([v16, v16], axis=1))
        u16 = _bmm(tinv, x)
        o16 = r_rows[:, :, 2 * c:] + _bmm(arbk, jnp.concatenate([u16, v16], axis=1))
        u = pick(u16)
        o = pick(o16)
        s_upd = _bmm_tn(jnp.concatenate([u, vh], axis=1), jnp.concatenate([b_h, k_h], axis=1))
        s_new = s_prev * p_end + jnp.where(block_diag, s_upd, 0.0)
        inv_n = 1.0 / RW_HEAD
        mean = head_sum(o) * inv_n
        var = head_sum(jnp.square(o - mean)) * inv_n
        on = ((o - mean) * lax.rsqrt(var + RW_GN_EPS)) * lnw + lnb
        out = on + head_sum(rh * kmh * rk) * vh
        for p in range(n_pairs):
            o_s[rows, p * pw:(p + 1) * pw] = out[p]
        return s_new

    state = sbd_s[...]
    for ch in range(n_chunks):
        state = body(ch, state)
    sbd_s[...] = state

    o_ref[...] = (o_s[...] * g).astype(BF16)
    zlast_ref[0] = carry_ref[...]

    @pl.when(i == pl.num_programs(1) - 1)
    def _():
        for p in range(n_pairs):
            blk = sbd_s[p]
            sout_ref[0, 2 * p] = blk[:RW_HEAD, :RW_HEAD]
            sout_ref[0, 2 * p + 1] = blk[RW_HEAD:, RW_HEAD:]


N_RWKV_INPUTS = 17
N_RWKV_OUTPUTS = 3


def _rwkv_and_cast_kernel(*refs, n_cast, **kw):
    ins = refs[:N_RWKV_INPUTS]
    cast_in = refs[N_RWKV_INPUTS:N_RWKV_INPUTS + n_cast]
    o0 = N_RWKV_INPUTS + n_cast
    outs = refs[o0:o0 + N_RWKV_OUTPUTS]
    cast_out = refs[o0 + N_RWKV_OUTPUTS:o0 + N_RWKV_OUTPUTS + n_cast]
    scratch = refs[o0 + N_RWKV_OUTPUTS + n_cast:]
    for src, dst in zip(cast_in, cast_out):
        dst[...] = src[...].astype(BF16)
    _rwkv_kernel(*ins, *outs, *scratch, **kw)


def _rwkv_mixer(z, row_block0, n_groups, n_tiles, zprev0, s0, p, *, tb, c, cast=()):
    n_factors = max(1, math.ceil(math.log2(c)))
    kern = functools.partial(_rwkv_and_cast_kernel, n_cast=len(cast), tb=tb, c=c, n_factors=n_factors)
    n_steps = n_groups * n_tiles
    cast_specs = [pl.BlockSpec((a.shape[0] // n_steps, a.shape[1]), lambda gi, i: (gi * n_tiles + i, 0))
                  for a in cast]
    cast_shapes = [jax.ShapeDtypeStruct(a.shape, BF16) for a in cast]
    cb = RW_OFF // RW_WIDTH
    full = lambda arr: pl.BlockSpec(arr.shape, lambda gi, i: (0,) * arr.ndim)
    lora_cb = (RW_OFF + 3 * RW_WIDTH) // LORA_PAD
    per_pair = lambda a: a.reshape(RW_HEADS // 2, 1, 2 * RW_HEAD)
    params = [p["mu"], p["w0"], p["w2"], p["a0"], p["a2"], p["g2"], p["kk"], p["ka"],
              per_pair(p["rk"]), per_pair(p["lnw"]), per_pair(p["lnb"])]
    dense = lambda: pltpu.VMEM((tb, RW_WIDTH), F32)
    zspec = lambda width, col: pl.BlockSpec((tb, width), lambda gi, i: (row_block0 + gi * n_tiles + i, col))
    sspec = pl.BlockSpec((1, RW_HEADS, RW_HEAD, RW_HEAD), lambda gi, i: (gi, 0, 0, 0))
    shared = lambda a: (lambda gi, i: (0,) * a.ndim) if a.shape[0] == 1 else (lambda gi, i: (gi,) + (0,) * (a.ndim - 1))
    return pl.pallas_call(
        kern,
        grid=(n_groups, n_tiles),
        in_specs=[zspec(RW_WIDTH, cb), zspec(RW_WIDTH, cb + 1), zspec(RW_WIDTH, cb + 2),
                  zspec(LORA_PAD, lora_cb),
                  pl.BlockSpec((1, 1, SHIFT_PAD), shared(zprev0)),
                  pl.BlockSpec((1, RW_HEADS, RW_HEAD, RW_HEAD), shared(s0))]
                 + [full(a) for a in params] + cast_specs,
        out_specs=[pl.BlockSpec((tb, RW_WIDTH), lambda gi, i: (gi * n_tiles + i, 0)), sspec,
                   pl.BlockSpec((1, 1, SHIFT_PAD), lambda gi, i: (gi, 0, 0))] + cast_specs,
        out_shape=[jax.ShapeDtypeStruct((n_groups * n_tiles * tb, RW_WIDTH), BF16),
                   jax.ShapeDtypeStruct((n_groups, RW_HEADS, RW_HEAD, RW_HEAD), F32),
                   jax.ShapeDtypeStruct((n_groups, 1, SHIFT_PAD), F32)] + cast_shapes,
        scratch_shapes=[pltpu.VMEM((1, SHIFT_PAD), F32),
                        pltpu.VMEM((RW_HEADS // 2, 2 * RW_HEAD, 2 * RW_HEAD), F32)]
                       + [dense() for _ in range(8)],
        compiler_params=_cparams(("parallel", "arbitrary")),
        name="rwkv7_mixer",
    )(z, z, z, z, zprev0, s0, *params, *cast)


def _rwkv_sample_kernel(zr_ref, zk_ref, zv_ref, zl_ref, shr_ref, shk_ref, shv_ref, shl_ref,
                        mur_ref, muk_ref, muv_ref, mul_ref, s0_ref,
                        pc_ref, w2t_ref, a2t_ref, g2t_ref,
                        o_ref, sout_ref,
                        dec_s, a_s, b_s, k_s, r_s, v_s, o_s,
                        *, n_tok, hpb):
    w0_ref, a0_ref, kk_ref, ka_ref, rk_ref, lnw_ref, lnb_ref = [pc_ref.at[j] for j in range(7)]
    ns = LANES
    ch = hpb * RW_HEAD

    def lanes(x):
        return jnp.concatenate([x] * n_tok, axis=1)

    def shifted(z_ref, sh_ref, mu_ref):
        z = z_ref[...]
        prev = jnp.concatenate([sh_ref[...], z[:, :(n_tok - 1) * ns]], axis=1)
        return z + lanes(mu_ref[...]) * (prev - z)

    r = shifted(zr_ref, shr_ref, mur_ref)
    k = shifted(zk_ref, shk_ref, muk_ref)
    v = shifted(zv_ref, shv_ref, muv_ref)
    zl = shifted(zl_ref, shl_ref, mul_ref)
    wd = zl[0:DECAY_LORA]
    ad = zl[DECAY_LORA:DECAY_LORA + AAA_LORA]
    gd = zl[DECAY_LORA + AAA_LORA:DECAY_LORA + AAA_LORA + GATE_LORA]

    wlog = -jax.nn.softplus(-(lanes(w0_ref[...]) + _mm(w2t_ref[...], jnp.tanh(wd)))) - 0.5
    lw = -jnp.exp(wlog)
    a = jax.nn.sigmoid(lanes(a0_ref[...]) + _mm(a2t_ref[...], ad))
    g = _mm(g2t_ref[...], jax.nn.sigmoid(gd))
    kk = k * lanes(kk_ref[...])
    km = k * (1.0 + (a - 1.0) * lanes(ka_ref[...]))

    def head_sum(x):
        x3 = x.reshape(hpb, RW_HEAD, n_tok * ns)
        s = jnp.sum(x3, axis=1, keepdims=True)
        return jnp.broadcast_to(s, x3.shape).reshape(ch, n_tok * ns)

    kkn = kk * lax.rsqrt(jnp.maximum(head_sum(kk * kk), 1e-24))
    dec_s[...] = jnp.exp(lw)
    a_s[...] = -kkn
    b_s[...] = kkn * a
    k_s[...] = km
    r_s[...] = r
    v_s[...] = v

    for hh in range(hpb):
        hrows = slice(hh * RW_HEAD, (hh + 1) * RW_HEAD)

        def body(i8, carry, hh=hh, hrows=hrows):
            base = pl.multiple_of(i8 * 8, 8)
            rows8 = pl.ds(pl.multiple_of(hh * RW_HEAD + base, 8), 8)
            o_rows = [[] for _ in range(n_tok)]
            for j in range(8):
                s = s0_ref[hh, base + j]
                for t in range(n_tok):
                    tl = slice(t * ns, (t + 1) * ns)
                    sa = jnp.sum(s * a_s[hrows, tl], axis=0, keepdims=True)
                    v_row = v_s[rows8, tl][j:j + 1, :]
                    s = s * dec_s[hrows, tl] + sa * b_s[hrows, tl] + v_row * k_s[hrows, tl]
                    o_rows[t].append(jnp.sum(s * r_s[hrows, tl], axis=0, keepdims=True))
                sout_ref[hh, base + j] = s
            for t in range(n_tok):
                o_s[rows8, t * ns:(t + 1) * ns] = jnp.concatenate(o_rows[t], axis=0)
            return carry

        lax.fori_loop(0, RW_HEAD // 8, body, 0)

    o = o_s[...]
    inv_n = 1.0 / RW_HEAD
    mean = head_sum(o) * inv_n
    var = head_sum(jnp.square(o - mean)) * inv_n
    on = ((o - mean) * lax.rsqrt(var + RW_GN_EPS)) * lanes(lnw_ref[...]) + lanes(lnb_ref[...])
    bonus = head_sum(r * km * lanes(rk_ref[...])) * v
    out = (on + bonus) * g
    for t in range(n_tok):
        o_ref[t] = out[:, t * ns:(t + 1) * ns].T.astype(BF16)


def _rwkv_sample_mixer(z_t, shift_t, mu_t, state_t, p_t, *, n_tok, hpb):
    ch = hpb * RW_HEAD
    n_steps = RW_HEADS // hpb
    nl = n_tok * LANES
    seg = RW_WIDTH // ch
    zspec = lambda s: pl.BlockSpec((ch, nl), lambda h: (s * seg + h, 0))
    cspec = lambda s: pl.BlockSpec((ch, LANES), lambda h: (s * seg + h, 0))
    lora_blk = 3 * RW_WIDTH // LORA_PAD
    zl_spec = pl.BlockSpec((LORA_PAD, nl), lambda h: (lora_blk, 0))
    cl_spec = pl.BlockSpec((LORA_PAD, LANES), lambda h: (lora_blk, 0))
    pc_spec = pl.BlockSpec((p_t["cols"].shape[0], ch, LANES), lambda h: (0, h, 0))
    wspec = lambda k: pl.BlockSpec((ch, k), lambda h: (h, 0))
    sspec = pl.BlockSpec((hpb, RW_HEAD, RW_HEAD, LANES), lambda h: (h, 0, 0, 0))
    buf = lambda: pltpu.VMEM((ch, nl), F32)
    return pl.pallas_call(
        functools.partial(_rwkv_sample_kernel, n_tok=n_tok, hpb=hpb),
        grid=(n_steps,),
        in_specs=[zspec(0), zspec(1), zspec(2), zl_spec,
                  cspec(0), cspec(1), cspec(2), cl_spec,
                  cspec(0), cspec(1), cspec(2), cl_spec,
                  sspec,
                  pc_spec, wspec(DECAY_LORA), wspec(AAA_LORA), wspec(GATE_LORA)],
        out_specs=[pl.BlockSpec((n_tok, LANES, ch), lambda h: (0, 0, h)), sspec],
        out_shape=[jax.ShapeDtypeStruct((n_tok, LANES, RW_WIDTH), BF16),
                   jax.ShapeDtypeStruct(state_t.shape, F32)],
        scratch_shapes=[buf() for _ in range(7)],
        compiler_params=_cparams(("parallel",)),
        name="rwkv7_sample",
    )(z_t, z_t, z_t, z_t, shift_t, shift_t, shift_t, shift_t, mu_t, mu_t, mu_t, mu_t, state_t,
      p_t["cols"], p_t["w2t"], p_t["a2t"], p_t["g2t"])


def _retention_kernel(zq_ref, zk_ref, zv_ref, zg_ref, cos_ref, sin_ref, dmask_ref, iscale_ref,
                      kscale_ref, sdec_ref, sel_ref, s0_ref, o_ref, sout_ref, st_s,
                      *, rows, c, n_real, per_chunk_state, pair_split):
    i = pl.program_id(1)
    n_chunks = rows // c
    assert not (pair_split and per_chunk_state)

    if per_chunk_state:
        pr = lax.broadcasted_iota(jnp.int32, (rows, n_chunks * n_real), 0)
        pc = lax.broadcasted_iota(jnp.int32, (rows, n_chunks * n_real), 1)
        place = jnp.where((pr // c == pc // n_real) & (pr % c - (c - n_real) == pc % n_real),
                          1.0, 0.0).astype(BF16)
        load = lambda ref: _mm_exact_lhs(place, ref[...])
    else:
        load = lambda ref: ref[...]

        @pl.when(i == 0)
        def _():
            if pair_split:
                for h in range(RET_HEADS):
                    st_s[h] = _mm_exact_lhs(sel_ref[...], s0_ref[0, h])
            else:
                st_s[...] = s0_ref[0]

    if pair_split:
        cos = cos_ref[...]
        sin = sin_ref[...]
        half = RET_HEAD // 2

        def rot(x):
            parts = []
            for h in range(RET_HEADS):
                x0 = x[:, h * RET_HEAD:h * RET_HEAD + half]
                x1 = x[:, h * RET_HEAD + half:(h + 1) * RET_HEAD]
                parts += [x0 * cos - x1 * sin, x0 * sin + x1 * cos]
            return jnp.concatenate(parts, axis=-1)
    else:
        lane = lax.broadcasted_iota(jnp.int32, (rows, RET_WIDTH), 1)
        even = (lane % 2) == 0
        cos = jnp.concatenate([cos_ref[...]] * RET_HEADS, axis=-1)
        sin = jnp.concatenate([sin_ref[...]] * RET_HEADS, axis=-1)

        def rot(x):
            partner = jnp.where(even, pltpu.roll(x, RET_WIDTH - 1, axis=1), pltpu.roll(x, 1, axis=1))
            return x * cos + partner * sin

    q = rot(load(zq_ref))
    k = rot(load(zk_ref)) * (RET_HEAD ** -0.5)
    v = load(zv_ref)
    g = load(zg_ref)

    out_rows = []
    for ch in range(n_chunks):
        rs = slice(ch * c, (ch + 1) * c)
        out_heads = []
        for h in range(RET_HEADS):
            hs = slice(h * RET_HEAD, (h + 1) * RET_HEAD)
            qh, kh, vh = q[rs, hs], k[rs, hs], v[rs, hs]
            s_prev = s0_ref[ch, h] if per_chunk_state else st_s[h]
            scores = _mm_nt(qh, kh) * dmask_ref[h]
            o = _mm(scores, vh) + _mm(qh, s_prev) * iscale_ref[h]
            s_new = s_prev * sdec_ref[h] + _mm_tn(kh * kscale_ref[h], vh)
            if per_chunk_state:
                sout_ref[ch, h] = s_new
            else:
                st_s[h] = s_new
            o = o * lax.rsqrt(jnp.mean(o * o, axis=-1, keepdims=True) + RET_GN_EPS)
            gh = g[rs, hs]
            out_heads.append(o * (gh * jax.nn.sigmoid(gh)))
        out_rows.append(jnp.concatenate(out_heads, axis=-1))
    o_out = jnp.concatenate(out_rows, axis=0).astype(BF16)
    if per_chunk_state:
        o_out = lax.dot_general(place, o_out, (((0,), (0,)), ((), ())),
                                preferred_element_type=F32).astype(BF16)
    o_ref[...] = o_out

    if not per_chunk_state:
        @pl.when(i == pl.num_programs(1) - 1)
        def _():
            if pair_split:
                for h in range(RET_HEADS):
                    sout_ref[0, h] = _mm_exact_lhs_t(sel_ref[...], st_s[h])
            else:
                sout_ref[0] = st_s[...]


def _retention_mixer(z, row_block0, n_groups, n_tiles, cos, sin, tabs, s0, *, rows, c, n_real,
                     per_chunk_state, pos_per_tile, pair_split=False):
    shared_s0 = s0.shape[0] == 1
    sb = 1 if shared_s0 else s0.shape[0] // n_groups
    kern = functools.partial(_retention_kernel, rows=rows, c=c, n_real=n_real,
                             per_chunk_state=per_chunk_state, pair_split=pair_split)
    rows_io = rows // c * n_real if per_chunk_state else rows
    zspec = lambda col: pl.BlockSpec((rows_io, RET_WIDTH), lambda gi, i: (row_block0 + gi * n_tiles + i, col))
    full = lambda arr: pl.BlockSpec(arr.shape, lambda gi, i: (0,) * arr.ndim)
    tbl_row0, tbl_advance = pos_per_tile
    assert tbl_row0 % rows == 0
    tspec = pl.BlockSpec((rows, cos.shape[1]),
                         lambda gi, i: (tbl_row0 // rows + (i if tbl_advance else 0), 0))
    state_blk = (sb, RET_HEADS, RET_HEAD, RET_HEAD)
    s0_spec = pl.BlockSpec(state_blk, lambda gi, i: (0 if shared_s0 else gi, 0, 0, 0))
    sout_spec = pl.BlockSpec(state_blk, lambda gi, i: (gi, 0, 0, 0))
    dmask, iscale, kscale, sdec = tabs
    sel = jnp.asarray(_pair_split_index(RET_HEAD)[:, None] == np.arange(RET_HEAD)[None, :], dtype=BF16)
    return pl.pallas_call(
        kern,
        grid=(n_groups, n_tiles),
        in_specs=[zspec(0), zspec(1), zspec(2), zspec(3), tspec, tspec,
                  full(dmask), full(iscale), full(kscale), full(sdec), full(sel), s0_spec],
        out_specs=[pl.BlockSpec((rows_io, RET_WIDTH), lambda gi, i: (gi * n_tiles + i, 0)), sout_spec],
        out_shape=[jax.ShapeDtypeStruct((n_groups * n_tiles * rows_io, RET_WIDTH), BF16),
                   jax.ShapeDtypeStruct((n_groups * sb,) + state_blk[1:], F32)],
        scratch_shapes=[pltpu.VMEM((RET_HEADS, RET_HEAD, RET_HEAD), F32)],
        compiler_params=_cparams(("parallel", "arbitrary")),
        name="retention_mixer",
    )(z, z, z, z, cos, sin, dmask, iscale, kscale, sdec, sel, s0)


def _retention_tables(c, n_real):
    log_gamma = np.log(1.0 - 2.0 ** (-5.0 - np.arange(RET_HEADS, dtype=np.float64)))
    r = np.arange(c, dtype=np.float64)
    idx = r - float(c - n_real)
    diff = r[:, None] - r[None, :]
    dmask = np.where(diff[None] >= 0, np.exp(log_gamma[:, None, None] * np.maximum(diff, 0.0)[None]), 0.0)
    iscale = np.exp(log_gamma[:, None] * (idx + 1.0)[None, :])[:, :, None]
    kscale = np.exp(log_gamma[:, None] * (n_real - 1.0 - idx)[None, :])[:, :, None]
    sdec = np.broadcast_to(np.exp(log_gamma * n_real)[:, None, None], (RET_HEADS, 1, RET_HEAD))
    return tuple(jnp.asarray(t, dtype=F32) for t in (dmask, iscale, kscale, sdec))


def _rotary_tables(pos, interleaved=True):
    inv_freq = 1.0 / (ROPE_BASE ** np.linspace(0.0, 1.0, RET_HEAD // 2))
    ang = np.asarray(pos, dtype=np.float64)[:, None] * inv_freq[None, :]
    cos = np.cos(ang)
    sin = np.sin(ang)
    if not interleaved:
        return jnp.asarray(cos, dtype=F32), jnp.asarray(sin, dtype=F32)
    cos2 = np.repeat(cos, 2, axis=-1)
    sin2 = np.stack([-sin, sin], axis=-1).reshape(len(pos), RET_HEAD)
    return jnp.asarray(cos2, dtype=F32), jnp.asarray(sin2, dtype=F32)


def _outproj_kernel(oa_ref, ob_ref, w_ref, x_ref, g_ref, h_ref, hn_ref):
    acc = jnp.dot(oa_ref[...], w_ref[0:RW_WIDTH, :], preferred_element_type=F32)
    acc = acc + jnp.dot(ob_ref[...], w_ref[RW_WIDTH:, :], preferred_element_type=F32)
    h = x_ref[...] + acc
    h_ref[...] = h
    hn_ref[...] = _rms_norm_bf16(h, g_ref[...])


def _out_projection(o_a, o_b, w_out, x2d, norm_g, tm):
    m = o_a.shape[0]
    return pl.pallas_call(
        _outproj_kernel,
        grid=(m // tm,),
        in_specs=[
            pl.BlockSpec((tm, RW_WIDTH), lambda i: (i, 0)),
            pl.BlockSpec((tm, RET_WIDTH), lambda i: (i, 0)),
            pl.BlockSpec((D_MODEL, D_MODEL), lambda i: (0, 0)),
            pl.BlockSpec((tm, D_MODEL), lambda i: (i, 0)),
            pl.BlockSpec((1, D_MODEL), lambda i: (0, 0)),
        ],
        out_specs=[pl.BlockSpec((tm, D_MODEL), lambda i: (i, 0)),
                   pl.BlockSpec((tm, D_MODEL), lambda i: (i, 0))],
        out_shape=[jax.ShapeDtypeStruct((m, D_MODEL), F32), jax.ShapeDtypeStruct((m, D_MODEL), BF16)],
        compiler_params=_cparams(("parallel",)),
        name="out_projection",
    )(o_a, o_b, w_out, x2d, norm_g)


def _ffn_kernel(hn_ref, wg_ref, wu_ref, wd_ref, h_ref, g_ref, y_ref, acc_ref, *, h_slices):
    f = pl.program_id(1)

    @pl.when(f == 0)
    def _():
        acc_ref[...] = jnp.zeros_like(acc_ref)

    @pl.when(f < h_slices)
    def _():
        rows = h_ref.shape[0]
        sl = pl.ds(pl.multiple_of(f * rows, rows), rows)
        acc_ref[sl, :] += h_ref[...]

    hn = hn_ref[...]
    gate = jnp.dot(hn, wg_ref[...], preferred_element_type=F32)
    up = jnp.dot(hn, wu_ref[...], preferred_element_type=F32)
    act = (gate * jax.nn.sigmoid(gate)) * up
    acc_ref[...] += jnp.dot(act.astype(BF16), wd_ref[...], preferred_element_type=F32)

    @pl.when(f == pl.num_programs(1) - 1)
    def _():
        h = acc_ref[...]
        ms = jnp.mean(h * h, axis=-1, keepdims=True)
        y_ref[...] = (h * lax.rsqrt(ms + RMS_EPS)) * g_ref[...]


def _ffn(hn, w_gate, w_up, w_down, h, norm_g, tm, tf, h_slices=8):
    m = hn.shape[0]
    assert D_FF // tf >= h_slices
    return pl.pallas_call(
        functools.partial(_ffn_kernel, h_slices=h_slices),
        grid=(m // tm, D_FF // tf),
        in_specs=[
            pl.BlockSpec((tm, D_MODEL), lambda i, f: (i, 0)),
            pl.BlockSpec((D_MODEL, tf), lambda i, f: (0, f)),
            pl.BlockSpec((D_MODEL, tf), lambda i, f: (0, f)),
            pl.BlockSpec((tf, D_MODEL), lambda i, f: (f, 0)),
            pl.BlockSpec((tm // h_slices, D_MODEL),
                         lambda i, f: (i * h_slices + jnp.minimum(f, h_slices - 1), 0)),
            pl.BlockSpec((1, D_MODEL), lambda i, f: (0, 0)),
        ],
        out_specs=pl.BlockSpec((tm, D_MODEL), lambda i, f: (i, 0)),
        out_shape=jax.ShapeDtypeStruct((m, D_MODEL), F32),
        scratch_shapes=[pltpu.VMEM((tm, D_MODEL), F32)],
        compiler_params=_cparams(("parallel", "arbitrary")),
        name="swiglu_ffn",
    )(hn, w_gate, w_up, w_down, h, norm_g)


def _pad_cols(a, n):
    return jnp.pad(a, ((0, 0), (0, n - a.shape[1])))


def _pad_rows(a, n):
    return jnp.pad(a, ((0, n - a.shape[0]), (0, 0)))


def kernel(x_prompt, x_sample, state_shift, state_rwkv, state_ret, meta_tokens, norm_mix, w_in,
           rwkv_mu, rwkv_w0, rwkv_w2, rwkv_a0, rwkv_a2, rwkv_g2, rwkv_kk, rwkv_ka, rwkv_rk,
           rwkv_ln_w, rwkv_ln_b, w_out, norm_ffn, w_gate, w_up, w_down, norm_final):
    n_b, seq, d = x_prompt.shape
    n_s, dec_seq, _ = x_sample.shape
    n_p = n_b * seq
    n_d = n_s * dec_seq
    depth = w_in.shape[0]
    assert depth == 1 and d == D_MODEL and n_s == LANES

    w_in_t = jnp.transpose(w_in[0])
    w_in_p = _w_in_layout(w_in_t, tn=Tiles.w_layout_cols)
    row = lambda a: a.reshape(1, -1).astype(F32)
    w2p = jnp.concatenate([rwkv_w2[0], jnp.zeros((128 - DECAY_LORA, RW_WIDTH), F32)], axis=0)
    a2p = jnp.concatenate([jnp.zeros((DECAY_LORA, RW_WIDTH), F32), rwkv_a2[0]], axis=0)
    g2p = jnp.concatenate([rwkv_g2[0], jnp.zeros((256 - GATE_LORA, RW_WIDTH), F32)], axis=0)
    rw_params = dict(mu=_pad_cols(row(rwkv_mu[0]), SHIFT_PAD), w0=row(rwkv_w0[0]), w2=w2p,
                     a0=row(rwkv_a0[0]), a2=a2p, g2=g2p, kk=row(rwkv_kk[0]), ka=row(rwkv_ka[0]),
                     rk=row(rwkv_rk[0]), lnw=row(rwkv_ln_w[0]), lnb=row(rwkv_ln_b[0]))
    col = lambda a: jnp.broadcast_to(a.reshape(-1, 1).astype(F32), (a.size, LANES))
    cols = jnp.stack([a.reshape(-1).astype(F32) for a in (rwkv_w0[0], rwkv_a0[0], rwkv_kk[0], rwkv_ka[0],
                                                         rwkv_rk[0], rwkv_ln_w[0], rwkv_ln_b[0])])
    rw_params_t = dict(cols=jnp.broadcast_to(cols[:, :, None], cols.shape + (LANES,)),
                       w2t=rwkv_w2[0].T, a2t=rwkv_a2[0].T, g2t=rwkv_g2[0].T)

    x_p = x_prompt.reshape(n_p, d)
    x_s = x_sample.reshape(n_d, d)
    meta = meta_tokens.astype(F32)
    x_sm = jnp.concatenate([x_s, meta], axis=0)
    x_tsm = jnp.concatenate([jnp.transpose(x_sample, (1, 0, 2)).reshape(n_d, d), meta,
                             jnp.zeros((LANES - N_META, d), F32)], axis=0)
    g_mix = row(norm_mix[0])
    z_p = _in_projection(x_p, g_mix, w_in_p, tm=Tiles.inproj_rows, tn=Tiles.inproj_cols)
    z_sm = _in_projection(x_sm, g_mix, w_in_p, tm=x_sm.shape[0], tn=Tiles.inproj_small_cols,
                          qk_interleaved=True, n_cols=RW_OFF)
    z_st = _in_projection_t(w_in_t, x_tsm, g_mix, SHIFT_PAD, tm=Tiles.inproj_t_rows)

    z_meta = jnp.pad(jnp.concatenate([z_sm[n_d:], jnp.transpose(z_st[:, n_d:n_d + N_META])], axis=1),
                     ((RET_CHUNK - N_META, 0), (0, 0)))
    srow = 2 * dec_seq

    zero_prev = jnp.zeros((1, 1, SHIFT_PAD), F32)
    zero_rw = jnp.zeros((1, RW_HEADS, RW_HEAD, RW_HEAD), F32)
    c_rw = Tiles.rwkv_chunk
    _, s_rw_meta, zlast_meta = _rwkv_mixer(z_meta, 1, 1, 1, zero_prev, zero_rw, rw_params, tb=c_rw, c=c_rw)
    oa_p, rwkv_p, zlast_p, w_out_b, w_gate_b, w_up_b, w_down_b = _rwkv_mixer(
        z_p, 0, n_b, seq // Tiles.rwkv_rows, zlast_meta, s_rw_meta, rw_params, tb=Tiles.rwkv_rows, c=c_rw,
        cast=(w_out[0], w_gate[0], w_up[0], w_down[0]))
    shift_t = _pad_rows(jnp.transpose(state_shift[0]), SHIFT_PAD)
    mu_t = _pad_rows(col(rwkv_mu[0]), SHIFT_PAD)
    state_t = jnp.transpose(state_rwkv[0], (1, 2, 3, 0))
    oa_st, rwkv_st = _rwkv_sample_mixer(z_st, shift_t, mu_t, state_t, rw_params_t, n_tok=dec_seq,
                                        hpb=Tiles.rwkv_sample_heads)
    oa_s = jnp.transpose(oa_st, (1, 0, 2)).reshape(n_d, RW_WIDTH)
    rwkv_s = jnp.transpose(rwkv_st, (3, 0, 1, 2))

    tabs_full = _retention_tables(RET_CHUNK, RET_CHUNK)
    tabs_meta = _retention_tables(RET_CHUNK, N_META)
    tabs_smp = _retention_tables(srow, dec_seq)
    sb_rt = Tiles.ret_sample_seqs
    cos_p, sin_p = _rotary_tables(N_META + np.arange(seq), interleaved=False)
    pos_ms = np.concatenate([np.arange(RET_CHUNK) - (RET_CHUNK - N_META),
                             PAST_LEN + np.tile(np.arange(srow) - (srow - dec_seq), sb_rt)])
    cos_ms, sin_ms = _rotary_tables(pos_ms)
    zero_rt = jnp.zeros((1, RET_HEADS, RET_HEAD, RET_HEAD), F32)
    _, s_rt_meta = _retention_mixer(z_meta, 0, 1, 1, cos_ms, sin_ms, tabs_meta, zero_rt,
                                    rows=RET_CHUNK, c=RET_CHUNK, n_real=N_META, per_chunk_state=False,
                                    pos_per_tile=(0, False))
    rows_rt = Tiles.ret_chunks_per_step * RET_CHUNK
    ob_p, ret_p = _retention_mixer(z_p, 0, n_b, seq // rows_rt, cos_p, sin_p, tabs_full, s_rt_meta,
                                   rows=rows_rt, c=RET_CHUNK, n_real=RET_CHUNK, per_chunk_state=False,
                                   pos_per_tile=(0, True), pair_split=True)
    ob_s, ret_s = _retention_mixer(z_sm, 0, n_s // sb_rt, 1, cos_ms, sin_ms, tabs_smp,
                                   state_ret.reshape(n_s, RET_HEADS, RET_HEAD, RET_HEAD),
                                   rows=sb_rt * srow, c=srow, n_real=dec_seq, per_chunk_state=True,
                                   pos_per_tile=(RET_CHUNK, False))

    g_ffn = row(norm_ffn[0])
    g_fin = row(norm_final)
    h_p, hn_p = _out_projection(oa_p, ob_p, w_out_b, x_p, g_ffn, tm=Tiles.outproj_rows)
    h_s, hn_s = _out_projection(oa_s, ob_s, w_out_b, x_s, g_ffn, tm=n_d)
    y_p = _ffn(hn_p, w_gate_b, w_up_b, w_down_b, h_p, g_fin, tm=Tiles.ffn_rows, tf=Tiles.ffn_cols)
    y_s = _ffn(hn_s, w_gate_b, w_up_b, w_down_b, h_s, g_fin, tm=n_d, tf=Tiles.ffn_cols)

    y_prompt = y_p.reshape(n_b, seq, d)
    y_sample = y_s.reshape(n_s, dec_seq, d)
    shift_p = zlast_p[:, 0, :RW_COLS][None]
    shift_s = jnp.transpose(z_st[:RW_COLS, (dec_seq - 1) * n_s:n_d])[None]
    return (y_prompt, y_sample, shift_p, rwkv_p[None], ret_p[None], shift_s, rwkv_s[None], ret_s[None])
```

```python
import functools
import math

import jax
import jax.numpy as jnp
import numpy as np
from jax import lax
from jax.experimental import pallas as pl
from jax.experimental.pallas import tpu as pltpu

F32 = jnp.float32
BF16 = jnp.bfloat16

D_MODEL = 2048
N_META = 16
RW_WIDTH = 1024
RW_HEAD = 64
RW_HEADS = 16
DECAY_LORA = 64
AAA_LORA = 64
GATE_LORA = 160
RW_COLS = 3 * RW_WIDTH + DECAY_LORA + AAA_LORA + GATE_LORA
RET_WIDTH = 1024
RET_HEADS = 4
RET_HEAD = 256
RET_CHUNK = 128
D_FF = 5632
RMS_EPS = 1e-6
RW_GN_EPS = 64e-5
RET_GN_EPS = 1e-6
ROPE_BASE = 10000.0
LANES = 128

LORA_PAD = 512
Z_COLS = 4 * RET_WIDTH + 3 * RW_WIDTH + LORA_PAD
RW_OFF = 4 * RET_WIDTH
SHIFT_PAD = 3 * RW_WIDTH + LORA_PAD

VMEM_LIMIT = 60 * 1024 * 1024
PAST_LEN = 16384


class Tiles:
    w_layout_cols = 512
    inproj_rows, inproj_cols = 1024, 1536
    inproj_small_cols = 1024
    inproj_t_rows = 896
    rwkv_rows, rwkv_chunk = 256, 64
    rwkv_sample_heads = 2
    ret_chunks_per_step = 4
    ret_sample_seqs = 8
    outproj_rows = 512
    ffn_rows, ffn_cols = 1024, 512


def _cparams(sem):
    return pltpu.CompilerParams(dimension_semantics=sem, vmem_limit_bytes=VMEM_LIMIT)


def _mm(a, b):
    return jnp.dot(a.astype(BF16), b.astype(BF16), preferred_element_type=F32)


def _mm_nt(a, b):
    return lax.dot_general(a.astype(BF16), b.astype(BF16), (((1,), (1,)), ((), ())),
                           preferred_element_type=F32)


def _mm_tn(a, b):
    return lax.dot_general(a.astype(BF16), b.astype(BF16), (((0,), (0,)), ((), ())),
                           preferred_element_type=F32)


def _bmm(a, b):
    return jnp.einsum("bmk,bkn->bmn", a.astype(BF16), b.astype(BF16), preferred_element_type=F32)


def _bmm_nt(a, b):
    return jnp.einsum("bmk,bnk->bmn", a.astype(BF16), b.astype(BF16), preferred_element_type=F32)


def _bmm_tn(a, b):
    return jnp.einsum("bkm,bkn->bmn", a.astype(BF16), b.astype(BF16), preferred_element_type=F32)


def _mm_exact_lhs(m_bf16, x):
    hi = x.astype(BF16)
    r1 = x - hi.astype(F32)
    mid = r1.astype(BF16)
    lo = (r1 - mid.astype(F32)).astype(BF16)
    d = functools.partial(jnp.dot, preferred_element_type=F32)
    return d(m_bf16, hi) + d(m_bf16, mid) + d(m_bf16, lo)


def _mm_exact_rhs(x, m_bf16):
    hi = x.astype(BF16)
    r1 = x - hi.astype(F32)
    mid = r1.astype(BF16)
    lo = (r1 - mid.astype(F32)).astype(BF16)
    d = functools.partial(jnp.dot, preferred_element_type=F32)
    return d(hi, m_bf16) + d(mid, m_bf16) + d(lo, m_bf16)


def _mm_exact_lhs_t(m_bf16, x):
    hi = x.astype(BF16)
    r1 = x - hi.astype(F32)
    mid = r1.astype(BF16)
    lo = (r1 - mid.astype(F32)).astype(BF16)
    d = functools.partial(lax.dot_general, dimension_numbers=(((0,), (0,)), ((), ())),
                          preferred_element_type=F32)
    return d(m_bf16, hi) + d(m_bf16, mid) + d(m_bf16, lo)


def _rms_norm_bf16(x, g):
    ms = jnp.mean(x * x, axis=-1, keepdims=True)
    return ((x * lax.rsqrt(ms + RMS_EPS)) * g).astype(BF16)


def _pair_split_index(n):
    j = np.arange(n)
    half = RET_HEAD // 2
    return (j // RET_HEAD) * RET_HEAD + 2 * (j % half) + (j % RET_HEAD) // half


def _w_in_layout_kernel(w_ref, sel_ref, o_ref, *, n_qk):
    c = pl.program_id(0)

    @pl.when(c >= n_qk)
    def _():
        o_ref[...] = w_ref[...].T.astype(BF16)

    @pl.when(c < n_qk)
    def _():
        picked = jnp.dot(sel_ref[...], w_ref[...].astype(BF16), preferred_element_type=F32)
        o_ref[...] = picked.T.astype(BF16)


def _w_in_layout(w_in_t, tn):
    n, d = w_in_t.shape
    n_ret = 4 * RET_WIDTH // tn
    n_qk = 2 * RET_WIDTH // tn
    sel = jnp.asarray(_pair_split_index(tn)[:, None] == np.arange(tn)[None, :], dtype=BF16)

    def src_row(c):
        align = math.gcd(RW_COLS, tn)
        return (pl.multiple_of(jnp.where(c < n_ret, RW_COLS + tn * c, tn * (c - n_ret)), align), 0)

    return pl.pallas_call(
        functools.partial(_w_in_layout_kernel, n_qk=n_qk),
        grid=(Z_COLS // tn,),
        in_specs=[pl.BlockSpec((pl.Element(tn), pl.Element(d)), src_row),
                  pl.BlockSpec((tn, tn), lambda c: (0, 0))],
        out_specs=pl.BlockSpec((d, tn), lambda c: (0, c)),
        out_shape=jax.ShapeDtypeStruct((d, Z_COLS), BF16),
        compiler_params=_cparams(("parallel",)),
        name="w_in_layout",
    )(w_in_t, sel)


def _inproj_kernel(x_ref, g_ref, w_ref, sel_ref, o_ref, xn_ref, *, n_qk_interleave):
    j = pl.program_id(1)

    @pl.when(j == 0)
    def _():
        xn_ref[...] = _rms_norm_bf16(x_ref[...], g_ref[...])

    @pl.when(j >= n_qk_interleave)
    def _():
        o_ref[...] = jnp.dot(xn_ref[...], w_ref[...], preferred_element_type=F32)

    if n_qk_interleave:
        @pl.when(j < n_qk_interleave)
        def _():
            z = jnp.dot(xn_ref[...], w_ref[...], preferred_element_type=F32)
            o_ref[...] = _mm_exact_rhs(z, sel_ref[...])


def _in_projection(x2d, norm_g, w_in_p, tm, tn, qk_interleaved=False, n_cols=Z_COLS):
    m = x2d.shape[0]
    n_qk = 2 * RET_WIDTH // tn if qk_interleaved else 0
    assert not qk_interleaved or tn % RET_HEAD == 0
    ts = tn if qk_interleaved else LANES
    sel = jnp.asarray(_pair_split_index(ts)[:, None] == np.arange(ts)[None, :], dtype=BF16)
    return pl.pallas_call(
        functools.partial(_inproj_kernel, n_qk_interleave=n_qk),
        grid=(m // tm, n_cols // tn),
        in_specs=[
            pl.BlockSpec((tm, D_MODEL), lambda i, j: (i, 0)),
            pl.BlockSpec((1, D_MODEL), lambda i, j: (0, 0)),
            pl.BlockSpec((D_MODEL, tn), lambda i, j: (0, j)),
            pl.BlockSpec((ts, ts), lambda i, j: (0, 0)),
        ],
        out_specs=pl.BlockSpec((tm, tn), lambda i, j: (i, j)),
        out_shape=jax.ShapeDtypeStruct((m, n_cols), F32),
        scratch_shapes=[pltpu.VMEM((tm, D_MODEL), BF16)],
        compiler_params=_cparams(("parallel", "arbitrary")),
        name="in_projection",
    )(x2d, norm_g, w_in_p, sel)


def _inproj_t_kernel(w_ref, x_ref, g_ref, o_ref, xn_ref):
    @pl.when(pl.program_id(0) == 0)
    def _():
        xn_ref[...] = _rms_norm_bf16(x_ref[...], g_ref[...])

    o_ref[...] = _mm_nt(w_ref[...], xn_ref[...])


def _in_projection_t(w_in_t, x2d, norm_g, n_rows, tm):
    m, d = x2d.shape
    return pl.pallas_call(
        _inproj_t_kernel,
        grid=(n_rows // tm,),
        in_specs=[
            pl.BlockSpec((tm, d), lambda i: (i, 0)),
            pl.BlockSpec((m, d), lambda i: (0, 0)),
            pl.BlockSpec((1, d), lambda i: (0, 0)),
        ],
        out_specs=pl.BlockSpec((tm, m), lambda i: (i, 0)),
        out_shape=jax.ShapeDtypeStruct((n_rows, m), F32),
        scratch_shapes=[pltpu.VMEM((m, d), BF16)],
        compiler_params=_cparams(("arbitrary",)),
        name="in_projection_t",
    )(w_in_t, x2d, norm_g)


def _rwkv_kernel(zr_ref, zk_ref, zv_ref, zl_ref, zprev0_ref, s0_ref,
                 mu_ref, w0_ref, w2_ref, a0_ref, a2_ref, g2_ref, kk_ref, ka_ref, rk_ref,
                 lnw_ref, lnb_ref,
                 o_ref, sout_ref, zlast_ref,
                 carry_ref, sbd_s, r_s, km_s, v_s, kk_s, a_s, lw_s, cum_s, o_s,
                 *, tb, c, n_factors):
    i = pl.program_id(1)
    n_chunks = tb // c
    n_pairs = RW_HEADS // 2
    pw = 2 * RW_HEAD

    @pl.when(i == 0)
    def _():
        carry_ref[...] = zprev0_ref[0]
        zero = jnp.zeros((RW_HEAD, RW_HEAD), F32)
        for p in range(n_pairs):
            top = jnp.concatenate([s0_ref[0, 2 * p], zero], axis=1)
            bot = jnp.concatenate([zero, s0_ref[0, 2 * p + 1]], axis=1)
            sbd_s[p] = jnp.concatenate([top, bot], axis=0)

    row = lax.broadcasted_iota(jnp.int32, (tb, 1), 0)

    def shifted(z_ref, lo, hi):
        z = z_ref[...]
        prev = pltpu.roll(z, 1, axis=0)
        prev = jnp.concatenate([jnp.where(row[:8] == 0, carry_ref[:, lo:hi], prev[:8]), prev[8:]], axis=0)
        zs = z + mu_ref[:, lo:hi] * (prev - z)
        carry_ref[:, lo:hi] = z[tb - 1:tb, :]
        return zs

    w = RW_WIDTH
    r = shifted(zr_ref, 0, w)
    k = shifted(zk_ref, w, 2 * w)
    v = shifted(zv_ref, 2 * w, 3 * w)
    zl = shifted(zl_ref, 3 * w, 3 * w + LORA_PAD)

    lo2 = zl[:, 0:128]
    wlog = -jax.nn.softplus(-(w0_ref[...] + _mm(jnp.tanh(lo2), w2_ref[...]))) - 0.5
    lw = -jnp.exp(wlog)
    a = jax.nn.sigmoid(a0_ref[...] + _mm(lo2, a2_ref[...]))
    g = _mm(jax.nn.sigmoid(zl[:, 128:384]), g2_ref[...])
    kk = k * kk_ref[...]
    km = k * (1.0 + (a - 1.0) * ka_ref[...])

    for dst, val in ((r_s, r), (km_s, km), (v_s, v), (a_s, a), (lw_s, lw), (kk_s, kk)):
        dst[...] = val

    rr = lax.broadcasted_iota(jnp.int32, (c, c), 0)
    cc = lax.broadcasted_iota(jnp.int32, (c, c), 1)
    tri = jnp.where(rr >= cc, 1.0, 0.0).astype(BF16)
    rr2 = lax.broadcasted_iota(jnp.int32, (c, 2 * c), 0)
    cc2 = lax.broadcasted_iota(jnp.int32, (c, 2 * c), 1)
    strict = (rr > cc)[None]
    eye = jnp.where(rr == cc, 1.0, 0.0).astype(F32)[None]
    strict_k = ((cc2 >= c) & (rr2 > cc2 - c))[None]
    incl_bk = (rr2 >= cc2 % c)[None]
    lo = (lax.broadcasted_iota(jnp.int32, (1, 1, pw), 2) < RW_HEAD)
    bd_r = lax.broadcasted_iota(jnp.int32, (pw, pw), 0) // RW_HEAD
    bd_c = lax.broadcasted_iota(jnp.int32, (pw, pw), 1) // RW_HEAD
    block_diag = (bd_r == bd_c)[None]
    rk = rk_ref[...]
    lnw = lnw_ref[...]
    lnb = lnb_ref[...]

    def head_sum(x):
        s_lo = jnp.sum(jnp.where(lo, x, 0.0), axis=-1, keepdims=True)
        s_hi = jnp.sum(jnp.where(lo, 0.0, x), axis=-1, keepdims=True)
        return jnp.where(lo, s_lo, s_hi)

    def pick(x16):
        x4 = x16.reshape(n_pairs, 2, c, pw)
        return jnp.where(lo, x4[:, 0], x4[:, 1])

    def body(ch, s_prev):
        rows = pl.ds(ch * c, c)

        def pairs(s):
            x = s[rows, :]
            return jnp.stack([x[:, p * pw:(p + 1) * pw] for p in range(n_pairs)], axis=0)

        cum_s[rows, :] = _mm_exact_lhs(tri, lw_s[rows, :])
        rh, kmh, vh, ah, lwh, cumh, kkh = [pairs(s) for s in (r_s, km_s, v_s, a_s, lw_s, cum_s, kk_s)]
        kkn = kkh * lax.rsqrt(jnp.maximum(head_sum(kkh * kkh), 1e-24))
        bvec = kkn * ah
        cum_end = cumh[:, c - 1:c, :]
        p_inc = jnp.exp(cumh)
        p_exc = jnp.exp(cumh - lwh)
        p_inv = jnp.exp(-cumh)
        p_end = jnp.exp(cum_end)
        p_rel = jnp.exp(cum_end - cumh)
        a_t = -(kkn * p_exc)
        r_t = rh * p_inc
        b_t = bvec * p_inv
        k_t = kmh * p_inv
        b_h = bvec * p_rel
        k_h = kmh * p_rel

        ar = jnp.concatenate([jnp.where(lo, a_t, 0.0), jnp.where(lo, r_t, 0.0),
                              jnp.where(lo, 0.0, a_t), jnp.where(lo, 0.0, r_t)], axis=1)
        prod = _bmm_nt(ar, jnp.concatenate([b_t, k_t, s_prev], axis=1))
        prod = prod.reshape(RW_HEADS, 2 * c, 2 * c + pw)
        a_rows, r_rows = prod[:, :c], prod[:, c:]
        aak_wide = jnp.where(strict_k, a_rows[:, :, :2 * c], 0.0)
        arbk = jnp.where(incl_bk, r_rows[:, :, :2 * c], 0.0)
        lmat = jnp.where(strict, a_rows[:, :, :c], 0.0)
        tinv = eye + lmat
        if n_factors > 1:
            lp = _bmm(lmat, lmat)
            for _ in range(n_factors - 2):
                st = _bmm(jnp.concatenate([lp, tinv], axis=1), lp)
                lp = st[:, :c]
                tinv = tinv + st[:, c:]
            tinv = tinv + _bmm(tinv, lp)
        v16 = jnp.repeat(vh, 2, axis=0)
        x = a_rows[:, :, 2 * c:] + _bmm(aak_wide, jnp.concatenate([v16, v16], axis=1))
        u16 = _bmm(tinv, x)
        o16 = r_rows[:, :, 2 * c:] + _bmm(arbk, jnp.concatenate([u16, v16], axis=1))
        u = pick(u16)
        o = pick(o16)
        s_upd = _bmm_tn(jnp.concatenate([u, vh], axis=1), jnp.concatenate([b_h, k_h], axis=1))
        s_new = s_prev * p_end + jnp.where(block_diag, s_upd, 0.0)
        inv_n = 1.0 / RW_HEAD
        mean = head_sum(o) * inv_n
        var = head_sum(jnp.square(o - mean)) * inv_n
        on = ((o - mean) * lax.rsqrt(var + RW_GN_EPS)) * lnw + lnb
        out = on + head_sum(rh * kmh * rk) * vh
        for p in range(n_pairs):
            o_s[rows, p * pw:(p + 1) * pw] = out[p]
        return s_new

    state = sbd_s[...]
    for ch in range(n_chunks):
        state = body(ch, state)
    sbd_s[...] = state

    o_ref[...] = (o_s[...] * g).astype(BF16)
    zlast_ref[0] = carry_ref[...]

    @pl.when(i == pl.num_programs(1) - 1)
    def _():
        for p in range(n_pairs):
            blk = sbd_s[p]
            sout_ref[0, 2 * p] = blk[:RW_HEAD, :RW_HEAD]
            sout_ref[0, 2 * p + 1] = blk[RW_HEAD:, RW_HEAD:]


N_RWKV_INPUTS = 17
N_RWKV_OUTPUTS = 3


def _rwkv_and_cast_kernel(*refs, n_cast, **kw):
    ins = refs[:N_RWKV_INPUTS]
    cast_in = refs[N_RWKV_INPUTS:N_RWKV_INPUTS + n_cast]
    o0 = N_RWKV_INPUTS + n_cast
    outs = refs[o0:o0 + N_RWKV_OUTPUTS]
    cast_out = refs[o0 + N_RWKV_OUTPUTS:o0 + N_RWKV_OUTPUTS + n_cast]
    scratch = refs[o0 + N_RWKV_OUTPUTS + n_cast:]
    for src, dst in zip(cast_in, cast_out):
        dst[...] = src[...].astype(BF16)
    _rwkv_kernel(*ins, *outs, *scratch, **kw)


def _rwkv_mixer(z, row_block0, n_groups, n_tiles, zprev0, s0, p, *, tb, c, cast=()):
    n_factors = max(1, math.ceil(math.log2(c)))
    kern = functools.partial(_rwkv_and_cast_kernel, n_cast=len(cast), tb=tb, c=c, n_factors=n_factors)
    n_steps = n_groups * n_tiles
    cast_specs = [pl.BlockSpec((a.shape[0] // n_steps, a.shape[1]), lambda gi, i: (gi * n_tiles + i, 0))
                  for a in cast]
    cast_shapes = [jax.ShapeDtypeStruct(a.shape, BF16) for a in cast]
    cb = RW_OFF // RW_WIDTH
    full = lambda arr: pl.BlockSpec(arr.shape, lambda gi, i: (0,) * arr.ndim)
    lora_cb = (RW_OFF + 3 * RW_WIDTH) // LORA_PAD
    per_pair = lambda a: a.reshape(RW_HEADS // 2, 1, 2 * RW_HEAD)
    params = [p["mu"], p["w0"], p["w2"], p["a0"], p["a2"], p["g2"], p["kk"], p["ka"],
              per_pair(p["rk"]), per_pair(p["lnw"]), per_pair(p["lnb"])]
    dense = lambda: pltpu.VMEM((tb, RW_WIDTH), F32)
    zspec = lambda width, col: pl.BlockSpec((tb, width), lambda gi, i: (row_block0 + gi * n_tiles + i, col))
    sspec = pl.BlockSpec((1, RW_HEADS, RW_HEAD, RW_HEAD), lambda gi, i: (gi, 0, 0, 0))
    shared = lambda a: (lambda gi, i: (0,) * a.ndim) if a.shape[0] == 1 else (lambda gi, i: (gi,) + (0,) * (a.ndim - 1))
    return pl.pallas_call(
        kern,
        grid=(n_groups, n_tiles),
        in_specs=[zspec(RW_WIDTH, cb), zspec(RW_WIDTH, cb + 1), zspec(RW_WIDTH, cb + 2),
                  zspec(LORA_PAD, lora_cb),
                  pl.BlockSpec((1, 1, SHIFT_PAD), shared(zprev0)),
                  pl.BlockSpec((1, RW_HEADS, RW_HEAD, RW_HEAD), shared(s0))]
                 + [full(a) for a in params] + cast_specs,
        out_specs=[pl.BlockSpec((tb, RW_WIDTH), lambda gi, i: (gi * n_tiles + i, 0)), sspec,
                   pl.BlockSpec((1, 1, SHIFT_PAD), lambda gi, i: (gi, 0, 0))] + cast_specs,
        out_shape=[jax.ShapeDtypeStruct((n_groups * n_tiles * tb, RW_WIDTH), BF16),
                   jax.ShapeDtypeStruct((n_groups, RW_HEADS, RW_HEAD, RW_HEAD), F32),
                   jax.ShapeDtypeStruct((n_groups, 1, SHIFT_PAD), F32)] + cast_shapes,
        scratch_shapes=[pltpu.VMEM((1, SHIFT_PAD), F32),
                        pltpu.VMEM((RW_HEADS // 2, 2 * RW_HEAD, 2 * RW_HEAD), F32)]
                       + [dense() for _ in range(8)],
        compiler_params=_cparams(("parallel", "arbitrary")),
        name="rwkv7_mixer",
    )(z, z, z, z, zprev0, s0, *params, *cast)


def _rwkv_sample_kernel(zr_ref, zk_ref, zv_ref, zl_ref, shr_ref, shk_ref, shv_ref, shl_ref,
                        mur_ref, muk_ref, muv_ref, mul_ref, s0_ref,
                        pc_ref, w2t_ref, a2t_ref, g2t_ref,
                        o_ref, sout_ref,
                        dec_s, a_s, b_s, k_s, r_s, v_s, o_s,
                        *, n_tok, hpb):
    w0_ref, a0_ref, kk_ref, ka_ref, rk_ref, lnw_ref, lnb_ref = [pc_ref.at[j] for j in range(7)]
    ns = LANES
    ch = hpb * RW_HEAD

    def lanes(x):
        return jnp.concatenate([x] * n_tok, axis=1)

    def shifted(z_ref, sh_ref, mu_ref):
        z = z_ref[...]
        prev = jnp.concatenate([sh_ref[...], z[:, :(n_tok - 1) * ns]], axis=1)
        return z + lanes(mu_ref[...]) * (prev - z)

    r = shifted(zr_ref, shr_ref, mur_ref)
    k = shifted(zk_ref, shk_ref, muk_ref)
    v = shifted(zv_ref, shv_ref, muv_ref)
    zl = shifted(zl_ref, shl_ref, mul_ref)
    wd = zl[0:DECAY_LORA]
    ad = zl[DECAY_LORA:DECAY_LORA + AAA_LORA]
    gd = zl[DECAY_LORA + AAA_LORA:DECAY_LORA + AAA_LORA + GATE_LORA]

    wlog = -jax.nn.softplus(-(lanes(w0_ref[...]) + _mm(w2t_ref[...], jnp.tanh(wd)))) - 0.5
    lw = -jnp.exp(wlog)
    a = jax.nn.sigmoid(lanes(a0_ref[...]) + _mm(a2t_ref[...], ad))
    g = _mm(g2t_ref[...], jax.nn.sigmoid(gd))
    kk = k * lanes(kk_ref[...])
    km = k * (1.0 + (a - 1.0) * lanes(ka_ref[...]))

    def head_sum(x):
        x3 = x.reshape(hpb, RW_HEAD, n_tok * ns)
        s = jnp.sum(x3, axis=1, keepdims=True)
        return jnp.broadcast_to(s, x3.shape).reshape(ch, n_tok * ns)

    kkn = kk * lax.rsqrt(jnp.maximum(head_sum(kk * kk), 1e-24))
    dec_s[...] = jnp.exp(lw)
    a_s[...] = -kkn
    b_s[...] = kkn * a
    k_s[...] = km
    r_s[...] = r
    v_s[...] = v

    for hh in range(hpb):
        hrows = slice(hh * RW_HEAD, (hh + 1) * RW_HEAD)

        def body(i8, carry, hh=hh, hrows=hrows):
            base = pl.multiple_of(i8 * 8, 8)
            rows8 = pl.ds(pl.multiple_of(hh * RW_HEAD + base, 8), 8)
            o_rows = [[] for _ in range(n_tok)]
            for j in range(8):
                s = s0_ref[hh, base + j]
                for t in range(n_tok):
                    tl = slice(t * ns, (t + 1) * ns)
                    sa = jnp.sum(s * a_s[hrows, tl], axis=0, keepdims=True)
                    v_row = v_s[rows8, tl][j:j + 1, :]
                    s = s * dec_s[hrows, tl] + sa * b_s[hrows, tl] + v_row * k_s[hrows, tl]
                    o_rows[t].append(jnp.sum(s * r_s[hrows, tl], axis=0, keepdims=True))
                sout_ref[hh, base + j] = s
            for t in range(n_tok):
                o_s[rows8, t * ns:(t + 1) * ns] = jnp.concatenate(o_rows[t], axis=0)
            return carry

        lax.fori_loop(0, RW_HEAD // 8, body, 0)

    o = o_s[...]
    inv_n = 1.0 / RW_HEAD
    mean = head_sum(o) * inv_n
    var = head_sum(jnp.square(o - mean)) * inv_n
    on = ((o - mean) * lax.rsqrt(var + RW_GN_EPS)) * lanes(lnw_ref[...]) + lanes(lnb_ref[...])
    bonus = head_sum(r * km * lanes(rk_ref[...])) * v
    out = (on + bonus) * g
    for t in range(n_tok):
        o_ref[t] = out[:, t * ns:(t + 1) * ns].T.astype(BF16)


def _rwkv_sample_mixer(z_t, shift_t, mu_t, state_t, p_t, *, n_tok, hpb):
    ch = hpb * RW_HEAD
    n_steps = RW_HEADS // hpb
    nl = n_tok * LANES
    seg = RW_WIDTH // ch
    zspec = lambda s: pl.BlockSpec((ch, nl), lambda h: (s * seg + h, 0))
    cspec = lambda s: pl.BlockSpec((ch, LANES), lambda h: (s * seg + h, 0))
    lora_blk = 3 * RW_WIDTH // LORA_PAD
    zl_spec = pl.BlockSpec((LORA_PAD, nl), lambda h: (lora_blk, 0))
    cl_spec = pl.BlockSpec((LORA_PAD, LANES), lambda h: (lora_blk, 0))
    pc_spec = pl.BlockSpec((p_t["cols"].shape[0], ch, LANES), lambda h: (0, h, 0))
    wspec = lambda k: pl.BlockSpec((ch, k), lambda h: (h, 0))
    sspec = pl.BlockSpec((hpb, RW_HEAD, RW_HEAD, LANES), lambda h: (h, 0, 0, 0))
    buf = lambda: pltpu.VMEM((ch, nl), F32)
    return pl.pallas_call(
        functools.partial(_rwkv_sample_kernel, n_tok=n_tok, hpb=hpb),
        grid=(n_steps,),
        in_specs=[zspec(0), zspec(1), zspec(2), zl_spec,
                  cspec(0), cspec(1), cspec(2), cl_spec,
                  cspec(0), cspec(1), cspec(2), cl_spec,
                  sspec,
                  pc_spec, wspec(DECAY_LORA), wspec(AAA_LORA), wspec(GATE_LORA)],
        out_specs=[pl.BlockSpec((n_tok, LANES, ch), lambda h: (0, 0, h)), sspec],
        out_shape=[jax.ShapeDtypeStruct((n_tok, LANES, RW_WIDTH), BF16),
                   jax.ShapeDtypeStruct(state_t.shape, F32)],
        scratch_shapes=[buf() for _ in range(7)],
        compiler_params=_cparams(("parallel",)),
        name="rwkv7_sample",
    )(z_t, z_t, z_t, z_t, shift_t, shift_t, shift_t, shift_t, mu_t, mu_t, mu_t, mu_t, state_t,
      p_t["cols"], p_t["w2t"], p_t["a2t"], p_t["g2t"])


def _retention_kernel(zq_ref, zk_ref, zv_ref, zg_ref, cos_ref, sin_ref, dmask_ref, iscale_ref,
                      kscale_ref, sdec_ref, sel_ref, s0_ref, o_ref, sout_ref, st_s,
                      *, rows, c, n_real, per_chunk_state, pair_split):
    i = pl.program_id(1)
    n_chunks = rows // c
    assert not (pair_split and per_chunk_state)

    if per_chunk_state:
        pr = lax.broadcasted_iota(jnp.int32, (rows, n_chunks * n_real), 0)
        pc = lax.broadcasted_iota(jnp.int32, (rows, n_chunks * n_real), 1)
        place = jnp.where((pr // c == pc // n_real) & (pr % c - (c - n_real) == pc % n_real),
                          1.0, 0.0).astype(BF16)
        load = lambda ref: _mm_exact_lhs(place, ref[...])
    else:
        load = lambda ref: ref[...]

        @pl.when(i == 0)
        def _():
            if pair_split:
                for h in range(RET_HEADS):
                    st_s[h] = _mm_exact_lhs(sel_ref[...], s0_ref[0, h])
            else:
                st_s[...] = s0_ref[0]

    if pair_split:
        cos = cos_ref[...]
        sin = sin_ref[...]
        half = RET_HEAD // 2

        def rot(x):
            parts = []
            for h in range(RET_HEADS):
                x0 = x[:, h * RET_HEAD:h * RET_HEAD + half]
                x1 = x[:, h * RET_HEAD + half:(h + 1) * RET_HEAD]
                parts += [x0 * cos - x1 * sin, x0 * sin + x1 * cos]
            return jnp.concatenate(parts, axis=-1)
    else:
        lane = lax.broadcasted_iota(jnp.int32, (rows, RET_WIDTH), 1)
        even = (lane % 2) == 0
        cos = jnp.concatenate([cos_ref[...]] * RET_HEADS, axis=-1)
        sin = jnp.concatenate([sin_ref[...]] * RET_HEADS, axis=-1)

        def rot(x):
            partner = jnp.where(even, pltpu.roll(x, RET_WIDTH - 1, axis=1), pltpu.roll(x, 1, axis=1))
            return x * cos + partner * sin

    q = rot(load(zq_ref))
    k = rot(load(zk_ref)) * (RET_HEAD ** -0.5)
    v = load(zv_ref)
    g = load(zg_ref)

    out_rows = []
    for ch in range(n_chunks):
        rs = slice(ch * c, (ch + 1) * c)
        out_heads = []
        for h in range(RET_HEADS):
            hs = slice(h * RET_HEAD, (h + 1) * RET_HEAD)
            qh, kh, vh = q[rs, hs], k[rs, hs], v[rs, hs]
            s_prev = s0_ref[ch, h] if per_chunk_state else st_s[h]
            scores = _mm_nt(qh, kh) * dmask_ref[h]
            o = _mm(scores, vh) + _mm(qh, s_prev) * iscale_ref[h]
            s_new = s_prev * sdec_ref[h] + _mm_tn(kh * kscale_ref[h], vh)
            if per_chunk_state:
                sout_ref[ch, h] = s_new
            else:
                st_s[h] = s_new
            o = o * lax.rsqrt(jnp.mean(o * o, axis=-1, keepdims=True) + RET_GN_EPS)
            gh = g[rs, hs]
            out_heads.append(o * (gh * jax.nn.sigmoid(gh)))
        out_rows.append(jnp.concatenate(out_heads, axis=-1))
    o_out = jnp.concatenate(out_rows, axis=0).astype(BF16)
    if per_chunk_state:
        o_out = lax.dot_general(place, o_out, (((0,), (0,)), ((), ())),
                                preferred_element_type=F32).astype(BF16)
    o_ref[...] = o_out

    if not per_chunk_state:
        @pl.when(i == pl.num_programs(1) - 1)
        def _():
            if pair_split:
                for h in range(RET_HEADS):
                    sout_ref[0, h] = _mm_exact_lhs_t(sel_ref[...], st_s[h])
            else:
                sout_ref[0] = st_s[...]


def _retention_mixer(z, row_block0, n_groups, n_tiles, cos, sin, tabs, s0, *, rows, c, n_real,
                     per_chunk_state, pos_per_tile, pair_split=False):
    shared_s0 = s0.shape[0] == 1
    sb = 1 if shared_s0 else s0.shape[0] // n_groups
    kern = functools.partial(_retention_kernel, rows=rows, c=c, n_real=n_real,
                             per_chunk_state=per_chunk_state, pair_split=pair_split)
    rows_io = rows // c * n_real if per_chunk_state else rows
    zspec = lambda col: pl.BlockSpec((rows_io, RET_WIDTH), lambda gi, i: (row_block0 + gi * n_tiles + i, col))
    full = lambda arr: pl.BlockSpec(arr.shape, lambda gi, i: (0,) * arr.ndim)
    tbl_row0, tbl_advance = pos_per_tile
    assert tbl_row0 % rows == 0
    tspec = pl.BlockSpec((rows, cos.shape[1]),
                         lambda gi, i: (tbl_row0 // rows + (i if tbl_advance else 0), 0))
    state_blk = (sb, RET_HEADS, RET_HEAD, RET_HEAD)
    s0_spec = pl.BlockSpec(state_blk, lambda gi, i: (0 if shared_s0 else gi, 0, 0, 0))
    sout_spec = pl.BlockSpec(state_blk, lambda gi, i: (gi, 0, 0, 0))
    dmask, iscale, kscale, sdec = tabs
    sel = jnp.asarray(_pair_split_index(RET_HEAD)[:, None] == np.arange(RET_HEAD)[None, :], dtype=BF16)
    return pl.pallas_call(
        kern,
        grid=(n_groups, n_tiles),
        in_specs=[zspec(0), zspec(1), zspec(2), zspec(3), tspec, tspec,
                  full(dmask), full(iscale), full(kscale), full(sdec), full(sel), s0_spec],
        out_specs=[pl.BlockSpec((rows_io, RET_WIDTH), lambda gi, i: (gi * n_tiles + i, 0)), sout_spec],
        out_shape=[jax.ShapeDtypeStruct((n_groups * n_tiles * rows_io, RET_WIDTH), BF16),
                   jax.ShapeDtypeStruct((n_groups * sb,) + state_blk[1:], F32)],
        scratch_shapes=[pltpu.VMEM((RET_HEADS, RET_HEAD, RET_HEAD), F32)],
        compiler_params=_cparams(("parallel", "arbitrary")),
        name="retention_mixer",
    )(z, z, z, z, cos, sin, dmask, iscale, kscale, sdec, sel, s0)


def _retention_tables(c, n_real):
    log_gamma = np.log(1.0 - 2.0 ** (-5.0 - np.arange(RET_HEADS, dtype=np.float64)))
    r = np.arange(c, dtype=np.float64)
    idx = r - float(c - n_real)
    diff = r[:, None] - r[None, :]
    dmask = np.where(diff[None] >= 0, np.exp(log_gamma[:, None, None] * np.maximum(diff, 0.0)[None]), 0.0)
    iscale = np.exp(log_gamma[:, None] * (idx + 1.0)[None, :])[:, :, None]
    kscale = np.exp(log_gamma[:, None] * (n_real - 1.0 - idx)[None, :])[:, :, None]
    sdec = np.broadcast_to(np.exp(log_gamma * n_real)[:, None, None], (RET_HEADS, 1, RET_HEAD))
    return tuple(jnp.asarray(t, dtype=F32) for t in (dmask, iscale, kscale, sdec))


def _rotary_tables(pos, interleaved=True):
    inv_freq = 1.0 / (ROPE_BASE ** np.linspace(0.0, 1.0, RET_HEAD // 2))
    ang = np.asarray(pos, dtype=np.float64)[:, None] * inv_freq[None, :]
    cos = np.cos(ang)
    sin = np.sin(ang)
    if not interleaved:
        return jnp.asarray(cos, dtype=F32), jnp.asarray(sin, dtype=F32)
    cos2 = np.repeat(cos, 2, axis=-1)
    sin2 = np.stack([-sin, sin], axis=-1).reshape(len(pos), RET_HEAD)
    return jnp.asarray(cos2, dtype=F32), jnp.asarray(sin2, dtype=F32)


def _outproj_kernel(oa_ref, ob_ref, w_ref, x_ref, g_ref, h_ref, hn_ref):
    o_ab = jnp.concatenate([oa_ref[...], ob_ref[...]], axis=1)
    h = x_ref[...] + jnp.dot(o_ab, w_ref[...], preferred_element_type=F32)
    h_ref[...] = h
    hn_ref[...] = _rms_norm_bf16(h, g_ref[...])


def _out_projection(o_a, o_b, w_out, x2d, norm_g, tm):
    m = o_a.shape[0]
    return pl.pallas_call(
        _outproj_kernel,
        grid=(m // tm,),
        in_specs=[
            pl.BlockSpec((tm, RW_WIDTH), lambda i: (i, 0)),
            pl.BlockSpec((tm, RET_WIDTH), lambda i: (i, 0)),
            pl.BlockSpec((D_MODEL, D_MODEL), lambda i: (0, 0)),
            pl.BlockSpec((tm, D_MODEL), lambda i: (i, 0)),
            pl.BlockSpec((1, D_MODEL), lambda i: (0, 0)),
        ],
        out_specs=[pl.BlockSpec((tm, D_MODEL), lambda i: (i, 0)),
                   pl.BlockSpec((tm, D_MODEL), lambda i: (i, 0))],
        out_shape=[jax.ShapeDtypeStruct((m, D_MODEL), F32), jax.ShapeDtypeStruct((m, D_MODEL), BF16)],
        compiler_params=_cparams(("parallel",)),
        name="out_projection",
    )(o_a, o_b, w_out, x2d, norm_g)


def _ffn_kernel(hn_ref, wg_ref, wu_ref, wd_ref, h_ref, g_ref, y_ref, acc_ref, *, h_slices):
    f = pl.program_id(1)

    @pl.when(f == 0)
    def _():
        acc_ref[...] = jnp.zeros_like(acc_ref)

    @pl.when(f < h_slices)
    def _():
        rows = h_ref.shape[0]
        sl = pl.ds(pl.multiple_of(f * rows, rows), rows)
        acc_ref[sl, :] += h_ref[...]

    hn = hn_ref[...]
    gate = jnp.dot(hn, wg_ref[...], preferred_element_type=F32)
    up = jnp.dot(hn, wu_ref[...], preferred_element_type=F32)
    act = (gate * jax.nn.sigmoid(gate)) * up
    acc_ref[...] += jnp.dot(act.astype(BF16), wd_ref[...], preferred_element_type=F32)

    @pl.when(f == pl.num_programs(1) - 1)
    def _():
        h = acc_ref[...]
        ms = jnp.mean(h * h, axis=-1, keepdims=True)
        y_ref[...] = (h * lax.rsqrt(ms + RMS_EPS)) * g_ref[...]


def _ffn(hn, w_gate, w_up, w_down, h, norm_g, tm, tf, h_slices=8):
    m = hn.shape[0]
    assert D_FF // tf >= h_slices
    return pl.pallas_call(
        functools.partial(_ffn_kernel, h_slices=h_slices),
        grid=(m // tm, D_FF // tf),
        in_specs=[
            pl.BlockSpec((tm, D_MODEL), lambda i, f: (i, 0)),
            pl.BlockSpec((D_MODEL, tf), lambda i, f: (0, f)),
            pl.BlockSpec((D_MODEL, tf), lambda i, f: (0, f)),
            pl.BlockSpec((tf, D_MODEL), lambda i, f: (f, 0)),
            pl.BlockSpec((tm // h_slices, D_MODEL),
                         lambda i, f: (i * h_slices + jnp.minimum(f, h_slices - 1), 0)),
            pl.BlockSpec((1, D_MODEL), lambda i, f: (0, 0)),
        ],
        out_specs=pl.BlockSpec((tm, D_MODEL), lambda i, f: (i, 0)),
        out_shape=jax.ShapeDtypeStruct((m, D_MODEL), F32),
        scratch_shapes=[pltpu.VMEM((tm, D_MODEL), F32)],
        compiler_params=_cparams(("parallel", "arbitrary")),
        name="swiglu_ffn",
    )(hn, w_gate, w_up, w_down, h, norm_g)


def _pad_cols(a, n):
    return jnp.pad(a, ((0, 0), (0, n - a.shape[1])))


def _pad_rows(a, n):
    return jnp.pad(a, ((0, n - a.shape[0]), (0, 0)))


def kernel(x_prompt, x_sample, state_shift, state_rwkv, state_ret, meta_tokens, norm_mix, w_in,
           rwkv_mu, rwkv_w0, rwkv_w2, rwkv_a0, rwkv_a2, rwkv_g2, rwkv_kk, rwkv_ka, rwkv_rk,
           rwkv_ln_w, rwkv_ln_b, w_out, norm_ffn, w_gate, w_up, w_down, norm_final):
    n_b, seq, d = x_prompt.shape
    n_s, dec_seq, _ = x_sample.shape
    n_p = n_b * seq
    n_d = n_s * dec_seq
    depth = w_in.shape[0]
    assert depth == 1 and d == D_MODEL and n_s == LANES

    w_in_t = jnp.transpose(w_in[0])
    w_in_p = _w_in_layout(w_in_t, tn=Tiles.w_layout_cols)
    row = lambda a: a.reshape(1, -1).astype(F32)
    w2p = jnp.concatenate([rwkv_w2[0], jnp.zeros((128 - DECAY_LORA, RW_WIDTH), F32)], axis=0)
    a2p = jnp.concatenate([jnp.zeros((DECAY_LORA, RW_WIDTH), F32), rwkv_a2[0]], axis=0)
    g2p = jnp.concatenate([rwkv_g2[0], jnp.zeros((256 - GATE_LORA, RW_WIDTH), F32)], axis=0)
    rw_params = dict(mu=_pad_cols(row(rwkv_mu[0]), SHIFT_PAD), w0=row(rwkv_w0[0]), w2=w2p,
                     a0=row(rwkv_a0[0]), a2=a2p, g2=g2p, kk=row(rwkv_kk[0]), ka=row(rwkv_ka[0]),
                     rk=row(rwkv_rk[0]), lnw=row(rwkv_ln_w[0]), lnb=row(rwkv_ln_b[0]))
    col = lambda a: jnp.broadcast_to(a.reshape(-1, 1).astype(F32), (a.size, LANES))
    cols = jnp.stack([a.reshape(-1).astype(F32) for a in (rwkv_w0[0], rwkv_a0[0], rwkv_kk[0], rwkv_ka[0],
                                                         rwkv_rk[0], rwkv_ln_w[0], rwkv_ln_b[0])])
    rw_params_t = dict(cols=jnp.broadcast_to(cols[:, :, None], cols.shape + (LANES,)),
                       w2t=rwkv_w2[0].T, a2t=rwkv_a2[0].T, g2t=rwkv_g2[0].T)

    x_p = x_prompt.reshape(n_p, d)
    x_s = x_sample.reshape(n_d, d)
    meta = meta_tokens.astype(F32)
    x_sm = jnp.concatenate([x_s, meta], axis=0)
    x_tsm = jnp.concatenate([jnp.transpose(x_sample, (1, 0, 2)).reshape(n_d, d), meta,
                             jnp.zeros((LANES - N_META, d), F32)], axis=0)
    g_mix = row(norm_mix[0])
    z_p = _in_projection(x_p, g_mix, w_in_p, tm=Tiles.inproj_rows, tn=Tiles.inproj_cols)
    z_sm = _in_projection(x_sm, g_mix, w_in_p, tm=x_sm.shape[0], tn=Tiles.inproj_small_cols,
                          qk_interleaved=True, n_cols=RW_OFF)
    z_st = _in_projection_t(w_in_t, x_tsm, g_mix, SHIFT_PAD, tm=Tiles.inproj_t_rows)

    z_meta = jnp.pad(jnp.concatenate([z_sm[n_d:], jnp.transpose(z_st[:, n_d:n_d + N_META])], axis=1),
                     ((RET_CHUNK - N_META, 0), (0, 0)))
    srow = 2 * dec_seq

    zero_prev = jnp.zeros((1, 1, SHIFT_PAD), F32)
    zero_rw = jnp.zeros((1, RW_HEADS, RW_HEAD, RW_HEAD), F32)
    c_rw = Tiles.rwkv_chunk
    _, s_rw_meta, zlast_meta = _rwkv_mixer(z_meta, 1, 1, 1, zero_prev, zero_rw, rw_params, tb=c_rw, c=c_rw)
    oa_p, rwkv_p, zlast_p, w_out_b, w_gate_b, w_up_b, w_down_b = _rwkv_mixer(
        z_p, 0, n_b, seq // Tiles.rwkv_rows, zlast_meta, s_rw_meta, rw_params, tb=Tiles.rwkv_rows, c=c_rw,
        cast=(w_out[0], w_gate[0], w_up[0], w_down[0]))
    shift_t = _pad_rows(jnp.transpose(state_shift[0]), SHIFT_PAD)
    mu_t = _pad_rows(col(rwkv_mu[0]), SHIFT_PAD)
    state_t = jnp.transpose(state_rwkv[0], (1, 2, 3, 0))
    oa_st, rwkv_st = _rwkv_sample_mixer(z_st, shift_t, mu_t, state_t, rw_params_t, n_tok=dec_seq,
                                        hpb=Tiles.rwkv_sample_heads)
    oa_s = jnp.transpose(oa_st, (1, 0, 2)).reshape(n_d, RW_WIDTH)
    rwkv_s = jnp.transpose(rwkv_st, (3, 0, 1, 2))

    tabs_full = _retention_tables(RET_CHUNK, RET_CHUNK)
    tabs_meta = _retention_tables(RET_CHUNK, N_META)
    tabs_smp = _retention_tables(srow, dec_seq)
    sb_rt = Tiles.ret_sample_seqs
    cos_p, sin_p = _rotary_tables(N_META + np.arange(seq), interleaved=False)
    pos_ms = np.concatenate([np.arange(RET_CHUNK) - (RET_CHUNK - N_META),
                             PAST_LEN + np.tile(np.arange(srow) - (srow - dec_seq), sb_rt)])
    cos_ms, sin_ms = _rotary_tables(pos_ms)
    zero_rt = jnp.zeros((1, RET_HEADS, RET_HEAD, RET_HEAD), F32)
    _, s_rt_meta = _retention_mixer(z_meta, 0, 1, 1, cos_ms, sin_ms, tabs_meta, zero_rt,
                                    rows=RET_CHUNK, c=RET_CHUNK, n_real=N_META, per_chunk_state=False,
                                    pos_per_tile=(0, False))
    rows_rt = Tiles.ret_chunks_per_step * RET_CHUNK
    ob_p, ret_p = _retention_mixer(z_p, 0, n_b, seq // rows_rt, cos_p, sin_p, tabs_full, s_rt_meta,
                                   rows=rows_rt, c=RET_CHUNK, n_real=RET_CHUNK, per_chunk_state=False,
                                   pos_per_tile=(0, True), pair_split=True)
    ob_s, ret_s = _retention_mixer(z_sm, 0, n_s // sb_rt, 1, cos_ms, sin_ms, tabs_smp,
                                   state_ret.reshape(n_s, RET_HEADS, RET_HEAD, RET_HEAD),
                                   rows=sb_rt * srow, c=srow, n_real=dec_seq, per_chunk_state=True,
                                   pos_per_tile=(RET_CHUNK, False))

    g_ffn = row(norm_ffn[0])
    g_fin = row(norm_final)
    h_p, hn_p = _out_projection(oa_p, ob_p, w_out_b, x_p, g_ffn, tm=Tiles.outproj_rows)
    h_s, hn_s = _out_projection(oa_s, ob_s, w_out_b, x_s, g_ffn, tm=n_d)
    y_p = _ffn(hn_p, w_gate_b, w_up_b, w_down_b, h_p, g_fin, tm=Tiles.ffn_rows, tf=Tiles.ffn_cols)
    y_s = _ffn(hn_s, w_gate_b, w_up_b, w_down_b, h_s, g_fin, tm=n_d, tf=Tiles.ffn_cols)

    y_prompt = y_p.reshape(n_b, seq, d)
    y_sample = y_s.reshape(n_s, dec_seq, d)
    shift_p = zlast_p[:, 0, :RW_COLS][None]
    shift_s = jnp.transpose(z_st[:RW_COLS, (dec_seq - 1) * n_s:n_d])[None]
    return (y_prompt, y_sample, shift_p, rwkv_p[None], ret_p[None], shift_s, rwkv_s[None], ret_s[None])
```

```python
import functools
import math

import jax
import jax.numpy as jnp
import numpy as np
from jax import lax
from jax.experimental import pallas as pl
from jax.experimental.pallas import tpu as pltpu

F32 = jnp.float32
BF16 = jnp.bfloat16

D_MODEL = 2048
N_META = 16
RW_WIDTH = 1024
RW_HEAD = 64
RW_HEADS = 16
DECAY_LORA = 64
AAA_LORA = 64
GATE_LORA = 160
RW_COLS = 3 * RW_WIDTH + DECAY_LORA + AAA_LORA + GATE_LORA
RET_WIDTH = 1024
RET_HEADS = 4
RET_HEAD = 256
RET_CHUNK = 128
D_FF = 5632
RMS_EPS = 1e-6
RW_GN_EPS = 64e-5
RET_GN_EPS = 1e-6
ROPE_BASE = 10000.0
LANES = 128

LORA_PAD = 512
Z_COLS = 4 * RET_WIDTH + 3 * RW_WIDTH + LORA_PAD
RW_OFF = 4 * RET_WIDTH
SHIFT_PAD = 3 * RW_WIDTH + LORA_PAD

VMEM_LIMIT = 60 * 1024 * 1024
PAST_LEN = 16384


class Tiles:
    w_layout_cols = 512
    inproj_rows, inproj_cols = 1024, 1536
    inproj_small_cols = 512
    inproj_t_rows = 896
    rwkv_rows, rwkv_chunk = 256, 64
    rwkv_sample_heads = 2
    ret_chunks_per_step = 4
    ret_sample_seqs = 8
    outproj_rows = 512
    ffn_rows, ffn_cols = 1024, 512


def _cparams(sem):
    return pltpu.CompilerParams(dimension_semantics=sem, vmem_limit_bytes=VMEM_LIMIT)


def _mm(a, b):
    return jnp.dot(a.astype(BF16), b.astype(BF16), preferred_element_type=F32)


def _mm_nt(a, b):
    return lax.dot_general(a.astype(BF16), b.astype(BF16), (((1,), (1,)), ((), ())),
                           preferred_element_type=F32)


def _mm_tn(a, b):
    return lax.dot_general(a.astype(BF16), b.astype(BF16), (((0,), (0,)), ((), ())),
                           preferred_element_type=F32)


def _bmm(a, b):
    return jnp.einsum("bmk,bkn->bmn", a.astype(BF16), b.astype(BF16), preferred_element_type=F32)


def _bmm_nt(a, b):
    return jnp.einsum("bmk,bnk->bmn", a.astype(BF16), b.astype(BF16), preferred_element_type=F32)


def _bmm_tn(a, b):
    return jnp.einsum("bkm,bkn->bmn", a.astype(BF16), b.astype(BF16), preferred_element_type=F32)


def _mm_exact_lhs(m_bf16, x):
    hi = x.astype(BF16)
    r1 = x - hi.astype(F32)
    mid = r1.astype(BF16)
    lo = (r1 - mid.astype(F32)).astype(BF16)
    d = functools.partial(jnp.dot, preferred_element_type=F32)
    return d(m_bf16, hi) + d(m_bf16, mid) + d(m_bf16, lo)


def _mm_exact_rhs(x, m_bf16):
    hi = x.astype(BF16)
    r1 = x - hi.astype(F32)
    mid = r1.astype(BF16)
    lo = (r1 - mid.astype(F32)).astype(BF16)
    d = functools.partial(jnp.dot, preferred_element_type=F32)
    return d(hi, m_bf16) + d(mid, m_bf16) + d(lo, m_bf16)


def _mm_exact_lhs_t(m_bf16, x):
    hi = x.astype(BF16)
    r1 = x - hi.astype(F32)
    mid = r1.astype(BF16)
    lo = (r1 - mid.astype(F32)).astype(BF16)
    d = functools.partial(lax.dot_general, dimension_numbers=(((0,), (0,)), ((), ())),
                          preferred_element_type=F32)
    return d(m_bf16, hi) + d(m_bf16, mid) + d(m_bf16, lo)


def _rms_norm_bf16(x, g):
    ms = jnp.mean(x * x, axis=-1, keepdims=True)
    return ((x * lax.rsqrt(ms + RMS_EPS)) * g).astype(BF16)


def _pair_split_index(n):
    j = np.arange(n)
    half = RET_HEAD // 2
    return (j // RET_HEAD) * RET_HEAD + 2 * (j % half) + (j % RET_HEAD) // half


def _w_in_layout_kernel(w_ref, sel_ref, o_ref, *, n_qk):
    c = pl.program_id(0)

    @pl.when(c >= n_qk)
    def _():
        o_ref[...] = w_ref[...].T.astype(BF16)

    @pl.when(c < n_qk)
    def _():
        picked = jnp.dot(sel_ref[...], w_ref[...].astype(BF16), preferred_element_type=F32)
        o_ref[...] = picked.T.astype(BF16)


def _w_in_layout(w_in_t, tn):
    n, d = w_in_t.shape
    n_ret = 4 * RET_WIDTH // tn
    n_qk = 2 * RET_WIDTH // tn
    sel = jnp.asarray(_pair_split_index(tn)[:, None] == np.arange(tn)[None, :], dtype=BF16)

    def src_row(c):
        align = math.gcd(RW_COLS, tn)
        return (pl.multiple_of(jnp.where(c < n_ret, RW_COLS + tn * c, tn * (c - n_ret)), align), 0)

    return pl.pallas_call(
        functools.partial(_w_in_layout_kernel, n_qk=n_qk),
        grid=(Z_COLS // tn,),
        in_specs=[pl.BlockSpec((pl.Element(tn), pl.Element(d)), src_row),
                  pl.BlockSpec((tn, tn), lambda c: (0, 0))],
        out_specs=pl.BlockSpec((d, tn), lambda c: (0, c)),
        out_shape=jax.ShapeDtypeStruct((d, Z_COLS), BF16),
        compiler_params=_cparams(("parallel",)),
        name="w_in_layout",
    )(w_in_t, sel)


def _inproj_kernel(x_ref, g_ref, w_ref, sel_ref, o_ref, xn_ref, *, n_qk_interleave):
    j = pl.program_id(1)

    @pl.when(j == 0)
    def _():
        xn_ref[...] = _rms_norm_bf16(x_ref[...], g_ref[...])

    @pl.when(j >= n_qk_interleave)
    def _():
        o_ref[...] = jnp.dot(xn_ref[...], w_ref[...], preferred_element_type=F32)

    if n_qk_interleave:
        @pl.when(j < n_qk_interleave)
        def _():
            z = jnp.dot(xn_ref[...], w_ref[...], preferred_element_type=F32)
            o_ref[...] = _mm_exact_rhs(z, sel_ref[...])


def _in_projection(x2d, norm_g, w_in_p, tm, tn, qk_interleaved=False, n_cols=Z_COLS):
    m = x2d.shape[0]
    n_qk = 2 * RET_WIDTH // tn if qk_interleaved else 0
    assert not qk_interleaved or tn % RET_HEAD == 0
    ts = tn if qk_interleaved else LANES
    sel = jnp.asarray(_pair_split_index(ts)[:, None] == np.arange(ts)[None, :], dtype=BF16)
    return pl.pallas_call(
        functools.partial(_inproj_kernel, n_qk_interleave=n_qk),
        grid=(m // tm, n_cols // tn),
        in_specs=[
            pl.BlockSpec((tm, D_MODEL), lambda i, j: (i, 0)),
            pl.BlockSpec((1, D_MODEL), lambda i, j: (0, 0)),
            pl.BlockSpec((D_MODEL, tn), lambda i, j: (0, j)),
            pl.BlockSpec((ts, ts), lambda i, j: (0, 0)),
        ],
        out_specs=pl.BlockSpec((tm, tn), lambda i, j: (i, j)),
        out_shape=jax.ShapeDtypeStruct((m, n_cols), F32),
        scratch_shapes=[pltpu.VMEM((tm, D_MODEL), BF16)],
        compiler_params=_cparams(("parallel", "arbitrary")),
        name="in_projection",
    )(x2d, norm_g, w_in_p, sel)


def _inproj_t_kernel(w_ref, x_ref, g_ref, o_ref, xn_ref):
    @pl.when(pl.program_id(0) == 0)
    def _():
        xn_ref[...] = _rms_norm_bf16(x_ref[...], g_ref[...])

    o_ref[...] = _mm_nt(w_ref[...], xn_ref[...])


def _in_projection_t(w_in_t, x2d, norm_g, n_rows, tm):
    m, d = x2d.shape
    return pl.pallas_call(
        _inproj_t_kernel,
        grid=(n_rows // tm,),
        in_specs=[
            pl.BlockSpec((tm, d), lambda i: (i, 0)),
            pl.BlockSpec((m, d), lambda i: (0, 0)),
            pl.BlockSpec((1, d), lambda i: (0, 0)),
        ],
        out_specs=pl.BlockSpec((tm, m), lambda i: (i, 0)),
        out_shape=jax.ShapeDtypeStruct((n_rows, m), F32),
        scratch_shapes=[pltpu.VMEM((m, d), BF16)],
        compiler_params=_cparams(("arbitrary",)),
        name="in_projection_t",
    )(w_in_t, x2d, norm_g)


def _rwkv_kernel(zr_ref, zk_ref, zv_ref, zl_ref, zprev0_ref, s0_ref,
                 mu_ref, w0_ref, w2_ref, a0_ref, a2_ref, g2_ref, kk_ref, ka_ref, rk_ref,
                 lnw_ref, lnb_ref,
                 o_ref, sout_ref, zlast_ref,
                 carry_ref, sbd_s, r_s, km_s, v_s, kk_s, a_s, lw_s, cum_s, o_s,
                 *, tb, c, n_factors):
    i = pl.program_id(1)
    n_chunks = tb // c
    n_pairs = RW_HEADS // 2
    pw = 2 * RW_HEAD

    @pl.when(i == 0)
    def _():
        carry_ref[...] = zprev0_ref[0]
        zero = jnp.zeros((RW_HEAD, RW_HEAD), F32)
        for p in range(n_pairs):
            top = jnp.concatenate([s0_ref[0, 2 * p], zero], axis=1)
            bot = jnp.concatenate([zero, s0_ref[0, 2 * p + 1]], axis=1)
            sbd_s[p] = jnp.concatenate([top, bot], axis=0)

    row = lax.broadcasted_iota(jnp.int32, (tb, 1), 0)

    def shifted(z_ref, lo, hi):
        z = z_ref[...]
        prev = pltpu.roll(z, 1, axis=0)
        prev = jnp.concatenate([jnp.where(row[:8] == 0, carry_ref[:, lo:hi], prev[:8]), prev[8:]], axis=0)
        zs = z + mu_ref[:, lo:hi] * (prev - z)
        carry_ref[:, lo:hi] = z[tb - 1:tb, :]
        return zs

    w = RW_WIDTH
    r = shifted(zr_ref, 0, w)
    k = shifted(zk_ref, w, 2 * w)
    v = shifted(zv_ref, 2 * w, 3 * w)
    zl = shifted(zl_ref, 3 * w, 3 * w + LORA_PAD)

    lo2 = zl[:, 0:128]
    wlog = -jax.nn.softplus(-(w0_ref[...] + _mm(jnp.tanh(lo2), w2_ref[...]))) - 0.5
    lw = -jnp.exp(wlog)
    a = jax.nn.sigmoid(a0_ref[...] + _mm(lo2, a2_ref[...]))
    g = _mm(jax.nn.sigmoid(zl[:, 128:384]), g2_ref[...])
    kk = k * kk_ref[...]
    km = k * (1.0 + (a - 1.0) * ka_ref[...])

    for dst, val in ((r_s, r), (km_s, km), (v_s, v), (a_s, a), (lw_s, lw), (kk_s, kk)):
        dst[...] = val

    rr = lax.broadcasted_iota(jnp.int32, (c, c), 0)
    cc = lax.broadcasted_iota(jnp.int32, (c, c), 1)
    tri = jnp.where(rr >= cc, 1.0, 0.0).astype(BF16)
    rr2 = lax.broadcasted_iota(jnp.int32, (c, 2 * c), 0)
    cc2 = lax.broadcasted_iota(jnp.int32, (c, 2 * c), 1)
    strict = (rr > cc)[None]
    eye = jnp.where(rr == cc, 1.0, 0.0).astype(F32)[None]
    strict_k = ((cc2 >= c) & (rr2 > cc2 - c))[None]
    incl_bk = (rr2 >= cc2 % c)[None]
    lo = (lax.broadcasted_iota(jnp.int32, (1, 1, pw), 2) < RW_HEAD)
    bd_r = lax.broadcasted_iota(jnp.int32, (pw, pw), 0) // RW_HEAD
    bd_c = lax.broadcasted_iota(jnp.int32, (pw, pw), 1) // RW_HEAD
    block_diag = (bd_r == bd_c)[None]
    rk = rk_ref[...]
    lnw = lnw_ref[...]
    lnb = lnb_ref[...]

    def head_sum(x):
        s_lo = jnp.sum(jnp.where(lo, x, 0.0), axis=-1, keepdims=True)
        s_hi = jnp.sum(jnp.where(lo, 0.0, x), axis=-1, keepdims=True)
        return jnp.where(lo, s_lo, s_hi)

    def pick(x16):
        x4 = x16.reshape(n_pairs, 2, c, pw)
        return jnp.where(lo, x4[:, 0], x4[:, 1])

    def body(ch, s_prev):
        rows = pl.ds(ch * c, c)

        def pairs(s):
            x = s[rows, :]
            return jnp.stack([x[:, p * pw:(p + 1) * pw] for p in range(n_pairs)], axis=0)

        cum_s[rows, :] = _mm_exact_lhs(tri, lw_s[rows, :])
        rh, kmh, vh, ah, lwh, cumh, kkh = [pairs(s) for s in (r_s, km_s, v_s, a_s, lw_s, cum_s, kk_s)]
        kkn = kkh * lax.rsqrt(jnp.maximum(head_sum(kkh * kkh), 1e-24))
        bvec = kkn * ah
        cum_end = cumh[:, c - 1:c, :]
        p_inc = jnp.exp(cumh)
        p_exc = jnp.exp(cumh - lwh)
        p_inv = jnp.exp(-cumh)
        p_end = jnp.exp(cum_end)
        p_rel = jnp.exp(cum_end - cumh)
        a_t = -(kkn * p_exc)
        r_t = rh * p_inc
        b_t = bvec * p_inv
        k_t = kmh * p_inv
        b_h = bvec * p_rel
        k_h = kmh * p_rel

        ar = jnp.concatenate([jnp.where(lo, a_t, 0.0), jnp.where(lo, r_t, 0.0),
                              jnp.where(lo, 0.0, a_t), jnp.where(lo, 0.0, r_t)], axis=1)
        prod = _bmm_nt(ar, jnp.concatenate([b_t, k_t, s_prev], axis=1))
        prod = prod.reshape(RW_HEADS, 2 * c, 2 * c + pw)
        a_rows, r_rows = prod[:, :c], prod[:, c:]
        aak_wide = jnp.where(strict_k, a_rows[:, :, :2 * c], 0.0)
        arbk = jnp.where(incl_bk, r_rows[:, :, :2 * c], 0.0)
        lmat = jnp.where(strict, a_rows[:, :, :c], 0.0)
        tinv = eye + lmat
        if n_factors > 1:
            lp = _bmm(lmat, lmat)
            for _ in range(n_factors - 2):
                st = _bmm(jnp.concatenate([lp, tinv], axis=1), lp)
                lp = st[:, :c]
                tinv = tinv + st[:, c:]
            tinv = tinv + _bmm(tinv, lp)
        v16 = jnp.repeat(vh, 2, axis=0)
        x = a_rows[:, :, 2 * c:] + _bmm(aak_wide, jnp.concatenate([v16, v16], axis=1))
        u16 = _bmm(tinv, x)
        o16 = r_rows[:, :, 2 * c:] + _bmm(arbk, jnp.concatenate([u16, v16], axis=1))
        u = pick(u16)
        o = pick(o16)
        s_upd = _bmm_tn(jnp.concatenate([u, vh], axis=1), jnp.concatenate([b_h, k_h], axis=1))
        s_new = s_prev * p_end + jnp.where(block_diag, s_upd, 0.0)
        inv_n = 1.0 / RW_HEAD
        mean = head_sum(o) * inv_n
        var = head_sum(jnp.square(o - mean)) * inv_n
        on = ((o - mean) * lax.rsqrt(var + RW_GN_EPS)) * lnw + lnb
        out = on + head_sum(rh * kmh * rk) * vh
        for p in range(n_pairs):
            o_s[rows, p * pw:(p + 1) * pw] = out[p]
        return s_new

    state = sbd_s[...]
    for ch in range(n_chunks):
        state = body(ch, state)
    sbd_s[...] = state

    o_ref[...] = (o_s[...] * g).astype(BF16)
    zlast_ref[0] = carry_ref[...]

    @pl.when(i == pl.num_programs(1) - 1)
    def _():
        for p in range(n_pairs):
            blk = sbd_s[p]
            sout_ref[0, 2 * p] = blk[:RW_HEAD, :RW_HEAD]
            sout_ref[0, 2 * p + 1] = blk[RW_HEAD:, RW_HEAD:]


N_RWKV_INPUTS = 17
N_RWKV_OUTPUTS = 3


def _rwkv_and_cast_kernel(*refs, n_cast, **kw):
    ins = refs[:N_RWKV_INPUTS]
    cast_in = refs[N_RWKV_INPUTS:N_RWKV_INPUTS + n_cast]
    o0 = N_RWKV_INPUTS + n_cast
    outs = refs[o0:o0 + N_RWKV_OUTPUTS]
    cast_out = refs[o0 + N_RWKV_OUTPUTS:o0 + N_RWKV_OUTPUTS + n_cast]
    scratch = refs[o0 + N_RWKV_OUTPUTS + n_cast:]
    for src, dst in zip(cast_in, cast_out):
        dst[...] = src[...].astype(BF16)
    _rwkv_kernel(*ins, *outs, *scratch, **kw)


def _rwkv_mixer(z, row_block0, n_groups, n_tiles, zprev0, s0, p, *, tb, c, cast=()):
    n_factors = max(1, math.ceil(math.log2(c)))
    kern = functools.partial(_rwkv_and_cast_kernel, n_cast=len(cast), tb=tb, c=c, n_factors=n_factors)
    n_steps = n_groups * n_tiles
    cast_specs = [pl.BlockSpec((a.shape[0] // n_steps, a.shape[1]), lambda gi, i: (gi * n_tiles + i, 0))
                  for a in cast]
    cast_shapes = [jax.ShapeDtypeStruct(a.shape, BF16) for a in cast]
    cb = RW_OFF // RW_WIDTH
    full = lambda arr: pl.BlockSpec(arr.shape, lambda gi, i: (0,) * arr.ndim)
    lora_cb = (RW_OFF + 3 * RW_WIDTH) // LORA_PAD
    per_pair = lambda a: a.reshape(RW_HEADS // 2, 1, 2 * RW_HEAD)
    params = [p["mu"], p["w0"], p["w2"], p["a0"], p["a2"], p["g2"], p["kk"], p["ka"],
              per_pair(p["rk"]), per_pair(p["lnw"]), per_pair(p["lnb"])]
    dense = lambda: pltpu.VMEM((tb, RW_WIDTH), F32)
    zspec = lambda width, col: pl.BlockSpec((tb, width), lambda gi, i: (row_block0 + gi * n_tiles + i, col))
    sspec = pl.BlockSpec((1, RW_HEADS, RW_HEAD, RW_HEAD), lambda gi, i: (gi, 0, 0, 0))
    shared = lambda a: (lambda gi, i: (0,) * a.ndim) if a.shape[0] == 1 else (lambda gi, i: (gi,) + (0,) * (a.ndim - 1))
    return pl.pallas_call(
        kern,
        grid=(n_groups, n_tiles),
        in_specs=[zspec(RW_WIDTH, cb), zspec(RW_WIDTH, cb + 1), zspec(RW_WIDTH, cb + 2),
                  zspec(LORA_PAD, lora_cb),
                  pl.BlockSpec((1, 1, SHIFT_PAD), shared(zprev0)),
                  pl.BlockSpec((1, RW_HEADS, RW_HEAD, RW_HEAD), shared(s0))]
                 + [full(a) for a in params] + cast_specs,
        out_specs=[pl.BlockSpec((tb, RW_WIDTH), lambda gi, i: (gi * n_tiles + i, 0)), sspec,
                   pl.BlockSpec((1, 1, SHIFT_PAD), lambda gi, i: (gi, 0, 0))] + cast_specs,
        out_shape=[jax.ShapeDtypeStruct((n_groups * n_tiles * tb, RW_WIDTH), BF16),
                   jax.ShapeDtypeStruct((n_groups, RW_HEADS, RW_HEAD, RW_HEAD), F32),
                   jax.ShapeDtypeStruct((n_groups, 1, SHIFT_PAD), F32)] + cast_shapes,
        scratch_shapes=[pltpu.VMEM((1, SHIFT_PAD), F32),
                        pltpu.VMEM((RW_HEADS // 2, 2 * RW_HEAD, 2 * RW_HEAD), F32)]
                       + [dense() for _ in range(8)],
        compiler_params=_cparams(("parallel", "arbitrary")),
        name="rwkv7_mixer",
    )(z, z, z, z, zprev0, s0, *params, *cast)


def _rwkv_sample_kernel(zr_ref, zk_ref, zv_ref, zl_ref, shr_ref, shk_ref, shv_ref, shl_ref,
                        mur_ref, muk_ref, muv_ref, mul_ref, s0_ref,
                        pc_ref, w2t_ref, a2t_ref, g2t_ref,
                        o_ref, sout_ref,
                        dec_s, a_s, b_s, k_s, r_s, v_s, o_s,
                        *, n_tok, hpb):
    w0_ref, a0_ref, kk_ref, ka_ref, rk_ref, lnw_ref, lnb_ref = [pc_ref.at[j] for j in range(7)]
    ns = LANES
    ch = hpb * RW_HEAD

    def lanes(x):
        return jnp.concatenate([x] * n_tok, axis=1)

    def shifted(z_ref, sh_ref, mu_ref):
        z = z_ref[...]
        prev = jnp.concatenate([sh_ref[...], z[:, :(n_tok - 1) * ns]], axis=1)
        return z + lanes(mu_ref[...]) * (prev - z)

    r = shifted(zr_ref, shr_ref, mur_ref)
    k = shifted(zk_ref, shk_ref, muk_ref)
    v = shifted(zv_ref, shv_ref, muv_ref)
    zl = shifted(zl_ref, shl_ref, mul_ref)
    wd = zl[0:DECAY_LORA]
    ad = zl[DECAY_LORA:DECAY_LORA + AAA_LORA]
    gd = zl[DECAY_LORA + AAA_LORA:DECAY_LORA + AAA_LORA + GATE_LORA]

    wlog = -jax.nn.softplus(-(lanes(w0_ref[...]) + _mm(w2t_ref[...], jnp.tanh(wd)))) - 0.5
    lw = -jnp.exp(wlog)
    a = jax.nn.sigmoid(lanes(a0_ref[...]) + _mm(a2t_ref[...], ad))
    g = _mm(g2t_ref[...], jax.nn.sigmoid(gd))
    kk = k * lanes(kk_ref[...])
    km = k * (1.0 + (a - 1.0) * lanes(ka_ref[...]))

    def head_sum(x):
        x3 = x.reshape(hpb, RW_HEAD, n_tok * ns)
        s = jnp.sum(x3, axis=1, keepdims=True)
        return jnp.broadcast_to(s, x3.shape).reshape(ch, n_tok * ns)

    kkn = kk * lax.rsqrt(jnp.maximum(head_sum(kk * kk), 1e-24))
    dec_s[...] = jnp.exp(lw)
    a_s[...] = -kkn
    b_s[...] = kkn * a
    k_s[...] = km
    r_s[...] = r
    v_s[...] = v

    for hh in range(hpb):
        hrows = slice(hh * RW_HEAD, (hh + 1) * RW_HEAD)

        def body(i8, carry, hh=hh, hrows=hrows):
            base = pl.multiple_of(i8 * 8, 8)
            rows8 = pl.ds(pl.multiple_of(hh * RW_HEAD + base, 8), 8)
            o_rows = [[] for _ in range(n_tok)]
            for j in range(8):
                s = s0_ref[hh, base + j]
                for t in range(n_tok):
                    tl = slice(t * ns, (t + 1) * ns)
                    sa = jnp.sum(s * a_s[hrows, tl], axis=0, keepdims=True)
                    v_row = v_s[rows8, tl][j:j + 1, :]
                    s = s * dec_s[hrows, tl] + sa * b_s[hrows, tl] + v_row * k_s[hrows, tl]
                    o_rows[t].append(jnp.sum(s * r_s[hrows, tl], axis=0, keepdims=True))
                sout_ref[hh, base + j] = s
            for t in range(n_tok):
                o_s[rows8, t * ns:(t + 1) * ns] = jnp.concatenate(o_rows[t], axis=0)
            return carry

        lax.fori_loop(0, RW_HEAD // 8, body, 0)

    o = o_s[...]
    inv_n = 1.0 / RW_HEAD
    mean = head_sum(o) * inv_n
    var = head_sum(jnp.square(o - mean)) * inv_n
    on = ((o - mean) * lax.rsqrt(var + RW_GN_EPS)) * lanes(lnw_ref[...]) + lanes(lnb_ref[...])
    bonus = head_sum(r * km * lanes(rk_ref[...])) * v
    out = (on + bonus) * g
    for t in range(n_tok):
        o_ref[t] = out[:, t * ns:(t + 1) * ns].T.astype(BF16)


def _rwkv_sample_mixer(z_t, shift_t, mu_t, state_t, p_t, *, n_tok, hpb):
    ch = hpb * RW_HEAD
    n_steps = RW_HEADS // hpb
    nl = n_tok * LANES
    seg = RW_WIDTH // ch
    zspec = lambda s: pl.BlockSpec((ch, nl), lambda h: (s * seg + h, 0))
    cspec = lambda s: pl.BlockSpec((ch, LANES), lambda h: (s * seg + h, 0))
    lora_blk = 3 * RW_WIDTH // LORA_PAD
    zl_spec = pl.BlockSpec((LORA_PAD, nl), lambda h: (lora_blk, 0))
    cl_spec = pl.BlockSpec((LORA_PAD, LANES), lambda h: (lora_blk, 0))
    pc_spec = pl.BlockSpec((p_t["cols"].shape[0], ch, LANES), lambda h: (0, h, 0))
    wspec = lambda k: pl.BlockSpec((ch, k), lambda h: (h, 0))
    sspec = pl.BlockSpec((hpb, RW_HEAD, RW_HEAD, LANES), lambda h: (h, 0, 0, 0))
    buf = lambda: pltpu.VMEM((ch, nl), F32)
    return pl.pallas_call(
        functools.partial(_rwkv_sample_kernel, n_tok=n_tok, hpb=hpb),
        grid=(n_steps,),
        in_specs=[zspec(0), zspec(1), zspec(2), zl_spec,
                  cspec(0), cspec(1), cspec(2), cl_spec,
                  cspec(0), cspec(1), cspec(2), cl_spec,
                  sspec,
                  pc_spec, wspec(DECAY_LORA), wspec(AAA_LORA), wspec(GATE_LORA)],
        out_specs=[pl.BlockSpec((n_tok, LANES, ch), lambda h: (0, 0, h)), sspec],
        out_shape=[jax.ShapeDtypeStruct((n_tok, LANES, RW_WIDTH), BF16),
                   jax.ShapeDtypeStruct(state_t.shape, F32)],
        scratch_shapes=[buf() for _ in range(7)],
        compiler_params=_cparams(("parallel",)),
        name="rwkv7_sample",
    )(z_t, z_t, z_t, z_t, shift_t, shift_t, shift_t, shift_t, mu_t, mu_t, mu_t, mu_t, state_t,
      p_t["cols"], p_t["w2t"], p_t["a2t"], p_t["g2t"])


def _retention_kernel(zq_ref, zk_ref, zv_ref, zg_ref, cos_ref, sin_ref, dmask_ref, iscale_ref,
                      kscale_ref, sdec_ref, sel_ref, s0_ref, o_ref, sout_ref, st_s,
                      *, rows, c, n_real, per_chunk_state, pair_split):
    i = pl.program_id(1)
    n_chunks = rows // c
    assert not (pair_split and per_chunk_state)

    if per_chunk_state:
        pr = lax.broadcasted_iota(jnp.int32, (rows, n_chunks * n_real), 0)
        pc = lax.broadcasted_iota(jnp.int32, (rows, n_chunks * n_real), 1)
        place = jnp.where((pr // c == pc // n_real) & (pr % c - (c - n_real) == pc % n_real),
                          1.0, 0.0).astype(BF16)
        load = lambda ref: _mm_exact_lhs(place, ref[...])
    else:
        load = lambda ref: ref[...]

        @pl.when(i == 0)
        def _():
            if pair_split:
                for h in range(RET_HEADS):
                    st_s[h] = _mm_exact_lhs(sel_ref[...], s0_ref[0, h])
            else:
                st_s[...] = s0_ref[0]

    if pair_split:
        cos = cos_ref[...]
        sin = sin_ref[...]
        half = RET_HEAD // 2

        def rot(x):
            parts = []
            for h in range(RET_HEADS):
                x0 = x[:, h * RET_HEAD:h * RET_HEAD + half]
                x1 = x[:, h * RET_HEAD + half:(h + 1) * RET_HEAD]
                parts += [x0 * cos - x1 * sin, x0 * sin + x1 * cos]
            return jnp.concatenate(parts, axis=-1)
    else:
        lane = lax.broadcasted_iota(jnp.int32, (rows, RET_WIDTH), 1)
        even = (lane % 2) == 0
        cos = jnp.concatenate([cos_ref[...]] * RET_HEADS, axis=-1)
        sin = jnp.concatenate([sin_ref[...]] * RET_HEADS, axis=-1)

        def rot(x):
            partner = jnp.where(even, pltpu.roll(x, RET_WIDTH - 1, axis=1), pltpu.roll(x, 1, axis=1))
            return x * cos + partner * sin

    q = rot(load(zq_ref))
    k = rot(load(zk_ref)) * (RET_HEAD ** -0.5)
    v = load(zv_ref)
    g = load(zg_ref)

    out_rows = []
    for ch in range(n_chunks):
        rs = slice(ch * c, (ch + 1) * c)
        out_heads = []
        for h in range(RET_HEADS):
            hs = slice(h * RET_HEAD, (h + 1) * RET_HEAD)
            qh, kh, vh = q[rs, hs], k[rs, hs], v[rs, hs]
            s_prev = s0_ref[ch, h] if per_chunk_state else st_s[h]
            scores = _mm_nt(qh, kh) * dmask_ref[h]
            o = _mm(scores, vh) + _mm(qh, s_prev) * iscale_ref[h]
            s_new = s_prev * sdec_ref[h] + _mm_tn(kh * kscale_ref[h], vh)
            if per_chunk_state:
                sout_ref[ch, h] = s_new
            else:
                st_s[h] = s_new
            o = o * lax.rsqrt(jnp.mean(o * o, axis=-1, keepdims=True) + RET_GN_EPS)
            gh = g[rs, hs]
            out_heads.append(o * (gh * jax.nn.sigmoid(gh)))
        out_rows.append(jnp.concatenate(out_heads, axis=-1))
    o_out = jnp.concatenate(out_rows, axis=0).astype(BF16)
    if per_chunk_state:
        o_out = lax.dot_general(place, o_out, (((0,), (0,)), ((), ())),
                                preferred_element_type=F32).astype(BF16)
    o_ref[...] = o_out

    if not per_chunk_state:
        @pl.when(i == pl.num_programs(1) - 1)
        def _():
            if pair_split:
                for h in range(RET_HEADS):
                    sout_ref[0, h] = _mm_exact_lhs_t(sel_ref[...], st_s[h])
            else:
                sout_ref[0] = st_s[...]


def _retention_mixer(z, row_block0, n_groups, n_tiles, cos, sin, tabs, s0, *, rows, c, n_real,
                     per_chunk_state, pos_per_tile, pair_split=False):
    shared_s0 = s0.shape[0] == 1
    sb = 1 if shared_s0 else s0.shape[0] // n_groups
    kern = functools.partial(_retention_kernel, rows=rows, c=c, n_real=n_real,
                             per_chunk_state=per_chunk_state, pair_split=pair_split)
    rows_io = rows // c * n_real if per_chunk_state else rows
    zspec = lambda col: pl.BlockSpec((rows_io, RET_WIDTH), lambda gi, i: (row_block0 + gi * n_tiles + i, col))
    full = lambda arr: pl.BlockSpec(arr.shape, lambda gi, i: (0,) * arr.ndim)
    tbl_row0, tbl_advance = pos_per_tile
    assert tbl_row0 % rows == 0
    tspec = pl.BlockSpec((rows, cos.shape[1]),
                         lambda gi, i: (tbl_row0 // rows + (i if tbl_advance else 0), 0))
    state_blk = (sb, RET_HEADS, RET_HEAD, RET_HEAD)
    s0_spec = pl.BlockSpec(state_blk, lambda gi, i: (0 if shared_s0 else gi, 0, 0, 0))
    sout_spec = pl.BlockSpec(state_blk, lambda gi, i: (gi, 0, 0, 0))
    dmask, iscale, kscale, sdec = tabs
    sel = jnp.asarray(_pair_split_index(RET_HEAD)[:, None] == np.arange(RET_HEAD)[None, :], dtype=BF16)
    return pl.pallas_call(
        kern,
        grid=(n_groups, n_tiles),
        in_specs=[zspec(0), zspec(1), zspec(2), zspec(3), tspec, tspec,
                  full(dmask), full(iscale), full(kscale), full(sdec), full(sel), s0_spec],
        out_specs=[pl.BlockSpec((rows_io, RET_WIDTH), lambda gi, i: (gi * n_tiles + i, 0)), sout_spec],
        out_shape=[jax.ShapeDtypeStruct((n_groups * n_tiles * rows_io, RET_WIDTH), BF16),
                   jax.ShapeDtypeStruct((n_groups * sb,) + state_blk[1:], F32)],
        scratch_shapes=[pltpu.VMEM((RET_HEADS, RET_HEAD, RET_HEAD), F32)],
        compiler_params=_cparams(("parallel", "arbitrary")),
        name="retention_mixer",
    )(z, z, z, z, cos, sin, dmask, iscale, kscale, sdec, sel, s0)


def _retention_tables(c, n_real):
    log_gamma = np.log(1.0 - 2.0 ** (-5.0 - np.arange(RET_HEADS, dtype=np.float64)))
    r = np.arange(c, dtype=np.float64)
    idx = r - float(c - n_real)
    diff = r[:, None] - r[None, :]
    dmask = np.where(diff[None] >= 0, np.exp(log_gamma[:, None, None] * np.maximum(diff, 0.0)[None]), 0.0)
    iscale = np.exp(log_gamma[:, None] * (idx + 1.0)[None, :])[:, :, None]
    kscale = np.exp(log_gamma[:, None] * (n_real - 1.0 - idx)[None, :])[:, :, None]
    sdec = np.broadcast_to(np.exp(log_gamma * n_real)[:, None, None], (RET_HEADS, 1, RET_HEAD))
    return tuple(jnp.asarray(t, dtype=F32) for t in (dmask, iscale, kscale, sdec))


def _rotary_tables(pos, interleaved=True):
    inv_freq = 1.0 / (ROPE_BASE ** np.linspace(0.0, 1.0, RET_HEAD // 2))
    ang = np.asarray(pos, dtype=np.float64)[:, None] * inv_freq[None, :]
    cos = np.cos(ang)
    sin = np.sin(ang)
    if not interleaved:
        return jnp.asarray(cos, dtype=F32), jnp.asarray(sin, dtype=F32)
    cos2 = np.repeat(cos, 2, axis=-1)
    sin2 = np.stack([-sin, sin], axis=-1).reshape(len(pos), RET_HEAD)
    return jnp.asarray(cos2, dtype=F32), jnp.asarray(sin2, dtype=F32)


def _outproj_kernel(oa_ref, ob_ref, w_ref, x_ref, g_ref, h_ref, hn_ref):
    o_ab = jnp.concatenate([oa_ref[...], ob_ref[...]], axis=1)
    h = x_ref[...] + jnp.dot(o_ab, w_ref[...], preferred_element_type=F32)
    h_ref[...] = h
    hn_ref[...] = _rms_norm_bf16(h, g_ref[...])


def _out_projection(o_a, o_b, w_out, x2d, norm_g, tm):
    m = o_a.shape[0]
    return pl.pallas_call(
        _outproj_kernel,
        grid=(m // tm,),
        in_specs=[
            pl.BlockSpec((tm, RW_WIDTH), lambda i: (i, 0)),
            pl.BlockSpec((tm, RET_WIDTH), lambda i: (i, 0)),
            pl.BlockSpec((D_MODEL, D_MODEL), lambda i: (0, 0)),
            pl.BlockSpec((tm, D_MODEL), lambda i: (i, 0)),
            pl.BlockSpec((1, D_MODEL), lambda i: (0, 0)),
        ],
        out_specs=[pl.BlockSpec((tm, D_MODEL), lambda i: (i, 0)),
                   pl.BlockSpec((tm, D_MODEL), lambda i: (i, 0))],
        out_shape=[jax.ShapeDtypeStruct((m, D_MODEL), F32), jax.ShapeDtypeStruct((m, D_MODEL), BF16)],
        compiler_params=_cparams(("parallel",)),
        name="out_projection",
    )(o_a, o_b, w_out, x2d, norm_g)


def _ffn_kernel(hn_ref, wg_ref, wu_ref, wd_ref, h_ref, g_ref, y_ref, acc_ref, *, h_slices):
    f = pl.program_id(1)

    @pl.when(f == 0)
    def _():
        acc_ref[...] = jnp.zeros_like(acc_ref)

    @pl.when(f < h_slices)
    def _():
        rows = h_ref.shape[0]
        sl = pl.ds(pl.multiple_of(f * rows, rows), rows)
        acc_ref[sl, :] += h_ref[...]

    hn = hn_ref[...]
    gate = jnp.dot(hn, wg_ref[...], preferred_element_type=F32)
    up = jnp.dot(hn, wu_ref[...], preferred_element_type=F32)
    act = (gate * jax.nn.sigmoid(gate)) * up
    acc_ref[...] += jnp.dot(act.astype(BF16), wd_ref[...], preferred_element_type=F32)

    @pl.when(f == pl.num_programs(1) - 1)
    def _():
        h = acc_ref[...]
        ms = jnp.mean(h * h, axis=-1, keepdims=True)
        y_ref[...] = (h * lax.rsqrt(ms + RMS_EPS)) * g_ref[...]


def _ffn(hn, w_gate, w_up, w_down, h, norm_g, tm, tf, h_slices=8):
    m = hn.shape[0]
    assert D_FF // tf >= h_slices
    return pl.pallas_call(
        functools.partial(_ffn_kernel, h_slices=h_slices),
        grid=(m // tm, D_FF // tf),
        in_specs=[
            pl.BlockSpec((tm, D_MODEL), lambda i, f: (i, 0)),
            pl.BlockSpec((D_MODEL, tf), lambda i, f: (0, f)),
            pl.BlockSpec((D_MODEL, tf), lambda i, f: (0, f)),
            pl.BlockSpec((tf, D_MODEL), lambda i, f: (f, 0)),
            pl.BlockSpec((tm // h_slices, D_MODEL),
                         lambda i, f: (i * h_slices + jnp.minimum(f, h_slices - 1), 0)),
            pl.BlockSpec((1, D_MODEL), lambda i, f: (0, 0)),
        ],
        out_specs=pl.BlockSpec((tm, D_MODEL), lambda i, f: (i, 0)),
        out_shape=jax.ShapeDtypeStruct((m, D_MODEL), F32),
        scratch_shapes=[pltpu.VMEM((tm, D_MODEL), F32)],
        compiler_params=_cparams(("parallel", "arbitrary")),
        name="swiglu_ffn",
    )(hn, w_gate, w_up, w_down, h, norm_g)


def _pad_cols(a, n):
    return jnp.pad(a, ((0, 0), (0, n - a.shape[1])))


def _pad_rows(a, n):
    return jnp.pad(a, ((0, n - a.shape[0]), (0, 0)))


def kernel(x_prompt, x_sample, state_shift, state_rwkv, state_ret, meta_tokens, norm_mix, w_in,
           rwkv_mu, rwkv_w0, rwkv_w2, rwkv_a0, rwkv_a2, rwkv_g2, rwkv_kk, rwkv_ka, rwkv_rk,
           rwkv_ln_w, rwkv_ln_b, w_out, norm_ffn, w_gate, w_up, w_down, norm_final):
    n_b, seq, d = x_prompt.shape
    n_s, dec_seq, _ = x_sample.shape
    n_p = n_b * seq
    n_d = n_s * dec_seq
    depth = w_in.shape[0]
    assert depth == 1 and d == D_MODEL and n_s == LANES

    w_in_t = jnp.transpose(w_in[0])
    w_in_p = _w_in_layout(w_in_t, tn=Tiles.w_layout_cols)
    row = lambda a: a.reshape(1, -1).astype(F32)
    w2p = jnp.concatenate([rwkv_w2[0], jnp.zeros((128 - DECAY_LORA, RW_WIDTH), F32)], axis=0)
    a2p = jnp.concatenate([jnp.zeros((DECAY_LORA, RW_WIDTH), F32), rwkv_a2[0]], axis=0)
    g2p = jnp.concatenate([rwkv_g2[0], jnp.zeros((256 - GATE_LORA, RW_WIDTH), F32)], axis=0)
    rw_params = dict(mu=_pad_cols(row(rwkv_mu[0]), SHIFT_PAD), w0=row(rwkv_w0[0]), w2=w2p,
                     a0=row(rwkv_a0[0]), a2=a2p, g2=g2p, kk=row(rwkv_kk[0]), ka=row(rwkv_ka[0]),
                     rk=row(rwkv_rk[0]), lnw=row(rwkv_ln_w[0]), lnb=row(rwkv_ln_b[0]))
    col = lambda a: jnp.broadcast_to(a.reshape(-1, 1).astype(F32), (a.size, LANES))
    cols = jnp.stack([a.reshape(-1).astype(F32) for a in (rwkv_w0[0], rwkv_a0[0], rwkv_kk[0], rwkv_ka[0],
                                                         rwkv_rk[0], rwkv_ln_w[0], rwkv_ln_b[0])])
    rw_params_t = dict(cols=jnp.broadcast_to(cols[:, :, None], cols.shape + (LANES,)),
                       w2t=rwkv_w2[0].T, a2t=rwkv_a2[0].T, g2t=rwkv_g2[0].T)

    x_p = x_prompt.reshape(n_p, d)
    x_s = x_sample.reshape(n_d, d)
    meta = meta_tokens.astype(F32)
    x_sm = jnp.concatenate([x_s, meta], axis=0)
    x_tsm = jnp.concatenate([jnp.transpose(x_sample, (1, 0, 2)).reshape(n_d, d), meta,
                             jnp.zeros((LANES - N_META, d), F32)], axis=0)
    g_mix = row(norm_mix[0])
    z_p = _in_projection(x_p, g_mix, w_in_p, tm=Tiles.inproj_rows, tn=Tiles.inproj_cols)
    z_sm = _in_projection(x_sm, g_mix, w_in_p, tm=x_sm.shape[0], tn=Tiles.inproj_small_cols,
                          qk_interleaved=True, n_cols=RW_OFF)
    z_st = _in_projection_t(w_in_t, x_tsm, g_mix, SHIFT_PAD, tm=Tiles.inproj_t_rows)

    z_meta = jnp.pad(jnp.concatenate([z_sm[n_d:], jnp.transpose(z_st[:, n_d:n_d + N_META])], axis=1),
                     ((RET_CHUNK - N_META, 0), (0, 0)))
    srow = 2 * dec_seq

    zero_prev = jnp.zeros((1, 1, SHIFT_PAD), F32)
    zero_rw = jnp.zeros((1, RW_HEADS, RW_HEAD, RW_HEAD), F32)
    c_rw = Tiles.rwkv_chunk
    _, s_rw_meta, zlast_meta = _rwkv_mixer(z_meta, 1, 1, 1, zero_prev, zero_rw, rw_params, tb=c_rw, c=c_rw)
    oa_p, rwkv_p, zlast_p, w_out_b, w_gate_b, w_up_b, w_down_b = _rwkv_mixer(
        z_p, 0, n_b, seq // Tiles.rwkv_rows, zlast_meta, s_rw_meta, rw_params, tb=Tiles.rwkv_rows, c=c_rw,
        cast=(w_out[0], w_gate[0], w_up[0], w_down[0]))
    shift_t = _pad_rows(jnp.transpose(state_shift[0]), SHIFT_PAD)
    mu_t = _pad_rows(col(rwkv_mu[0]), SHIFT_PAD)
    state_t = jnp.transpose(state_rwkv[0], (1, 2, 3, 0))
    oa_st, rwkv_st = _rwkv_sample_mixer(z_st, shift_t, mu_t, state_t, rw_params_t, n_tok=dec_seq,
                                        hpb=Tiles.rwkv_sample_heads)
    oa_s = jnp.transpose(oa_st, (1, 0, 2)).reshape(n_d, RW_WIDTH)
    rwkv_s = jnp.transpose(rwkv_st, (3, 0, 1, 2))

    tabs_full = _retention_tables(RET_CHUNK, RET_CHUNK)
    tabs_meta = _retention_tables(RET_CHUNK, N_META)
    tabs_smp = _retention_tables(srow, dec_seq)
    sb_rt = Tiles.ret_sample_seqs
    cos_p, sin_p = _rotary_tables(N_META + np.arange(seq), interleaved=False)
    pos_ms = np.concatenate([np.arange(RET_CHUNK) - (RET_CHUNK - N_META),
                             PAST_LEN + np.tile(np.arange(srow) - (srow - dec_seq), sb_rt)])
    cos_ms, sin_ms = _rotary_tables(pos_ms)
    zero_rt = jnp.zeros((1, RET_HEADS, RET_HEAD, RET_HEAD), F32)
    _, s_rt_meta = _retention_mixer(z_meta, 0, 1, 1, cos_ms, sin_ms, tabs_meta, zero_rt,
                                    rows=RET_CHUNK, c=RET_CHUNK, n_real=N_META, per_chunk_state=False,
                                    pos_per_tile=(0, False))
    rows_rt = Tiles.ret_chunks_per_step * RET_CHUNK
    ob_p, ret_p = _retention_mixer(z_p, 0, n_b, seq // rows_rt, cos_p, sin_p, tabs_full, s_rt_meta,
                                   rows=rows_rt, c=RET_CHUNK, n_real=RET_CHUNK, per_chunk_state=False,
                                   pos_per_tile=(0, True), pair_split=True)
    ob_s, ret_s = _retention_mixer(z_sm, 0, n_s // sb_rt, 1, cos_ms, sin_ms, tabs_smp,
                                   state_ret.reshape(n_s, RET_HEADS, RET_HEAD, RET_HEAD),
                                   rows=sb_rt * srow, c=srow, n_real=dec_seq, per_chunk_state=True,
                                   pos_per_tile=(RET_CHUNK, False))

    g_ffn = row(norm_ffn[0])
    g_fin = row(norm_final)
    h_p, hn_p = _out_projection(oa_p, ob_p, w_out_b, x_p, g_ffn, tm=Tiles.outproj_rows)
    h_s, hn_s = _out_projection(oa_s, ob_s, w_out_b, x_s, g_ffn, tm=n_d)
    y_p = _ffn(hn_p, w_gate_b, w_up_b, w_down_b, h_p, g_fin, tm=Tiles.ffn_rows, tf=Tiles.ffn_cols)
    y_s = _ffn(hn_s, w_gate_b, w_up_b, w_down_b, h_s, g_fin, tm=n_d, tf=Tiles.ffn_cols)

    y_prompt = y_p.reshape(n_b, seq, d)
    y_sample = y_s.reshape(n_s, dec_seq, d)
    shift_p = zlast_p[:, 0, :RW_COLS][None]
    shift_s = jnp.transpose(z_st[:RW_COLS, (dec_seq - 1) * n_s:n_d])[None]
    return (y_prompt, y_sample, shift_p, rwkv_p[None], ret_p[None], shift_s, rwkv_s[None], ret_s[None])
```

```python
import functools
import math

import jax
import jax.numpy as jnp
import numpy as np
from jax import lax
from jax.experimental import pallas as pl
from jax.experimental.pallas import tpu as pltpu

F32 = jnp.float32
BF16 = jnp.bfloat16

D_MODEL = 2048
N_META = 16
RW_WIDTH = 1024
RW_HEAD = 64
RW_HEADS = 16
DECAY_LORA = 64
AAA_LORA = 64
GATE_LORA = 160
RW_COLS = 3 * RW_WIDTH + DECAY_LORA + AAA_LORA + GATE_LORA
RET_WIDTH = 1024
RET_HEADS = 4
RET_HEAD = 256
RET_CHUNK = 128
D_FF = 5632
RMS_EPS = 1e-6
RW_GN_EPS = 64e-5
RET_GN_EPS = 1e-6
ROPE_BASE = 10000.0
LANES = 128

LORA_PAD = 512
Z_COLS = 4 * RET_WIDTH + 3 * RW_WIDTH + LORA_PAD
RW_OFF = 4 * RET_WIDTH
SHIFT_PAD = 3 * RW_WIDTH + LORA_PAD

VMEM_LIMIT = 60 * 1024 * 1024
PAST_LEN = 16384


class Tiles:
    w_layout_cols = 512
    inproj_rows, inproj_cols = 1024, 1536
    inproj_small_cols = 512
    inproj_t_rows = 896
    rwkv_rows, rwkv_chunk = 256, 64
    rwkv_sample_heads = 2
    ret_chunks_per_step = 4
    ret_sample_seqs = 8
    outproj_rows = 512
    ffn_rows, ffn_cols = 1024, 512


def _cparams(sem):
    return pltpu.CompilerParams(dimension_semantics=sem, vmem_limit_bytes=VMEM_LIMIT)


def _mm(a, b):
    return jnp.dot(a.astype(BF16), b.astype(BF16), preferred_element_type=F32)


def _mm_nt(a, b):
    return lax.dot_general(a.astype(BF16), b.astype(BF16), (((1,), (1,)), ((), ())),
                           preferred_element_type=F32)


def _mm_tn(a, b):
    return lax.dot_general(a.astype(BF16), b.astype(BF16), (((0,), (0,)), ((), ())),
                           preferred_element_type=F32)


def _bmm(a, b):
    return jnp.einsum("bmk,bkn->bmn", a.astype(BF16), b.astype(BF16), preferred_element_type=F32)


def _bmm_nt(a, b):
    return jnp.einsum("bmk,bnk->bmn", a.astype(BF16), b.astype(BF16), preferred_element_type=F32)


def _bmm_tn(a, b):
    return jnp.einsum("bkm,bkn->bmn", a.astype(BF16), b.astype(BF16), preferred_element_type=F32)


def _mm_exact_lhs(m_bf16, x):
    hi = x.astype(BF16)
    r1 = x - hi.astype(F32)
    mid = r1.astype(BF16)
    lo = (r1 - mid.astype(F32)).astype(BF16)
    d = functools.partial(jnp.dot, preferred_element_type=F32)
    return d(m_bf16, hi) + d(m_bf16, mid) + d(m_bf16, lo)


def _mm_exact_rhs(x, m_bf16):
    hi = x.astype(BF16)
    r1 = x - hi.astype(F32)
    mid = r1.astype(BF16)
    lo = (r1 - mid.astype(F32)).astype(BF16)
    d = functools.partial(jnp.dot, preferred_element_type=F32)
    return d(hi, m_bf16) + d(mid, m_bf16) + d(lo, m_bf16)


def _mm_exact_lhs_t(m_bf16, x):
    hi = x.astype(BF16)
    r1 = x - hi.astype(F32)
    mid = r1.astype(BF16)
    lo = (r1 - mid.astype(F32)).astype(BF16)
    d = functools.partial(lax.dot_general, dimension_numbers=(((0,), (0,)), ((), ())),
                          preferred_element_type=F32)
    return d(m_bf16, hi) + d(m_bf16, mid) + d(m_bf16, lo)


def _rms_norm_bf16(x, g):
    ms = jnp.mean(x * x, axis=-1, keepdims=True)
    return ((x * lax.rsqrt(ms + RMS_EPS)) * g).astype(BF16)


def _pair_split_index(n):
    j = np.arange(n)
    half = RET_HEAD // 2
    return (j // RET_HEAD) * RET_HEAD + 2 * (j % half) + (j % RET_HEAD) // half


def _w_in_layout_kernel(w_ref, sel_ref, o_ref, *, n_qk):
    c = pl.program_id(0)

    @pl.when(c >= n_qk)
    def _():
        o_ref[...] = w_ref[...].T.astype(BF16)

    @pl.when(c < n_qk)
    def _():
        picked = jnp.dot(sel_ref[...], w_ref[...].astype(BF16), preferred_element_type=F32)
        o_ref[...] = picked.T.astype(BF16)


def _w_in_layout(w_in_t, tn):
    n, d = w_in_t.shape
    n_ret = 4 * RET_WIDTH // tn
    n_qk = 2 * RET_WIDTH // tn
    sel = jnp.asarray(_pair_split_index(tn)[:, None] == np.arange(tn)[None, :], dtype=BF16)

    def src_row(c):
        align = math.gcd(RW_COLS, tn)
        return (pl.multiple_of(jnp.where(c < n_ret, RW_COLS + tn * c, tn * (c - n_ret)), align), 0)

    return pl.pallas_call(
        functools.partial(_w_in_layout_kernel, n_qk=n_qk),
        grid=(Z_COLS // tn,),
        in_specs=[pl.BlockSpec((pl.Element(tn), pl.Element(d)), src_row),
                  pl.BlockSpec((tn, tn), lambda c: (0, 0))],
        out_specs=pl.BlockSpec((d, tn), lambda c: (0, c)),
        out_shape=jax.ShapeDtypeStruct((d, Z_COLS), BF16),
        compiler_params=_cparams(("parallel",)),
        name="w_in_layout",
    )(w_in_t, sel)


def _inproj_kernel(x_ref, g_ref, w_ref, sel_ref, o_ref, xn_ref, *, n_qk_interleave):
    j = pl.program_id(1)

    @pl.when(j == 0)
    def _():
        xn_ref[...] = _rms_norm_bf16(x_ref[...], g_ref[...])

    @pl.when(j >= n_qk_interleave)
    def _():
        o_ref[...] = jnp.dot(xn_ref[...], w_ref[...], preferred_element_type=F32)

    if n_qk_interleave:
        @pl.when(j < n_qk_interleave)
        def _():
            z = jnp.dot(xn_ref[...], w_ref[...], preferred_element_type=F32)
            o_ref[...] = _mm_exact_rhs(z, sel_ref[...])


def _in_projection(x2d, norm_g, w_in_p, tm, tn, qk_interleaved=False, n_cols=Z_COLS):
    m = x2d.shape[0]
    n_qk = 2 * RET_WIDTH // tn if qk_interleaved else 0
    assert not qk_interleaved or tn % RET_HEAD == 0
    ts = tn if qk_interleaved else LANES
    sel = jnp.asarray(_pair_split_index(ts)[:, None] == np.arange(ts)[None, :], dtype=BF16)
    return pl.pallas_call(
        functools.partial(_inproj_kernel, n_qk_interleave=n_qk),
        grid=(m // tm, n_cols // tn),
        in_specs=[
            pl.BlockSpec((tm, D_MODEL), lambda i, j: (i, 0)),
            pl.BlockSpec((1, D_MODEL), lambda i, j: (0, 0)),
            pl.BlockSpec((D_MODEL, tn), lambda i, j: (0, j)),
            pl.BlockSpec((ts, ts), lambda i, j: (0, 0)),
        ],
        out_specs=pl.BlockSpec((tm, tn), lambda i, j: (i, j)),
        out_shape=jax.ShapeDtypeStruct((m, n_cols), F32),
        scratch_shapes=[pltpu.VMEM((tm, D_MODEL), BF16)],
        compiler_params=_cparams(("parallel", "arbitrary")),
        name="in_projection",
    )(x2d, norm_g, w_in_p, sel)


def _inproj_t_kernel(w_ref, x_ref, g_ref, o_ref, xn_ref):
    @pl.when(pl.program_id(0) == 0)
    def _():
        xn_ref[...] = _rms_norm_bf16(x_ref[...], g_ref[...])

    o_ref[...] = _mm_nt(w_ref[...], xn_ref[...])


def _in_projection_t(w_in_t, x2d, norm_g, n_rows, tm):
    m, d = x2d.shape
    return pl.pallas_call(
        _inproj_t_kernel,
        grid=(n_rows // tm,),
        in_specs=[
            pl.BlockSpec((tm, d), lambda i: (i, 0)),
            pl.BlockSpec((m, d), lambda i: (0, 0)),
            pl.BlockSpec((1, d), lambda i: (0, 0)),
        ],
        out_specs=pl.BlockSpec((tm, m), lambda i: (i, 0)),
        out_shape=jax.ShapeDtypeStruct((n_rows, m), F32),
        scratch_shapes=[pltpu.VMEM((m, d), BF16)],
        compiler_params=_cparams(("arbitrary",)),
        name="in_projection_t",
    )(w_in_t, x2d, norm_g)


def _rwkv_kernel(zr_ref, zk_ref, zv_ref, zl_ref, zprev0_ref, s0_ref,
                 mu_ref, w0_ref, w2_ref, a0_ref, a2_ref, g2_ref, kk_ref, ka_ref, rk_ref,
                 lnw_ref, lnb_ref,
                 o_ref, sout_ref, zlast_ref,
                 carry_ref, sbd_s, r_s, km_s, v_s, kk_s, a_s, lw_s, cum_s, o_s,
                 *, tb, c, n_factors):
    i = pl.program_id(1)
    n_chunks = tb // c
    n_pairs = RW_HEADS // 2
    pw = 2 * RW_HEAD

    @pl.when(i == 0)
    def _():
        carry_ref[...] = zprev0_ref[0]
        zero = jnp.zeros((RW_HEAD, RW_HEAD), F32)
        for p in range(n_pairs):
            top = jnp.concatenate([s0_ref[0, 2 * p], zero], axis=1)
            bot = jnp.concatenate([zero, s0_ref[0, 2 * p + 1]], axis=1)
            sbd_s[p] = jnp.concatenate([top, bot], axis=0)

    row8 = lax.broadcasted_iota(jnp.int32, (8, 1), 0)
    w = RW_WIDTH

    def token_parallel(ch):
        rows = pl.ds(ch * c, c)

        def shifted(z_ref, lo, hi):
            z = z_ref[rows, :]
            first_prev = carry_ref[:, lo:hi] if ch == 0 else z_ref[ch * c - 1:ch * c, :]
            prev = pltpu.roll(z, 1, axis=0)
            prev = jnp.concatenate([jnp.where(row8 == 0, first_prev, prev[:8]), prev[8:]], axis=0)
            return z + mu_ref[:, lo:hi] * (prev - z)

        r = shifted(zr_ref, 0, w)
        k = shifted(zk_ref, w, 2 * w)
        v = shifted(zv_ref, 2 * w, 3 * w)
        zl = shifted(zl_ref, 3 * w, 3 * w + LORA_PAD)
        lo2 = zl[:, 0:128]
        wlog = -jax.nn.softplus(-(w0_ref[...] + _mm(jnp.tanh(lo2), w2_ref[...]))) - 0.5
        lw = -jnp.exp(wlog)
        a = jax.nn.sigmoid(a0_ref[...] + _mm(lo2, a2_ref[...]))
        g = _mm(jax.nn.sigmoid(zl[:, 128:384]), g2_ref[...])
        kk = k * kk_ref[...]
        km = k * (1.0 + (a - 1.0) * ka_ref[...])
        for dst, val in ((r_s, r), (km_s, km), (v_s, v), (a_s, a), (lw_s, lw), (kk_s, kk)):
            dst[rows, :] = val
        return g

    rr = lax.broadcasted_iota(jnp.int32, (c, c), 0)
    cc = lax.broadcasted_iota(jnp.int32, (c, c), 1)
    tri = jnp.where(rr >= cc, 1.0, 0.0).astype(BF16)
    rr2 = lax.broadcasted_iota(jnp.int32, (c, 2 * c), 0)
    cc2 = lax.broadcasted_iota(jnp.int32, (c, 2 * c), 1)
    strict = (rr > cc)[None]
    eye = jnp.where(rr == cc, 1.0, 0.0).astype(F32)[None]
    strict_k = ((cc2 >= c) & (rr2 > cc2 - c))[None]
    incl_bk = (rr2 >= cc2 % c)[None]
    lo = (lax.broadcasted_iota(jnp.int32, (1, 1, pw), 2) < RW_HEAD)
    bd_r = lax.broadcasted_iota(jnp.int32, (pw, pw), 0) // RW_HEAD
    bd_c = lax.broadcasted_iota(jnp.int32, (pw, pw), 1) // RW_HEAD
    block_diag = (bd_r == bd_c)[None]
    rk = rk_ref[...]
    lnw = lnw_ref[...]
    lnb = lnb_ref[...]

    def head_sum(x):
        s_lo = jnp.sum(jnp.where(lo, x, 0.0), axis=-1, keepdims=True)
        s_hi = jnp.sum(jnp.where(lo, 0.0, x), axis=-1, keepdims=True)
        return jnp.where(lo, s_lo, s_hi)

    def pick(x16):
        x4 = x16.reshape(n_pairs, 2, c, pw)
        return jnp.where(lo, x4[:, 0], x4[:, 1])

    def body(ch, s_prev):
        rows = pl.ds(ch * c, c)

        def pairs(s):
            x = s[rows, :]
            return jnp.stack([x[:, p * pw:(p + 1) * pw] for p in range(n_pairs)], axis=0)

        cum_s[rows, :] = _mm_exact_lhs(tri, lw_s[rows, :])
        rh, kmh, vh, ah, lwh, cumh, kkh = [pairs(s) for s in (r_s, km_s, v_s, a_s, lw_s, cum_s, kk_s)]
        kkn = kkh * lax.rsqrt(jnp.maximum(head_sum(kkh * kkh), 1e-24))
        bvec = kkn * ah
        cum_end = cumh[:, c - 1:c, :]
        p_inc = jnp.exp(cumh)
        p_exc = jnp.exp(cumh - lwh)
        p_inv = jnp.exp(-cumh)
        p_end = jnp.exp(cum_end)
        p_rel = jnp.exp(cum_end - cumh)
        a_t = -(kkn * p_exc)
        r_t = rh * p_inc
        b_t = bvec * p_inv
        k_t = kmh * p_inv
        b_h = bvec * p_rel
        k_h = kmh * p_rel

        ar = jnp.concatenate([jnp.where(lo, a_t, 0.0), jnp.where(lo, r_t, 0.0),
                              jnp.where(lo, 0.0, a_t), jnp.where(lo, 0.0, r_t)], axis=1)
        prod = _bmm_nt(ar, jnp.concatenate([b_t, k_t, s_prev], axis=1))
        prod = prod.reshape(RW_HEADS, 2 * c, 2 * c + pw)
        a_rows, r_rows = prod[:, :c], prod[:, c:]
        aak_wide = jnp.where(strict_k, a_rows[:, :, :2 * c], 0.0)
        arbk = jnp.where(incl_bk, r_rows[:, :, :2 * c], 0.0)
        lmat = jnp.where(strict, a_rows[:, :, :c], 0.0)
        tinv = eye + lmat
        if n_factors > 1:
            lp = _bmm(lmat, lmat)
            for _ in range(n_factors - 2):
                st = _bmm(jnp.concatenate([lp, tinv], axis=1), lp)
                lp = st[:, :c]
                tinv = tinv + st[:, c:]
            tinv = tinv + _bmm(tinv, lp)
        v16 = jnp.repeat(vh, 2, axis=0)
        x = a_rows[:, :, 2 * c:] + _bmm(aak_wide, jnp.concatenate([v16, v16], axis=1))
        u16 = _bmm(tinv, x)
        o16 = r_rows[:, :, 2 * c:] + _bmm(arbk, jnp.concatenate([u16, v16], axis=1))
        u = pick(u16)
        o = pick(o16)
        s_upd = _bmm_tn(jnp.concatenate([u, vh], axis=1), jnp.concatenate([b_h, k_h], axis=1))
        s_new = s_prev * p_end + jnp.where(block_diag, s_upd, 0.0)
        inv_n = 1.0 / RW_HEAD
        mean = head_sum(o) * inv_n
        var = head_sum(jnp.square(o - mean)) * inv_n
        on = ((o - mean) * lax.rsqrt(var + RW_GN_EPS)) * lnw + lnb
        out = on + head_sum(rh * kmh * rk) * vh
        for p in range(n_pairs):
            o_s[rows, p * pw:(p + 1) * pw] = out[p]
        return s_new

    state = sbd_s[...]
    for ch in range(n_chunks):
        g = token_parallel(ch)
        state = body(ch, state)
        rows = pl.ds(ch * c, c)
        o_ref[rows, :] = (o_s[rows, :] * g).astype(BF16)
    sbd_s[...] = state

    for z_ref, lo in ((zr_ref, 0), (zk_ref, w), (zv_ref, 2 * w), (zl_ref, 3 * w)):
        carry_ref[:, lo:lo + z_ref.shape[1]] = z_ref[tb - 1:tb, :]
    zlast_ref[0] = carry_ref[...]

    @pl.when(i == pl.num_programs(1) - 1)
    def _():
        for p in range(n_pairs):
            blk = sbd_s[p]
            sout_ref[0, 2 * p] = blk[:RW_HEAD, :RW_HEAD]
            sout_ref[0, 2 * p + 1] = blk[RW_HEAD:, RW_HEAD:]


N_RWKV_INPUTS = 17
N_RWKV_OUTPUTS = 3


def _rwkv_and_cast_kernel(*refs, n_cast, **kw):
    ins = refs[:N_RWKV_INPUTS]
    cast_in = refs[N_RWKV_INPUTS:N_RWKV_INPUTS + n_cast]
    o0 = N_RWKV_INPUTS + n_cast
    outs = refs[o0:o0 + N_RWKV_OUTPUTS]
    cast_out = refs[o0 + N_RWKV_OUTPUTS:o0 + N_RWKV_OUTPUTS + n_cast]
    scratch = refs[o0 + N_RWKV_OUTPUTS + n_cast:]
    for src, dst in zip(cast_in, cast_out):
        dst[...] = src[...].astype(BF16)
    _rwkv_kernel(*ins, *outs, *scratch, **kw)


def _rwkv_mixer(z, row_block0, n_groups, n_tiles, zprev0, s0, p, *, tb, c, cast=()):
    n_factors = max(1, math.ceil(math.log2(c)))
    kern = functools.partial(_rwkv_and_cast_kernel, n_cast=len(cast), tb=tb, c=c, n_factors=n_factors)
    n_steps = n_groups * n_tiles
    cast_specs = [pl.BlockSpec((a.shape[0] // n_steps, a.shape[1]), lambda gi, i: (gi * n_tiles + i, 0))
                  for a in cast]
    cast_shapes = [jax.ShapeDtypeStruct(a.shape, BF16) for a in cast]
    cb = RW_OFF // RW_WIDTH
    full = lambda arr: pl.BlockSpec(arr.shape, lambda gi, i: (0,) * arr.ndim)
    lora_cb = (RW_OFF + 3 * RW_WIDTH) // LORA_PAD
    per_pair = lambda a: a.reshape(RW_HEADS // 2, 1, 2 * RW_HEAD)
    params = [p["mu"], p["w0"], p["w2"], p["a0"], p["a2"], p["g2"], p["kk"], p["ka"],
              per_pair(p["rk"]), per_pair(p["lnw"]), per_pair(p["lnb"])]
    dense = lambda: pltpu.VMEM((tb, RW_WIDTH), F32)
    zspec = lambda width, col: pl.BlockSpec((tb, width), lambda gi, i: (row_block0 + gi * n_tiles + i, col))
    sspec = pl.BlockSpec((1, RW_HEADS, RW_HEAD, RW_HEAD), lambda gi, i: (gi, 0, 0, 0))
    shared = lambda a: (lambda gi, i: (0,) * a.ndim) if a.shape[0] == 1 else (lambda gi, i: (gi,) + (0,) * (a.ndim - 1))
    return pl.pallas_call(
        kern,
        grid=(n_groups, n_tiles),
        in_specs=[zspec(RW_WIDTH, cb), zspec(RW_WIDTH, cb + 1), zspec(RW_WIDTH, cb + 2),
                  zspec(LORA_PAD, lora_cb),
                  pl.BlockSpec((1, 1, SHIFT_PAD), shared(zprev0)),
                  pl.BlockSpec((1, RW_HEADS, RW_HEAD, RW_HEAD), shared(s0))]
                 + [full(a) for a in params] + cast_specs,
        out_specs=[pl.BlockSpec((tb, RW_WIDTH), lambda gi, i: (gi * n_tiles + i, 0)), sspec,
                   pl.BlockSpec((1, 1, SHIFT_PAD), lambda gi, i: (gi, 0, 0))] + cast_specs,
        out_shape=[jax.ShapeDtypeStruct((n_groups * n_tiles * tb, RW_WIDTH), BF16),
                   jax.ShapeDtypeStruct((n_groups, RW_HEADS, RW_HEAD, RW_HEAD), F32),
                   jax.ShapeDtypeStruct((n_groups, 1, SHIFT_PAD), F32)] + cast_shapes,
        scratch_shapes=[pltpu.VMEM((1, SHIFT_PAD), F32),
                        pltpu.VMEM((RW_HEADS // 2, 2 * RW_HEAD, 2 * RW_HEAD), F32)]
                       + [dense() for _ in range(8)],
        compiler_params=_cparams(("parallel", "arbitrary")),
        name="rwkv7_mixer",
    )(z, z, z, z, zprev0, s0, *params, *cast)


def _rwkv_sample_kernel(zr_ref, zk_ref, zv_ref, zl_ref, shr_ref, shk_ref, shv_ref, shl_ref,
                        mur_ref, muk_ref, muv_ref, mul_ref, s0_ref,
                        pc_ref, w2t_ref, a2t_ref, g2t_ref,
                        o_ref, sout_ref,
                        dec_s, a_s, b_s, k_s, r_s, v_s, o_s,
                        *, n_tok, hpb):
    w0_ref, a0_ref, kk_ref, ka_ref, rk_ref, lnw_ref, lnb_ref = [pc_ref.at[j] for j in range(7)]
    ns = LANES
    ch = hpb * RW_HEAD

    def lanes(x):
        return jnp.concatenate([x] * n_tok, axis=1)

    def shifted(z_ref, sh_ref, mu_ref):
        z = z_ref[...]
        prev = jnp.concatenate([sh_ref[...], z[:, :(n_tok - 1) * ns]], axis=1)
        return z + lanes(mu_ref[...]) * (prev - z)

    r = shifted(zr_ref, shr_ref, mur_ref)
    k = shifted(zk_ref, shk_ref, muk_ref)
    v = shifted(zv_ref, shv_ref, muv_ref)
    zl = shifted(zl_ref, shl_ref, mul_ref)
    wd = zl[0:DECAY_LORA]
    ad = zl[DECAY_LORA:DECAY_LORA + AAA_LORA]
    gd = zl[DECAY_LORA + AAA_LORA:DECAY_LORA + AAA_LORA + GATE_LORA]

    wlog = -jax.nn.softplus(-(lanes(w0_ref[...]) + _mm(w2t_ref[...], jnp.tanh(wd)))) - 0.5
    lw = -jnp.exp(wlog)
    a = jax.nn.sigmoid(lanes(a0_ref[...]) + _mm(a2t_ref[...], ad))
    g = _mm(g2t_ref[...], jax.nn.sigmoid(gd))
    kk = k * lanes(kk_ref[...])
    km = k * (1.0 + (a - 1.0) * lanes(ka_ref[...]))

    def head_sum(x):
        x3 = x.reshape(hpb, RW_HEAD, n_tok * ns)
        s = jnp.sum(x3, axis=1, keepdims=True)
        return jnp.broadcast_to(s, x3.shape).reshape(ch, n_tok * ns)

    kkn = kk * lax.rsqrt(jnp.maximum(head_sum(kk * kk), 1e-24))
    dec_s[...] = jnp.exp(lw)
    a_s[...] = -kkn
    b_s[...] = kkn * a
    k_s[...] = km
    r_s[...] = r
    v_s[...] = v

    for hh in range(hpb):
        hrows = slice(hh * RW_HEAD, (hh + 1) * RW_HEAD)

        def body(i8, carry, hh=hh, hrows=hrows):
            base = pl.multiple_of(i8 * 8, 8)
            rows8 = pl.ds(pl.multiple_of(hh * RW_HEAD + base, 8), 8)
            o_rows = [[] for _ in range(n_tok)]
            for j in range(8):
                s = s0_ref[hh, base + j]
                for t in range(n_tok):
                    tl = slice(t * ns, (t + 1) * ns)
                    sa = jnp.sum(s * a_s[hrows, tl], axis=0, keepdims=True)
                    v_row = v_s[rows8, tl][j:j + 1, :]
                    s = s * dec_s[hrows, tl] + sa * b_s[hrows, tl] + v_row * k_s[hrows, tl]
                    o_rows[t].append(jnp.sum(s * r_s[hrows, tl], axis=0, keepdims=True))
                sout_ref[hh, base + j] = s
            for t in range(n_tok):
                o_s[rows8, t * ns:(t + 1) * ns] = jnp.concatenate(o_rows[t], axis=0)
            return carry

        lax.fori_loop(0, RW_HEAD // 8, body, 0)

    o = o_s[...]
    inv_n = 1.0 / RW_HEAD
    mean = head_sum(o) * inv_n
    var = head_sum(jnp.square(o - mean)) * inv_n
    on = ((o - mean) * lax.rsqrt(var + RW_GN_EPS)) * lanes(lnw_ref[...]) + lanes(lnb_ref[...])
    bonus = head_sum(r * km * lanes(rk_ref[...])) * v
    out = (on + bonus) * g
    for t in range(n_tok):
        o_ref[t] = out[:, t * ns:(t + 1) * ns].T.astype(BF16)


def _rwkv_sample_mixer(z_t, shift_t, mu_t, state_t, p_t, *, n_tok, hpb):
    ch = hpb * RW_HEAD
    n_steps = RW_HEADS // hpb
    nl = n_tok * LANES
    seg = RW_WIDTH // ch
    zspec = lambda s: pl.BlockSpec((ch, nl), lambda h: (s * seg + h, 0))
    cspec = lambda s: pl.BlockSpec((ch, LANES), lambda h: (s * seg + h, 0))
    lora_blk = 3 * RW_WIDTH // LORA_PAD
    zl_spec = pl.BlockSpec((LORA_PAD, nl), lambda h: (lora_blk, 0))
    cl_spec = pl.BlockSpec((LORA_PAD, LANES), lambda h: (lora_blk, 0))
    pc_spec = pl.BlockSpec((p_t["cols"].shape[0], ch, LANES), lambda h: (0, h, 0))
    wspec = lambda k: pl.BlockSpec((ch, k), lambda h: (h, 0))
    sspec = pl.BlockSpec((hpb, RW_HEAD, RW_HEAD, LANES), lambda h: (h, 0, 0, 0))
    buf = lambda: pltpu.VMEM((ch, nl), F32)
    return pl.pallas_call(
        functools.partial(_rwkv_sample_kernel, n_tok=n_tok, hpb=hpb),
        grid=(n_steps,),
        in_specs=[zspec(0), zspec(1), zspec(2), zl_spec,
                  cspec(0), cspec(1), cspec(2), cl_spec,
                  cspec(0), cspec(1), cspec(2), cl_spec,
                  sspec,
                  pc_spec, wspec(DECAY_LORA), wspec(AAA_LORA), wspec(GATE_LORA)],
        out_specs=[pl.BlockSpec((n_tok, LANES, ch), lambda h: (0, 0, h)), sspec],
        out_shape=[jax.ShapeDtypeStruct((n_tok, LANES, RW_WIDTH), BF16),
                   jax.ShapeDtypeStruct(state_t.shape, F32)],
        scratch_shapes=[buf() for _ in range(7)],
        compiler_params=_cparams(("parallel",)),
        name="rwkv7_sample",
    )(z_t, z_t, z_t, z_t, shift_t, shift_t, shift_t, shift_t, mu_t, mu_t, mu_t, mu_t, state_t,
      p_t["cols"], p_t["w2t"], p_t["a2t"], p_t["g2t"])


def _retention_kernel(zq_ref, zk_ref, zv_ref, zg_ref, cos_ref, sin_ref, dmask_ref, iscale_ref,
                      kscale_ref, sdec_ref, sel_ref, s0_ref, o_ref, sout_ref, st_s,
                      *, rows, c, n_real, per_chunk_state, pair_split):
    i = pl.program_id(1)
    n_chunks = rows // c
    assert not (pair_split and per_chunk_state)

    if per_chunk_state:
        pr = lax.broadcasted_iota(jnp.int32, (rows, n_chunks * n_real), 0)
        pc = lax.broadcasted_iota(jnp.int32, (rows, n_chunks * n_real), 1)
        place = jnp.where((pr // c == pc // n_real) & (pr % c - (c - n_real) == pc % n_real),
                          1.0, 0.0).astype(BF16)
        load = lambda ref: _mm_exact_lhs(place, ref[...])
    else:
        load = lambda ref: ref[...]

        @pl.when(i == 0)
        def _():
            if pair_split:
                for h in range(RET_HEADS):
                    st_s[h] = _mm_exact_lhs(sel_ref[...], s0_ref[0, h])
            else:
                st_s[...] = s0_ref[0]

    if pair_split:
        cos = cos_ref[...]
        sin = sin_ref[...]
        half = RET_HEAD // 2

        def rot(x):
            parts = []
            for h in range(RET_HEADS):
                x0 = x[:, h * RET_HEAD:h * RET_HEAD + half]
                x1 = x[:, h * RET_HEAD + half:(h + 1) * RET_HEAD]
                parts += [x0 * cos - x1 * sin, x0 * sin + x1 * cos]
            return jnp.concatenate(parts, axis=-1)
    else:
        lane = lax.broadcasted_iota(jnp.int32, (rows, RET_WIDTH), 1)
        even = (lane % 2) == 0
        cos = jnp.concatenate([cos_ref[...]] * RET_HEADS, axis=-1)
        sin = jnp.concatenate([sin_ref[...]] * RET_HEADS, axis=-1)

        def rot(x):
            partner = jnp.where(even, pltpu.roll(x, RET_WIDTH - 1, axis=1), pltpu.roll(x, 1, axis=1))
            return x * cos + partner * sin

    q = rot(load(zq_ref))
    k = rot(load(zk_ref)) * (RET_HEAD ** -0.5)
    v = load(zv_ref)
    g = load(zg_ref)

    out_rows = []
    for ch in range(n_chunks):
        rs = slice(ch * c, (ch + 1) * c)
        out_heads = []
        for h in range(RET_HEADS):
            hs = slice(h * RET_HEAD, (h + 1) * RET_HEAD)
            qh, kh, vh = q[rs, hs], k[rs, hs], v[rs, hs]
            s_prev = s0_ref[ch, h] if per_chunk_state else st_s[h]
            scores = _mm_nt(qh, kh) * dmask_ref[h]
            o = _mm(scores, vh) + _mm(qh, s_prev) * iscale_ref[h]
            s_new = s_prev * sdec_ref[h] + _mm_tn(kh * kscale_ref[h], vh)
            if per_chunk_state:
                sout_ref[ch, h] = s_new
            else:
                st_s[h] = s_new
            o = o * lax.rsqrt(jnp.mean(o * o, axis=-1, keepdims=True) + RET_GN_EPS)
            gh = g[rs, hs]
            out_heads.append(o * (gh * jax.nn.sigmoid(gh)))
        out_rows.append(jnp.concatenate(out_heads, axis=-1))
    o_out = jnp.concatenate(out_rows, axis=0).astype(BF16)
    if per_chunk_state:
        o_out = lax.dot_general(place, o_out, (((0,), (0,)), ((), ())),
                                preferred_element_type=F32).astype(BF16)
    o_ref[...] = o_out

    if not per_chunk_state:
        @pl.when(i == pl.num_programs(1) - 1)
        def _():
            if pair_split:
                for h in range(RET_HEADS):
                    sout_ref[0, h] = _mm_exact_lhs_t(sel_ref[...], st_s[h])
            else:
                sout_ref[0] = st_s[...]


def _retention_mixer(z, row_block0, n_groups, n_tiles, cos, sin, tabs, s0, *, rows, c, n_real,
                     per_chunk_state, pos_per_tile, pair_split=False):
    shared_s0 = s0.shape[0] == 1
    sb = 1 if shared_s0 else s0.shape[0] // n_groups
    kern = functools.partial(_retention_kernel, rows=rows, c=c, n_real=n_real,
                             per_chunk_state=per_chunk_state, pair_split=pair_split)
    rows_io = rows // c * n_real if per_chunk_state else rows
    zspec = lambda col: pl.BlockSpec((rows_io, RET_WIDTH), lambda gi, i: (row_block0 + gi * n_tiles + i, col))
    full = lambda arr: pl.BlockSpec(arr.shape, lambda gi, i: (0,) * arr.ndim)
    tbl_row0, tbl_advance = pos_per_tile
    assert tbl_row0 % rows == 0
    tspec = pl.BlockSpec((rows, cos.shape[1]),
                         lambda gi, i: (tbl_row0 // rows + (i if tbl_advance else 0), 0))
    state_blk = (sb, RET_HEADS, RET_HEAD, RET_HEAD)
    s0_spec = pl.BlockSpec(state_blk, lambda gi, i: (0 if shared_s0 else gi, 0, 0, 0))
    sout_spec = pl.BlockSpec(state_blk, lambda gi, i: (gi, 0, 0, 0))
    dmask, iscale, kscale, sdec = tabs
    sel = jnp.asarray(_pair_split_index(RET_HEAD)[:, None] == np.arange(RET_HEAD)[None, :], dtype=BF16)
    return pl.pallas_call(
        kern,
        grid=(n_groups, n_tiles),
        in_specs=[zspec(0), zspec(1), zspec(2), zspec(3), tspec, tspec,
                  full(dmask), full(iscale), full(kscale), full(sdec), full(sel), s0_spec],
        out_specs=[pl.BlockSpec((rows_io, RET_WIDTH), lambda gi, i: (gi * n_tiles + i, 0)), sout_spec],
        out_shape=[jax.ShapeDtypeStruct((n_groups * n_tiles * rows_io, RET_WIDTH), BF16),
                   jax.ShapeDtypeStruct((n_groups * sb,) + state_blk[1:], F32)],
        scratch_shapes=[pltpu.VMEM((RET_HEADS, RET_HEAD, RET_HEAD), F32)],
        compiler_params=_cparams(("parallel", "arbitrary")),
        name="retention_mixer",
    )(z, z, z, z, cos, sin, dmask, iscale, kscale, sdec, sel, s0)


def _retention_tables(c, n_real):
    log_gamma = np.log(1.0 - 2.0 ** (-5.0 - np.arange(RET_HEADS, dtype=np.float64)))
    r = np.arange(c, dtype=np.float64)
    idx = r - float(c - n_real)
    diff = r[:, None] - r[None, :]
    dmask = np.where(diff[None] >= 0, np.exp(log_gamma[:, None, None] * np.maximum(diff, 0.0)[None]), 0.0)
    iscale = np.exp(log_gamma[:, None] * (idx + 1.0)[None, :])[:, :, None]
    kscale = np.exp(log_gamma[:, None] * (n_real - 1.0 - idx)[None, :])[:, :, None]
    sdec = np.broadcast_to(np.exp(log_gamma * n_real)[:, None, None], (RET_HEADS, 1, RET_HEAD))
    return tuple(jnp.asarray(t, dtype=F32) for t in (dmask, iscale, kscale, sdec))


def _rotary_tables(pos, interleaved=True):
    inv_freq = 1.0 / (ROPE_BASE ** np.linspace(0.0, 1.0, RET_HEAD // 2))
    ang = np.asarray(pos, dtype=np.float64)[:, None] * inv_freq[None, :]
    cos = np.cos(ang)
    sin = np.sin(ang)
    if not interleaved:
        return jnp.asarray(cos, dtype=F32), jnp.asarray(sin, dtype=F32)
    cos2 = np.repeat(cos, 2, axis=-1)
    sin2 = np.stack([-sin, sin], axis=-1).reshape(len(pos), RET_HEAD)
    return jnp.asarray(cos2, dtype=F32), jnp.asarray(sin2, dtype=F32)


def _outproj_kernel(oa_ref, ob_ref, w_ref, x_ref, g_ref, h_ref, hn_ref):
    o_ab = jnp.concatenate([oa_ref[...], ob_ref[...]], axis=1)
    h = x_ref[...] + jnp.dot(o_ab, w_ref[...], preferred_element_type=F32)
    h_ref[...] = h
    hn_ref[...] = _rms_norm_bf16(h, g_ref[...])


def _out_projection(o_a, o_b, w_out, x2d, norm_g, tm):
    m = o_a.shape[0]
    return pl.pallas_call(
        _outproj_kernel,
        grid=(m // tm,),
        in_specs=[
            pl.BlockSpec((tm, RW_WIDTH), lambda i: (i, 0)),
            pl.BlockSpec((tm, RET_WIDTH), lambda i: (i, 0)),
            pl.BlockSpec((D_MODEL, D_MODEL), lambda i: (0, 0)),
            pl.BlockSpec((tm, D_MODEL), lambda i: (i, 0)),
            pl.BlockSpec((1, D_MODEL), lambda i: (0, 0)),
        ],
        out_specs=[pl.BlockSpec((tm, D_MODEL), lambda i: (i, 0)),
                   pl.BlockSpec((tm, D_MODEL), lambda i: (i, 0))],
        out_shape=[jax.ShapeDtypeStruct((m, D_MODEL), F32), jax.ShapeDtypeStruct((m, D_MODEL), BF16)],
        compiler_params=_cparams(("parallel",)),
        name="out_projection",
    )(o_a, o_b, w_out, x2d, norm_g)


def _ffn_kernel(hn_ref, wg_ref, wu_ref, wd_ref, h_ref, g_ref, y_ref, acc_ref, *, h_slices):
    f = pl.program_id(1)

    @pl.when(f == 0)
    def _():
        acc_ref[...] = jnp.zeros_like(acc_ref)

    @pl.when(f < h_slices)
    def _():
        rows = h_ref.shape[0]
        sl = pl.ds(pl.multiple_of(f * rows, rows), rows)
        acc_ref[sl, :] += h_ref[...]

    hn = hn_ref[...]
    gate = jnp.dot(hn, wg_ref[...], preferred_element_type=F32)
    up = jnp.dot(hn, wu_ref[...], preferred_element_type=F32)
    act = (gate * jax.nn.sigmoid(gate)) * up
    acc_ref[...] += jnp.dot(act.astype(BF16), wd_ref[...], preferred_element_type=F32)

    @pl.when(f == pl.num_programs(1) - 1)
    def _():
        h = acc_ref[...]
        ms = jnp.mean(h * h, axis=-1, keepdims=True)
        y_ref[...] = (h * lax.rsqrt(ms + RMS_EPS)) * g_ref[...]


def _ffn(hn, w_gate, w_up, w_down, h, norm_g, tm, tf, h_slices=8):
    m = hn.shape[0]
    assert D_FF // tf >= h_slices
    return pl.pallas_call(
        functools.partial(_ffn_kernel, h_slices=h_slices),
        grid=(m // tm, D_FF // tf),
        in_specs=[
            pl.BlockSpec((tm, D_MODEL), lambda i, f: (i, 0)),
            pl.BlockSpec((D_MODEL, tf), lambda i, f: (0, f)),
            pl.BlockSpec((D_MODEL, tf), lambda i, f: (0, f)),
            pl.BlockSpec((tf, D_MODEL), lambda i, f: (f, 0)),
            pl.BlockSpec((tm // h_slices, D_MODEL),
                         lambda i, f: (i * h_slices + jnp.minimum(f, h_slices - 1), 0)),
            pl.BlockSpec((1, D_MODEL), lambda i, f: (0, 0)),
        ],
        out_specs=pl.BlockSpec((tm, D_MODEL), lambda i, f: (i, 0)),
        out_shape=jax.ShapeDtypeStruct((m, D_MODEL), F32),
        scratch_shapes=[pltpu.VMEM((tm, D_MODEL), F32)],
        compiler_params=_cparams(("parallel", "arbitrary")),
        name="swiglu_ffn",
    )(hn, w_gate, w_up, w_down, h, norm_g)


def _pad_cols(a, n):
    return jnp.pad(a, ((0, 0), (0, n - a.shape[1])))


def _pad_rows(a, n):
    return jnp.pad(a, ((0, n - a.shape[0]), (0, 0)))


def kernel(x_prompt, x_sample, state_shift, state_rwkv, state_ret, meta_tokens, norm_mix, w_in,
           rwkv_mu, rwkv_w0, rwkv_w2, rwkv_a0, rwkv_a2, rwkv_g2, rwkv_kk, rwkv_ka, rwkv_rk,
           rwkv_ln_w, rwkv_ln_b, w_out, norm_ffn, w_gate, w_up, w_down, norm_final):
    n_b, seq, d = x_prompt.shape
    n_s, dec_seq, _ = x_sample.shape
    n_p = n_b * seq
    n_d = n_s * dec_seq
    depth = w_in.shape[0]
    assert depth == 1 and d == D_MODEL and n_s == LANES

    w_in_t = jnp.transpose(w_in[0])
    w_in_p = _w_in_layout(w_in_t, tn=Tiles.w_layout_cols)
    row = lambda a: a.reshape(1, -1).astype(F32)
    w2p = jnp.concatenate([rwkv_w2[0], jnp.zeros((128 - DECAY_LORA, RW_WIDTH), F32)], axis=0)
    a2p = jnp.concatenate([jnp.zeros((DECAY_LORA, RW_WIDTH), F32), rwkv_a2[0]], axis=0)
    g2p = jnp.concatenate([rwkv_g2[0], jnp.zeros((256 - GATE_LORA, RW_WIDTH), F32)], axis=0)
    rw_params = dict(mu=_pad_cols(row(rwkv_mu[0]), SHIFT_PAD), w0=row(rwkv_w0[0]), w2=w2p,
                     a0=row(rwkv_a0[0]), a2=a2p, g2=g2p, kk=row(rwkv_kk[0]), ka=row(rwkv_ka[0]),
                     rk=row(rwkv_rk[0]), lnw=row(rwkv_ln_w[0]), lnb=row(rwkv_ln_b[0]))
    col = lambda a: jnp.broadcast_to(a.reshape(-1, 1).astype(F32), (a.size, LANES))
    cols = jnp.stack([a.reshape(-1).astype(F32) for a in (rwkv_w0[0], rwkv_a0[0], rwkv_kk[0], rwkv_ka[0],
                                                         rwkv_rk[0], rwkv_ln_w[0], rwkv_ln_b[0])])
    rw_params_t = dict(cols=jnp.broadcast_to(cols[:, :, None], cols.shape + (LANES,)),
                       w2t=rwkv_w2[0].T, a2t=rwkv_a2[0].T, g2t=rwkv_g2[0].T)

    x_p = x_prompt.reshape(n_p, d)
    x_s = x_sample.reshape(n_d, d)
    meta = meta_tokens.astype(F32)
    x_sm = jnp.concatenate([x_s, meta], axis=0)
    x_tsm = jnp.concatenate([jnp.transpose(x_sample, (1, 0, 2)).reshape(n_d, d), meta,
                             jnp.zeros((LANES - N_META, d), F32)], axis=0)
    g_mix = row(norm_mix[0])
    z_p = _in_projection(x_p, g_mix, w_in_p, tm=Tiles.inproj_rows, tn=Tiles.inproj_cols)
    z_sm = _in_projection(x_sm, g_mix, w_in_p, tm=x_sm.shape[0], tn=Tiles.inproj_small_cols,
                          qk_interleaved=True, n_cols=RW_OFF)
    z_st = _in_projection_t(w_in_t, x_tsm, g_mix, SHIFT_PAD, tm=Tiles.inproj_t_rows)

    z_meta = jnp.pad(jnp.concatenate([z_sm[n_d:], jnp.transpose(z_st[:, n_d:n_d + N_META])], axis=1),
                     ((RET_CHUNK - N_META, 0), (0, 0)))
    srow = 2 * dec_seq

    zero_prev = jnp.zeros((1, 1, SHIFT_PAD), F32)
    zero_rw = jnp.zeros((1, RW_HEADS, RW_HEAD, RW_HEAD), F32)
    c_rw = Tiles.rwkv_chunk
    _, s_rw_meta, zlast_meta = _rwkv_mixer(z_meta, 1, 1, 1, zero_prev, zero_rw, rw_params, tb=c_rw, c=c_rw)
    oa_p, rwkv_p, zlast_p, w_out_b, w_gate_b, w_up_b, w_down_b = _rwkv_mixer(
        z_p, 0, n_b, seq // Tiles.rwkv_rows, zlast_meta, s_rw_meta, rw_params, tb=Tiles.rwkv_rows, c=c_rw,
        cast=(w_out[0], w_gate[0], w_up[0], w_down[0]))
    shift_t = _pad_rows(jnp.transpose(state_shift[0]), SHIFT_PAD)
    mu_t = _pad_rows(col(rwkv_mu[0]), SHIFT_PAD)
    state_t = jnp.transpose(state_rwkv[0], (1, 2, 3, 0))
    oa_st, rwkv_st = _rwkv_sample_mixer(z_st, shift_t, mu_t, state_t, rw_params_t, n_tok=dec_seq,
                                        hpb=Tiles.rwkv_sample_heads)
    oa_s = jnp.transpose(oa_st, (1, 0, 2)).reshape(n_d, RW_WIDTH)
    rwkv_s = jnp.transpose(rwkv_st, (3, 0, 1, 2))

    tabs_full = _retention_tables(RET_CHUNK, RET_CHUNK)
    tabs_meta = _retention_tables(RET_CHUNK, N_META)
    tabs_smp = _retention_tables(srow, dec_seq)
    sb_rt = Tiles.ret_sample_seqs
    cos_p, sin_p = _rotary_tables(N_META + np.arange(seq), interleaved=False)
    pos_ms = np.concatenate([np.arange(RET_CHUNK) - (RET_CHUNK - N_META),
                             PAST_LEN + np.tile(np.arange(srow) - (srow - dec_seq), sb_rt)])
    cos_ms, sin_ms = _rotary_tables(pos_ms)
    zero_rt = jnp.zeros((1, RET_HEADS, RET_HEAD, RET_HEAD), F32)
    _, s_rt_meta = _retention_mixer(z_meta, 0, 1, 1, cos_ms, sin_ms, tabs_meta, zero_rt,
                                    rows=RET_CHUNK, c=RET_CHUNK, n_real=N_META, per_chunk_state=False,
                                    pos_per_tile=(0, False))
    rows_rt = Tiles.ret_chunks_per_step * RET_CHUNK
    ob_p, ret_p = _retention_mixer(z_p, 0, n_b, seq // rows_rt, cos_p, sin_p, tabs_full, s_rt_meta,
                                   rows=rows_rt, c=RET_CHUNK, n_real=RET_CHUNK, per_chunk_state=False,
                                   pos_per_tile=(0, True), pair_split=True)
    ob_s, ret_s = _retention_mixer(z_sm, 0, n_s // sb_rt, 1, cos_ms, sin_ms, tabs_smp,
                                   state_ret.reshape(n_s, RET_HEADS, RET_HEAD, RET_HEAD),
                                   rows=sb_rt * srow, c=srow, n_real=dec_seq, per_chunk_state=True,
                                   pos_per_tile=(RET_CHUNK, False))

    g_ffn = row(norm_ffn[0])
    g_fin = row(norm_final)
    h_p, hn_p = _out_projection(oa_p, ob_p, w_out_b, x_p, g_ffn, tm=Tiles.outproj_rows)
    h_s, hn_s = _out_projection(oa_s, ob_s, w_out_b, x_s, g_ffn, tm=n_d)
    y_p = _ffn(hn_p, w_gate_b, w_up_b, w_down_b, h_p, g_fin, tm=Tiles.ffn_rows, tf=Tiles.ffn_cols)
    y_s = _ffn(hn_s, w_gate_b, w_up_b, w_down_b, h_s, g_fin, tm=n_d, tf=Tiles.ffn_cols)

    y_prompt = y_p.reshape(n_b, seq, d)
    y_sample = y_s.reshape(n_s, dec_seq, d)
    shift_p = zlast_p[:, 0, :RW_COLS][None]
    shift_s = jnp.transpose(z_st[:RW_COLS, (dec_seq - 1) * n_s:n_d])[None]
    return (y_prompt, y_sample, shift_p, rwkv_p[None], ret_p[None], shift_s, rwkv_s[None], ret_s[None])
```
